```python
import jax
import jax.numpy as jnp
from jax import lax
import numpy as np

D_MODEL = 2048
BATCH = 2
SEQ = 8192
DEPTH = 4

N_HEADS = 32
HEAD_DIM = 64
SWA_KV_HEADS = 4
SWA_WINDOW = 128
Q_BLOCK = 128
NSA_KV_HEADS = 4
CMP_LEN = 32
CMP_STRIDE = 16
CMP_HIDDEN = 512
SEL_LEN = 64
SEL_TOPK = 16
NSA_WINDOW = 512
D_FF = 5632
CONV_WIDTH = 3
RMS_EPS = 1e-6
MASK_VALUE = -1e30
FORCE_VALUE = 1e30
N_MIXERS = 2
N_SWA_LAYERS = (DEPTH + 1) // 2
N_NSA_LAYERS = DEPTH // 2
SWA_IN = N_HEADS * HEAD_DIM + 2 * SWA_KV_HEADS * HEAD_DIM
NSA_IN = N_HEADS * HEAD_DIM + 6 * NSA_KV_HEADS * HEAD_DIM + 3 * N_HEADS
F32 = jnp.float32

kernel_name = 'hybrid_swa_sink_nsa_convffn'


def rmsnorm(x, g):
    xf = x.astype(F32)
    y = xf * lax.rsqrt(jnp.mean(xf * xf, axis=-1, keepdims=True) + RMS_EPS)
    return (y * g.astype(F32)).astype(x.dtype)


def alibi_slopes(n):
    return jnp.exp2(-8.0 * jnp.arange(1, n + 1, dtype=F32) / n)


def masked_softmax(s, valid):
    s = jnp.where(valid, s, MASK_VALUE)
    m = jnp.max(s, axis=-1, keepdims=True)
    e = jnp.where(valid, jnp.exp(s - m), 0.0)
    return e / jnp.maximum(jnp.sum(e, axis=-1, keepdims=True), 1e-30)


def swa_sink_attention(h, w_in, b_in, sinks, w_o, b_o):
    B, S, _ = h.shape
    G = N_HEADS // SWA_KV_HEADS
    nb = S // Q_BLOCK
    proj = h @ w_in + b_in
    q, k, v = jnp.split(proj, [N_HEADS * HEAD_DIM, (N_HEADS + SWA_KV_HEADS) * HEAD_DIM], axis=-1)
    q = q.reshape(B, nb, Q_BLOCK, SWA_KV_HEADS, G, HEAD_DIM).astype(F32) * HEAD_DIM ** -0.5
    k = k.reshape(B, nb, Q_BLOCK, SWA_KV_HEADS, HEAD_DIM).astype(F32)
    v = v.reshape(B, nb, Q_BLOCK, SWA_KV_HEADS, HEAD_DIM).astype(F32)

    def with_prev(t):
        prev = jnp.pad(t[:, :-1], ((0, 0), (1, 0), (0, 0), (0, 0), (0, 0)))
        return jnp.concatenate([prev, t], axis=2)

    kk, vv = with_prev(k), with_prev(v)
    s = jnp.einsum('bnqkgd,bnskd->bnkgqs', q, kk)
    dist = Q_BLOCK + jnp.arange(Q_BLOCK)[:, None] - jnp.arange(2 * Q_BLOCK)[None, :]
    band = (dist >= 0) & (dist < SWA_WINDOW)
    key_pos = jnp.arange(nb)[:, None] * Q_BLOCK - Q_BLOCK + jnp.arange(2 * Q_BLOCK)[None, :]
    valid = band[None] & (key_pos >= 0)[:, None, :]
    slopes = alibi_slopes(N_HEADS).reshape(SWA_KV_HEADS, G)
    s = s - slopes[:, :, None, None] * dist.astype(F32)
    s = jnp.where(valid[:, None, None], s, MASK_VALUE)
    sink = sinks.astype(F32).reshape(SWA_KV_HEADS, G)[:, :, None]
    m = jnp.maximum(jnp.max(s, axis=-1), sink)
    e = jnp.exp(s - m[..., None])
    p = e / (jnp.sum(e, axis=-1, keepdims=True) + jnp.exp(sink - m)[..., None])
    o = jnp.einsum('bnkgqs,bnskd->bnqkgd', p, vv).reshape(B, S, N_HEADS * HEAD_DIM).astype(h.dtype)
    return o @ w_o + b_o


def compress_blocks(kv, pe, w1, b1, w2, b2):
    B, S, G, Dh = kv.shape
    n_cmp = (S - CMP_LEN) // CMP_STRIDE + 1
    idx = jnp.arange(n_cmp)[:, None] * CMP_STRIDE + jnp.arange(CMP_LEN)[None, :]
    blk = kv[:, idx] + pe[None, None, :, None, :]
    blk = blk.transpose(0, 1, 3, 2, 4).reshape(B, n_cmp, G, CMP_LEN * Dh)
    return jax.nn.gelu(blk @ w1 + b1) @ w2 + b2


def cmp_sel_overlap(S):
    n_cmp = (S - CMP_LEN) // CMP_STRIDE + 1
    cs = np.arange(n_cmp) * CMP_STRIDE
    ss = np.arange(S // SEL_LEN) * SEL_LEN
    ov = (cs[:, None] < ss[None, :] + SEL_LEN) & (cs[:, None] + CMP_LEN > ss[None, :])
    return jnp.asarray(ov.astype(np.float32))


def nsa_attention(h, w_in, cmp_pe, cmp_w1, cmp_b1, cmp_w2, cmp_b2, w_o):
    B, S, _ = h.shape
    G = NSA_KV_HEADS
    R = N_HEADS // G
    HD = N_HEADS * HEAD_DIM
    KD = G * HEAD_DIM
    proj = h @ w_in
    splits = [HD + i * KD for i in range(7)]
    q, kc, vc, ks, vs, kw, vw, gate = jnp.split(proj, splits, axis=-1)
    q = q.reshape(B, S, G, R, HEAD_DIM).astype(F32) * HEAD_DIM ** -0.5

    def to_kv(t):
        return t.reshape(B, S, G, HEAD_DIM)

    k_cmp = compress_blocks(to_kv(kc), cmp_pe[0], cmp_w1[0], cmp_b1[0], cmp_w2[0], cmp_b2[0]).astype(F32)
    v_cmp = compress_blocks(to_kv(vc), cmp_pe[1], cmp_w1[1], cmp_b1[1], cmp_w2[1], cmp_b2[1]).astype(F32)
    n_cmp = k_cmp.shape[1]
    cmp_end = jnp.arange(n_cmp) * CMP_STRIDE + CMP_LEN - 1
    overlap = cmp_sel_overlap(S)
    n_sel = S // SEL_LEN
    topk = min(SEL_TOPK, n_sel)
    k_sel = to_kv(ks).astype(F32).reshape(B, n_sel, SEL_LEN, G, HEAD_DIM).transpose(0, 3, 1, 2, 4)
    v_sel = to_kv(vs).astype(F32).reshape(B, n_sel, SEL_LEN, G, HEAD_DIM).transpose(0, 3, 1, 2, 4)
    pad = ((0, 0), (NSA_WINDOW, 0), (0, 0), (0, 0))
    k_win = jnp.pad(to_kv(kw).astype(F32), pad)
    v_win = jnp.pad(to_kv(vw).astype(F32), pad)
    gates = jax.nn.sigmoid(gate.astype(F32)).reshape(B, S, G, R, 3)
    slopes = alibi_slopes(N_HEADS).reshape(G, R)
    bi = jnp.arange(B)[:, None, None, None]
    gi = jnp.arange(G)[None, :, None, None]
    sel_ids = jnp.arange(n_sel)
    sel_off = jnp.arange(SEL_LEN)
    win_off = jnp.arange(Q_BLOCK + NSA_WINDOW)
    q_idx = jnp.arange(Q_BLOCK)

    def query_block(c):
        t0 = c * Q_BLOCK
        t = t0 + q_idx
        qc = lax.dynamic_slice_in_dim(q, t0, Q_BLOCK, axis=1)
        gc = lax.dynamic_slice_in_dim(gates, t0, Q_BLOCK, axis=1)
        d_cmp = t[:, None] - cmp_end[None, :]
        s = jnp.einsum('bqgrd,bngd->bgrqn', qc, k_cmp) - slopes[:, :, None, None] * d_cmp.astype(F32)
        p_cmp = masked_softmax(s, d_cmp >= 0)
        o_cmp = jnp.einsum('bgrqn,bngd->bqgrd', p_cmp, v_cmp)
        imp = jnp.einsum('bgrqn,nj->bgqj', p_cmp, overlap)
        cur = t // SEL_LEN
        causal = sel_ids[None, :] <= cur[:, None]
        forced = (sel_ids[None, :] == 0) | (sel_ids[None, :] == cur[:, None]) | (sel_ids[None, :] == cur[:, None] - 1)
        score = jnp.where(forced, FORCE_VALUE, jnp.where(causal, imp, MASK_VALUE))
        _, top = lax.top_k(score, topk)
        kg = k_sel[bi, gi, top].reshape(B, G, Q_BLOCK, topk * SEL_LEN, HEAD_DIM)
        vg = v_sel[bi, gi, top].reshape(B, G, Q_BLOCK, topk * SEL_LEN, HEAD_DIM)
        pos = (top[..., None] * SEL_LEN + sel_off).reshape(B, G, Q_BLOCK, topk * SEL_LEN)
        blk_ok = jnp.repeat(top <= cur[None, None, :, None], SEL_LEN, axis=-1)
        d_sel = t[None, None, :, None] - pos
        valid_sel = (d_sel >= 0) & blk_ok
        s = jnp.einsum('bqgrd,bgqsd->bgrqs', qc, kg) - slopes[None, :, :, None, None] * d_sel[:, :, None].astype(F32)
        p = masked_softmax(s, valid_sel[:, :, None])
        o_sel = jnp.einsum('bgrqs,bgqsd->bqgrd', p, vg)
        kwc = lax.dynamic_slice_in_dim(k_win, t0, Q_BLOCK + NSA_WINDOW, axis=1)
        vwc = lax.dynamic_slice_in_dim(v_win, t0, Q_BLOCK + NSA_WINDOW, axis=1)
        d_win = q_idx[:, None] + NSA_WINDOW - win_off[None, :]
        key_pos = t0 - NSA_WINDOW + win_off
        valid_win = (d_win >= 0) & (d_win < NSA_WINDOW) & (key_pos >= 0)[None, :]
        s = jnp.einsum('bqgrd,bsgd->bgrqs', qc, kwc) - slopes[:, :, None, None] * d_win.astype(F32)
        p = masked_softmax(s, valid_win)
        o_win = jnp.einsum('bgrqs,bsgd->bqgrd', p, vwc)
        o = gc[..., 0:1] * o_cmp + gc[..., 1:2] * o_sel + gc[..., 2:3] * o_win
        return o.reshape(B, Q_BLOCK, HD)

    out = lax.map(query_block, jnp.arange(S // Q_BLOCK))
    out = out.transpose(1, 0, 2, 3).reshape(B, S, HD).astype(h.dtype)
    return out @ w_o


def causal_dwconv(a, w, b):
    S = a.shape[1]
    ap = jnp.pad(a, ((0, 0), (CONV_WIDTH - 1, 0), (0, 0)))
    y = b + ap[:, 0:S] * w[0]
    for kk in range(1, CONV_WIDTH):
        y = y + ap[:, kk:kk + S] * w[kk]
    return y


def conv_ffn(h, w_gate, w_up, conv_w, conv_b, w_down):
    a = causal_dwconv(h @ w_gate, conv_w, conv_b)
    return (jax.nn.silu(a) * (h @ w_up)) @ w_down


def setup_inputs(seed: int = 0) -> dict:
    key = jax.random.key(seed)
    ks = iter(jax.random.split(key, 32))

    def rnd(shape, scale):
        return jax.random.normal(next(ks), shape, jnp.float32) * scale

    HD = N_HEADS * HEAD_DIM
    return {
        'x': rnd((BATCH, SEQ, D_MODEL), 1.0),
        'norm_g': 1.0 + rnd((DEPTH, 4, D_MODEL), 0.02),
        'swa_w_in': rnd((N_SWA_LAYERS, D_MODEL, SWA_IN), D_MODEL ** -0.5),
        'swa_b_in': rnd((N_SWA_LAYERS, SWA_IN), 0.02),
        'swa_sinks': rnd((N_SWA_LAYERS, N_HEADS), 0.5),
        'swa_w_o': rnd((N_SWA_LAYERS, HD, D_MODEL), HD ** -0.5),
        'swa_b_o': rnd((N_SWA_LAYERS, D_MODEL), 0.02),
        'nsa_w_in': rnd((N_NSA_LAYERS, D_MODEL, NSA_IN), D_MODEL ** -0.5),
        'nsa_cmp_pe': rnd((N_NSA_LAYERS, 2, CMP_LEN, HEAD_DIM), 0.1),
        'nsa_cmp_w1': rnd((N_NSA_LAYERS, 2, CMP_LEN * HEAD_DIM, CMP_HIDDEN), (CMP_LEN * HEAD_DIM) ** -0.5),
        'nsa_cmp_b1': rnd((N_NSA_LAYERS, 2, CMP_HIDDEN), 0.02),
        'nsa_cmp_w2': rnd((N_NSA_LAYERS, 2, CMP_HIDDEN, HEAD_DIM), CMP_HIDDEN ** -0.5),
        'nsa_cmp_b2': rnd((N_NSA_LAYERS, 2, HEAD_DIM), 0.02),
        'nsa_w_o': rnd((N_NSA_LAYERS, HD, D_MODEL), HD ** -0.5),
        'ffn_w_gate': rnd((DEPTH, D_MODEL, D_FF), D_MODEL ** -0.5),
        'ffn_w_up': rnd((DEPTH, D_MODEL, D_FF), D_MODEL ** -0.5),
        'ffn_conv_w': rnd((DEPTH, CONV_WIDTH, D_FF), CONV_WIDTH ** -0.5),
        'ffn_conv_b': rnd((DEPTH, D_FF), 0.02),
        'ffn_w_down': rnd((DEPTH, D_FF, D_MODEL), D_FF ** -0.5),
    }


def reference(x, norm_g, swa_w_in, swa_b_in, swa_sinks, swa_w_o, swa_b_o,
              nsa_w_in, nsa_cmp_pe, nsa_cmp_w1, nsa_cmp_b1, nsa_cmp_w2, nsa_cmp_b2, nsa_w_o,
              ffn_w_gate, ffn_w_up, ffn_conv_w, ffn_conv_b, ffn_w_down):
    for i in range(DEPTH):
        g = norm_g[i]
        h = rmsnorm(x, g[0])
        j = i // N_MIXERS
        if i % N_MIXERS == 0:
            mix = swa_sink_attention(h, swa_w_in[j], swa_b_in[j], swa_sinks[j], swa_w_o[j], swa_b_o[j])
        else:
            mix = nsa_attention(h, nsa_w_in[j], nsa_cmp_pe[j], nsa_cmp_w1[j], nsa_cmp_b1[j],
                                nsa_cmp_w2[j], nsa_cmp_b2[j], nsa_w_o[j])
        x = x + rmsnorm(mix, g[1])
        h = rmsnorm(x, g[2])
        x = x + rmsnorm(conv_ffn(h, ffn_w_gate[i], ffn_w_up[i], ffn_conv_w[i], ffn_conv_b[i], ffn_w_down[i]), g[3])
    return x
```

```python
import functools

import numpy as np
import jax
import jax.numpy as jnp
from jax import lax
from jax.experimental import pallas as pl
from jax.experimental.pallas import tpu as pltpu

F32 = jnp.float32
MXU_DTYPE = jnp.bfloat16

N_HEADS = 32
HEAD_DIM = 64
KV_GROUPS = 4
GROUP = N_HEADS // KV_GROUPS
PAIRS = GROUP // 2
LANES = 128
Q_BLOCK = 128
ROWS = GROUP * Q_BLOCK
SWA_WINDOW = 128
CMP_LEN = 32
CMP_STRIDE = 16
SEL_LEN = 64
SEL_TOPK = 16
SEL_FORCED = 3
SEL_CHUNK = 256
NSA_WINDOW = 512
CONV_WIDTH = 3
RMS_EPS = 1e-6
MASK_VALUE = -1e30
VMEM_LIMIT = 56 * 1024 * 1024


def _params(*sem):
    return pltpu.CompilerParams(dimension_semantics=sem, vmem_limit_bytes=VMEM_LIMIT)


def _alibi_neg_slopes():
    return (-np.exp2(-8.0 * np.arange(1, N_HEADS + 1, dtype=np.float64) / N_HEADS)).astype(np.float32)


def _rms_cast_kernel(x_ref, g_ref, o_ref):
    x = x_ref[...]
    ms = jnp.mean(x * x, axis=-1, keepdims=True)
    o_ref[...] = (x * lax.rsqrt(ms + RMS_EPS) * g_ref[...]).astype(o_ref.dtype)


def rms_cast(x, g, tm=512):
    t, d = x.shape
    return pl.pallas_call(
        _rms_cast_kernel,
        grid=(t // tm,),
        in_specs=[pl.BlockSpec((tm, d), lambda i: (i, 0)), pl.BlockSpec((1, d), lambda i: (0, 0))],
        out_specs=pl.BlockSpec((tm, d), lambda i: (i, 0)),
        out_shape=jax.ShapeDtypeStruct((t, d), MXU_DTYPE),
        compiler_params=_params("parallel"),
        name="rms_cast",
    )(x, g.reshape(1, d))


def _matmul_kernel(a_ref, w_ref, b_ref, o_ref):
    acc = jnp.dot(a_ref[...], w_ref[...], preferred_element_type=F32)
    o_ref[...] = (acc + b_ref[...]).astype(o_ref.dtype)


def matmul_bias(a, w, b, out_dtype, tm=1024, tn=512, name="matmul_bias"):
    t, k = a.shape
    n = w.shape[1]
    tn = min(tn, n)
    return pl.pallas_call(
        _matmul_kernel,
        grid=(t // tm, n // tn),
        in_specs=[
            pl.BlockSpec((tm, k), lambda i, j: (i, 0)),
            pl.BlockSpec((k, tn), lambda i, j: (0, j)),
            pl.BlockSpec((1, tn), lambda i, j: (0, j)),
        ],
        out_specs=pl.BlockSpec((tm, tn), lambda i, j: (i, j)),
        out_shape=jax.ShapeDtypeStruct((t, n), out_dtype),
        compiler_params=_params("parallel", "arbitrary"),
        name=name,
    )(a, w, b.reshape(1, n))


def _mm_norm_res_kernel(a_ref, w_ref, b_ref, gpost_ref, gnext_ref, x_ref, xo_ref, ho_ref, y_scr, *, nj, tn, n):
    j = pl.program_id(1)
    y_scr[j] = jnp.dot(a_ref[...], w_ref[...], preferred_element_type=F32) + b_ref[...]

    @pl.when(j == nj - 1)
    def _():
        ss = jnp.sum(y_scr[0] * y_scr[0], axis=1, keepdims=True)
        for jj in range(1, nj):
            ss = ss + jnp.sum(y_scr[jj] * y_scr[jj], axis=1, keepdims=True)
        r = lax.rsqrt(ss / n + RMS_EPS)
        ss2 = jnp.zeros_like(ss)
        for jj in range(nj):
            cols = slice(jj * tn, (jj + 1) * tn)
            xn = x_ref[:, cols] + y_scr[jj] * r * gpost_ref[:, cols]
            xo_ref[:, cols] = xn
            ss2 = ss2 + jnp.sum(xn * xn, axis=1, keepdims=True)
        r2 = lax.rsqrt(ss2 / n + RMS_EPS)
        for jj in range(nj):
            cols = slice(jj * tn, (jj + 1) * tn)
            ho_ref[:, cols] = (xo_ref[:, cols] * r2 * gnext_ref[:, cols]).astype(ho_ref.dtype)


def matmul_norm_res(a, w, b, g_post, g_next, x, tm=512, tn=512, name="matmul_norm_res"):
    t, k = a.shape
    n = w.shape[1]
    nj = n // tn
    kern = functools.partial(_mm_norm_res_kernel, nj=nj, tn=tn, n=n)
    return pl.pallas_call(
        kern,
        grid=(t // tm, nj),
        in_specs=[
            pl.BlockSpec((tm, k), lambda i, j: (i, 0)),
            pl.BlockSpec((k, tn), lambda i, j: (0, j)),
            pl.BlockSpec((1, tn), lambda i, j: (0, j)),
            pl.BlockSpec((1, n), lambda i, j: (0, 0)),
            pl.BlockSpec((1, n), lambda i, j: (0, 0)),
            pl.BlockSpec((tm, n), lambda i, j: (i, 0)),
        ],
        out_specs=[
            pl.BlockSpec((tm, n), lambda i, j: (i, 0)),
            pl.BlockSpec((tm, n), lambda i, j: (i, 0)),
        ],
        out_shape=[jax.ShapeDtypeStruct((t, n), F32), jax.ShapeDtypeStruct((t, n), MXU_DTYPE)],
        scratch_shapes=[pltpu.VMEM((nj, tm, tn), F32)],
        compiler_params=_params("parallel", "arbitrary"),
        name=name,
    )(a, w, b.reshape(1, n), g_post.reshape(1, n), g_next.reshape(1, n), x)


FIX_ROWS = 16
CARRY_ROWS = 8


def _ffn_up_kernel(h_ref, wg_ref, wu_ref, cw_ref, cb_ref, o_ref, carry_scr, *, tm, tiles_per_seq):
    i = pl.program_id(0)
    j = pl.program_id(1)
    h = h_ref[...]
    gate = jnp.dot(h, wg_ref[...], preferred_element_type=F32)
    up = jnp.dot(h, wu_ref[...], preferred_element_type=F32)
    w0 = cw_ref[0:1, :]
    w1 = cw_ref[1:2, :]
    w2 = cw_ref[2:3, :]
    b = cb_ref[...]
    a = b + pltpu.roll(gate, 2, axis=0) * w0
    a = a + pltpu.roll(gate, 1, axis=0) * w1
    a = a + gate * w2
    o_ref[...] = (jax.nn.silu(a) * up).astype(o_ref.dtype)

    seq_start = (i % tiles_per_seq) == 0
    prev = jnp.where(seq_start, 0.0, carry_scr[j])
    head = gate[0:FIX_ROWS]
    ext = jnp.concatenate([prev, head], axis=0)
    af = b + ext[CARRY_ROWS - 2:CARRY_ROWS - 2 + FIX_ROWS] * w0
    af = af + ext[CARRY_ROWS - 1:CARRY_ROWS - 1 + FIX_ROWS] * w1
    af = af + head * w2
    o_ref[0:FIX_ROWS, :] = (jax.nn.silu(af) * up[0:FIX_ROWS]).astype(o_ref.dtype)
    carry_scr[j] = gate[tm - CARRY_ROWS:tm]


def ffn_up(h, wg, wu, conv_w, conv_b, seq, tm=1024, tn=512):
    t, k = h.shape
    n = wg.shape[1]
    nj = n // tn
    kern = functools.partial(_ffn_up_kernel, tm=tm, tiles_per_seq=seq // tm)
    return pl.pallas_call(
        kern,
        grid=(t // tm, nj),
        in_specs=[
            pl.BlockSpec((tm, k), lambda i, j: (i, 0)),
            pl.BlockSpec((k, tn), lambda i, j: (0, j)),
            pl.BlockSpec((k, tn), lambda i, j: (0, j)),
            pl.BlockSpec((CONV_WIDTH, tn), lambda i, j: (0, j)),
            pl.BlockSpec((1, tn), lambda i, j: (0, j)),
        ],
        out_specs=pl.BlockSpec((tm, tn), lambda i, j: (i, j)),
        out_shape=jax.ShapeDtypeStruct((t, n), MXU_DTYPE),
        scratch_shapes=[pltpu.VMEM((nj, CARRY_ROWS, tn), F32)],
        compiler_params=_params("arbitrary", "arbitrary"),
        name="ffn_up",
    )(h, wg, wu, conv_w, conv_b.reshape(1, n))


def _head_of_row_block(group, rb):
    return group * GROUP + 2 * (rb % PAIRS) + rb // PAIRS


def _build_q_stack(q_ref, qs_ref):
    lane = lax.broadcasted_iota(jnp.int32, (Q_BLOCK, LANES), 1)
    even = lane < HEAD_DIM
    for p in range(PAIRS):
        qp = q_ref[:, p * LANES:(p + 1) * LANES].astype(F32) * (HEAD_DIM ** -0.5)
        qs_ref[p * Q_BLOCK:(p + 1) * Q_BLOCK, :] = jnp.where(even, qp, 0.0).astype(qs_ref.dtype)
        qs_ref[(PAIRS + p) * Q_BLOCK:(PAIRS + p + 1) * Q_BLOCK, :] = jnp.where(even, 0.0, qp).astype(qs_ref.dtype)


def _rep(x, size):
    return x if size == LANES else jnp.concatenate([x] * (size // LANES), axis=1)


def _attend_chunk(qs_ref, k, v, dist, negb, nslope, m_ref, acc_ref, p_scr, alpha_scr):
    size = k.shape[0]
    s_all = lax.dot_general(qs_ref[...], k, (((1,), (1,)), ((), ())), preferred_element_type=F32)
    for rb in range(GROUP):
        rows = slice(rb * Q_BLOCK, (rb + 1) * Q_BLOCK)
        s = s_all[rows] + nslope(rb) * dist + negb
        m_prev = m_ref[rows, :]
        m_new = jnp.maximum(m_prev, jnp.max(s, axis=1, keepdims=True))
        alpha_scr[rows, :] = jnp.exp(m_prev - m_new)
        p_scr[rows, 0:size] = jnp.exp(s - _rep(m_new, size)).astype(p_scr.dtype)
        m_ref[rows, :] = m_new
    pv = jnp.dot(p_scr[:, 0:size], v, preferred_element_type=F32)
    alpha = alpha_scr[...]
    acc_ref[...] = acc_ref[...] * jnp.concatenate([alpha, alpha], axis=1) + pv


def _finish_pairs(acc_ref):
    lane = lax.broadcasted_iota(jnp.int32, (Q_BLOCK, LANES), 1)
    even = lane < HEAD_DIM
    outs = []
    for p in range(PAIRS):
        re = slice(p * Q_BLOCK, (p + 1) * Q_BLOCK)
        ro = slice((PAIRS + p) * Q_BLOCK, (PAIRS + p + 1) * Q_BLOCK)
        oe = acc_ref[re, 0:LANES] / jnp.maximum(acc_ref[re, LANES:2 * LANES], 1e-30)
        oo = acc_ref[ro, 0:LANES] / jnp.maximum(acc_ref[ro, LANES:2 * LANES], 1e-30)
        outs.append(jnp.where(even, oe, oo))
    return outs


def _pos_tiles(t0, start, size):
    qi = lax.broadcasted_iota(jnp.int32, (Q_BLOCK, size), 0)
    ki = lax.broadcasted_iota(jnp.int32, (Q_BLOCK, size), 1)
    return (t0 - start) + (qi - ki)


_ATT_SCRATCH = lambda: [
    pltpu.VMEM((ROWS, LANES), MXU_DTYPE),
    pltpu.VMEM((ROWS, LANES), F32),
    pltpu.VMEM((ROWS, 2 * LANES), F32),
    pltpu.VMEM((ROWS, SEL_CHUNK), MXU_DTYPE),
    pltpu.VMEM((ROWS, LANES), F32),
]


def _swa_kernel(nslope_ref, sink_ref, q_ref, k_ref, v_ref, o_ref, qs_ref, m_ref, acc_ref, p_scr, alpha_scr):
    grp = pl.program_id(1)
    c = pl.program_id(2)
    t0 = c * Q_BLOCK
    _build_q_stack(q_ref, qs_ref)
    for rb in range(GROUP):
        rows = slice(rb * Q_BLOCK, (rb + 1) * Q_BLOCK)
        m_ref[rows, :] = jnp.full((Q_BLOCK, LANES), sink_ref[_head_of_row_block(grp, rb)], F32)
    acc_ref[:, 0:LANES] = jnp.zeros((ROWS, LANES), F32)
    acc_ref[:, LANES:2 * LANES] = jnp.ones((ROWS, LANES), F32)
    nslope = lambda rb: nslope_ref[_head_of_row_block(grp, rb)]

    def block(blk):
        start = pl.multiple_of(blk * Q_BLOCK, Q_BLOCK)
        d = _pos_tiles(t0, start, Q_BLOCK)
        negb = jnp.where((d >= 0) & (d < SWA_WINDOW), 0.0, MASK_VALUE)
        _attend_chunk(qs_ref, k_ref[0, 0, pl.ds(start, Q_BLOCK), :], v_ref[0, 0, pl.ds(start, Q_BLOCK), :],
                      d.astype(F32), negb, nslope, m_ref, acc_ref, p_scr, alpha_scr)

    @pl.when(c >= 1)
    def _():
        block(c - 1)

    block(c)
    for p, o in enumerate(_finish_pairs(acc_ref)):
        o_ref[:, p * LANES:(p + 1) * LANES] = o.astype(o_ref.dtype)


def swa_attention(qkv, k2, v2, nslopes, sinks, batch, seq):
    nblk = seq // Q_BLOCK
    gw = PAIRS * LANES
    return pl.pallas_call(
        _swa_kernel,
        grid=(batch, KV_GROUPS, nblk),
        in_specs=[
            pl.BlockSpec(memory_space=pltpu.SMEM),
            pl.BlockSpec(memory_space=pltpu.SMEM),
            pl.BlockSpec((Q_BLOCK, gw), lambda b, g, c: (b * nblk + c, g)),
            pl.BlockSpec((1, 1, seq, LANES), lambda b, g, c: (b, g, 0, 0)),
            pl.BlockSpec((1, 1, seq, 2 * LANES), lambda b, g, c: (b, g, 0, 0)),
        ],
        out_specs=pl.BlockSpec((Q_BLOCK, gw), lambda b, g, c: (b * nblk + c, g)),
        out_shape=jax.ShapeDtypeStruct((batch * seq, N_HEADS * HEAD_DIM), MXU_DTYPE),
        scratch_shapes=_ATT_SCRATCH(),
        compiler_params=_params("parallel", "parallel", "arbitrary"),
        name="swa_attention",
    )(nslopes, sinks, qkv, k2, v2)


def _compress_kernel(z_ref, pe_ref, w1_ref, b1_ref, w2_ref, b2_ref, o_ref, *, ncp):
    half = CMP_STRIDE * HEAD_DIM
    z = z_ref[0, 0, 0].astype(F32)
    top = (z + pe_ref[0, 0:1, :]).astype(MXU_DTYPE)
    bot = (z + pe_ref[0, 1:2, :]).astype(MXU_DTYPE)
    a = jnp.dot(top, w1_ref[0, 0:half, :], preferred_element_type=F32)
    bm = jnp.dot(bot, w1_ref[0, half:2 * half, :], preferred_element_type=F32)
    hid = a + pltpu.roll(bm, ncp - 1, axis=0) + b1_ref[0]
    act = jax.nn.gelu(hid).astype(MXU_DTYPE)
    o_ref[0, 0, 0] = jnp.dot(act, w2_ref[0], preferred_element_type=F32) + b2_ref[0]


def compress(z, pe, w1, b1, w2, b2):
    _, batch, groups, ncp, zw = z.shape
    hid = w1.shape[-1]
    kern = functools.partial(_compress_kernel, ncp=ncp)
    return pl.pallas_call(
        kern,
        grid=(2, batch, groups),
        in_specs=[
            pl.BlockSpec((1, 1, 1, ncp, zw), lambda s, b, g: (s, b, g, 0, 0)),
            pl.BlockSpec((1, 2, zw), lambda s, b, g: (s, 0, 0)),
            pl.BlockSpec((1, 2 * zw, hid), lambda s, b, g: (s, 0, 0)),
            pl.BlockSpec((1, 1, hid), lambda s, b, g: (s, 0, 0)),
            pl.BlockSpec((1, hid, HEAD_DIM), lambda s, b, g: (s, 0, 0)),
            pl.BlockSpec((1, 1, HEAD_DIM), lambda s, b, g: (s, 0, 0)),
        ],
        out_specs=pl.BlockSpec((1, 1, 1, ncp, HEAD_DIM), lambda s, b, g: (s, b, g, 0, 0)),
        out_shape=jax.ShapeDtypeStruct((2, batch, groups, ncp, HEAD_DIM), F32),
        compiler_params=_params("parallel", "parallel", "parallel"),
        name="nsa_compress",
    )(z, pe, w1, b1, w2, b2)


def _nsa_cmp_kernel(nslope_ref, q_ref, kc_ref, vc_ref, o_ref, sel_ref, qs_ref, e_scr, *, ncp, n_cmp, n_sel):
    grp = pl.program_id(1)
    c = pl.program_id(2)
    t0 = c * Q_BLOCK
    _build_q_stack(q_ref, qs_ref)
    qi = lax.broadcasted_iota(jnp.int32, (Q_BLOCK, ncp), 0)
    ni = lax.broadcasted_iota(jnp.int32, (Q_BLOCK, ncp), 1)
    d = (t0 + qi) - (ni * CMP_STRIDE + (CMP_LEN - 1))
    negb = jnp.where((d >= 0) & (ni < n_cmp), 0.0, MASK_VALUE)
    dist = d.astype(F32)
    s_all = lax.dot_general(qs_ref[...], kc_ref[0, 0], (((1,), (1,)), ((), ())), preferred_element_type=F32)
    for rb in range(GROUP):
        rows = slice(rb * Q_BLOCK, (rb + 1) * Q_BLOCK)
        s = s_all[rows] + nslope_ref[_head_of_row_block(grp, rb)] * dist + negb
        m = jnp.max(s, axis=1, keepdims=True)
        e_scr[rows, :] = jnp.exp(s - m).astype(e_scr.dtype)
    r = jnp.dot(e_scr[...], vc_ref[0, 0], preferred_element_type=F32)
    lane = lax.broadcasted_iota(jnp.int32, (Q_BLOCK, LANES), 1)
    even = lane < HEAD_DIM
    row_t = t0 + lax.broadcasted_iota(jnp.int32, (Q_BLOCK, LANES), 0)
    has_cmp = row_t >= (CMP_LEN - 1)
    imp = jnp.zeros((Q_BLOCK, LANES), F32)
    for p in range(PAIRS):
        re = slice(p * Q_BLOCK, (p + 1) * Q_BLOCK)
        ro = slice((PAIRS + p) * Q_BLOCK, (PAIRS + p + 1) * Q_BLOCK)
        de = jnp.maximum(r[re, LANES:2 * LANES], 1e-30)
        do = jnp.maximum(r[ro, LANES:2 * LANES], 1e-30)
        o = jnp.where(even, r[re, 0:LANES] / de, r[ro, 0:LANES] / do)
        o_ref[:, p * LANES:(p + 1) * LANES] = jnp.where(has_cmp, o, 0.0)
        imp = imp + r[re, 2 * LANES:3 * LANES] / de + r[ro, 2 * LANES:3 * LANES] / do
    imp = jnp.where(has_cmp, imp, 0.0)

    imp_t = imp.T
    ji = lax.broadcasted_iota(jnp.int32, (LANES, Q_BLOCK), 0)
    qt = lax.broadcasted_iota(jnp.int32, (LANES, Q_BLOCK), 1)
    cur = (t0 + qt) // SEL_LEN
    causal = ji <= cur
    forced = (ji == 0) | (ji == cur) | (ji == cur - 1)
    neg_inf = -jnp.inf
    score = jnp.where(forced, neg_inf, jnp.where(causal, imp_t, MASK_VALUE))
    score = jnp.where(ji < n_sel, score, neg_inf)
    picked = jnp.where(forced, 1.0, 0.0)
    for _ in range(SEL_TOPK - SEL_FORCED):
        mx = jnp.max(score, axis=0, keepdims=True)
        first = jnp.min(jnp.where(score == mx, ji, LANES), axis=0, keepdims=True)
        hit = ji == first
        picked = jnp.where(hit, 1.0, picked)
        score = jnp.where(hit, neg_inf, score)
    picked = jnp.where(causal, picked, 0.0)
    sel_ref[0, 0] = picked.T.astype(sel_ref.dtype)


def nsa_cmp_select(q, kc2, vc_aug, nslopes, batch, seq, q_col_block):
    nblk = seq // Q_BLOCK
    ncp = seq // CMP_STRIDE
    n_cmp = (seq - CMP_LEN) // CMP_STRIDE + 1
    n_sel = seq // SEL_LEN
    gw = PAIRS * LANES
    kern = functools.partial(_nsa_cmp_kernel, ncp=ncp, n_cmp=n_cmp, n_sel=n_sel)
    return pl.pallas_call(
        kern,
        grid=(batch, KV_GROUPS, nblk),
        in_specs=[
            pl.BlockSpec(memory_space=pltpu.SMEM),
            pl.BlockSpec((Q_BLOCK, gw), lambda b, g, c: (b * nblk + c, q_col_block + g)),
            pl.BlockSpec((1, 1, ncp, LANES), lambda b, g, c: (b, g, 0, 0)),
            pl.BlockSpec((1, 1, ncp, 3 * LANES), lambda b, g, c: (b, g, 0, 0)),
        ],
        out_specs=[
            pl.BlockSpec((Q_BLOCK, gw), lambda b, g, c: (b * nblk + c, g)),
            pl.BlockSpec((1, 1, Q_BLOCK, LANES), lambda b, g, c: (b, g, c, 0)),
        ],
        out_shape=[
            jax.ShapeDtypeStruct((batch * seq, N_HEADS * HEAD_DIM), F32),
            jax.ShapeDtypeStruct((batch, KV_GROUPS, seq, LANES), MXU_DTYPE),
        ],
        scratch_shapes=[pltpu.VMEM((ROWS, LANES), MXU_DTYPE), pltpu.VMEM((ROWS, ncp), MXU_DTYPE)],
        compiler_params=_params("parallel", "parallel", "arbitrary"),
        name="nsa_cmp_select",
    )(nslopes, q, kc2, vc_aug)


def _nsa_sel_kernel(nslope_ref, q_ref, sel_ref, k_ref, v_ref, o_ref, qs_ref, m_ref, acc_ref, p_scr, alpha_scr):
    grp = pl.program_id(1)
    c = pl.program_id(2)
    t0 = c * Q_BLOCK
    _build_q_stack(q_ref, qs_ref)
    m_ref[...] = jnp.full((ROWS, LANES), MASK_VALUE, F32)
    acc_ref[...] = jnp.zeros((ROWS, 2 * LANES), F32)
    nslope = lambda rb: nslope_ref[_head_of_row_block(grp, rb)]
    sel = sel_ref[0, 0]
    bj = lax.broadcasted_iota(jnp.int32, (LANES, SEL_CHUNK), 0)
    bk = lax.broadcasted_iota(jnp.int32, (LANES, SEL_CHUNK), 1) // SEL_LEN

    def body(ch, carry):
        start = pl.multiple_of(ch * SEL_CHUNK, SEL_CHUNK)
        expand = jnp.where(bj == bk + ch * (SEL_CHUNK // SEL_LEN), 1.0, 0.0).astype(sel.dtype)
        picked = jnp.dot(sel, expand, preferred_element_type=F32)
        d = _pos_tiles(t0, start, SEL_CHUNK)
        negb = jnp.where((picked > 0.5) & (d >= 0), 0.0, MASK_VALUE)
        _attend_chunk(qs_ref, k_ref[0, 0, pl.ds(start, SEL_CHUNK), :], v_ref[0, 0, pl.ds(start, SEL_CHUNK), :],
                      d.astype(F32), negb, nslope, m_ref, acc_ref, p_scr, alpha_scr)
        return carry

    lax.fori_loop(0, (t0 + Q_BLOCK - 1) // SEL_CHUNK + 1, body, 0)
    for p, o in enumerate(_finish_pairs(acc_ref)):
        o_ref[:, p * LANES:(p + 1) * LANES] = o


def nsa_sel_attention(q, sel, k2, v2, nslopes, batch, seq, q_col_block):
    nblk = seq // Q_BLOCK
    gw = PAIRS * LANES
    return pl.pallas_call(
        _nsa_sel_kernel,
        grid=(batch, KV_GROUPS, nblk),
        in_specs=[
            pl.BlockSpec(memory_space=pltpu.SMEM),
            pl.BlockSpec((Q_BLOCK, gw), lambda b, g, c: (b * nblk + c, q_col_block + g)),
            pl.BlockSpec((1, 1, Q_BLOCK, LANES), lambda b, g, c: (b, g, c, 0)),
            pl.BlockSpec((1, 1, seq, LANES), lambda b, g, c: (b, g, 0, 0)),
            pl.BlockSpec((1, 1, seq, 2 * LANES), lambda b, g, c: (b, g, 0, 0)),
        ],
        out_specs=pl.BlockSpec((Q_BLOCK, gw), lambda b, g, c: (b * nblk + c, g)),
        out_shape=jax.ShapeDtypeStruct((batch * seq, N_HEADS * HEAD_DIM), F32),
        scratch_shapes=_ATT_SCRATCH(),
        compiler_params=_params("parallel", "parallel", "arbitrary"),
        name="nsa_sel_attention",
    )(nslopes, q, sel, k2, v2)


def _nsa_win_kernel(nslope_ref, q_ref, k_ref, v_ref, o_ref, qs_ref, m_ref, acc_ref, p_scr, alpha_scr):
    grp = pl.program_id(1)
    c = pl.program_id(2)
    t0 = c * Q_BLOCK
    _build_q_stack(q_ref, qs_ref)
    m_ref[...] = jnp.full((ROWS, LANES), MASK_VALUE, F32)
    acc_ref[...] = jnp.zeros((ROWS, 2 * LANES), F32)
    nslope = lambda rb: nslope_ref[_head_of_row_block(grp, rb)]

    for back in range(NSA_WINDOW // Q_BLOCK, -1, -1):
        def block(back=back):
            start = pl.multiple_of((c - back) * Q_BLOCK, Q_BLOCK)
            d = _pos_tiles(t0, start, Q_BLOCK)
            negb = jnp.where((d >= 0) & (d < NSA_WINDOW), 0.0, MASK_VALUE)
            _attend_chunk(qs_ref, k_ref[0, 0, pl.ds(start, Q_BLOCK), :], v_ref[0, 0, pl.ds(start, Q_BLOCK), :],
                          d.astype(F32), negb, nslope, m_ref, acc_ref, p_scr, alpha_scr)

        if back == 0:
            block()
        else:
            pl.when(c >= back)(block)
    for p, o in enumerate(_finish_pairs(acc_ref)):
        o_ref[:, p * LANES:(p + 1) * LANES] = o


def nsa_win_attention(q, k2, v2, nslopes, batch, seq, q_col_block):
    nblk = seq // Q_BLOCK
    gw = PAIRS * LANES
    return pl.pallas_call(
        _nsa_win_kernel,
        grid=(batch, KV_GROUPS, nblk),
        in_specs=[
            pl.BlockSpec(memory_space=pltpu.SMEM),
            pl.BlockSpec((Q_BLOCK, gw), lambda b, g, c: (b * nblk + c, q_col_block + g)),
            pl.BlockSpec((1, 1, seq, LANES), lambda b, g, c: (b, g, 0, 0)),
            pl.BlockSpec((1, 1, seq, 2 * LANES), lambda b, g, c: (b, g, 0, 0)),
        ],
        out_specs=pl.BlockSpec((Q_BLOCK, gw), lambda b, g, c: (b * nblk + c, g)),
        out_shape=jax.ShapeDtypeStruct((batch * seq, N_HEADS * HEAD_DIM), F32),
        scratch_shapes=_ATT_SCRATCH(),
        compiler_params=_params("parallel", "parallel", "arbitrary"),
        name="nsa_win_attention",
    )(nslopes, q, k2, v2)


def _nsa_combine_kernel(gate_ref, ex_ref, oc_ref, os_ref, ow_ref, o_ref):
    hd = o_ref.shape[1]
    sig = jax.nn.sigmoid(gate_ref[...])
    hi = sig.astype(MXU_DTYPE)
    lo = (sig - hi.astype(F32)).astype(MXU_DTYPE)
    ex = ex_ref[...]
    g = jnp.dot(hi, ex, preferred_element_type=F32) + jnp.dot(lo, ex, preferred_element_type=F32)
    out = g[:, 0:hd] * oc_ref[...] + g[:, hd:2 * hd] * os_ref[...] + g[:, 2 * hd:3 * hd] * ow_ref[...]
    o_ref[...] = out.astype(o_ref.dtype)


def _gate_expansion():
    hd = N_HEADS * HEAD_DIM
    ex = np.zeros((LANES, 3 * hd), np.float32)
    for h in range(N_HEADS):
        for i in range(3):
            ex[3 * h + i, i * hd + h * HEAD_DIM:i * hd + (h + 1) * HEAD_DIM] = 1.0
    return ex


def nsa_combine(gate, oc, osel, ow, tm=256):
    t, hd = oc.shape
    ex = jnp.asarray(_gate_expansion(), MXU_DTYPE)
    row = lambda i: (i, 0)
    return pl.pallas_call(
        _nsa_combine_kernel,
        grid=(t // tm,),
        in_specs=[
            pl.BlockSpec((tm, LANES), row),
            pl.BlockSpec((LANES, 3 * hd), lambda i: (0, 0)),
            pl.BlockSpec((tm, hd), row),
            pl.BlockSpec((tm, hd), row),
            pl.BlockSpec((tm, hd), row),
        ],
        out_specs=pl.BlockSpec((tm, hd), row),
        out_shape=jax.ShapeDtypeStruct((t, hd), MXU_DTYPE),
        compiler_params=_params("parallel"),
        name="nsa_combine",
    )(gate, ex, oc, osel, ow)


def _dup_lanes(t, batch, seq):
    t = t.reshape(batch, seq, KV_GROUPS, HEAD_DIM).transpose(0, 2, 1, 3)
    return jnp.concatenate([t, t], axis=-1)


def _dup_lanes_ones(t, batch, seq):
    d = _dup_lanes(t, batch, seq)
    return jnp.concatenate([d, jnp.ones_like(d)], axis=-1)


def _overlap_matrix(seq):
    ncp = seq // CMP_STRIDE
    n_cmp = (seq - CMP_LEN) // CMP_STRIDE + 1
    cs = np.arange(n_cmp) * CMP_STRIDE
    ss = np.arange(seq // SEL_LEN) * SEL_LEN
    ov = (cs[:, None] < ss[None, :] + SEL_LEN) & (cs[:, None] + CMP_LEN > ss[None, :])
    out = np.zeros((ncp, LANES), np.float32)
    out[:n_cmp, :seq // SEL_LEN] = ov
    return out


def _swa_layer(h, x, w_in, b_in, sinks, w_o, b_o, g_post, g_next, nslopes, batch, seq):
    hd = N_HEADS * HEAD_DIM
    kd = KV_GROUPS * HEAD_DIM
    qkv = matmul_bias(h, w_in.astype(MXU_DTYPE), b_in, MXU_DTYPE, name="swa_in_proj")
    k2 = _dup_lanes(qkv[:, hd:hd + kd], batch, seq)
    v2 = _dup_lanes_ones(qkv[:, hd + kd:hd + 2 * kd], batch, seq)
    o = swa_attention(qkv, k2, v2, nslopes, sinks.astype(F32), batch, seq)
    return matmul_norm_res(o, w_o.astype(MXU_DTYPE), b_o, g_post, g_next, x, name="swa_out_proj")


def _nsa_layer(h, x, w_in, cmp_pe, cmp_w1, cmp_b1, cmp_w2, cmp_b2, w_o, g_post, g_next, nslopes, batch, seq):
    hd = N_HEADS * HEAD_DIM
    kd = KV_GROUPS * HEAD_DIM
    t = batch * seq
    ncp = seq // CMP_STRIDE
    qkv = matmul_bias(h, w_in[:, :hd + 6 * kd].astype(MXU_DTYPE), jnp.zeros((hd + 6 * kd,), F32), MXU_DTYPE,
                      name="nsa_in_proj")
    n_gate = 3 * N_HEADS
    w_gate = jnp.pad(w_in[:, hd + 6 * kd:], ((0, 0), (0, LANES - n_gate))).astype(MXU_DTYPE)
    gate = matmul_bias(h, w_gate, jnp.zeros((LANES,), F32), F32, name="nsa_gate_proj")

    def kv(i):
        return qkv[:, hd + i * kd:hd + (i + 1) * kd]

    def slabs(a):
        return a.reshape(batch, seq, KV_GROUPS, HEAD_DIM).transpose(0, 2, 1, 3).reshape(
            batch, KV_GROUPS, ncp, CMP_STRIDE * HEAD_DIM)

    z = jnp.stack([slabs(kv(0)), slabs(kv(1))])
    half = CMP_STRIDE * HEAD_DIM
    cmp_out = compress(z, cmp_pe.reshape(2, 2, half).astype(F32), cmp_w1.astype(MXU_DTYPE),
                       cmp_b1.reshape(2, 1, -1), cmp_w2.astype(MXU_DTYPE), cmp_b2.reshape(2, 1, -1))
    kcm = cmp_out[0].astype(MXU_DTYPE)
    vcm = cmp_out[1].astype(MXU_DTYPE)
    kc2 = jnp.concatenate([kcm, kcm], axis=-1)
    ov = jnp.broadcast_to(jnp.asarray(_overlap_matrix(seq), MXU_DTYPE), (batch, KV_GROUPS, ncp, LANES))
    vc_aug = jnp.concatenate([vcm, vcm, jnp.ones((batch, KV_GROUPS, ncp, LANES), MXU_DTYPE), ov], axis=-1)

    o_cmp, sel = nsa_cmp_select(qkv, kc2, vc_aug, nslopes, batch, seq, 0)
    o_sel = nsa_sel_attention(qkv, sel, _dup_lanes(kv(2), batch, seq), _dup_lanes_ones(kv(3), batch, seq),
                              nslopes, batch, seq, 0)
    o_win = nsa_win_attention(qkv, _dup_lanes(kv(4), batch, seq), _dup_lanes_ones(kv(5), batch, seq),
                              nslopes, batch, seq, 0)
    o = nsa_combine(gate, o_cmp, o_sel, o_win)
    return matmul_norm_res(o, w_o.astype(MXU_DTYPE), jnp.zeros((w_o.shape[1],), F32), g_post, g_next, x,
                           name="nsa_out_proj")


def kernel(x, norm_g, swa_w_in, swa_b_in, swa_sinks, swa_w_o, swa_b_o, nsa_w_in, nsa_cmp_pe, nsa_cmp_w1, nsa_cmp_b1, nsa_cmp_w2, nsa_cmp_b2, nsa_w_o, ffn_w_gate, ffn_w_up, ffn_conv_w, ffn_conv_b, ffn_w_down):
    batch, seq, d = x.shape
    depth = norm_g.shape[0]
    nslopes = jnp.asarray(_alibi_neg_slopes())
    xf = x.reshape(batch * seq, d)
    h = rms_cast(xf, norm_g[0, 0])
    for i in range(depth):
        g = norm_g[i]
        j = i // 2
        if i % 2 == 0:
            xf, h = _swa_layer(h, xf, swa_w_in[j], swa_b_in[j], swa_sinks[j], swa_w_o[j], swa_b_o[j],
                               g[1], g[2], nslopes, batch, seq)
        else:
            xf, h = _nsa_layer(h, xf, nsa_w_in[j], nsa_cmp_pe[j], nsa_cmp_w1[j], nsa_cmp_b1[j], nsa_cmp_w2[j],
                               nsa_cmp_b2[j], nsa_w_o[j], g[1], g[2], nslopes, batch, seq)
        act = ffn_up(h, ffn_w_gate[i].astype(MXU_DTYPE), ffn_w_up[i].astype(MXU_DTYPE), ffn_conv_w[i],
                     ffn_conv_b[i], seq)
        g_next = norm_g[i + 1, 0] if i + 1 < depth else jnp.ones((d,), F32)
        xf, h = matmul_norm_res(act, ffn_w_down[i].astype(MXU_DTYPE), jnp.zeros((d,), F32), g[3], g_next, xf,
                                tn=256, name="ffn_down")
    return xf.reshape(batch, seq, d)
```

```python
import functools

import numpy as np
import jax
import jax.numpy as jnp
from jax import lax
from jax.experimental import pallas as pl
from jax.experimental.pallas import tpu as pltpu

F32 = jnp.float32
MXU_DTYPE = jnp.bfloat16

N_HEADS = 32
HEAD_DIM = 64
KV_GROUPS = 4
GROUP = N_HEADS // KV_GROUPS
PAIRS = GROUP // 2
LANES = 128
Q_BLOCK = 128
ROWS = GROUP * Q_BLOCK
SWA_WINDOW = 128
CMP_LEN = 32
CMP_STRIDE = 16
SEL_LEN = 64
SEL_TOPK = 16
SEL_FORCED = 3
SEL_STEP = 512
NSA_WINDOW = 512
CONV_WIDTH = 3
RMS_EPS = 1e-6
MASK_VALUE = -1e30
VMEM_LIMIT = 56 * 1024 * 1024


def _params(*sem):
    return pltpu.CompilerParams(dimension_semantics=sem, vmem_limit_bytes=VMEM_LIMIT)


def _alibi_neg_slopes():
    return (-np.exp2(-8.0 * np.arange(1, N_HEADS + 1, dtype=np.float64) / N_HEADS)).astype(np.float32)


def _rms_cast_kernel(x_ref, g_ref, o_ref):
    x = x_ref[...]
    ms = jnp.mean(x * x, axis=-1, keepdims=True)
    o_ref[...] = (x * lax.rsqrt(ms + RMS_EPS) * g_ref[...]).astype(o_ref.dtype)


def rms_cast(x, g, tm=512):
    t, d = x.shape
    return pl.pallas_call(
        _rms_cast_kernel,
        grid=(t // tm,),
        in_specs=[pl.BlockSpec((tm, d), lambda i: (i, 0)), pl.BlockSpec((1, d), lambda i: (0, 0))],
        out_specs=pl.BlockSpec((tm, d), lambda i: (i, 0)),
        out_shape=jax.ShapeDtypeStruct((t, d), MXU_DTYPE),
        compiler_params=_params("parallel"),
        name="rms_cast",
    )(x, g.reshape(1, d))


def _matmul_kernel(a_ref, w_ref, b_ref, o_ref):
    acc = jnp.dot(a_ref[...], w_ref[...], preferred_element_type=F32)
    o_ref[...] = (acc + b_ref[...]).astype(o_ref.dtype)


def matmul_bias(a, w, b, out_dtype, tm=1024, tn=512, name="matmul_bias"):
    t, k = a.shape
    n = w.shape[1]
    tn = min(tn, n)
    return pl.pallas_call(
        _matmul_kernel,
        grid=(t // tm, n // tn),
        in_specs=[
            pl.BlockSpec((tm, k), lambda i, j: (i, 0)),
            pl.BlockSpec((k, tn), lambda i, j: (0, j)),
            pl.BlockSpec((1, tn), lambda i, j: (0, j)),
        ],
        out_specs=pl.BlockSpec((tm, tn), lambda i, j: (i, j)),
        out_shape=jax.ShapeDtypeStruct((t, n), out_dtype),
        compiler_params=_params("parallel", "arbitrary"),
        name=name,
    )(a, w, b.reshape(1, n))


def _mm_norm_res_kernel(a_ref, w_ref, b_ref, gpost_ref, gnext_ref, x_ref, xo_ref, ho_ref, y_scr, *, nj, tn, n):
    j = pl.program_id(1)
    y_scr[j] = jnp.dot(a_ref[...], w_ref[...], preferred_element_type=F32) + b_ref[...]

    @pl.when(j == nj - 1)
    def _():
        ss = jnp.sum(y_scr[0] * y_scr[0], axis=1, keepdims=True)
        for jj in range(1, nj):
            ss = ss + jnp.sum(y_scr[jj] * y_scr[jj], axis=1, keepdims=True)
        r = lax.rsqrt(ss / n + RMS_EPS)
        ss2 = jnp.zeros_like(ss)
        for jj in range(nj):
            cols = slice(jj * tn, (jj + 1) * tn)
            xn = x_ref[:, cols] + y_scr[jj] * r * gpost_ref[:, cols]
            xo_ref[:, cols] = xn
            ss2 = ss2 + jnp.sum(xn * xn, axis=1, keepdims=True)
        r2 = lax.rsqrt(ss2 / n + RMS_EPS)
        for jj in range(nj):
            cols = slice(jj * tn, (jj + 1) * tn)
            ho_ref[:, cols] = (xo_ref[:, cols] * r2 * gnext_ref[:, cols]).astype(ho_ref.dtype)


def matmul_norm_res(a, w, b, g_post, g_next, x, tm=512, tn=512, name="matmul_norm_res"):
    t, k = a.shape
    n = w.shape[1]
    nj = n // tn
    kern = functools.partial(_mm_norm_res_kernel, nj=nj, tn=tn, n=n)
    return pl.pallas_call(
        kern,
        grid=(t // tm, nj),
        in_specs=[
            pl.BlockSpec((tm, k), lambda i, j: (i, 0)),
            pl.BlockSpec((k, tn), lambda i, j: (0, j)),
            pl.BlockSpec((1, tn), lambda i, j: (0, j)),
            pl.BlockSpec((1, n), lambda i, j: (0, 0)),
            pl.BlockSpec((1, n), lambda i, j: (0, 0)),
            pl.BlockSpec((tm, n), lambda i, j: (i, 0)),
        ],
        out_specs=[
            pl.BlockSpec((tm, n), lambda i, j: (i, 0)),
            pl.BlockSpec((tm, n), lambda i, j: (i, 0)),
        ],
        out_shape=[jax.ShapeDtypeStruct((t, n), F32), jax.ShapeDtypeStruct((t, n), MXU_DTYPE)],
        scratch_shapes=[pltpu.VMEM((nj, tm, tn), F32)],
        compiler_params=_params("parallel", "arbitrary"),
        name=name,
    )(a, w, b.reshape(1, n), g_post.reshape(1, n), g_next.reshape(1, n), x)


FIX_ROWS = 16
CARRY_ROWS = 8


def _ffn_up_kernel(h_ref, wg_ref, wu_ref, cw_ref, cb_ref, o_ref, carry_scr, *, tm, tiles_per_seq):
    i = pl.program_id(0)
    j = pl.program_id(1)
    h = h_ref[...]
    gate = jnp.dot(h, wg_ref[...], preferred_element_type=F32)
    up = jnp.dot(h, wu_ref[...], preferred_element_type=F32)
    w0 = cw_ref[0:1, :]
    w1 = cw_ref[1:2, :]
    w2 = cw_ref[2:3, :]
    b = cb_ref[...]
    a = b + pltpu.roll(gate, 2, axis=0) * w0
    a = a + pltpu.roll(gate, 1, axis=0) * w1
    a = a + gate * w2
    o_ref[...] = (jax.nn.silu(a) * up).astype(o_ref.dtype)

    seq_start = (i % tiles_per_seq) == 0
    prev = jnp.where(seq_start, 0.0, carry_scr[j])
    head = gate[0:FIX_ROWS]
    ext = jnp.concatenate([prev, head], axis=0)
    af = b + ext[CARRY_ROWS - 2:CARRY_ROWS - 2 + FIX_ROWS] * w0
    af = af + ext[CARRY_ROWS - 1:CARRY_ROWS - 1 + FIX_ROWS] * w1
    af = af + head * w2
    o_ref[0:FIX_ROWS, :] = (jax.nn.silu(af) * up[0:FIX_ROWS]).astype(o_ref.dtype)
    carry_scr[j] = gate[tm - CARRY_ROWS:tm]


def ffn_up(h, wg, wu, conv_w, conv_b, seq, tm=1024, tn=512):
    t, k = h.shape
    n = wg.shape[1]
    nj = n // tn
    kern = functools.partial(_ffn_up_kernel, tm=tm, tiles_per_seq=seq // tm)
    return pl.pallas_call(
        kern,
        grid=(t // tm, nj),
        in_specs=[
            pl.BlockSpec((tm, k), lambda i, j: (i, 0)),
            pl.BlockSpec((k, tn), lambda i, j: (0, j)),
            pl.BlockSpec((k, tn), lambda i, j: (0, j)),
            pl.BlockSpec((CONV_WIDTH, tn), lambda i, j: (0, j)),
            pl.BlockSpec((1, tn), lambda i, j: (0, j)),
        ],
        out_specs=pl.BlockSpec((tm, tn), lambda i, j: (i, j)),
        out_shape=jax.ShapeDtypeStruct((t, n), MXU_DTYPE),
        scratch_shapes=[pltpu.VMEM((nj, CARRY_ROWS, tn), F32)],
        compiler_params=_params("arbitrary", "arbitrary"),
        name="ffn_up",
    )(h, wg, wu, conv_w, conv_b.reshape(1, n))


def _head_of_row_block(group, rb):
    return group * GROUP + 2 * (rb % PAIRS) + rb // PAIRS


def _build_q_stack(q_ref, qs_ref):
    lane = lax.broadcasted_iota(jnp.int32, (Q_BLOCK, LANES), 1)
    even = lane < HEAD_DIM
    for p in range(PAIRS):
        qp = q_ref[:, p * LANES:(p + 1) * LANES].astype(F32) * (HEAD_DIM ** -0.5)
        qs_ref[p * Q_BLOCK:(p + 1) * Q_BLOCK, :] = jnp.where(even, qp, 0.0).astype(qs_ref.dtype)
        qs_ref[(PAIRS + p) * Q_BLOCK:(PAIRS + p + 1) * Q_BLOCK, :] = jnp.where(even, 0.0, qp).astype(qs_ref.dtype)


def _rep(x, size):
    return x if size == LANES else jnp.concatenate([x] * (size // LANES), axis=1)


def _pairs(num, den):
    lane = lax.broadcasted_iota(jnp.int32, (Q_BLOCK, LANES), 1)
    even = lane < HEAD_DIM
    outs = []
    for p in range(PAIRS):
        oe = num(p) / jnp.maximum(den(p), 1e-30)
        oo = num(PAIRS + p) / jnp.maximum(den(PAIRS + p), 1e-30)
        outs.append(jnp.where(even, oe, oo))
    return outs


def _rows(rb):
    return slice(rb * Q_BLOCK, (rb + 1) * Q_BLOCK)


def _band_bias(nslope, d, window):
    return jnp.where((d >= 0) & (d < window), nslope * d.astype(F32), MASK_VALUE)


def _window_attend(qs_ref, k, v, bias, extra_logit=None):
    s_all = lax.dot_general(qs_ref[...], k, (((1,), (1,)), ((), ())), preferred_element_type=F32)
    ps, extras = [], []
    for rb in range(GROUP):
        s = s_all[_rows(rb)] + bias(rb)
        m = jnp.max(s, axis=1, keepdims=True)
        if extra_logit is not None:
            m = jnp.maximum(m, extra_logit(rb))
            extras.append(jnp.exp(extra_logit(rb) - m))
        ps.append(jnp.exp(s - m).astype(MXU_DTYPE))
    r = jnp.dot(jnp.concatenate(ps, axis=0), v, preferred_element_type=F32)
    num = lambda rb: r[_rows(rb), 0:LANES]
    if extra_logit is None:
        den = lambda rb: r[_rows(rb), LANES:2 * LANES]
    else:
        den = lambda rb: r[_rows(rb), LANES:2 * LANES] + extras[rb]
    return _pairs(num, den)


def _pos_tiles(t0, start, size):
    qi = lax.broadcasted_iota(jnp.int32, (Q_BLOCK, size), 0)
    ki = lax.broadcasted_iota(jnp.int32, (Q_BLOCK, size), 1)
    return (t0 - start) + (qi - ki)


def _swa_kernel(nslope_ref, sink_ref, q_ref, k_ref, v_ref, o_ref, qs_ref, bias_scr):
    grp = pl.program_id(1)
    c = pl.program_id(2)
    t0 = c * Q_BLOCK
    span = SWA_WINDOW + Q_BLOCK
    _build_q_stack(q_ref, qs_ref)
    sink = lambda rb: sink_ref[_head_of_row_block(grp, rb)]

    def write(outs):
        for p, o in enumerate(outs):
            o_ref[:, p * LANES:(p + 1) * LANES] = o.astype(o_ref.dtype)

    @pl.when(c == 0)
    def _():
        d = _pos_tiles(SWA_WINDOW, 0, span)
        for rb in range(GROUP):
            bias_scr[rb] = _band_bias(nslope_ref[_head_of_row_block(grp, rb)], d, SWA_WINDOW)
        write(_window_attend(qs_ref, k_ref[0, 0, 0:Q_BLOCK, :], v_ref[0, 0, 0:Q_BLOCK, :],
                             lambda rb: bias_scr[rb, :, SWA_WINDOW:span], sink))

    @pl.when(c > 0)
    def _():
        start = pl.multiple_of(t0 - SWA_WINDOW, Q_BLOCK)
        write(_window_attend(qs_ref, k_ref[0, 0, pl.ds(start, span), :], v_ref[0, 0, pl.ds(start, span), :],
                             lambda rb: bias_scr[rb], sink))


def swa_attention(qkv, k2, v2, nslopes, sinks, batch, seq):
    nblk = seq // Q_BLOCK
    gw = PAIRS * LANES
    span = SWA_WINDOW + Q_BLOCK
    return pl.pallas_call(
        _swa_kernel,
        grid=(batch, KV_GROUPS, nblk),
        in_specs=[
            pl.BlockSpec(memory_space=pltpu.SMEM),
            pl.BlockSpec(memory_space=pltpu.SMEM),
            pl.BlockSpec((Q_BLOCK, gw), lambda b, g, c: (b * nblk + c, g)),
            pl.BlockSpec((1, 1, seq, LANES), lambda b, g, c: (b, g, 0, 0)),
            pl.BlockSpec((1, 1, seq, 2 * LANES), lambda b, g, c: (b, g, 0, 0)),
        ],
        out_specs=pl.BlockSpec((Q_BLOCK, gw), lambda b, g, c: (b * nblk + c, g)),
        out_shape=jax.ShapeDtypeStruct((batch * seq, N_HEADS * HEAD_DIM), MXU_DTYPE),
        scratch_shapes=[
            pltpu.VMEM((ROWS, LANES), MXU_DTYPE),
            pltpu.VMEM((GROUP, Q_BLOCK, span), F32),
        ],
        compiler_params=_params("parallel", "parallel", "arbitrary"),
        name="swa_attention",
    )(nslopes, sinks, qkv, k2, v2)


def _compress_kernel(z_ref, pe_ref, w1_ref, b1_ref, w2_ref, b2_ref, o_ref, *, ncp):
    half = CMP_STRIDE * HEAD_DIM
    z = z_ref[0, 0, 0].astype(F32)
    top = (z + pe_ref[0, 0:1, :]).astype(MXU_DTYPE)
    bot = (z + pe_ref[0, 1:2, :]).astype(MXU_DTYPE)
    a = jnp.dot(top, w1_ref[0, 0:half, :], preferred_element_type=F32)
    bm = jnp.dot(bot, w1_ref[0, half:2 * half, :], preferred_element_type=F32)
    hid = a + pltpu.roll(bm, ncp - 1, axis=0) + b1_ref[0]
    act = jax.nn.gelu(hid).astype(MXU_DTYPE)
    o_ref[0, 0, 0] = jnp.dot(act, w2_ref[0], preferred_element_type=F32) + b2_ref[0]


def compress(z, pe, w1, b1, w2, b2):
    _, batch, groups, ncp, zw = z.shape
    hid = w1.shape[-1]
    kern = functools.partial(_compress_kernel, ncp=ncp)
    return pl.pallas_call(
        kern,
        grid=(2, batch, groups),
        in_specs=[
            pl.BlockSpec((1, 1, 1, ncp, zw), lambda s, b, g: (s, b, g, 0, 0)),
            pl.BlockSpec((1, 2, zw), lambda s, b, g: (s, 0, 0)),
            pl.BlockSpec((1, 2 * zw, hid), lambda s, b, g: (s, 0, 0)),
            pl.BlockSpec((1, 1, hid), lambda s, b, g: (s, 0, 0)),
            pl.BlockSpec((1, hid, HEAD_DIM), lambda s, b, g: (s, 0, 0)),
            pl.BlockSpec((1, 1, HEAD_DIM), lambda s, b, g: (s, 0, 0)),
        ],
        out_specs=pl.BlockSpec((1, 1, 1, ncp, HEAD_DIM), lambda s, b, g: (s, b, g, 0, 0)),
        out_shape=jax.ShapeDtypeStruct((2, batch, groups, ncp, HEAD_DIM), F32),
        compiler_params=_params("parallel", "parallel", "parallel"),
        name="nsa_compress",
    )(z, pe, w1, b1, w2, b2)


def _nsa_cmp_kernel(nslope_ref, q_ref, kc_ref, vc_ref, o_ref, sel_ref, qs_ref, e_scr, *, ncp, n_cmp, n_sel):
    grp = pl.program_id(1)
    c = pl.program_id(2)
    t0 = c * Q_BLOCK
    _build_q_stack(q_ref, qs_ref)
    qi = lax.broadcasted_iota(jnp.int32, (Q_BLOCK, ncp), 0)
    ni = lax.broadcasted_iota(jnp.int32, (Q_BLOCK, ncp), 1)
    d = (t0 + qi) - (ni * CMP_STRIDE + (CMP_LEN - 1))
    negb = jnp.where((d >= 0) & (ni < n_cmp), 0.0, MASK_VALUE)
    dist = d.astype(F32)
    s_all = lax.dot_general(qs_ref[...], kc_ref[0, 0], (((1,), (1,)), ((), ())), preferred_element_type=F32)
    for rb in range(GROUP):
        rows = slice(rb * Q_BLOCK, (rb + 1) * Q_BLOCK)
        s = s_all[rows] + nslope_ref[_head_of_row_block(grp, rb)] * dist + negb
        m = jnp.max(s, axis=1, keepdims=True)
        e_scr[rows, :] = jnp.exp(s - m).astype(e_scr.dtype)
    r = jnp.dot(e_scr[...], vc_ref[0, 0], preferred_element_type=F32)
    lane = lax.broadcasted_iota(jnp.int32, (Q_BLOCK, LANES), 1)
    even = lane < HEAD_DIM
    row_t = t0 + lax.broadcasted_iota(jnp.int32, (Q_BLOCK, LANES), 0)
    has_cmp = row_t >= (CMP_LEN - 1)
    imp = jnp.zeros((Q_BLOCK, LANES), F32)
    for p in range(PAIRS):
        re = slice(p * Q_BLOCK, (p + 1) * Q_BLOCK)
        ro = slice((PAIRS + p) * Q_BLOCK, (PAIRS + p + 1) * Q_BLOCK)
        de = jnp.maximum(r[re, LANES:2 * LANES], 1e-30)
        do = jnp.maximum(r[ro, LANES:2 * LANES], 1e-30)
        o = jnp.where(even, r[re, 0:LANES] / de, r[ro, 0:LANES] / do)
        o_ref[:, p * LANES:(p + 1) * LANES] = jnp.where(has_cmp, o, 0.0)
        imp = imp + r[re, 2 * LANES:3 * LANES] / de + r[ro, 2 * LANES:3 * LANES] / do
    imp = jnp.where(has_cmp, imp, 0.0)

    imp_t = imp.T
    ji = lax.broadcasted_iota(jnp.int32, (LANES, Q_BLOCK), 0)
    qt = lax.broadcasted_iota(jnp.int32, (LANES, Q_BLOCK), 1)
    cur = (t0 + qt) // SEL_LEN
    causal = ji <= cur
    forced = (ji == 0) | (ji == cur) | (ji == cur - 1)
    neg_inf = -jnp.inf
    score = jnp.where(forced, neg_inf, jnp.where(causal, imp_t, MASK_VALUE))
    score = jnp.where(ji < n_sel, score, neg_inf)
    picked = jnp.where(forced, 1.0, 0.0)
    for _ in range(SEL_TOPK - SEL_FORCED):
        mx = jnp.max(score, axis=0, keepdims=True)
        first = jnp.min(jnp.where(score == mx, ji, LANES), axis=0, keepdims=True)
        hit = ji == first
        picked = jnp.where(hit, 1.0, picked)
        score = jnp.where(hit, neg_inf, score)
    picked = jnp.where(ji < cur, picked, 0.0)
    sel_ref[0, 0] = picked.T.astype(sel_ref.dtype)


def nsa_cmp_select(q, kc2, vc_aug, nslopes, batch, seq, q_col_block):
    nblk = seq // Q_BLOCK
    ncp = seq // CMP_STRIDE
    n_cmp = (seq - CMP_LEN) // CMP_STRIDE + 1
    n_sel = seq // SEL_LEN
    gw = PAIRS * LANES
    kern = functools.partial(_nsa_cmp_kernel, ncp=ncp, n_cmp=n_cmp, n_sel=n_sel)
    return pl.pallas_call(
        kern,
        grid=(batch, KV_GROUPS, nblk),
        in_specs=[
            pl.BlockSpec(memory_space=pltpu.SMEM),
            pl.BlockSpec((Q_BLOCK, gw), lambda b, g, c: (b * nblk + c, q_col_block + g)),
            pl.BlockSpec((1, 1, ncp, LANES), lambda b, g, c: (b, g, 0, 0)),
            pl.BlockSpec((1, 1, ncp, 3 * LANES), lambda b, g, c: (b, g, 0, 0)),
        ],
        out_specs=[
            pl.BlockSpec((Q_BLOCK, gw), lambda b, g, c: (b * nblk + c, g)),
            pl.BlockSpec((1, 1, Q_BLOCK, LANES), lambda b, g, c: (b, g, c, 0)),
        ],
        out_shape=[
            jax.ShapeDtypeStruct((batch * seq, N_HEADS * HEAD_DIM), F32),
            jax.ShapeDtypeStruct((batch, KV_GROUPS, seq, LANES), MXU_DTYPE),
        ],
        scratch_shapes=[pltpu.VMEM((ROWS, LANES), MXU_DTYPE), pltpu.VMEM((ROWS, ncp), MXU_DTYPE)],
        compiler_params=_params("parallel", "parallel", "arbitrary"),
        name="nsa_cmp_select",
    )(nslopes, q, kc2, vc_aug)


N_FEAT = 6


def sel_query_features():
    s = -jnp.asarray(_alibi_neg_slopes())
    s1 = s.astype(MXU_DTYPE).astype(F32)
    s2 = (s - s1).astype(MXU_DTYPE).astype(F32)
    s3 = (s - s1 - s2).astype(MXU_DTYPE).astype(F32)
    feat = jnp.zeros((N_HEADS, LANES), F32).at[:, HEAD_DIM:HEAD_DIM + N_FEAT].set(
        jnp.stack([s1, s2, s3, s1, s2, s3], axis=1))
    feat = feat.at[:, HEAD_DIM + N_FEAT].set(MASK_VALUE)
    order = np.array([[_head_of_row_block(g, rb) for rb in range(GROUP)] for g in range(KV_GROUPS)])
    return feat[order]


def sel_key_features(seq):
    pos = np.arange(seq)
    kk = pos % SEL_STEP
    f = np.zeros((seq + SEL_STEP, LANES - HEAD_DIM + LANES), np.float32)
    f[:seq, 0:3] = (SEL_LEN * (kk // SEL_LEN))[:, None]
    f[:seq, 3:6] = (kk % SEL_LEN)[:, None]
    f[seq:, N_FEAT] = 1.0
    f[pos, (LANES - HEAD_DIM) + pos // SEL_LEN] = 1.0
    return f


def _nsa_sel_kernel(delta_ref, qfeat_ref, q_ref, sel_ref, k_ref, v_ref, o_ref,
                    qa_ref, m_ref, acc_ref, s_a, s_b, p_a, p_b, al_a, al_b):
    grp = pl.program_id(1)
    c = pl.program_id(2)
    t0 = c * Q_BLOCK
    lane = lax.broadcasted_iota(jnp.int32, (Q_BLOCK, LANES), 1)
    low = lane < HEAD_DIM
    selneg = ((1.0 - sel_ref[0, 0].astype(F32)) * MASK_VALUE).astype(qa_ref.dtype)
    for p in range(PAIRS):
        qp = q_ref[:, p * LANES:(p + 1) * LANES].astype(F32) * (HEAD_DIM ** -0.5)
        for rb, src in ((p, qp), (PAIRS + p, pltpu.roll(qp, HEAD_DIM, axis=1))):
            qa_ref[_rows(rb), 0:LANES] = jnp.where(low, src, qfeat_ref[0, rb:rb + 1, :]).astype(qa_ref.dtype)
            qa_ref[_rows(rb), LANES:2 * LANES] = selneg
    n_steps = t0 // SEL_STEP + 1
    n_pad_step = k_ref.shape[2] // SEL_STEP - 1
    dot_nt = (((1,), (1,)), ((), ()))

    own = pl.multiple_of(t0, Q_BLOCK)
    qi = lax.broadcasted_iota(jnp.int32, (Q_BLOCK, Q_BLOCK), 0)
    ki = lax.broadcasted_iota(jnp.int32, (Q_BLOCK, Q_BLOCK), 1)
    own_bias = jnp.where((ki <= qi) & (ki // SEL_LEN == qi // SEL_LEN), 0.0, MASK_VALUE)
    s_own = lax.dot_general(qa_ref[:, 0:LANES], k_ref[0, 0, pl.ds(own, Q_BLOCK), 0:LANES], dot_nt,
                            preferred_element_type=F32)
    back = jnp.full((Q_BLOCK, LANES), n_steps, jnp.int32).astype(F32)
    p_own = []
    for rb in range(GROUP):
        s = s_own[_rows(rb)] + own_bias
        m = jnp.max(s, axis=1, keepdims=True)
        p_own.append(jnp.exp(s - m).astype(MXU_DTYPE))
        m_ref[_rows(rb), :] = m + back * delta_ref[_head_of_row_block(grp, rb)]
    acc_ref[...] = jnp.dot(jnp.concatenate(p_own, axis=0), v_ref[0, 0, pl.ds(own, Q_BLOCK), :],
                           preferred_element_type=F32)
    p_b[...] = jnp.zeros((ROWS, SEL_STEP), p_b.dtype)
    al_b[...] = jnp.ones((ROWS, LANES), F32)

    def key_start(step):
        return pl.multiple_of(jnp.clip(step, 0, n_pad_step) * SEL_STEP, SEL_STEP)

    def scores(step, s_out):
        k = k_ref[0, 0, pl.ds(key_start(step), SEL_STEP), :]
        s_out[...] = lax.dot_general(qa_ref[...], k, dot_nt, preferred_element_type=F32)

    def softmax(s_in, p_out, al_out):
        for rb in range(GROUP):
            m_prev = m_ref[_rows(rb), :] - delta_ref[_head_of_row_block(grp, rb)]
            m_new = jnp.maximum(m_prev, jnp.max(s_in[_rows(rb), :], axis=1, keepdims=True))
            al_out[_rows(rb), :] = jnp.exp(m_prev - m_new)
            m_ref[_rows(rb), :] = m_new
        for rb in range(GROUP):
            p_out[_rows(rb), :] = jnp.exp(s_in[_rows(rb), :] - _rep(m_ref[_rows(rb), :], SEL_STEP)).astype(p_out.dtype)

    def values(step, p_in, al_in):
        v = v_ref[0, 0, pl.ds(key_start(step), SEL_STEP), :]
        pv = jnp.dot(p_in[...], v, preferred_element_type=F32)
        alpha = al_in[...]
        acc_ref[...] = acc_ref[...] * jnp.concatenate([alpha, alpha], axis=1) + pv

    scores(0, s_a)

    def pair(j, carry):
        t = 2 * j + 1
        scores(t, s_b)
        softmax(s_a, p_a, al_a)
        values(t - 2, p_b, al_b)
        scores(t + 1, s_a)
        softmax(s_b, p_b, al_b)
        values(t - 1, p_a, al_a)
        return carry

    lax.fori_loop(0, (n_steps + 2) // 2, pair, 0)
    outs = _pairs(lambda rb: acc_ref[_rows(rb), 0:LANES], lambda rb: acc_ref[_rows(rb), LANES:2 * LANES])
    for p, o in enumerate(outs):
        o_ref[:, p * LANES:(p + 1) * LANES] = o


def nsa_sel_attention(q, sel, k_aug, v2, batch, seq, q_col_block):
    nblk = seq // Q_BLOCK
    gw = PAIRS * LANES
    deltas = jnp.asarray((-_alibi_neg_slopes() * SEL_STEP).astype(np.float32))
    return pl.pallas_call(
        _nsa_sel_kernel,
        grid=(batch, KV_GROUPS, nblk),
        in_specs=[
            pl.BlockSpec(memory_space=pltpu.SMEM),
            pl.BlockSpec((1, GROUP, LANES), lambda b, g, c: (g, 0, 0)),
            pl.BlockSpec((Q_BLOCK, gw), lambda b, g, c: (b * nblk + c, q_col_block + g)),
            pl.BlockSpec((1, 1, Q_BLOCK, LANES), lambda b, g, c: (b, g, c, 0)),
            pl.BlockSpec((1, 1, seq + SEL_STEP, 2 * LANES), lambda b, g, c: (b, g, 0, 0)),
            pl.BlockSpec((1, 1, seq + SEL_STEP, 2 * LANES), lambda b, g, c: (b, g, 0, 0)),
        ],
        out_specs=pl.BlockSpec((Q_BLOCK, gw), lambda b, g, c: (b * nblk + c, g)),
        out_shape=jax.ShapeDtypeStruct((batch * seq, N_HEADS * HEAD_DIM), F32),
        scratch_shapes=[
            pltpu.VMEM((ROWS, 2 * LANES), MXU_DTYPE),
            pltpu.VMEM((ROWS, LANES), F32),
            pltpu.VMEM((ROWS, 2 * LANES), F32),
            pltpu.VMEM((ROWS, SEL_STEP), F32),
            pltpu.VMEM((ROWS, SEL_STEP), F32),
            pltpu.VMEM((ROWS, SEL_STEP), MXU_DTYPE),
            pltpu.VMEM((ROWS, SEL_STEP), MXU_DTYPE),
            pltpu.VMEM((ROWS, LANES), F32),
            pltpu.VMEM((ROWS, LANES), F32),
        ],
        compiler_params=_params("parallel", "parallel", "arbitrary"),
        name="nsa_sel_attention",
    )(deltas, sel_query_features(), q, sel, k_aug, v2)


def _nsa_win_kernel(nslope_ref, q_ref, k_ref, v_ref, o_ref, qs_ref, bias_scr):
    grp = pl.program_id(1)
    c = pl.program_id(2)
    t0 = c * Q_BLOCK
    span = NSA_WINDOW + Q_BLOCK
    lead = NSA_WINDOW // Q_BLOCK
    _build_q_stack(q_ref, qs_ref)
    nslope = lambda rb: nslope_ref[_head_of_row_block(grp, rb)]

    def write(outs):
        for p, o in enumerate(outs):
            o_ref[:, p * LANES:(p + 1) * LANES] = o

    @pl.when(c == 0)
    def _():
        d = _pos_tiles(NSA_WINDOW, 0, span)
        for rb in range(GROUP):
            bias_scr[rb] = _band_bias(nslope(rb), d, NSA_WINDOW)

    @pl.when(c < lead)
    def _():
        d = _pos_tiles(t0, 0, span)
        write(_window_attend(qs_ref, k_ref[0, 0, 0:span, :], v_ref[0, 0, 0:span, :],
                             lambda rb: _band_bias(nslope(rb), d, NSA_WINDOW)))

    @pl.when(c >= lead)
    def _():
        start = pl.multiple_of(t0 - NSA_WINDOW, Q_BLOCK)
        write(_window_attend(qs_ref, k_ref[0, 0, pl.ds(start, span), :], v_ref[0, 0, pl.ds(start, span), :],
                             lambda rb: bias_scr[rb]))


def nsa_win_attention(q, k2, v2, nslopes, batch, seq, q_col_block):
    nblk = seq // Q_BLOCK
    gw = PAIRS * LANES
    span = NSA_WINDOW + Q_BLOCK
    return pl.pallas_call(
        _nsa_win_kernel,
        grid=(batch, KV_GROUPS, nblk),
        in_specs=[
            pl.BlockSpec(memory_space=pltpu.SMEM),
            pl.BlockSpec((Q_BLOCK, gw), lambda b, g, c: (b * nblk + c, q_col_block + g)),
            pl.BlockSpec((1, 1, seq, LANES), lambda b, g, c: (b, g, 0, 0)),
            pl.BlockSpec((1, 1, seq, 2 * LANES), lambda b, g, c: (b, g, 0, 0)),
        ],
        out_specs=pl.BlockSpec((Q_BLOCK, gw), lambda b, g, c: (b * nblk + c, g)),
        out_shape=jax.ShapeDtypeStruct((batch * seq, N_HEADS * HEAD_DIM), F32),
        scratch_shapes=[
            pltpu.VMEM((ROWS, LANES), MXU_DTYPE),
            pltpu.VMEM((GROUP, Q_BLOCK, span), F32),
        ],
        compiler_params=_params("parallel", "parallel", "arbitrary"),
        name="nsa_win_attention",
    )(nslopes, q, k2, v2)


def _nsa_combine_kernel(gate_ref, ex_ref, oc_ref, os_ref, ow_ref, o_ref):
    hd = o_ref.shape[1]
    sig = jax.nn.sigmoid(gate_ref[...])
    hi = sig.astype(MXU_DTYPE)
    lo = (sig - hi.astype(F32)).astype(MXU_DTYPE)
    ex = ex_ref[...]
    g = jnp.dot(hi, ex, preferred_element_type=F32) + jnp.dot(lo, ex, preferred_element_type=F32)
    out = g[:, 0:hd] * oc_ref[...] + g[:, hd:2 * hd] * os_ref[...] + g[:, 2 * hd:3 * hd] * ow_ref[...]
    o_ref[...] = out.astype(o_ref.dtype)


def _gate_expansion():
    hd = N_HEADS * HEAD_DIM
    ex = np.zeros((LANES, 3 * hd), np.float32)
    for h in range(N_HEADS):
        for i in range(3):
            ex[3 * h + i, i * hd + h * HEAD_DIM:i * hd + (h + 1) * HEAD_DIM] = 1.0
    return ex


def nsa_combine(gate, oc, osel, ow, tm=256):
    t, hd = oc.shape
    ex = jnp.asarray(_gate_expansion(), MXU_DTYPE)
    row = lambda i: (i, 0)
    return pl.pallas_call(
        _nsa_combine_kernel,
        grid=(t // tm,),
        in_specs=[
            pl.BlockSpec((tm, LANES), row),
            pl.BlockSpec((LANES, 3 * hd), lambda i: (0, 0)),
            pl.BlockSpec((tm, hd), row),
            pl.BlockSpec((tm, hd), row),
            pl.BlockSpec((tm, hd), row),
        ],
        out_specs=pl.BlockSpec((tm, hd), row),
        out_shape=jax.ShapeDtypeStruct((t, hd), MXU_DTYPE),
        compiler_params=_params("parallel"),
        name="nsa_combine",
    )(gate, ex, oc, osel, ow)


def _dup_lanes(t, batch, seq):
    t = t.reshape(batch, seq, KV_GROUPS, HEAD_DIM).transpose(0, 2, 1, 3)
    return jnp.concatenate([t, t], axis=-1)


def _dup_lanes_ones(t, batch, seq):
    d = _dup_lanes(t, batch, seq)
    return jnp.concatenate([d, jnp.ones_like(d)], axis=-1)


def _overlap_matrix(seq):
    ncp = seq // CMP_STRIDE
    n_cmp = (seq - CMP_LEN) // CMP_STRIDE + 1
    cs = np.arange(n_cmp) * CMP_STRIDE
    ss = np.arange(seq // SEL_LEN) * SEL_LEN
    ov = (cs[:, None] < ss[None, :] + SEL_LEN) & (cs[:, None] + CMP_LEN > ss[None, :])
    out = np.zeros((ncp, LANES), np.float32)
    out[:n_cmp, :seq // SEL_LEN] = ov
    return out


def _swa_layer(h, x, w_in, b_in, sinks, w_o, b_o, g_post, g_next, nslopes, batch, seq):
    hd = N_HEADS * HEAD_DIM
    kd = KV_GROUPS * HEAD_DIM
    qkv = matmul_bias(h, w_in.astype(MXU_DTYPE), b_in, MXU_DTYPE, name="swa_in_proj")
    k2 = _dup_lanes(qkv[:, hd:hd + kd], batch, seq)
    v2 = _dup_lanes_ones(qkv[:, hd + kd:hd + 2 * kd], batch, seq)
    o = swa_attention(qkv, k2, v2, nslopes, sinks.astype(F32), batch, seq)
    return matmul_norm_res(o, w_o.astype(MXU_DTYPE), b_o, g_post, g_next, x, name="swa_out_proj")


def _nsa_layer(h, x, w_in, cmp_pe, cmp_w1, cmp_b1, cmp_w2, cmp_b2, w_o, g_post, g_next, nslopes, batch, seq):
    hd = N_HEADS * HEAD_DIM
    kd = KV_GROUPS * HEAD_DIM
    t = batch * seq
    ncp = seq // CMP_STRIDE
    qkv = matmul_bias(h, w_in[:, :hd + 6 * kd].astype(MXU_DTYPE), jnp.zeros((hd + 6 * kd,), F32), MXU_DTYPE,
                      name="nsa_in_proj")
    n_gate = 3 * N_HEADS
    w_gate = jnp.pad(w_in[:, hd + 6 * kd:], ((0, 0), (0, LANES - n_gate))).astype(MXU_DTYPE)
    gate = matmul_bias(h, w_gate, jnp.zeros((LANES,), F32), F32, name="nsa_gate_proj")

    def kv(i):
        return qkv[:, hd + i * kd:hd + (i + 1) * kd]

    def slabs(a):
        return a.reshape(batch, seq, KV_GROUPS, HEAD_DIM).transpose(0, 2, 1, 3).reshape(
            batch, KV_GROUPS, ncp, CMP_STRIDE * HEAD_DIM)

    z = jnp.stack([slabs(kv(0)), slabs(kv(1))])
    half = CMP_STRIDE * HEAD_DIM
    cmp_out = compress(z, cmp_pe.reshape(2, 2, half).astype(F32), cmp_w1.astype(MXU_DTYPE),
                       cmp_b1.reshape(2, 1, -1), cmp_w2.astype(MXU_DTYPE), cmp_b2.reshape(2, 1, -1))
    kcm = cmp_out[0].astype(MXU_DTYPE)
    vcm = cmp_out[1].astype(MXU_DTYPE)
    kc2 = jnp.concatenate([kcm, kcm], axis=-1)
    ov = jnp.broadcast_to(jnp.asarray(_overlap_matrix(seq), MXU_DTYPE), (batch, KV_GROUPS, ncp, LANES))
    vc_aug = jnp.concatenate([vcm, vcm, jnp.ones((batch, KV_GROUPS, ncp, LANES), MXU_DTYPE), ov], axis=-1)

    o_cmp, sel = nsa_cmp_select(qkv, kc2, vc_aug, nslopes, batch, seq, 0)
    pad_rows = ((0, 0), (0, 0), (0, SEL_STEP), (0, 0))
    k_sel = jnp.pad(kv(2).reshape(batch, seq, KV_GROUPS, HEAD_DIM).transpose(0, 2, 1, 3), pad_rows)
    kfeat = jnp.asarray(sel_key_features(seq), MXU_DTYPE)
    k_aug = jnp.concatenate([k_sel, jnp.broadcast_to(kfeat, (batch, KV_GROUPS) + kfeat.shape)], axis=-1)
    v_sel = jnp.pad(_dup_lanes_ones(kv(3), batch, seq), pad_rows)
    o_sel = nsa_sel_attention(qkv, sel, k_aug, v_sel, batch, seq, 0)
    o_win = nsa_win_attention(qkv, _dup_lanes(kv(4), batch, seq), _dup_lanes_ones(kv(5), batch, seq),
                              nslopes, batch, seq, 0)
    o = nsa_combine(gate, o_cmp, o_sel, o_win)
    return matmul_norm_res(o, w_o.astype(MXU_DTYPE), jnp.zeros((w_o.shape[1],), F32), g_post, g_next, x,
                           name="nsa_out_proj")


def kernel(x, norm_g, swa_w_in, swa_b_in, swa_sinks, swa_w_o, swa_b_o, nsa_w_in, nsa_cmp_pe, nsa_cmp_w1, nsa_cmp_b1, nsa_cmp_w2, nsa_cmp_b2, nsa_w_o, ffn_w_gate, ffn_w_up, ffn_conv_w, ffn_conv_b, ffn_w_down):
    batch, seq, d = x.shape
    depth = norm_g.shape[0]
    nslopes = jnp.asarray(_alibi_neg_slopes())
    xf = x.reshape(batch * seq, d)
    h = rms_cast(xf, norm_g[0, 0])
    for i in range(depth):
        g = norm_g[i]
        j = i // 2
        if i % 2 == 0:
            xf, h = _swa_layer(h, xf, swa_w_in[j], swa_b_in[j], swa_sinks[j], swa_w_o[j], swa_b_o[j],
                               g[1], g[2], nslopes, batch, seq)
        else:
            xf, h = _nsa_layer(h, xf, nsa_w_in[j], nsa_cmp_pe[j], nsa_cmp_w1[j], nsa_cmp_b1[j], nsa_cmp_w2[j],
                               nsa_cmp_b2[j], nsa_w_o[j], g[1], g[2], nslopes, batch, seq)
        act = ffn_up(h, ffn_w_gate[i].astype(MXU_DTYPE), ffn_w_up[i].astype(MXU_DTYPE), ffn_conv_w[i],
                     ffn_conv_b[i], seq)
        g_next = norm_g[i + 1, 0] if i + 1 < depth else jnp.ones((d,), F32)
        xf, h = matmul_norm_res(act, ffn_w_down[i].astype(MXU_DTYPE), jnp.zeros((d,), F32), g[3], g_next, xf,
                                tn=256, name="ffn_down")
    return xf.reshape(batch, seq, d)
```

```python
import functools

import numpy as np
import jax
import jax.numpy as jnp
from jax import lax
from jax.experimental import pallas as pl
from jax.experimental.pallas import tpu as pltpu

F32 = jnp.float32
MXU_DTYPE = jnp.bfloat16

N_HEADS = 32
HEAD_DIM = 64
KV_GROUPS = 4
GROUP = N_HEADS // KV_GROUPS
PAIRS = GROUP // 2
LANES = 128
Q_BLOCK = 128
ROWS = GROUP * Q_BLOCK
SWA_WINDOW = 128
CMP_LEN = 32
CMP_STRIDE = 16
SEL_LEN = 64
SEL_TOPK = 16
SEL_FORCED = 3
SEL_STEP = 512
NSA_WINDOW = 512
CONV_WIDTH = 3
RMS_EPS = 1e-6
MASK_VALUE = -1e30
VMEM_LIMIT = 56 * 1024 * 1024


def _params(*sem):
    return pltpu.CompilerParams(dimension_semantics=sem, vmem_limit_bytes=VMEM_LIMIT)


def _alibi_neg_slopes():
    return (-np.exp2(-8.0 * np.arange(1, N_HEADS + 1, dtype=np.float64) / N_HEADS)).astype(np.float32)


def _rms_cast_kernel(x_ref, g_ref, o_ref):
    x = x_ref[...]
    ms = jnp.mean(x * x, axis=-1, keepdims=True)
    o_ref[...] = (x * lax.rsqrt(ms + RMS_EPS) * g_ref[...]).astype(o_ref.dtype)


def rms_cast(x, g, tm=512):
    t, d = x.shape
    return pl.pallas_call(
        _rms_cast_kernel,
        grid=(t // tm,),
        in_specs=[pl.BlockSpec((tm, d), lambda i: (i, 0)), pl.BlockSpec((1, d), lambda i: (0, 0))],
        out_specs=pl.BlockSpec((tm, d), lambda i: (i, 0)),
        out_shape=jax.ShapeDtypeStruct((t, d), MXU_DTYPE),
        compiler_params=_params("parallel"),
        name="rms_cast",
    )(x, g.reshape(1, d))


def _matmul_kernel(a_ref, w_ref, b_ref, o_ref):
    acc = jnp.dot(a_ref[...], w_ref[...], preferred_element_type=F32)
    o_ref[...] = (acc + b_ref[...]).astype(o_ref.dtype)


def matmul_bias(a, w, b, out_dtype, tm=1024, tn=512, name="matmul_bias"):
    t, k = a.shape
    n = w.shape[1]
    tn = min(tn, n)
    return pl.pallas_call(
        _matmul_kernel,
        grid=(t // tm, n // tn),
        in_specs=[
            pl.BlockSpec((tm, k), lambda i, j: (i, 0)),
            pl.BlockSpec((k, tn), lambda i, j: (0, j)),
            pl.BlockSpec((1, tn), lambda i, j: (0, j)),
        ],
        out_specs=pl.BlockSpec((tm, tn), lambda i, j: (i, j)),
        out_shape=jax.ShapeDtypeStruct((t, n), out_dtype),
        compiler_params=_params("parallel", "arbitrary"),
        name=name,
    )(a, w, b.reshape(1, n))


def _mm_norm_res_kernel(a_ref, w_ref, b_ref, gpost_ref, gnext_ref, x_ref, xo_ref, ho_ref, y_scr, *, nj, tn, n):
    j = pl.program_id(1)
    y_scr[j] = jnp.dot(a_ref[...], w_ref[...], preferred_element_type=F32) + b_ref[...]

    @pl.when(j == nj - 1)
    def _():
        ss = jnp.sum(y_scr[0] * y_scr[0], axis=1, keepdims=True)
        for jj in range(1, nj):
            ss = ss + jnp.sum(y_scr[jj] * y_scr[jj], axis=1, keepdims=True)
        r = lax.rsqrt(ss / n + RMS_EPS)
        ss2 = jnp.zeros_like(ss)
        for jj in range(nj):
            cols = slice(jj * tn, (jj + 1) * tn)
            xn = x_ref[:, cols] + y_scr[jj] * r * gpost_ref[:, cols]
            xo_ref[:, cols] = xn
            ss2 = ss2 + jnp.sum(xn * xn, axis=1, keepdims=True)
        r2 = lax.rsqrt(ss2 / n + RMS_EPS)
        for jj in range(nj):
            cols = slice(jj * tn, (jj + 1) * tn)
            ho_ref[:, cols] = (xo_ref[:, cols] * r2 * gnext_ref[:, cols]).astype(ho_ref.dtype)


def matmul_norm_res(a, w, b, g_post, g_next, x, tm=512, tn=512, name="matmul_norm_res"):
    t, k = a.shape
    n = w.shape[1]
    nj = n // tn
    kern = functools.partial(_mm_norm_res_kernel, nj=nj, tn=tn, n=n)
    return pl.pallas_call(
        kern,
        grid=(t // tm, nj),
        in_specs=[
            pl.BlockSpec((tm, k), lambda i, j: (i, 0)),
            pl.BlockSpec((k, tn), lambda i, j: (0, j)),
            pl.BlockSpec((1, tn), lambda i, j: (0, j)),
            pl.BlockSpec((1, n), lambda i, j: (0, 0)),
            pl.BlockSpec((1, n), lambda i, j: (0, 0)),
            pl.BlockSpec((tm, n), lambda i, j: (i, 0)),
        ],
        out_specs=[
            pl.BlockSpec((tm, n), lambda i, j: (i, 0)),
            pl.BlockSpec((tm, n), lambda i, j: (i, 0)),
        ],
        out_shape=[jax.ShapeDtypeStruct((t, n), F32), jax.ShapeDtypeStruct((t, n), MXU_DTYPE)],
        scratch_shapes=[pltpu.VMEM((nj, tm, tn), F32)],
        compiler_params=_params("parallel", "arbitrary"),
        name=name,
    )(a, w, b.reshape(1, n), g_post.reshape(1, n), g_next.reshape(1, n), x)


FIX_ROWS = 16
CARRY_ROWS = 8


def _ffn_up_kernel(h_ref, wg_ref, wu_ref, cw_ref, cb_ref, o_ref, wg_scr, wu_scr, carry_scr, *, tm, tiles_per_seq):
    i = pl.program_id(1)

    @pl.when(i == 0)
    def _():
        wg_scr[...] = wg_ref[...].astype(wg_scr.dtype)
        wu_scr[...] = wu_ref[...].astype(wu_scr.dtype)

    h = h_ref[...]
    gate = jnp.dot(h, wg_scr[...], preferred_element_type=F32)
    up = jnp.dot(h, wu_scr[...], preferred_element_type=F32)
    w0 = cw_ref[0:1, :]
    w1 = cw_ref[1:2, :]
    w2 = cw_ref[2:3, :]
    b = cb_ref[...]
    a = b + pltpu.roll(gate, 2, axis=0) * w0
    a = a + pltpu.roll(gate, 1, axis=0) * w1
    a = a + gate * w2
    o_ref[...] = (jax.nn.silu(a) * up).astype(o_ref.dtype)

    seq_start = (i % tiles_per_seq) == 0
    prev = jnp.where(seq_start, 0.0, carry_scr[...])
    head = gate[0:FIX_ROWS]
    ext = jnp.concatenate([prev, head], axis=0)
    af = b + ext[CARRY_ROWS - 2:CARRY_ROWS - 2 + FIX_ROWS] * w0
    af = af + ext[CARRY_ROWS - 1:CARRY_ROWS - 1 + FIX_ROWS] * w1
    af = af + head * w2
    o_ref[0:FIX_ROWS, :] = (jax.nn.silu(af) * up[0:FIX_ROWS]).astype(o_ref.dtype)
    carry_scr[...] = gate[tm - CARRY_ROWS:tm]


def ffn_up(h, wg, wu, layer, conv_w, conv_b, seq, tm=1024, tn=512):
    t, k = h.shape
    n = wg.shape[2]
    kern = functools.partial(_ffn_up_kernel, tm=tm, tiles_per_seq=seq // tm)
    return pl.pallas_call(
        kern,
        grid=(n // tn, t // tm),
        in_specs=[
            pl.BlockSpec((tm, k), lambda j, i: (i, 0)),
            pl.BlockSpec((None, k, tn), lambda j, i: (layer, 0, j)),
            pl.BlockSpec((None, k, tn), lambda j, i: (layer, 0, j)),
            pl.BlockSpec((CONV_WIDTH, tn), lambda j, i: (0, j)),
            pl.BlockSpec((1, tn), lambda j, i: (0, j)),
        ],
        out_specs=pl.BlockSpec((tm, tn), lambda j, i: (i, j)),
        out_shape=jax.ShapeDtypeStruct((t, n), MXU_DTYPE),
        scratch_shapes=[
            pltpu.VMEM((k, tn), MXU_DTYPE),
            pltpu.VMEM((k, tn), MXU_DTYPE),
            pltpu.VMEM((CARRY_ROWS, tn), F32),
        ],
        compiler_params=_params("arbitrary", "arbitrary"),
        name="ffn_up",
    )(h, wg, wu, conv_w, conv_b.reshape(1, n))


def _head_of_row_block(group, rb):
    return group * GROUP + 2 * (rb % PAIRS) + rb // PAIRS


def _build_q_stack(q_ref, qs_ref):
    lane = lax.broadcasted_iota(jnp.int32, (Q_BLOCK, LANES), 1)
    even = lane < HEAD_DIM
    for p in range(PAIRS):
        qp = q_ref[:, p * LANES:(p + 1) * LANES].astype(F32) * (HEAD_DIM ** -0.5)
        qs_ref[p * Q_BLOCK:(p + 1) * Q_BLOCK, :] = jnp.where(even, qp, 0.0).astype(qs_ref.dtype)
        qs_ref[(PAIRS + p) * Q_BLOCK:(PAIRS + p + 1) * Q_BLOCK, :] = jnp.where(even, 0.0, qp).astype(qs_ref.dtype)


def _rep(x, size):
    return x if size == LANES else jnp.concatenate([x] * (size // LANES), axis=1)


def _pairs(num, den):
    lane = lax.broadcasted_iota(jnp.int32, (Q_BLOCK, LANES), 1)
    even = lane < HEAD_DIM
    outs = []
    for p in range(PAIRS):
        oe = num(p) / jnp.maximum(den(p), 1e-30)
        oo = num(PAIRS + p) / jnp.maximum(den(PAIRS + p), 1e-30)
        outs.append(jnp.where(even, oe, oo))
    return outs


def _rows(rb):
    return slice(rb * Q_BLOCK, (rb + 1) * Q_BLOCK)


def _band_bias(nslope, d, window):
    return jnp.where((d >= 0) & (d < window), nslope * d.astype(F32), MASK_VALUE)


def _window_attend(qs_ref, k, v, bias, extra_logit=None):
    s_all = lax.dot_general(qs_ref[...], k, (((1,), (1,)), ((), ())), preferred_element_type=F32)
    ps, extras = [], []
    for rb in range(GROUP):
        s = s_all[_rows(rb)] + bias(rb)
        m = jnp.max(s, axis=1, keepdims=True)
        if extra_logit is not None:
            m = jnp.maximum(m, extra_logit(rb))
            extras.append(jnp.exp(extra_logit(rb) - m))
        ps.append(jnp.exp(s - m).astype(MXU_DTYPE))
    r = jnp.dot(jnp.concatenate(ps, axis=0), v, preferred_element_type=F32)
    num = lambda rb: r[_rows(rb), 0:LANES]
    if extra_logit is None:
        den = lambda rb: r[_rows(rb), LANES:2 * LANES]
    else:
        den = lambda rb: r[_rows(rb), LANES:2 * LANES] + extras[rb]
    return _pairs(num, den)


def _pos_tiles(t0, start, size):
    qi = lax.broadcasted_iota(jnp.int32, (Q_BLOCK, size), 0)
    ki = lax.broadcasted_iota(jnp.int32, (Q_BLOCK, size), 1)
    return (t0 - start) + (qi - ki)


def _swa_kernel(nslope_ref, sink_ref, q_ref, k_ref, v_ref, o_ref, qs_ref, bias_scr):
    grp = pl.program_id(1)
    c = pl.program_id(2)
    t0 = c * Q_BLOCK
    span = SWA_WINDOW + Q_BLOCK
    _build_q_stack(q_ref, qs_ref)
    sink = lambda rb: sink_ref[_head_of_row_block(grp, rb)]

    def write(outs):
        for p, o in enumerate(outs):
            o_ref[:, p * LANES:(p + 1) * LANES] = o.astype(o_ref.dtype)

    @pl.when(c == 0)
    def _():
        d = _pos_tiles(SWA_WINDOW, 0, span)
        for rb in range(GROUP):
            bias_scr[rb] = _band_bias(nslope_ref[_head_of_row_block(grp, rb)], d, SWA_WINDOW)
        write(_window_attend(qs_ref, k_ref[0, 0, 0:Q_BLOCK, :], v_ref[0, 0, 0:Q_BLOCK, :],
                             lambda rb: bias_scr[rb, :, SWA_WINDOW:span], sink))

    @pl.when(c > 0)
    def _():
        start = pl.multiple_of(t0 - SWA_WINDOW, Q_BLOCK)
        write(_window_attend(qs_ref, k_ref[0, 0, pl.ds(start, span), :], v_ref[0, 0, pl.ds(start, span), :],
                             lambda rb: bias_scr[rb], sink))


def swa_attention(qkv, k2, v2, nslopes, sinks, batch, seq):
    nblk = seq // Q_BLOCK
    gw = PAIRS * LANES
    span = SWA_WINDOW + Q_BLOCK
    return pl.pallas_call(
        _swa_kernel,
        grid=(batch, KV_GROUPS, nblk),
        in_specs=[
            pl.BlockSpec(memory_space=pltpu.SMEM),
            pl.BlockSpec(memory_space=pltpu.SMEM),
            pl.BlockSpec((Q_BLOCK, gw), lambda b, g, c: (b * nblk + c, g)),
            pl.BlockSpec((1, 1, seq, LANES), lambda b, g, c: (b, g, 0, 0)),
            pl.BlockSpec((1, 1, seq, 2 * LANES), lambda b, g, c: (b, g, 0, 0)),
        ],
        out_specs=pl.BlockSpec((Q_BLOCK, gw), lambda b, g, c: (b * nblk + c, g)),
        out_shape=jax.ShapeDtypeStruct((batch * seq, N_HEADS * HEAD_DIM), MXU_DTYPE),
        scratch_shapes=[
            pltpu.VMEM((ROWS, LANES), MXU_DTYPE),
            pltpu.VMEM((GROUP, Q_BLOCK, span), F32),
        ],
        compiler_params=_params("parallel", "parallel", "arbitrary"),
        name="swa_attention",
    )(nslopes, sinks, qkv, k2, v2)


def _compress_kernel(z_ref, pe_ref, w1_ref, b1_ref, w2_ref, b2_ref, o_ref, *, ncp):
    half = CMP_STRIDE * HEAD_DIM
    z = z_ref[0, 0, 0].astype(F32)
    top = (z + pe_ref[0, 0:1, :]).astype(MXU_DTYPE)
    bot = (z + pe_ref[0, 1:2, :]).astype(MXU_DTYPE)
    a = jnp.dot(top, w1_ref[0, 0:half, :], preferred_element_type=F32)
    bm = jnp.dot(bot, w1_ref[0, half:2 * half, :], preferred_element_type=F32)
    hid = a + pltpu.roll(bm, ncp - 1, axis=0) + b1_ref[0]
    act = jax.nn.gelu(hid).astype(MXU_DTYPE)
    o_ref[0, 0, 0] = jnp.dot(act, w2_ref[0], preferred_element_type=F32) + b2_ref[0]


def compress(z, pe, w1, b1, w2, b2):
    _, batch, groups, ncp, zw = z.shape
    hid = w1.shape[-1]
    kern = functools.partial(_compress_kernel, ncp=ncp)
    return pl.pallas_call(
        kern,
        grid=(2, batch, groups),
        in_specs=[
            pl.BlockSpec((1, 1, 1, ncp, zw), lambda s, b, g: (s, b, g, 0, 0)),
            pl.BlockSpec((1, 2, zw), lambda s, b, g: (s, 0, 0)),
            pl.BlockSpec((1, 2 * zw, hid), lambda s, b, g: (s, 0, 0)),
            pl.BlockSpec((1, 1, hid), lambda s, b, g: (s, 0, 0)),
            pl.BlockSpec((1, hid, HEAD_DIM), lambda s, b, g: (s, 0, 0)),
            pl.BlockSpec((1, 1, HEAD_DIM), lambda s, b, g: (s, 0, 0)),
        ],
        out_specs=pl.BlockSpec((1, 1, 1, ncp, HEAD_DIM), lambda s, b, g: (s, b, g, 0, 0)),
        out_shape=jax.ShapeDtypeStruct((2, batch, groups, ncp, HEAD_DIM), F32),
        compiler_params=_params("parallel", "parallel", "parallel"),
        name="nsa_compress",
    )(z, pe, w1, b1, w2, b2)


def _nsa_cmp_kernel(nslope_ref, q_ref, kc_ref, vc_ref, o_ref, sel_ref, qs_ref, e_scr, r_scr, *, ncp, n_cmp, n_sel):
    grp = pl.program_id(1)
    c = pl.program_id(2)
    t0 = c * Q_BLOCK
    _build_q_stack(q_ref, qs_ref)

    def attend(width):
        qi = lax.broadcasted_iota(jnp.int32, (Q_BLOCK, width), 0)
        ni = lax.broadcasted_iota(jnp.int32, (Q_BLOCK, width), 1)
        d = (t0 + qi) - (ni * CMP_STRIDE + (CMP_LEN - 1))
        negb = jnp.where((d >= 0) & (ni < n_cmp), 0.0, MASK_VALUE)
        dist = d.astype(F32)
        s_all = lax.dot_general(qs_ref[...], kc_ref[0, 0, 0:width, :], (((1,), (1,)), ((), ())),
                                preferred_element_type=F32)
        for rb in range(GROUP):
            s = s_all[_rows(rb)] + nslope_ref[_head_of_row_block(grp, rb)] * dist + negb
            m = jnp.max(s, axis=1, keepdims=True)
            e_scr[_rows(rb), 0:width] = jnp.exp(s - m).astype(e_scr.dtype)
        r_scr[...] = jnp.dot(e_scr[:, 0:width], vc_ref[0, 0, 0:width, :], preferred_element_type=F32)

    n_chunks = ncp // LANES
    need = jnp.minimum((t0 + Q_BLOCK - CMP_LEN) // CMP_STRIDE // LANES + 1, n_chunks)
    for kq in range(1, n_chunks + 1):
        pl.when(need == kq)(functools.partial(attend, kq * LANES))

    r = r_scr
    lane = lax.broadcasted_iota(jnp.int32, (Q_BLOCK, LANES), 1)
    even = lane < HEAD_DIM
    row_t = t0 + lax.broadcasted_iota(jnp.int32, (Q_BLOCK, LANES), 0)
    has_cmp = row_t >= (CMP_LEN - 1)
    imp = jnp.zeros((Q_BLOCK, LANES), F32)
    for p in range(PAIRS):
        re = slice(p * Q_BLOCK, (p + 1) * Q_BLOCK)
        ro = slice((PAIRS + p) * Q_BLOCK, (PAIRS + p + 1) * Q_BLOCK)
        de = jnp.maximum(r[re, LANES:2 * LANES], 1e-30)
        do = jnp.maximum(r[ro, LANES:2 * LANES], 1e-30)
        o = jnp.where(even, r[re, 0:LANES] / de, r[ro, 0:LANES] / do)
        o_ref[:, p * LANES:(p + 1) * LANES] = jnp.where(has_cmp, o, 0.0)
        imp = imp + r[re, 2 * LANES:3 * LANES] / de + r[ro, 2 * LANES:3 * LANES] / do
    imp = jnp.where(has_cmp, imp, 0.0)

    imp_t = imp.T
    ji = lax.broadcasted_iota(jnp.int32, (LANES, Q_BLOCK), 0)
    qt = lax.broadcasted_iota(jnp.int32, (LANES, Q_BLOCK), 1)
    cur = (t0 + qt) // SEL_LEN
    causal = ji <= cur
    forced = (ji == 0) | (ji == cur) | (ji == cur - 1)
    neg_inf = -jnp.inf
    score = jnp.where(forced, neg_inf, jnp.where(causal, imp_t, MASK_VALUE))
    score = jnp.where(ji < n_sel, score, neg_inf)
    picked = jnp.where(forced, 1.0, 0.0)
    for _ in range(SEL_TOPK - SEL_FORCED):
        mx = jnp.max(score, axis=0, keepdims=True)
        first = jnp.min(jnp.where(score == mx, ji, LANES), axis=0, keepdims=True)
        hit = ji == first
        picked = jnp.where(hit, 1.0, picked)
        score = jnp.where(hit, neg_inf, score)
    picked = jnp.where(ji < cur, picked, 0.0)
    sel_ref[0, 0] = picked.T.astype(sel_ref.dtype)


def nsa_cmp_select(q, kc2, vc_aug, nslopes, batch, seq, q_col_block):
    nblk = seq // Q_BLOCK
    ncp = seq // CMP_STRIDE
    n_cmp = (seq - CMP_LEN) // CMP_STRIDE + 1
    n_sel = seq // SEL_LEN
    gw = PAIRS * LANES
    kern = functools.partial(_nsa_cmp_kernel, ncp=ncp, n_cmp=n_cmp, n_sel=n_sel)
    return pl.pallas_call(
        kern,
        grid=(batch, KV_GROUPS, nblk),
        in_specs=[
            pl.BlockSpec(memory_space=pltpu.SMEM),
            pl.BlockSpec((Q_BLOCK, gw), lambda b, g, c: (b * nblk + c, q_col_block + g)),
            pl.BlockSpec((1, 1, ncp, LANES), lambda b, g, c: (b, g, 0, 0)),
            pl.BlockSpec((1, 1, ncp, 3 * LANES), lambda b, g, c: (b, g, 0, 0)),
        ],
        out_specs=[
            pl.BlockSpec((Q_BLOCK, gw), lambda b, g, c: (b * nblk + c, g)),
            pl.BlockSpec((1, 1, Q_BLOCK, LANES), lambda b, g, c: (b, g, c, 0)),
        ],
        out_shape=[
            jax.ShapeDtypeStruct((batch * seq, N_HEADS * HEAD_DIM), F32),
            jax.ShapeDtypeStruct((batch, KV_GROUPS, seq, LANES), MXU_DTYPE),
        ],
        scratch_shapes=[pltpu.VMEM((ROWS, LANES), MXU_DTYPE), pltpu.VMEM((ROWS, ncp), MXU_DTYPE),
                        pltpu.VMEM((ROWS, 3 * LANES), F32)],
        compiler_params=_params("parallel", "parallel", "arbitrary"),
        name="nsa_cmp_select",
    )(nslopes, q, kc2, vc_aug)


N_FEAT = 6


def sel_query_features():
    s = -jnp.asarray(_alibi_neg_slopes())
    s1 = s.astype(MXU_DTYPE).astype(F32)
    s2 = (s - s1).astype(MXU_DTYPE).astype(F32)
    s3 = (s - s1 - s2).astype(MXU_DTYPE).astype(F32)
    feat = jnp.zeros((N_HEADS, LANES), F32).at[:, HEAD_DIM:HEAD_DIM + N_FEAT].set(
        jnp.stack([s1, s2, s3, s1, s2, s3], axis=1))
    feat = feat.at[:, HEAD_DIM + N_FEAT].set(MASK_VALUE)
    order = np.array([[_head_of_row_block(g, rb) for rb in range(GROUP)] for g in range(KV_GROUPS)])
    return feat[order]


def sel_key_features(seq):
    pos = np.arange(seq)
    kk = pos % SEL_STEP
    f = np.zeros((seq + SEL_STEP, LANES - HEAD_DIM + LANES), np.float32)
    f[:seq, 0:3] = (SEL_LEN * (kk // SEL_LEN))[:, None]
    f[:seq, 3:6] = (kk % SEL_LEN)[:, None]
    f[seq:, N_FEAT] = 1.0
    f[pos, (LANES - HEAD_DIM) + pos // SEL_LEN] = 1.0
    return f


def _nsa_sel_kernel(delta_ref, qfeat_ref, q_ref, sel_ref, k_ref, v_ref, o_ref,
                    qa_ref, m_ref, acc_ref, s_a, s_b, p_a, p_b, al_a, al_b):
    grp = pl.program_id(1)
    c = pl.program_id(2)
    t0 = c * Q_BLOCK
    lane = lax.broadcasted_iota(jnp.int32, (Q_BLOCK, LANES), 1)
    low = lane < HEAD_DIM
    selneg = ((1.0 - sel_ref[0, 0].astype(F32)) * MASK_VALUE).astype(qa_ref.dtype)
    for p in range(PAIRS):
        qp = q_ref[:, p * LANES:(p + 1) * LANES].astype(F32) * (HEAD_DIM ** -0.5)
        for rb, src in ((p, qp), (PAIRS + p, pltpu.roll(qp, HEAD_DIM, axis=1))):
            qa_ref[_rows(rb), 0:LANES] = jnp.where(low, src, qfeat_ref[0, rb:rb + 1, :]).astype(qa_ref.dtype)
            qa_ref[_rows(rb), LANES:2 * LANES] = selneg
    n_steps = t0 // SEL_STEP + 1
    n_pad_step = k_ref.shape[2] // SEL_STEP - 1
    dot_nt = (((1,), (1,)), ((), ()))

    def key_start(step):
        return pl.multiple_of(jnp.clip(step, 0, n_pad_step) * SEL_STEP, SEL_STEP)

    def scores(step, s_out):
        k = k_ref[0, 0, pl.ds(key_start(step), SEL_STEP), :]
        s_out[...] = lax.dot_general(qa_ref[...], k, dot_nt, preferred_element_type=F32)

    def softmax(s_in, p_out, al_out):
        for rb in range(GROUP):
            m_prev = m_ref[_rows(rb), :] - delta_ref[_head_of_row_block(grp, rb)]
            m_new = jnp.maximum(m_prev, jnp.max(s_in[_rows(rb), :], axis=1, keepdims=True))
            al_out[_rows(rb), :] = jnp.exp(m_prev - m_new)
            m_ref[_rows(rb), :] = m_new
        for rb in range(GROUP):
            p_out[_rows(rb), :] = jnp.exp(s_in[_rows(rb), :] - _rep(m_ref[_rows(rb), :], SEL_STEP)).astype(p_out.dtype)

    def values(step, p_in, al_in):
        v = v_ref[0, 0, pl.ds(key_start(step), SEL_STEP), :]
        pv = jnp.dot(p_in[...], v, preferred_element_type=F32)
        alpha = al_in[...]
        acc_ref[...] = acc_ref[...] * jnp.concatenate([alpha, alpha], axis=1) + pv

    def even_half(t):
        scores(t, s_a)
        softmax(s_b, p_b, al_b)
        values(t - 2, p_a, al_a)

    def odd_half(t):
        scores(t, s_b)
        softmax(s_a, p_a, al_a)
        values(t - 2, p_b, al_b)

    scores(0, s_a)
    own = pl.multiple_of(t0, Q_BLOCK)
    qi = lax.broadcasted_iota(jnp.int32, (Q_BLOCK, Q_BLOCK), 0)
    ki = lax.broadcasted_iota(jnp.int32, (Q_BLOCK, Q_BLOCK), 1)
    own_bias = jnp.where((ki <= qi) & (ki // SEL_LEN == qi // SEL_LEN), 0.0, MASK_VALUE)
    s_own = lax.dot_general(qa_ref[:, 0:LANES], k_ref[0, 0, pl.ds(own, Q_BLOCK), 0:LANES], dot_nt,
                            preferred_element_type=F32)
    scores(1, s_b)
    back = jnp.full((Q_BLOCK, LANES), n_steps, jnp.int32).astype(F32)
    p_own = []
    for rb in range(GROUP):
        s = s_own[_rows(rb)] + own_bias
        m = jnp.max(s, axis=1, keepdims=True)
        p_own.append(jnp.exp(s - m).astype(MXU_DTYPE))
        m_ref[_rows(rb), :] = m + back * delta_ref[_head_of_row_block(grp, rb)]
    softmax(s_a, p_a, al_a)
    acc_ref[...] = jnp.dot(jnp.concatenate(p_own, axis=0), v_ref[0, 0, pl.ds(own, Q_BLOCK), :],
                           preferred_element_type=F32)

    def pair(j, carry):
        even_half(2 * j + 2)
        odd_half(2 * j + 3)
        return carry

    lax.fori_loop(0, n_steps // 2, pair, 0)

    @pl.when(n_steps % 2 == 1)
    def _():
        even_half(n_steps + 1)
    outs = _pairs(lambda rb: acc_ref[_rows(rb), 0:LANES], lambda rb: acc_ref[_rows(rb), LANES:2 * LANES])
    for p, o in enumerate(outs):
        o_ref[:, p * LANES:(p + 1) * LANES] = o


def nsa_sel_attention(q, sel, k_aug, v2, batch, seq, q_col_block):
    nblk = seq // Q_BLOCK
    gw = PAIRS * LANES
    deltas = jnp.asarray((-_alibi_neg_slopes() * SEL_STEP).astype(np.float32))
    return pl.pallas_call(
        _nsa_sel_kernel,
        grid=(batch, KV_GROUPS, nblk),
        in_specs=[
            pl.BlockSpec(memory_space=pltpu.SMEM),
            pl.BlockSpec((1, GROUP, LANES), lambda b, g, c: (g, 0, 0)),
            pl.BlockSpec((Q_BLOCK, gw), lambda b, g, c: (b * nblk + c, q_col_block + g)),
            pl.BlockSpec((1, 1, Q_BLOCK, LANES), lambda b, g, c: (b, g, c, 0)),
            pl.BlockSpec((1, 1, seq + SEL_STEP, 2 * LANES), lambda b, g, c: (b, g, 0, 0)),
            pl.BlockSpec((1, 1, seq + SEL_STEP, 2 * LANES), lambda b, g, c: (b, g, 0, 0)),
        ],
        out_specs=pl.BlockSpec((Q_BLOCK, gw), lambda b, g, c: (b * nblk + c, g)),
        out_shape=jax.ShapeDtypeStruct((batch * seq, N_HEADS * HEAD_DIM), F32),
        scratch_shapes=[
            pltpu.VMEM((ROWS, 2 * LANES), MXU_DTYPE),
            pltpu.VMEM((ROWS, LANES), F32),
            pltpu.VMEM((ROWS, 2 * LANES), F32),
            pltpu.VMEM((ROWS, SEL_STEP), F32),
            pltpu.VMEM((ROWS, SEL_STEP), F32),
            pltpu.VMEM((ROWS, SEL_STEP), MXU_DTYPE),
            pltpu.VMEM((ROWS, SEL_STEP), MXU_DTYPE),
            pltpu.VMEM((ROWS, LANES), F32),
            pltpu.VMEM((ROWS, LANES), F32),
        ],
        compiler_params=_params("parallel", "parallel", "arbitrary"),
        name="nsa_sel_attention",
    )(deltas, sel_query_features(), q, sel, k_aug, v2)


def _nsa_win_kernel(nslope_ref, q_ref, k_ref, v_ref, o_ref, qs_ref, bias_scr):
    grp = pl.program_id(1)
    c = pl.program_id(2)
    t0 = c * Q_BLOCK
    span = NSA_WINDOW + Q_BLOCK
    lead = NSA_WINDOW // Q_BLOCK
    _build_q_stack(q_ref, qs_ref)
    nslope = lambda rb: nslope_ref[_head_of_row_block(grp, rb)]

    def write(outs):
        for p, o in enumerate(outs):
            o_ref[:, p * LANES:(p + 1) * LANES] = o

    @pl.when(c == 0)
    def _():
        d = _pos_tiles(NSA_WINDOW, 0, span)
        for rb in range(GROUP):
            bias_scr[rb] = _band_bias(nslope(rb), d, NSA_WINDOW)

    @pl.when(c < lead)
    def _():
        d = _pos_tiles(t0, 0, span)
        write(_window_attend(qs_ref, k_ref[0, 0, 0:span, :], v_ref[0, 0, 0:span, :],
                             lambda rb: _band_bias(nslope(rb), d, NSA_WINDOW)))

    @pl.when(c >= lead)
    def _():
        start = pl.multiple_of(t0 - NSA_WINDOW, Q_BLOCK)
        write(_window_attend(qs_ref, k_ref[0, 0, pl.ds(start, span), :], v_ref[0, 0, pl.ds(start, span), :],
                             lambda rb: bias_scr[rb]))


def nsa_win_attention(q, k2, v2, nslopes, batch, seq, q_col_block):
    nblk = seq // Q_BLOCK
    gw = PAIRS * LANES
    span = NSA_WINDOW + Q_BLOCK
    return pl.pallas_call(
        _nsa_win_kernel,
        grid=(batch, KV_GROUPS, nblk),
        in_specs=[
            pl.BlockSpec(memory_space=pltpu.SMEM),
            pl.BlockSpec((Q_BLOCK, gw), lambda b, g, c: (b * nblk + c, q_col_block + g)),
            pl.BlockSpec((1, 1, seq, LANES), lambda b, g, c: (b, g, 0, 0)),
            pl.BlockSpec((1, 1, seq, 2 * LANES), lambda b, g, c: (b, g, 0, 0)),
        ],
        out_specs=pl.BlockSpec((Q_BLOCK, gw), lambda b, g, c: (b * nblk + c, g)),
        out_shape=jax.ShapeDtypeStruct((batch * seq, N_HEADS * HEAD_DIM), F32),
        scratch_shapes=[
            pltpu.VMEM((ROWS, LANES), MXU_DTYPE),
            pltpu.VMEM((GROUP, Q_BLOCK, span), F32),
        ],
        compiler_params=_params("parallel", "parallel", "arbitrary"),
        name="nsa_win_attention",
    )(nslopes, q, k2, v2)


def _nsa_combine_kernel(gate_ref, ex_ref, oc_ref, os_ref, ow_ref, o_ref):
    hd = o_ref.shape[1]
    sig = jax.nn.sigmoid(gate_ref[...])
    hi = sig.astype(MXU_DTYPE)
    lo = (sig - hi.astype(F32)).astype(MXU_DTYPE)
    ex = ex_ref[...]
    g = jnp.dot(hi, ex, preferred_element_type=F32) + jnp.dot(lo, ex, preferred_element_type=F32)
    out = g[:, 0:hd] * oc_ref[...] + g[:, hd:2 * hd] * os_ref[...] + g[:, 2 * hd:3 * hd] * ow_ref[...]
    o_ref[...] = out.astype(o_ref.dtype)


def _gate_expansion():
    hd = N_HEADS * HEAD_DIM
    ex = np.zeros((LANES, 3 * hd), np.float32)
    for h in range(N_HEADS):
        for i in range(3):
            ex[3 * h + i, i * hd + h * HEAD_DIM:i * hd + (h + 1) * HEAD_DIM] = 1.0
    return ex


def nsa_combine(gate, oc, osel, ow, tm=256):
    t, hd = oc.shape
    ex = jnp.asarray(_gate_expansion(), MXU_DTYPE)
    row = lambda i: (i, 0)
    return pl.pallas_call(
        _nsa_combine_kernel,
        grid=(t // tm,),
        in_specs=[
            pl.BlockSpec((tm, LANES), row),
            pl.BlockSpec((LANES, 3 * hd), lambda i: (0, 0)),
            pl.BlockSpec((tm, hd), row),
            pl.BlockSpec((tm, hd), row),
            pl.BlockSpec((tm, hd), row),
        ],
        out_specs=pl.BlockSpec((tm, hd), row),
        out_shape=jax.ShapeDtypeStruct((t, hd), MXU_DTYPE),
        compiler_params=_params("parallel"),
        name="nsa_combine",
    )(gate, ex, oc, osel, ow)


def _dup_lanes(t, batch, seq):
    t = t.reshape(batch, seq, KV_GROUPS, HEAD_DIM).transpose(0, 2, 1, 3)
    return jnp.concatenate([t, t], axis=-1)


def _dup_lanes_ones(t, batch, seq):
    d = _dup_lanes(t, batch, seq)
    return jnp.concatenate([d, jnp.ones_like(d)], axis=-1)


def _overlap_matrix(seq):
    ncp = seq // CMP_STRIDE
    n_cmp = (seq - CMP_LEN) // CMP_STRIDE + 1
    cs = np.arange(n_cmp) * CMP_STRIDE
    ss = np.arange(seq // SEL_LEN) * SEL_LEN
    ov = (cs[:, None] < ss[None, :] + SEL_LEN) & (cs[:, None] + CMP_LEN > ss[None, :])
    out = np.zeros((ncp, LANES), np.float32)
    out[:n_cmp, :seq // SEL_LEN] = ov
    return out


def _swa_layer(h, x, w_in, b_in, sinks, w_o, b_o, g_post, g_next, nslopes, batch, seq):
    hd = N_HEADS * HEAD_DIM
    kd = KV_GROUPS * HEAD_DIM
    qkv = matmul_bias(h, w_in.astype(MXU_DTYPE), b_in, MXU_DTYPE, name="swa_in_proj")
    k2 = _dup_lanes(qkv[:, hd:hd + kd], batch, seq)
    v2 = _dup_lanes_ones(qkv[:, hd + kd:hd + 2 * kd], batch, seq)
    o = swa_attention(qkv, k2, v2, nslopes, sinks.astype(F32), batch, seq)
    return matmul_norm_res(o, w_o.astype(MXU_DTYPE), b_o, g_post, g_next, x, name="swa_out_proj")


def _nsa_layer(h, x, w_in, cmp_pe, cmp_w1, cmp_b1, cmp_w2, cmp_b2, w_o, g_post, g_next, nslopes, batch, seq):
    hd = N_HEADS * HEAD_DIM
    kd = KV_GROUPS * HEAD_DIM
    t = batch * seq
    ncp = seq // CMP_STRIDE
    qkv = matmul_bias(h, w_in[:, :hd + 6 * kd].astype(MXU_DTYPE), jnp.zeros((hd + 6 * kd,), F32), MXU_DTYPE,
                      name="nsa_in_proj")
    n_gate = 3 * N_HEADS
    w_gate = jnp.pad(w_in[:, hd + 6 * kd:], ((0, 0), (0, LANES - n_gate))).astype(MXU_DTYPE)
    gate = matmul_bias(h, w_gate, jnp.zeros((LANES,), F32), F32, name="nsa_gate_proj")

    def kv(i):
        return qkv[:, hd + i * kd:hd + (i + 1) * kd]

    def slabs(a):
        return a.reshape(batch, seq, KV_GROUPS, HEAD_DIM).transpose(0, 2, 1, 3).reshape(
            batch, KV_GROUPS, ncp, CMP_STRIDE * HEAD_DIM)

    z = jnp.stack([slabs(kv(0)), slabs(kv(1))])
    half = CMP_STRIDE * HEAD_DIM
    cmp_out = compress(z, cmp_pe.reshape(2, 2, half).astype(F32), cmp_w1.astype(MXU_DTYPE),
                       cmp_b1.reshape(2, 1, -1), cmp_w2.astype(MXU_DTYPE), cmp_b2.reshape(2, 1, -1))
    kcm = cmp_out[0].astype(MXU_DTYPE)
    vcm = cmp_out[1].astype(MXU_DTYPE)
    kc2 = jnp.concatenate([kcm, kcm], axis=-1)
    ov = jnp.broadcast_to(jnp.asarray(_overlap_matrix(seq), MXU_DTYPE), (batch, KV_GROUPS, ncp, LANES))
    vc_aug = jnp.concatenate([vcm, vcm, jnp.ones((batch, KV_GROUPS, ncp, LANES), MXU_DTYPE), ov], axis=-1)

    o_cmp, sel = nsa_cmp_select(qkv, kc2, vc_aug, nslopes, batch, seq, 0)
    pad_rows = ((0, 0), (0, 0), (0, SEL_STEP), (0, 0))
    k_sel = jnp.pad(kv(2).reshape(batch, seq, KV_GROUPS, HEAD_DIM).transpose(0, 2, 1, 3), pad_rows)
    kfeat = jnp.asarray(sel_key_features(seq), MXU_DTYPE)
    k_aug = jnp.concatenate([k_sel, jnp.broadcast_to(kfeat, (batch, KV_GROUPS) + kfeat.shape)], axis=-1)
    v_sel = jnp.pad(_dup_lanes_ones(kv(3), batch, seq), pad_rows)
    o_sel = nsa_sel_attention(qkv, sel, k_aug, v_sel, batch, seq, 0)
    o_win = nsa_win_attention(qkv, _dup_lanes(kv(4), batch, seq), _dup_lanes_ones(kv(5), batch, seq),
                              nslopes, batch, seq, 0)
    o = nsa_combine(gate, o_cmp, o_sel, o_win)
    return matmul_norm_res(o, w_o.astype(MXU_DTYPE), jnp.zeros((w_o.shape[1],), F32), g_post, g_next, x,
                           name="nsa_out_proj")


def kernel(x, norm_g, swa_w_in, swa_b_in, swa_sinks, swa_w_o, swa_b_o, nsa_w_in, nsa_cmp_pe, nsa_cmp_w1, nsa_cmp_b1, nsa_cmp_w2, nsa_cmp_b2, nsa_w_o, ffn_w_gate, ffn_w_up, ffn_conv_w, ffn_conv_b, ffn_w_down):
    batch, seq, d = x.shape
    depth = norm_g.shape[0]
    nslopes = jnp.asarray(_alibi_neg_slopes())
    xf = x.reshape(batch * seq, d)
    h = rms_cast(xf, norm_g[0, 0])
    for i in range(depth):
        g = norm_g[i]
        j = i // 2
        if i % 2 == 0:
            xf, h = _swa_layer(h, xf, swa_w_in[j], swa_b_in[j], swa_sinks[j], swa_w_o[j], swa_b_o[j],
                               g[1], g[2], nslopes, batch, seq)
        else:
            xf, h = _nsa_layer(h, xf, nsa_w_in[j], nsa_cmp_pe[j], nsa_cmp_w1[j], nsa_cmp_b1[j], nsa_cmp_w2[j],
                               nsa_cmp_b2[j], nsa_w_o[j], g[1], g[2], nslopes, batch, seq)
        act = ffn_up(h, ffn_w_gate, ffn_w_up, i, ffn_conv_w[i], ffn_conv_b[i], seq)
        g_next = norm_g[i + 1, 0] if i + 1 < depth else jnp.ones((d,), F32)
        xf, h = matmul_norm_res(act, ffn_w_down[i].astype(MXU_DTYPE), jnp.zeros((d,), F32), g[3], g_next, xf,
                                tn=256, name="ffn_down")
    return xf.reshape(batch, seq, d)
```

```python
import functools

import numpy as np
import jax
import jax.numpy as jnp
from jax import lax
from jax.experimental import pallas as pl
from jax.experimental.pallas import tpu as pltpu

F32 = jnp.float32
MXU_DTYPE = jnp.bfloat16

N_HEADS = 32
HEAD_DIM = 64
KV_GROUPS = 4
GROUP = N_HEADS // KV_GROUPS
PAIRS = GROUP // 2
LANES = 128
Q_BLOCK = 128
ROWS = GROUP * Q_BLOCK
SWA_WINDOW = 128
CMP_LEN = 32
CMP_STRIDE = 16
SEL_LEN = 64
SEL_TOPK = 16
SEL_FORCED = 3
SEL_STEP = 512
NSA_WINDOW = 512
CONV_WIDTH = 3
RMS_EPS = 1e-6
MASK_VALUE = -1e30
VMEM_LIMIT = 56 * 1024 * 1024


def _params(*sem):
    return pltpu.CompilerParams(dimension_semantics=sem, vmem_limit_bytes=VMEM_LIMIT)


def _alibi_neg_slopes():
    return (-np.exp2(-8.0 * np.arange(1, N_HEADS + 1, dtype=np.float64) / N_HEADS)).astype(np.float32)


def _rms_cast_kernel(x_ref, g_ref, o_ref):
    x = x_ref[...]
    ms = jnp.mean(x * x, axis=-1, keepdims=True)
    o_ref[...] = (x * lax.rsqrt(ms + RMS_EPS) * g_ref[...]).astype(o_ref.dtype)


def rms_cast(x, g, tm=512):
    t, d = x.shape
    return pl.pallas_call(
        _rms_cast_kernel,
        grid=(t // tm,),
        in_specs=[pl.BlockSpec((tm, d), lambda i: (i, 0)), pl.BlockSpec((1, d), lambda i: (0, 0))],
        out_specs=pl.BlockSpec((tm, d), lambda i: (i, 0)),
        out_shape=jax.ShapeDtypeStruct((t, d), MXU_DTYPE),
        compiler_params=_params("parallel"),
        name="rms_cast",
    )(x, g.reshape(1, d))


def _matmul_kernel(a_ref, w_ref, b_ref, o_ref):
    acc = jnp.dot(a_ref[...], w_ref[...], preferred_element_type=F32)
    o_ref[...] = (acc + b_ref[...]).astype(o_ref.dtype)


def matmul_bias(a, w, b, out_dtype, tm=1024, tn=512, name="matmul_bias"):
    t, k = a.shape
    n = w.shape[1]
    tn = min(tn, n)
    return pl.pallas_call(
        _matmul_kernel,
        grid=(t // tm, n // tn),
        in_specs=[
            pl.BlockSpec((tm, k), lambda i, j: (i, 0)),
            pl.BlockSpec((k, tn), lambda i, j: (0, j)),
            pl.BlockSpec((1, tn), lambda i, j: (0, j)),
        ],
        out_specs=pl.BlockSpec((tm, tn), lambda i, j: (i, j)),
        out_shape=jax.ShapeDtypeStruct((t, n), out_dtype),
        compiler_params=_params("parallel", "arbitrary"),
        name=name,
    )(a, w, b.reshape(1, n))


def _mm_norm_res_kernel(a_ref, w_ref, b_ref, gpost_ref, gnext_ref, x_ref, xo_ref, ho_ref, y_scr, *, nj, tn, n):
    j = pl.program_id(1)
    y_scr[j] = jnp.dot(a_ref[...], w_ref[...], preferred_element_type=F32) + b_ref[...]

    @pl.when(j == nj - 1)
    def _():
        ss = jnp.sum(y_scr[0] * y_scr[0], axis=1, keepdims=True)
        for jj in range(1, nj):
            ss = ss + jnp.sum(y_scr[jj] * y_scr[jj], axis=1, keepdims=True)
        r = lax.rsqrt(ss / n + RMS_EPS)
        ss2 = jnp.zeros_like(ss)
        for jj in range(nj):
            cols = slice(jj * tn, (jj + 1) * tn)
            xn = x_ref[:, cols] + y_scr[jj] * r * gpost_ref[:, cols]
            xo_ref[:, cols] = xn
            ss2 = ss2 + jnp.sum(xn * xn, axis=1, keepdims=True)
        r2 = lax.rsqrt(ss2 / n + RMS_EPS)
        for jj in range(nj):
            cols = slice(jj * tn, (jj + 1) * tn)
            ho_ref[:, cols] = (xo_ref[:, cols] * r2 * gnext_ref[:, cols]).astype(ho_ref.dtype)


def matmul_norm_res(a, w, b, g_post, g_next, x, tm=512, tn=512, name="matmul_norm_res"):
    t, k = a.shape
    n = w.shape[1]
    nj = n // tn
    kern = functools.partial(_mm_norm_res_kernel, nj=nj, tn=tn, n=n)
    return pl.pallas_call(
        kern,
        grid=(t // tm, nj),
        in_specs=[
            pl.BlockSpec((tm, k), lambda i, j: (i, 0)),
            pl.BlockSpec((k, tn), lambda i, j: (0, j)),
            pl.BlockSpec((1, tn), lambda i, j: (0, j)),
            pl.BlockSpec((1, n), lambda i, j: (0, 0)),
            pl.BlockSpec((1, n), lambda i, j: (0, 0)),
            pl.BlockSpec((tm, n), lambda i, j: (i, 0)),
        ],
        out_specs=[
            pl.BlockSpec((tm, n), lambda i, j: (i, 0)),
            pl.BlockSpec((tm, n), lambda i, j: (i, 0)),
        ],
        out_shape=[jax.ShapeDtypeStruct((t, n), F32), jax.ShapeDtypeStruct((t, n), MXU_DTYPE)],
        scratch_shapes=[pltpu.VMEM((nj, tm, tn), F32)],
        compiler_params=_params("parallel", "arbitrary"),
        name=name,
    )(a, w, b.reshape(1, n), g_post.reshape(1, n), g_next.reshape(1, n), x)


FIX_ROWS = 16
CARRY_ROWS = 8


def _ffn_up_kernel(h_ref, wg_ref, wu_ref, cw_ref, cb_ref, o_ref, wg_scr, wu_scr, carry_scr, *, tm, tiles_per_seq):
    i = pl.program_id(1)

    @pl.when(i == 0)
    def _():
        wg_scr[...] = wg_ref[...].astype(wg_scr.dtype)
        wu_scr[...] = wu_ref[...].astype(wu_scr.dtype)

    h = h_ref[...]
    gate = jnp.dot(h, wg_scr[...], preferred_element_type=F32)
    up = jnp.dot(h, wu_scr[...], preferred_element_type=F32)
    w0 = cw_ref[0:1, :]
    w1 = cw_ref[1:2, :]
    w2 = cw_ref[2:3, :]
    b = cb_ref[...]
    a = b + pltpu.roll(gate, 2, axis=0) * w0
    a = a + pltpu.roll(gate, 1, axis=0) * w1
    a = a + gate * w2
    o_ref[...] = (jax.nn.silu(a) * up).astype(o_ref.dtype)

    seq_start = (i % tiles_per_seq) == 0
    prev = jnp.where(seq_start, 0.0, carry_scr[...])
    head = gate[0:FIX_ROWS]
    ext = jnp.concatenate([prev, head], axis=0)
    af = b + ext[CARRY_ROWS - 2:CARRY_ROWS - 2 + FIX_ROWS] * w0
    af = af + ext[CARRY_ROWS - 1:CARRY_ROWS - 1 + FIX_ROWS] * w1
    af = af + head * w2
    o_ref[0:FIX_ROWS, :] = (jax.nn.silu(af) * up[0:FIX_ROWS]).astype(o_ref.dtype)
    carry_scr[...] = gate[tm - CARRY_ROWS:tm]


def ffn_up(h, wg, wu, layer, conv_w, conv_b, seq, tm=1024, tn=512):
    t, k = h.shape
    n = wg.shape[2]
    kern = functools.partial(_ffn_up_kernel, tm=tm, tiles_per_seq=seq // tm)
    return pl.pallas_call(
        kern,
        grid=(n // tn, t // tm),
        in_specs=[
            pl.BlockSpec((tm, k), lambda j, i: (i, 0)),
            pl.BlockSpec((None, k, tn), lambda j, i: (layer, 0, j)),
            pl.BlockSpec((None, k, tn), lambda j, i: (layer, 0, j)),
            pl.BlockSpec((CONV_WIDTH, tn), lambda j, i: (0, j)),
            pl.BlockSpec((1, tn), lambda j, i: (0, j)),
        ],
        out_specs=pl.BlockSpec((tm, tn), lambda j, i: (i, j)),
        out_shape=jax.ShapeDtypeStruct((t, n), MXU_DTYPE),
        scratch_shapes=[
            pltpu.VMEM((k, tn), MXU_DTYPE),
            pltpu.VMEM((k, tn), MXU_DTYPE),
            pltpu.VMEM((CARRY_ROWS, tn), F32),
        ],
        compiler_params=_params("arbitrary", "arbitrary"),
        name="ffn_up",
    )(h, wg, wu, conv_w, conv_b.reshape(1, n))


def _head_of_row_block(group, rb):
    return group * GROUP + 2 * (rb % PAIRS) + rb // PAIRS


def _build_q_stack(q_ref, qs_ref):
    lane = lax.broadcasted_iota(jnp.int32, (Q_BLOCK, LANES), 1)
    even = lane < HEAD_DIM
    for p in range(PAIRS):
        qp = q_ref[:, p * LANES:(p + 1) * LANES].astype(F32) * (HEAD_DIM ** -0.5)
        qs_ref[p * Q_BLOCK:(p + 1) * Q_BLOCK, :] = jnp.where(even, qp, 0.0).astype(qs_ref.dtype)
        qs_ref[(PAIRS + p) * Q_BLOCK:(PAIRS + p + 1) * Q_BLOCK, :] = jnp.where(even, 0.0, qp).astype(qs_ref.dtype)


def _rep(x, size):
    return x if size == LANES else jnp.concatenate([x] * (size // LANES), axis=1)


def _pairs(num, den):
    lane = lax.broadcasted_iota(jnp.int32, (Q_BLOCK, LANES), 1)
    even = lane < HEAD_DIM
    outs = []
    for p in range(PAIRS):
        oe = num(p) / jnp.maximum(den(p), 1e-30)
        oo = num(PAIRS + p) / jnp.maximum(den(PAIRS + p), 1e-30)
        outs.append(jnp.where(even, oe, oo))
    return outs


def _rows(rb):
    return slice(rb * Q_BLOCK, (rb + 1) * Q_BLOCK)


def _band_bias(nslope, d, window):
    return jnp.where((d >= 0) & (d < window), nslope * d.astype(F32), MASK_VALUE)


def _window_attend(qs_ref, k, v, bias, extra_logit=None):
    half = ROWS // 2
    dot_nt = (((1,), (1,)), ((), ()))
    s_halves = [lax.dot_general(qs_ref[hh * half:(hh + 1) * half, :], k, dot_nt, preferred_element_type=F32)
                for hh in range(2)]
    ps, extras = [], []
    for rb in range(GROUP):
        lo = (rb % PAIRS) * Q_BLOCK
        s = s_halves[rb // PAIRS][lo:lo + Q_BLOCK] + bias(rb)
        m = jnp.max(s, axis=1, keepdims=True)
        if extra_logit is not None:
            m = jnp.maximum(m, extra_logit(rb))
            extras.append(jnp.exp(extra_logit(rb) - m))
        ps.append(jnp.exp(s - m).astype(MXU_DTYPE))
    r_halves = [jnp.dot(jnp.concatenate(ps[hh * PAIRS:(hh + 1) * PAIRS], axis=0), v, preferred_element_type=F32)
                for hh in range(2)]

    def part(rb, cols):
        lo = (rb % PAIRS) * Q_BLOCK
        return r_halves[rb // PAIRS][lo:lo + Q_BLOCK, cols]

    num = lambda rb: part(rb, slice(0, LANES))
    if extra_logit is None:
        den = lambda rb: part(rb, slice(LANES, 2 * LANES))
    else:
        den = lambda rb: part(rb, slice(LANES, 2 * LANES)) + extras[rb]
    return _pairs(num, den)


def _pos_tiles(t0, start, size):
    qi = lax.broadcasted_iota(jnp.int32, (Q_BLOCK, size), 0)
    ki = lax.broadcasted_iota(jnp.int32, (Q_BLOCK, size), 1)
    return (t0 - start) + (qi - ki)


def _swa_kernel(nslope_ref, sink_ref, q_ref, k_ref, v_ref, o_ref, qs_ref, bias_scr):
    grp = pl.program_id(1)
    c = pl.program_id(2)
    t0 = c * Q_BLOCK
    span = SWA_WINDOW + Q_BLOCK
    _build_q_stack(q_ref, qs_ref)
    sink = lambda rb: sink_ref[_head_of_row_block(grp, rb)]

    def write(outs):
        for p, o in enumerate(outs):
            o_ref[:, p * LANES:(p + 1) * LANES] = o.astype(o_ref.dtype)

    @pl.when(c == 0)
    def _():
        d = _pos_tiles(SWA_WINDOW, 0, span)
        for rb in range(GROUP):
            bias_scr[rb] = _band_bias(nslope_ref[_head_of_row_block(grp, rb)], d, SWA_WINDOW)
        write(_window_attend(qs_ref, k_ref[0, 0, 0:Q_BLOCK, :], v_ref[0, 0, 0:Q_BLOCK, :],
                             lambda rb: bias_scr[rb, :, SWA_WINDOW:span], sink))

    @pl.when(c > 0)
    def _():
        start = pl.multiple_of(t0 - SWA_WINDOW, Q_BLOCK)
        write(_window_attend(qs_ref, k_ref[0, 0, pl.ds(start, span), :], v_ref[0, 0, pl.ds(start, span), :],
                             lambda rb: bias_scr[rb], sink))


def swa_attention(qkv, k2, v2, nslopes, sinks, batch, seq):
    nblk = seq // Q_BLOCK
    gw = PAIRS * LANES
    span = SWA_WINDOW + Q_BLOCK
    return pl.pallas_call(
        _swa_kernel,
        grid=(batch, KV_GROUPS, nblk),
        in_specs=[
            pl.BlockSpec(memory_space=pltpu.SMEM),
            pl.BlockSpec(memory_space=pltpu.SMEM),
            pl.BlockSpec((Q_BLOCK, gw), lambda b, g, c: (b * nblk + c, g)),
            pl.BlockSpec((1, 1, seq, LANES), lambda b, g, c: (b, g, 0, 0)),
            pl.BlockSpec((1, 1, seq, 2 * LANES), lambda b, g, c: (b, g, 0, 0)),
        ],
        out_specs=pl.BlockSpec((Q_BLOCK, gw), lambda b, g, c: (b * nblk + c, g)),
        out_shape=jax.ShapeDtypeStruct((batch * seq, N_HEADS * HEAD_DIM), MXU_DTYPE),
        scratch_shapes=[
            pltpu.VMEM((ROWS, LANES), MXU_DTYPE),
            pltpu.VMEM((GROUP, Q_BLOCK, span), F32),
        ],
        compiler_params=_params("parallel", "parallel", "arbitrary"),
        name="swa_attention",
    )(nslopes, sinks, qkv, k2, v2)


def _compress_kernel(z_ref, pe_ref, w1_ref, b1_ref, w2_ref, b2_ref, o_ref, *, ncp):
    half = CMP_STRIDE * HEAD_DIM
    z = z_ref[0, 0, 0].astype(F32)
    top = (z + pe_ref[0, 0:1, :]).astype(MXU_DTYPE)
    bot = (z + pe_ref[0, 1:2, :]).astype(MXU_DTYPE)
    a = jnp.dot(top, w1_ref[0, 0:half, :], preferred_element_type=F32)
    bm = jnp.dot(bot, w1_ref[0, half:2 * half, :], preferred_element_type=F32)
    hid = a + pltpu.roll(bm, ncp - 1, axis=0) + b1_ref[0]
    act = jax.nn.gelu(hid).astype(MXU_DTYPE)
    o_ref[0, 0, 0] = jnp.dot(act, w2_ref[0], preferred_element_type=F32) + b2_ref[0]


def compress(z, pe, w1, b1, w2, b2):
    _, batch, groups, ncp, zw = z.shape
    hid = w1.shape[-1]
    kern = functools.partial(_compress_kernel, ncp=ncp)
    return pl.pallas_call(
        kern,
        grid=(2, batch, groups),
        in_specs=[
            pl.BlockSpec((1, 1, 1, ncp, zw), lambda s, b, g: (s, b, g, 0, 0)),
            pl.BlockSpec((1, 2, zw), lambda s, b, g: (s, 0, 0)),
            pl.BlockSpec((1, 2 * zw, hid), lambda s, b, g: (s, 0, 0)),
            pl.BlockSpec((1, 1, hid), lambda s, b, g: (s, 0, 0)),
            pl.BlockSpec((1, hid, HEAD_DIM), lambda s, b, g: (s, 0, 0)),
            pl.BlockSpec((1, 1, HEAD_DIM), lambda s, b, g: (s, 0, 0)),
        ],
        out_specs=pl.BlockSpec((1, 1, 1, ncp, HEAD_DIM), lambda s, b, g: (s, b, g, 0, 0)),
        out_shape=jax.ShapeDtypeStruct((2, batch, groups, ncp, HEAD_DIM), F32),
        compiler_params=_params("parallel", "parallel", "parallel"),
        name="nsa_compress",
    )(z, pe, w1, b1, w2, b2)


def _nsa_cmp_kernel(nslope_ref, q_ref, kc_ref, vc_ref, o_ref, sel_ref, qs_ref, e_scr, r_scr, *, ncp, n_cmp, n_sel):
    grp = pl.program_id(1)
    c = pl.program_id(2)
    t0 = c * Q_BLOCK
    _build_q_stack(q_ref, qs_ref)

    def attend(width):
        qi = lax.broadcasted_iota(jnp.int32, (Q_BLOCK, width), 0)
        ni = lax.broadcasted_iota(jnp.int32, (Q_BLOCK, width), 1)
        d = (t0 + qi) - (ni * CMP_STRIDE + (CMP_LEN - 1))
        negb = jnp.where((d >= 0) & (ni < n_cmp), 0.0, MASK_VALUE)
        dist = d.astype(F32)
        half = ROWS // 2
        for hh in range(2):
            hrows = slice(hh * half, (hh + 1) * half)
            s_half = lax.dot_general(qs_ref[hrows, :], kc_ref[0, 0, 0:width, :], (((1,), (1,)), ((), ())),
                                     preferred_element_type=F32)
            for pr in range(PAIRS):
                rb = hh * PAIRS + pr
                s = s_half[_rows(pr)] + nslope_ref[_head_of_row_block(grp, rb)] * dist + negb
                m = jnp.max(s, axis=1, keepdims=True)
                e_scr[_rows(rb), 0:width] = jnp.exp(s - m).astype(e_scr.dtype)
            r_scr[hrows, :] = jnp.dot(e_scr[hrows, 0:width], vc_ref[0, 0, 0:width, :], preferred_element_type=F32)

    n_chunks = ncp // LANES
    need = jnp.minimum((t0 + Q_BLOCK - CMP_LEN) // CMP_STRIDE // LANES + 1, n_chunks)
    for kq in range(1, n_chunks + 1):
        pl.when(need == kq)(functools.partial(attend, kq * LANES))

    r = r_scr
    lane = lax.broadcasted_iota(jnp.int32, (Q_BLOCK, LANES), 1)
    even = lane < HEAD_DIM
    row_t = t0 + lax.broadcasted_iota(jnp.int32, (Q_BLOCK, LANES), 0)
    has_cmp = row_t >= (CMP_LEN - 1)
    imp = jnp.zeros((Q_BLOCK, LANES), F32)
    for p in range(PAIRS):
        re = slice(p * Q_BLOCK, (p + 1) * Q_BLOCK)
        ro = slice((PAIRS + p) * Q_BLOCK, (PAIRS + p + 1) * Q_BLOCK)
        de = jnp.maximum(r[re, LANES:2 * LANES], 1e-30)
        do = jnp.maximum(r[ro, LANES:2 * LANES], 1e-30)
        o = jnp.where(even, r[re, 0:LANES] / de, r[ro, 0:LANES] / do)
        o_ref[:, p * LANES:(p + 1) * LANES] = jnp.where(has_cmp, o, 0.0)
        imp = imp + r[re, 2 * LANES:3 * LANES] / de + r[ro, 2 * LANES:3 * LANES] / do
    imp = jnp.where(has_cmp, imp, 0.0)

    imp_t = imp.T
    ji = lax.broadcasted_iota(jnp.int32, (LANES, Q_BLOCK), 0)
    qt = lax.broadcasted_iota(jnp.int32, (LANES, Q_BLOCK), 1)
    cur = (t0 + qt) // SEL_LEN
    causal = ji <= cur
    forced = (ji == 0) | (ji == cur) | (ji == cur - 1)
    neg_inf = -jnp.inf
    score = jnp.where(forced, neg_inf, jnp.where(causal, imp_t, MASK_VALUE))
    score = jnp.where(ji < n_sel, score, neg_inf)
    picked = jnp.where(forced, 1.0, 0.0)
    for _ in range(SEL_TOPK - SEL_FORCED):
        mx = jnp.max(score, axis=0, keepdims=True)
        first = jnp.min(jnp.where(score == mx, ji, LANES), axis=0, keepdims=True)
        hit = ji == first
        picked = jnp.where(hit, 1.0, picked)
        score = jnp.where(hit, neg_inf, score)
    picked = jnp.where(ji < cur, picked, 0.0)
    sel_ref[0, 0] = picked.T.astype(sel_ref.dtype)


def nsa_cmp_select(q, kc2, vc_aug, nslopes, batch, seq, q_col_block):
    nblk = seq // Q_BLOCK
    ncp = seq // CMP_STRIDE
    n_cmp = (seq - CMP_LEN) // CMP_STRIDE + 1
    n_sel = seq // SEL_LEN
    gw = PAIRS * LANES
    kern = functools.partial(_nsa_cmp_kernel, ncp=ncp, n_cmp=n_cmp, n_sel=n_sel)
    return pl.pallas_call(
        kern,
        grid=(batch, KV_GROUPS, nblk),
        in_specs=[
            pl.BlockSpec(memory_space=pltpu.SMEM),
            pl.BlockSpec((Q_BLOCK, gw), lambda b, g, c: (b * nblk + c, q_col_block + g)),
            pl.BlockSpec((1, 1, ncp, LANES), lambda b, g, c: (b, g, 0, 0)),
            pl.BlockSpec((1, 1, ncp, 3 * LANES), lambda b, g, c: (b, g, 0, 0)),
        ],
        out_specs=[
            pl.BlockSpec((Q_BLOCK, gw), lambda b, g, c: (b * nblk + c, g)),
            pl.BlockSpec((1, 1, Q_BLOCK, LANES), lambda b, g, c: (b, g, c, 0)),
        ],
        out_shape=[
            jax.ShapeDtypeStruct((batch * seq, N_HEADS * HEAD_DIM), F32),
            jax.ShapeDtypeStruct((batch, KV_GROUPS, seq, LANES), MXU_DTYPE),
        ],
        scratch_shapes=[pltpu.VMEM((ROWS, LANES), MXU_DTYPE), pltpu.VMEM((ROWS, ncp), MXU_DTYPE),
                        pltpu.VMEM((ROWS, 3 * LANES), F32)],
        compiler_params=_params("parallel", "parallel", "arbitrary"),
        name="nsa_cmp_select",
    )(nslopes, q, kc2, vc_aug)


N_FEAT = 6


def sel_query_features():
    s = -jnp.asarray(_alibi_neg_slopes())
    s1 = s.astype(MXU_DTYPE).astype(F32)
    s2 = (s - s1).astype(MXU_DTYPE).astype(F32)
    s3 = (s - s1 - s2).astype(MXU_DTYPE).astype(F32)
    feat = jnp.zeros((N_HEADS, LANES), F32).at[:, HEAD_DIM:HEAD_DIM + N_FEAT].set(
        jnp.stack([s1, s2, s3, s1, s2, s3], axis=1))
    feat = feat.at[:, HEAD_DIM + N_FEAT].set(MASK_VALUE)
    order = np.array([[_head_of_row_block(g, rb) for rb in range(GROUP)] for g in range(KV_GROUPS)])
    return feat[order]


def sel_key_features(seq):
    pos = np.arange(seq)
    kk = pos % SEL_STEP
    f = np.zeros((seq + SEL_STEP, LANES - HEAD_DIM + LANES), np.float32)
    f[:seq, 0:3] = (SEL_LEN * (kk // SEL_LEN))[:, None]
    f[:seq, 3:6] = (kk % SEL_LEN)[:, None]
    f[seq:, N_FEAT] = 1.0
    f[pos, (LANES - HEAD_DIM) + pos // SEL_LEN] = 1.0
    return f


def _nsa_sel_kernel(delta_ref, qfeat_ref, q_ref, sel_ref, k_ref, v_ref, o_ref,
                    qa_ref, m_ref, acc_ref, s_a, s_b, p_a, p_b, al_a, al_b):
    grp = pl.program_id(1)
    c = pl.program_id(2)
    t0 = c * Q_BLOCK
    lane = lax.broadcasted_iota(jnp.int32, (Q_BLOCK, LANES), 1)
    low = lane < HEAD_DIM
    selneg = ((1.0 - sel_ref[0, 0].astype(F32)) * MASK_VALUE).astype(qa_ref.dtype)
    for p in range(PAIRS):
        qp = q_ref[:, p * LANES:(p + 1) * LANES].astype(F32) * (HEAD_DIM ** -0.5)
        for rb, src in ((p, qp), (PAIRS + p, pltpu.roll(qp, HEAD_DIM, axis=1))):
            qa_ref[_rows(rb), 0:LANES] = jnp.where(low, src, qfeat_ref[0, rb:rb + 1, :]).astype(qa_ref.dtype)
            qa_ref[_rows(rb), LANES:2 * LANES] = selneg
    n_steps = t0 // SEL_STEP + 1
    n_pad_step = k_ref.shape[2] // SEL_STEP - 1
    dot_nt = (((1,), (1,)), ((), ()))

    def key_start(step):
        return pl.multiple_of(jnp.clip(step, 0, n_pad_step) * SEL_STEP, SEL_STEP)

    def scores(step, s_out):
        k = k_ref[0, 0, pl.ds(key_start(step), SEL_STEP), :]
        s_out[...] = lax.dot_general(qa_ref[...], k, dot_nt, preferred_element_type=F32)

    def softmax(s_in, p_out, al_out):
        for rb in range(GROUP):
            m_prev = m_ref[_rows(rb), :] - delta_ref[_head_of_row_block(grp, rb)]
            m_new = jnp.maximum(m_prev, jnp.max(s_in[_rows(rb), :], axis=1, keepdims=True))
            al_out[_rows(rb), :] = jnp.exp(m_prev - m_new)
            m_ref[_rows(rb), :] = m_new
        for rb in range(GROUP):
            p_out[_rows(rb), :] = jnp.exp(s_in[_rows(rb), :] - _rep(m_ref[_rows(rb), :], SEL_STEP)).astype(p_out.dtype)

    def values(step, p_in, al_in):
        v = v_ref[0, 0, pl.ds(key_start(step), SEL_STEP), :]
        pv = jnp.dot(p_in[...], v, preferred_element_type=F32)
        alpha = al_in[...]
        acc_ref[...] = acc_ref[...] * jnp.concatenate([alpha, alpha], axis=1) + pv

    def even_half(t):
        scores(t, s_a)
        softmax(s_b, p_b, al_b)
        values(t - 2, p_a, al_a)

    def odd_half(t):
        scores(t, s_b)
        softmax(s_a, p_a, al_a)
        values(t - 2, p_b, al_b)

    scores(0, s_a)
    own = pl.multiple_of(t0, Q_BLOCK)
    qi = lax.broadcasted_iota(jnp.int32, (Q_BLOCK, Q_BLOCK), 0)
    ki = lax.broadcasted_iota(jnp.int32, (Q_BLOCK, Q_BLOCK), 1)
    own_bias = jnp.where((ki <= qi) & (ki // SEL_LEN == qi // SEL_LEN), 0.0, MASK_VALUE)
    s_own = lax.dot_general(qa_ref[:, 0:LANES], k_ref[0, 0, pl.ds(own, Q_BLOCK), 0:LANES], dot_nt,
                            preferred_element_type=F32)
    scores(1, s_b)
    back = jnp.full((Q_BLOCK, LANES), n_steps, jnp.int32).astype(F32)
    p_own = []
    for rb in range(GROUP):
        s = s_own[_rows(rb)] + own_bias
        m = jnp.max(s, axis=1, keepdims=True)
        p_own.append(jnp.exp(s - m).astype(MXU_DTYPE))
        m_ref[_rows(rb), :] = m + back * delta_ref[_head_of_row_block(grp, rb)]
    softmax(s_a, p_a, al_a)
    acc_ref[...] = jnp.dot(jnp.concatenate(p_own, axis=0), v_ref[0, 0, pl.ds(own, Q_BLOCK), :],
                           preferred_element_type=F32)

    def quad(j, carry):
        even_half(4 * j + 2)
        odd_half(4 * j + 3)
        even_half(4 * j + 4)
        odd_half(4 * j + 5)
        return carry

    lax.fori_loop(0, n_steps // 4, quad, 0)
    rest = 4 * (n_steps // 4) + 2

    @pl.when(n_steps % 4 >= 2)
    def _():
        even_half(rest)
        odd_half(rest + 1)

    @pl.when(n_steps % 2 == 1)
    def _():
        even_half(n_steps + 1)
    outs = _pairs(lambda rb: acc_ref[_rows(rb), 0:LANES], lambda rb: acc_ref[_rows(rb), LANES:2 * LANES])
    for p, o in enumerate(outs):
        o_ref[:, p * LANES:(p + 1) * LANES] = o


def nsa_sel_attention(q, sel, k_aug, v2, batch, seq, q_col_block):
    nblk = seq // Q_BLOCK
    gw = PAIRS * LANES
    deltas = jnp.asarray((-_alibi_neg_slopes() * SEL_STEP).astype(np.float32))
    return pl.pallas_call(
        _nsa_sel_kernel,
        grid=(batch, KV_GROUPS, nblk),
        in_specs=[
            pl.BlockSpec(memory_space=pltpu.SMEM),
            pl.BlockSpec((1, GROUP, LANES), lambda b, g, c: (g, 0, 0)),
            pl.BlockSpec((Q_BLOCK, gw), lambda b, g, c: (b * nblk + c, q_col_block + g)),
            pl.BlockSpec((1, 1, Q_BLOCK, LANES), lambda b, g, c: (b, g, c, 0)),
            pl.BlockSpec((1, 1, seq + SEL_STEP, 2 * LANES), lambda b, g, c: (b, g, 0, 0)),
            pl.BlockSpec((1, 1, seq + SEL_STEP, 2 * LANES), lambda b, g, c: (b, g, 0, 0)),
        ],
        out_specs=pl.BlockSpec((Q_BLOCK, gw), lambda b, g, c: (b * nblk + c, g)),
        out_shape=jax.ShapeDtypeStruct((batch * seq, N_HEADS * HEAD_DIM), F32),
        scratch_shapes=[
            pltpu.VMEM((ROWS, 2 * LANES), MXU_DTYPE),
            pltpu.VMEM((ROWS, LANES), F32),
            pltpu.VMEM((ROWS, 2 * LANES), F32),
            pltpu.VMEM((ROWS, SEL_STEP), F32),
            pltpu.VMEM((ROWS, SEL_STEP), F32),
            pltpu.VMEM((ROWS, SEL_STEP), MXU_DTYPE),
            pltpu.VMEM((ROWS, SEL_STEP), MXU_DTYPE),
            pltpu.VMEM((ROWS, LANES), F32),
            pltpu.VMEM((ROWS, LANES), F32),
        ],
        compiler_params=_params("parallel", "parallel", "arbitrary"),
        name="nsa_sel_attention",
    )(deltas, sel_query_features(), q, sel, k_aug, v2)


def _nsa_win_kernel(nslope_ref, q_ref, k_ref, v_ref, o_ref, qs_ref, bias_scr):
    grp = pl.program_id(1)
    c = pl.program_id(2)
    t0 = c * Q_BLOCK
    span = NSA_WINDOW + Q_BLOCK
    lead = NSA_WINDOW // Q_BLOCK
    _build_q_stack(q_ref, qs_ref)
    nslope = lambda rb: nslope_ref[_head_of_row_block(grp, rb)]

    def write(outs):
        for p, o in enumerate(outs):
            o_ref[:, p * LANES:(p + 1) * LANES] = o

    @pl.when(c == 0)
    def _():
        d = _pos_tiles(NSA_WINDOW, 0, span)
        for rb in range(GROUP):
            bias_scr[rb] = _band_bias(nslope(rb), d, NSA_WINDOW)

    @pl.when(c < lead)
    def _():
        d = _pos_tiles(t0, 0, span)
        write(_window_attend(qs_ref, k_ref[0, 0, 0:span, :], v_ref[0, 0, 0:span, :],
                             lambda rb: _band_bias(nslope(rb), d, NSA_WINDOW)))

    @pl.when(c >= lead)
    def _():
        start = pl.multiple_of(t0 - NSA_WINDOW, Q_BLOCK)
        write(_window_attend(qs_ref, k_ref[0, 0, pl.ds(start, span), :], v_ref[0, 0, pl.ds(start, span), :],
                             lambda rb: bias_scr[rb]))


def nsa_win_attention(q, k2, v2, nslopes, batch, seq, q_col_block):
    nblk = seq // Q_BLOCK
    gw = PAIRS * LANES
    span = NSA_WINDOW + Q_BLOCK
    return pl.pallas_call(
        _nsa_win_kernel,
        grid=(batch, KV_GROUPS, nblk),
        in_specs=[
            pl.BlockSpec(memory_space=pltpu.SMEM),
            pl.BlockSpec((Q_BLOCK, gw), lambda b, g, c: (b * nblk + c, q_col_block + g)),
            pl.BlockSpec((1, 1, seq, LANES), lambda b, g, c: (b, g, 0, 0)),
            pl.BlockSpec((1, 1, seq, 2 * LANES), lambda b, g, c: (b, g, 0, 0)),
        ],
        out_specs=pl.BlockSpec((Q_BLOCK, gw), lambda b, g, c: (b * nblk + c, g)),
        out_shape=jax.ShapeDtypeStruct((batch * seq, N_HEADS * HEAD_DIM), F32),
        scratch_shapes=[
            pltpu.VMEM((ROWS, LANES), MXU_DTYPE),
            pltpu.VMEM((GROUP, Q_BLOCK, span), F32),
        ],
        compiler_params=_params("parallel", "parallel", "arbitrary"),
        name="nsa_win_attention",
    )(nslopes, q, k2, v2)


def _nsa_combine_kernel(gate_ref, ex_ref, oc_ref, os_ref, ow_ref, o_ref):
    hd = o_ref.shape[1]
    sig = jax.nn.sigmoid(gate_ref[...])
    hi = sig.astype(MXU_DTYPE)
    lo = (sig - hi.astype(F32)).astype(MXU_DTYPE)
    ex = ex_ref[...]
    g = jnp.dot(hi, ex, preferred_element_type=F32) + jnp.dot(lo, ex, preferred_element_type=F32)
    out = g[:, 0:hd] * oc_ref[...] + g[:, hd:2 * hd] * os_ref[...] + g[:, 2 * hd:3 * hd] * ow_ref[...]
    o_ref[...] = out.astype(o_ref.dtype)


def _gate_expansion():
    hd = N_HEADS * HEAD_DIM
    ex = np.zeros((LANES, 3 * hd), np.float32)
    for h in range(N_HEADS):
        for i in range(3):
            ex[3 * h + i, i * hd + h * HEAD_DIM:i * hd + (h + 1) * HEAD_DIM] = 1.0
    return ex


def nsa_combine(gate, oc, osel, ow, tm=256):
    t, hd = oc.shape
    ex = jnp.asarray(_gate_expansion(), MXU_DTYPE)
    row = lambda i: (i, 0)
    return pl.pallas_call(
        _nsa_combine_kernel,
        grid=(t // tm,),
        in_specs=[
            pl.BlockSpec((tm, LANES), row),
            pl.BlockSpec((LANES, 3 * hd), lambda i: (0, 0)),
            pl.BlockSpec((tm, hd), row),
            pl.BlockSpec((tm, hd), row),
            pl.BlockSpec((tm, hd), row),
        ],
        out_specs=pl.BlockSpec((tm, hd), row),
        out_shape=jax.ShapeDtypeStruct((t, hd), MXU_DTYPE),
        compiler_params=_params("parallel"),
        name="nsa_combine",
    )(gate, ex, oc, osel, ow)


def _dup_lanes(t, batch, seq):
    t = t.reshape(batch, seq, KV_GROUPS, HEAD_DIM).transpose(0, 2, 1, 3)
    return jnp.concatenate([t, t], axis=-1)


def _dup_lanes_ones(t, batch, seq):
    d = _dup_lanes(t, batch, seq)
    return jnp.concatenate([d, jnp.ones_like(d)], axis=-1)


def _overlap_matrix(seq):
    ncp = seq // CMP_STRIDE
    n_cmp = (seq - CMP_LEN) // CMP_STRIDE + 1
    cs = np.arange(n_cmp) * CMP_STRIDE
    ss = np.arange(seq // SEL_LEN) * SEL_LEN
    ov = (cs[:, None] < ss[None, :] + SEL_LEN) & (cs[:, None] + CMP_LEN > ss[None, :])
    out = np.zeros((ncp, LANES), np.float32)
    out[:n_cmp, :seq // SEL_LEN] = ov
    return out


def _swa_layer(h, x, w_in, b_in, sinks, w_o, b_o, g_post, g_next, nslopes, batch, seq):
    hd = N_HEADS * HEAD_DIM
    kd = KV_GROUPS * HEAD_DIM
    qkv = matmul_bias(h, w_in.astype(MXU_DTYPE), b_in, MXU_DTYPE, name="swa_in_proj")
    k2 = _dup_lanes(qkv[:, hd:hd + kd], batch, seq)
    v2 = _dup_lanes_ones(qkv[:, hd + kd:hd + 2 * kd], batch, seq)
    o = swa_attention(qkv, k2, v2, nslopes, sinks.astype(F32), batch, seq)
    return matmul_norm_res(o, w_o.astype(MXU_DTYPE), b_o, g_post, g_next, x, name="swa_out_proj")


def _nsa_layer(h, x, w_in, cmp_pe, cmp_w1, cmp_b1, cmp_w2, cmp_b2, w_o, g_post, g_next, nslopes, batch, seq):
    hd = N_HEADS * HEAD_DIM
    kd = KV_GROUPS * HEAD_DIM
    t = batch * seq
    ncp = seq // CMP_STRIDE
    qkv = matmul_bias(h, w_in[:, :hd + 6 * kd].astype(MXU_DTYPE), jnp.zeros((hd + 6 * kd,), F32), MXU_DTYPE,
                      name="nsa_in_proj")
    n_gate = 3 * N_HEADS
    w_gate = jnp.pad(w_in[:, hd + 6 * kd:], ((0, 0), (0, LANES - n_gate))).astype(MXU_DTYPE)
    gate = matmul_bias(h, w_gate, jnp.zeros((LANES,), F32), F32, name="nsa_gate_proj")

    def kv(i):
        return qkv[:, hd + i * kd:hd + (i + 1) * kd]

    def slabs(a):
        return a.reshape(batch, seq, KV_GROUPS, HEAD_DIM).transpose(0, 2, 1, 3).reshape(
            batch, KV_GROUPS, ncp, CMP_STRIDE * HEAD_DIM)

    z = jnp.stack([slabs(kv(0)), slabs(kv(1))])
    half = CMP_STRIDE * HEAD_DIM
    cmp_out = compress(z, cmp_pe.reshape(2, 2, half).astype(F32), cmp_w1.astype(MXU_DTYPE),
                       cmp_b1.reshape(2, 1, -1), cmp_w2.astype(MXU_DTYPE), cmp_b2.reshape(2, 1, -1))
    kcm = cmp_out[0].astype(MXU_DTYPE)
    vcm = cmp_out[1].astype(MXU_DTYPE)
    kc2 = jnp.concatenate([kcm, kcm], axis=-1)
    ov = jnp.broadcast_to(jnp.asarray(_overlap_matrix(seq), MXU_DTYPE), (batch, KV_GROUPS, ncp, LANES))
    vc_aug = jnp.concatenate([vcm, vcm, jnp.ones((batch, KV_GROUPS, ncp, LANES), MXU_DTYPE), ov], axis=-1)

    o_cmp, sel = nsa_cmp_select(qkv, kc2, vc_aug, nslopes, batch, seq, 0)
    pad_rows = ((0, 0), (0, 0), (0, SEL_STEP), (0, 0))
    k_sel = jnp.pad(kv(2).reshape(batch, seq, KV_GROUPS, HEAD_DIM).transpose(0, 2, 1, 3), pad_rows)
    kfeat = jnp.asarray(sel_key_features(seq), MXU_DTYPE)
    k_aug = jnp.concatenate([k_sel, jnp.broadcast_to(kfeat, (batch, KV_GROUPS) + kfeat.shape)], axis=-1)
    v_sel = jnp.pad(_dup_lanes_ones(kv(3), batch, seq), pad_rows)
    o_sel = nsa_sel_attention(qkv, sel, k_aug, v_sel, batch, seq, 0)
    o_win = nsa_win_attention(qkv, _dup_lanes(kv(4), batch, seq), _dup_lanes_ones(kv(5), batch, seq),
                              nslopes, batch, seq, 0)
    o = nsa_combine(gate, o_cmp, o_sel, o_win)
    return matmul_norm_res(o, w_o.astype(MXU_DTYPE), jnp.zeros((w_o.shape[1],), F32), g_post, g_next, x,
                           name="nsa_out_proj")


def kernel(x, norm_g, swa_w_in, swa_b_in, swa_sinks, swa_w_o, swa_b_o, nsa_w_in, nsa_cmp_pe, nsa_cmp_w1, nsa_cmp_b1, nsa_cmp_w2, nsa_cmp_b2, nsa_w_o, ffn_w_gate, ffn_w_up, ffn_conv_w, ffn_conv_b, ffn_w_down):
    batch, seq, d = x.shape
    depth = norm_g.shape[0]
    nslopes = jnp.asarray(_alibi_neg_slopes())
    xf = x.reshape(batch * seq, d)
    h = rms_cast(xf, norm_g[0, 0])
    for i in range(depth):
        g = norm_g[i]
        j = i // 2
        if i % 2 == 0:
            xf, h = _swa_layer(h, xf, swa_w_in[j], swa_b_in[j], swa_sinks[j], swa_w_o[j], swa_b_o[j],
                               g[1], g[2], nslopes, batch, seq)
        else:
            xf, h = _nsa_layer(h, xf, nsa_w_in[j], nsa_cmp_pe[j], nsa_cmp_w1[j], nsa_cmp_b1[j], nsa_cmp_w2[j],
                               nsa_cmp_b2[j], nsa_w_o[j], g[1], g[2], nslopes, batch, seq)
        act = ffn_up(h, ffn_w_gate, ffn_w_up, i, ffn_conv_w[i], ffn_conv_b[i], seq)
        g_next = norm_g[i + 1, 0] if i + 1 < depth else jnp.ones((d,), F32)
        xf, h = matmul_norm_res(act, ffn_w_down[i].astype(MXU_DTYPE), jnp.zeros((d,), F32), g[3], g_next, xf,
                                tn=256, name="ffn_down")
    return xf.reshape(batch, seq, d)
```

```python
import functools

import numpy as np
import jax
import jax.numpy as jnp
from jax import lax
from jax.experimental import pallas as pl
from jax.experimental.pallas import tpu as pltpu

F32 = jnp.float32
MXU_DTYPE = jnp.bfloat16

N_HEADS = 32
HEAD_DIM = 64
KV_GROUPS = 4
GROUP = N_HEADS // KV_GROUPS
PAIRS = GROUP // 2
LANES = 128
Q_BLOCK = 128
ROWS = GROUP * Q_BLOCK
SWA_WINDOW = 128
CMP_LEN = 32
CMP_STRIDE = 16
SEL_LEN = 64
SEL_TOPK = 16
SEL_FORCED = 3
SEL_STEP = 512
NSA_WINDOW = 512
CONV_WIDTH = 3
RMS_EPS = 1e-6
MASK_VALUE = -1e30
VMEM_LIMIT = 56 * 1024 * 1024


def _params(*sem):
    return pltpu.CompilerParams(dimension_semantics=sem, vmem_limit_bytes=VMEM_LIMIT)


def _alibi_neg_slopes():
    return (-np.exp2(-8.0 * np.arange(1, N_HEADS + 1, dtype=np.float64) / N_HEADS)).astype(np.float32)


def _rms_cast_kernel(x_ref, g_ref, o_ref):
    x = x_ref[...]
    ms = jnp.mean(x * x, axis=-1, keepdims=True)
    o_ref[...] = (x * lax.rsqrt(ms + RMS_EPS) * g_ref[...]).astype(o_ref.dtype)


def rms_cast(x, g, tm=512):
    t, d = x.shape
    return pl.pallas_call(
        _rms_cast_kernel,
        grid=(t // tm,),
        in_specs=[pl.BlockSpec((tm, d), lambda i: (i, 0)), pl.BlockSpec((1, d), lambda i: (0, 0))],
        out_specs=pl.BlockSpec((tm, d), lambda i: (i, 0)),
        out_shape=jax.ShapeDtypeStruct((t, d), MXU_DTYPE),
        compiler_params=_params("parallel"),
        name="rms_cast",
    )(x, g.reshape(1, d))


def _matmul_kernel(a_ref, w_ref, b_ref, o_ref):
    acc = jnp.dot(a_ref[...], w_ref[...], preferred_element_type=F32)
    o_ref[...] = (acc + b_ref[...]).astype(o_ref.dtype)


def matmul_bias(a, w, b, out_dtype, tm=1024, tn=512, name="matmul_bias"):
    t, k = a.shape
    n = w.shape[1]
    tn = min(tn, n)
    return pl.pallas_call(
        _matmul_kernel,
        grid=(t // tm, n // tn),
        in_specs=[
            pl.BlockSpec((tm, k), lambda i, j: (i, 0)),
            pl.BlockSpec((k, tn), lambda i, j: (0, j)),
            pl.BlockSpec((1, tn), lambda i, j: (0, j)),
        ],
        out_specs=pl.BlockSpec((tm, tn), lambda i, j: (i, j)),
        out_shape=jax.ShapeDtypeStruct((t, n), out_dtype),
        compiler_params=_params("parallel", "arbitrary"),
        name=name,
    )(a, w, b.reshape(1, n))


def _mm_norm_res_kernel(a_ref, w_ref, b_ref, gpost_ref, gnext_ref, x_ref, xo_ref, ho_ref, y_scr, *, nj, tn, n):
    j = pl.program_id(1)
    y_scr[j] = jnp.dot(a_ref[...], w_ref[...], preferred_element_type=F32) + b_ref[...]

    @pl.when(j == nj - 1)
    def _():
        ss = jnp.sum(y_scr[0] * y_scr[0], axis=1, keepdims=True)
        for jj in range(1, nj):
            ss = ss + jnp.sum(y_scr[jj] * y_scr[jj], axis=1, keepdims=True)
        r = lax.rsqrt(ss / n + RMS_EPS)
        ss2 = jnp.zeros_like(ss)
        for jj in range(nj):
            cols = slice(jj * tn, (jj + 1) * tn)
            xn = x_ref[:, cols] + y_scr[jj] * r * gpost_ref[:, cols]
            xo_ref[:, cols] = xn
            ss2 = ss2 + jnp.sum(xn * xn, axis=1, keepdims=True)
        r2 = lax.rsqrt(ss2 / n + RMS_EPS)
        for jj in range(nj):
            cols = slice(jj * tn, (jj + 1) * tn)
            ho_ref[:, cols] = (xo_ref[:, cols] * r2 * gnext_ref[:, cols]).astype(ho_ref.dtype)


def matmul_norm_res(a, w, b, g_post, g_next, x, tm=512, tn=512, name="matmul_norm_res"):
    t, k = a.shape
    n = w.shape[1]
    nj = n // tn
    kern = functools.partial(_mm_norm_res_kernel, nj=nj, tn=tn, n=n)
    return pl.pallas_call(
        kern,
        grid=(t // tm, nj),
        in_specs=[
            pl.BlockSpec((tm, k), lambda i, j: (i, 0)),
            pl.BlockSpec((k, tn), lambda i, j: (0, j)),
            pl.BlockSpec((1, tn), lambda i, j: (0, j)),
            pl.BlockSpec((1, n), lambda i, j: (0, 0)),
            pl.BlockSpec((1, n), lambda i, j: (0, 0)),
            pl.BlockSpec((tm, n), lambda i, j: (i, 0)),
        ],
        out_specs=[
            pl.BlockSpec((tm, n), lambda i, j: (i, 0)),
            pl.BlockSpec((tm, n), lambda i, j: (i, 0)),
        ],
        out_shape=[jax.ShapeDtypeStruct((t, n), F32), jax.ShapeDtypeStruct((t, n), MXU_DTYPE)],
        scratch_shapes=[pltpu.VMEM((nj, tm, tn), F32)],
        compiler_params=_params("parallel", "arbitrary"),
        name=name,
    )(a, w, b.reshape(1, n), g_post.reshape(1, n), g_next.reshape(1, n), x)


FIX_ROWS = 16
CARRY_ROWS = 8


def _ffn_up_kernel(h_ref, wg_ref, wu_ref, cw_ref, cb_ref, o_ref, wg_scr, wu_scr, carry_scr, *, tm, tiles_per_seq):
    i = pl.program_id(1)

    @pl.when(i == 0)
    def _():
        wg_scr[...] = wg_ref[...].astype(wg_scr.dtype)
        wu_scr[...] = wu_ref[...].astype(wu_scr.dtype)

    h = h_ref[...]
    gate = jnp.dot(h, wg_scr[...], preferred_element_type=F32)
    up = jnp.dot(h, wu_scr[...], preferred_element_type=F32)
    w0 = cw_ref[0:1, :]
    w1 = cw_ref[1:2, :]
    w2 = cw_ref[2:3, :]
    b = cb_ref[...]
    a = b + pltpu.roll(gate, 2, axis=0) * w0
    a = a + pltpu.roll(gate, 1, axis=0) * w1
    a = a + gate * w2
    o_ref[...] = (jax.nn.silu(a) * up).astype(o_ref.dtype)

    seq_start = (i % tiles_per_seq) == 0
    prev = jnp.where(seq_start, 0.0, carry_scr[...])
    head = gate[0:FIX_ROWS]
    ext = jnp.concatenate([prev, head], axis=0)
    af = b + ext[CARRY_ROWS - 2:CARRY_ROWS - 2 + FIX_ROWS] * w0
    af = af + ext[CARRY_ROWS - 1:CARRY_ROWS - 1 + FIX_ROWS] * w1
    af = af + head * w2
    o_ref[0:FIX_ROWS, :] = (jax.nn.silu(af) * up[0:FIX_ROWS]).astype(o_ref.dtype)
    carry_scr[...] = gate[tm - CARRY_ROWS:tm]


def ffn_up(h, wg, wu, layer, conv_w, conv_b, seq, tm=1024, tn=512):
    t, k = h.shape
    n = wg.shape[2]
    kern = functools.partial(_ffn_up_kernel, tm=tm, tiles_per_seq=seq // tm)
    return pl.pallas_call(
        kern,
        grid=(n // tn, t // tm),
        in_specs=[
            pl.BlockSpec((tm, k), lambda j, i: (i, 0)),
            pl.BlockSpec((None, k, tn), lambda j, i: (layer, 0, j)),
            pl.BlockSpec((None, k, tn), lambda j, i: (layer, 0, j)),
            pl.BlockSpec((CONV_WIDTH, tn), lambda j, i: (0, j)),
            pl.BlockSpec((1, tn), lambda j, i: (0, j)),
        ],
        out_specs=pl.BlockSpec((tm, tn), lambda j, i: (i, j)),
        out_shape=jax.ShapeDtypeStruct((t, n), MXU_DTYPE),
        scratch_shapes=[
            pltpu.VMEM((k, tn), MXU_DTYPE),
            pltpu.VMEM((k, tn), MXU_DTYPE),
            pltpu.VMEM((CARRY_ROWS, tn), F32),
        ],
        compiler_params=_params("arbitrary", "arbitrary"),
        name="ffn_up",
    )(h, wg, wu, conv_w, conv_b.reshape(1, n))


def _head_of_row_block(group, rb):
    return group * GROUP + 2 * (rb % PAIRS) + rb // PAIRS


def _build_q_stack(q_ref, qs_ref):
    lane = lax.broadcasted_iota(jnp.int32, (Q_BLOCK, LANES), 1)
    even = lane < HEAD_DIM
    for p in range(PAIRS):
        qp = q_ref[:, p * LANES:(p + 1) * LANES].astype(F32) * (HEAD_DIM ** -0.5)
        qs_ref[p * Q_BLOCK:(p + 1) * Q_BLOCK, :] = jnp.where(even, qp, 0.0).astype(qs_ref.dtype)
        qs_ref[(PAIRS + p) * Q_BLOCK:(PAIRS + p + 1) * Q_BLOCK, :] = jnp.where(even, 0.0, qp).astype(qs_ref.dtype)


def _rep(x, size):
    return x if size == LANES else jnp.concatenate([x] * (size // LANES), axis=1)


def _pairs(num, den):
    lane = lax.broadcasted_iota(jnp.int32, (Q_BLOCK, LANES), 1)
    even = lane < HEAD_DIM
    outs = []
    for p in range(PAIRS):
        oe = num(p) / jnp.maximum(den(p), 1e-30)
        oo = num(PAIRS + p) / jnp.maximum(den(PAIRS + p), 1e-30)
        outs.append(jnp.where(even, oe, oo))
    return outs


def _rows(rb):
    return slice(rb * Q_BLOCK, (rb + 1) * Q_BLOCK)


def _band_bias(nslope, d, window):
    return jnp.where((d >= 0) & (d < window), nslope * d.astype(F32), MASK_VALUE)


def _window_attend(qs_ref, k, v, bias, extra_logit=None):
    half = ROWS // 2
    dot_nt = (((1,), (1,)), ((), ()))
    s_halves = [lax.dot_general(qs_ref[hh * half:(hh + 1) * half, :], k, dot_nt, preferred_element_type=F32)
                for hh in range(2)]
    ps, extras = [], []
    for rb in range(GROUP):
        lo = (rb % PAIRS) * Q_BLOCK
        s = s_halves[rb // PAIRS][lo:lo + Q_BLOCK] + bias(rb)
        m = jnp.max(s, axis=1, keepdims=True)
        if extra_logit is not None:
            m = jnp.maximum(m, extra_logit(rb))
            extras.append(jnp.exp(extra_logit(rb) - m))
        ps.append(jnp.exp(s - m).astype(MXU_DTYPE))
    r_halves = [jnp.dot(jnp.concatenate(ps[hh * PAIRS:(hh + 1) * PAIRS], axis=0), v, preferred_element_type=F32)
                for hh in range(2)]

    def part(rb, cols):
        lo = (rb % PAIRS) * Q_BLOCK
        return r_halves[rb // PAIRS][lo:lo + Q_BLOCK, cols]

    num = lambda rb: part(rb, slice(0, LANES))
    if extra_logit is None:
        den = lambda rb: part(rb, slice(LANES, 2 * LANES))
    else:
        den = lambda rb: part(rb, slice(LANES, 2 * LANES)) + extras[rb]
    return _pairs(num, den)


KV_CHUNK = 1024


def _unpack_group(kv_ref, grp, write_chunk, fill_ref=None):
    n = kv_ref.shape[0]
    for gi in range(KV_GROUPS):
        def fill(gi=gi):
            def body(i, carry):
                r0 = pl.multiple_of(i * KV_CHUNK, KV_CHUNK)
                x = kv_ref[pl.ds(r0, KV_CHUNK), (gi // 2) * LANES:(gi // 2 + 1) * LANES]
                xi = pltpu.bitcast(x, jnp.int32)
                xr = pltpu.roll(xi, HEAD_DIM, axis=1)
                low = lax.broadcasted_iota(jnp.int32, xi.shape, 1) < HEAD_DIM
                if fill_ref is None:
                    y = jnp.where(low, xi, xr) if gi % 2 == 0 else jnp.where(low, xr, xi)
                else:
                    f = pltpu.bitcast(fill_ref[pl.ds(r0, KV_CHUNK), 0:LANES], jnp.int32)
                    y = jnp.where(low, xi if gi % 2 == 0 else xr, f)
                write_chunk(r0, pltpu.bitcast(y, x.dtype))
                return carry

            lax.fori_loop(0, n // KV_CHUNK, body, 0)

        pl.when(grp == gi)(fill)


def _pos_tiles(t0, start, size):
    qi = lax.broadcasted_iota(jnp.int32, (Q_BLOCK, size), 0)
    ki = lax.broadcasted_iota(jnp.int32, (Q_BLOCK, size), 1)
    return (t0 - start) + (qi - ki)


def _unpack_kv(k_in, v_in, grp, k_ref, v_ref):
    def put_k(r0, y):
        k_ref[pl.ds(r0, KV_CHUNK), :] = y

    def put_v(r0, y):
        v_ref[pl.ds(r0, KV_CHUNK), 0:LANES] = y
        v_ref[pl.ds(r0, KV_CHUNK), LANES:2 * LANES] = jnp.ones((KV_CHUNK, LANES), v_ref.dtype)

    _unpack_group(k_in, grp, put_k)
    _unpack_group(v_in, grp, put_v)


def _swa_kernel(nslope_ref, sink_ref, q_ref, k_in, v_in, o_ref, qs_ref, bias_scr, k_ref, v_ref):
    grp = pl.program_id(1)
    c = pl.program_id(2)
    t0 = c * Q_BLOCK
    span = SWA_WINDOW + Q_BLOCK
    _build_q_stack(q_ref, qs_ref)
    sink = lambda rb: sink_ref[_head_of_row_block(grp, rb)]

    def write(outs):
        for p, o in enumerate(outs):
            o_ref[:, p * LANES:(p + 1) * LANES] = o.astype(o_ref.dtype)

    @pl.when(c == 0)
    def _():
        _unpack_kv(k_in, v_in, grp, k_ref, v_ref)
        d = _pos_tiles(SWA_WINDOW, 0, span)
        for rb in range(GROUP):
            bias_scr[rb] = _band_bias(nslope_ref[_head_of_row_block(grp, rb)], d, SWA_WINDOW)

    @pl.when(c == 0)
    def _():
        write(_window_attend(qs_ref, k_ref[0:Q_BLOCK, :], v_ref[0:Q_BLOCK, :],
                             lambda rb: bias_scr[rb, :, SWA_WINDOW:span], sink))

    @pl.when(c > 0)
    def _():
        start = pl.multiple_of(t0 - SWA_WINDOW, Q_BLOCK)
        write(_window_attend(qs_ref, k_ref[pl.ds(start, span), :], v_ref[pl.ds(start, span), :],
                             lambda rb: bias_scr[rb], sink))


def swa_attention(qkv, nslopes, sinks, batch, seq):
    nblk = seq // Q_BLOCK
    gw = PAIRS * LANES
    kd = KV_GROUPS * HEAD_DIM
    span = SWA_WINDOW + Q_BLOCK
    k_col = N_HEADS * HEAD_DIM // kd
    return pl.pallas_call(
        _swa_kernel,
        grid=(batch, KV_GROUPS, nblk),
        in_specs=[
            pl.BlockSpec(memory_space=pltpu.SMEM),
            pl.BlockSpec(memory_space=pltpu.SMEM),
            pl.BlockSpec((Q_BLOCK, gw), lambda b, g, c: (b * nblk + c, g)),
            pl.BlockSpec((seq, kd), lambda b, g, c: (b, k_col)),
            pl.BlockSpec((seq, kd), lambda b, g, c: (b, k_col + 1)),
        ],
        out_specs=pl.BlockSpec((Q_BLOCK, gw), lambda b, g, c: (b * nblk + c, g)),
        out_shape=jax.ShapeDtypeStruct((batch * seq, N_HEADS * HEAD_DIM), MXU_DTYPE),
        scratch_shapes=[
            pltpu.VMEM((ROWS, LANES), MXU_DTYPE),
            pltpu.VMEM((GROUP, Q_BLOCK, span), F32),
            pltpu.VMEM((seq, LANES), MXU_DTYPE),
            pltpu.VMEM((seq, 2 * LANES), MXU_DTYPE),
        ],
        compiler_params=_params("parallel", "parallel", "arbitrary"),
        name="swa_attention",
    )(nslopes, sinks, qkv, qkv, qkv)


def _compress_kernel(z_ref, pe_ref, w1_ref, b1_ref, w2_ref, b2_ref, o_ref, *, ncp):
    half = CMP_STRIDE * HEAD_DIM
    z = z_ref[0, 0, 0].astype(F32)
    top = (z + pe_ref[0, 0:1, :]).astype(MXU_DTYPE)
    bot = (z + pe_ref[0, 1:2, :]).astype(MXU_DTYPE)
    a = jnp.dot(top, w1_ref[0, 0:half, :], preferred_element_type=F32)
    bm = jnp.dot(bot, w1_ref[0, half:2 * half, :], preferred_element_type=F32)
    hid = a + pltpu.roll(bm, ncp - 1, axis=0) + b1_ref[0]
    act = jax.nn.gelu(hid).astype(MXU_DTYPE)
    o_ref[0, 0, 0] = jnp.dot(act, w2_ref[0], preferred_element_type=F32) + b2_ref[0]


def compress(z, pe, w1, b1, w2, b2):
    _, batch, groups, ncp, zw = z.shape
    hid = w1.shape[-1]
    kern = functools.partial(_compress_kernel, ncp=ncp)
    return pl.pallas_call(
        kern,
        grid=(2, batch, groups),
        in_specs=[
            pl.BlockSpec((1, 1, 1, ncp, zw), lambda s, b, g: (s, b, g, 0, 0)),
            pl.BlockSpec((1, 2, zw), lambda s, b, g: (s, 0, 0)),
            pl.BlockSpec((1, 2 * zw, hid), lambda s, b, g: (s, 0, 0)),
            pl.BlockSpec((1, 1, hid), lambda s, b, g: (s, 0, 0)),
            pl.BlockSpec((1, hid, HEAD_DIM), lambda s, b, g: (s, 0, 0)),
            pl.BlockSpec((1, 1, HEAD_DIM), lambda s, b, g: (s, 0, 0)),
        ],
        out_specs=pl.BlockSpec((1, 1, 1, ncp, HEAD_DIM), lambda s, b, g: (s, b, g, 0, 0)),
        out_shape=jax.ShapeDtypeStruct((2, batch, groups, ncp, HEAD_DIM), F32),
        compiler_params=_params("parallel", "parallel", "parallel"),
        name="nsa_compress",
    )(z, pe, w1, b1, w2, b2)


def _nsa_cmp_kernel(nslope_ref, q_ref, kc_ref, vc_ref, o_ref, sel_ref, qs_ref, e_scr, r_scr, *, ncp, n_cmp, n_sel):
    grp = pl.program_id(1)
    c = pl.program_id(2)
    t0 = c * Q_BLOCK
    _build_q_stack(q_ref, qs_ref)

    def attend(width):
        qi = lax.broadcasted_iota(jnp.int32, (Q_BLOCK, width), 0)
        ni = lax.broadcasted_iota(jnp.int32, (Q_BLOCK, width), 1)
        d = (t0 + qi) - (ni * CMP_STRIDE + (CMP_LEN - 1))
        negb = jnp.where((d >= 0) & (ni < n_cmp), 0.0, MASK_VALUE)
        dist = d.astype(F32)
        s_all = lax.dot_general(qs_ref[...], kc_ref[0, 0, 0:width, :], (((1,), (1,)), ((), ())),
                                preferred_element_type=F32)
        for rb in range(GROUP):
            s = s_all[_rows(rb)] + nslope_ref[_head_of_row_block(grp, rb)] * dist + negb
            m = jnp.max(s, axis=1, keepdims=True)
            e_scr[_rows(rb), 0:width] = jnp.exp(s - m).astype(e_scr.dtype)
        r_scr[...] = jnp.dot(e_scr[:, 0:width], vc_ref[0, 0, 0:width, :], preferred_element_type=F32)

    n_chunks = ncp // LANES
    need = jnp.minimum((t0 + Q_BLOCK - CMP_LEN) // CMP_STRIDE // LANES + 1, n_chunks)
    for kq in range(1, n_chunks + 1):
        pl.when(need == kq)(functools.partial(attend, kq * LANES))

    r = r_scr
    lane = lax.broadcasted_iota(jnp.int32, (Q_BLOCK, LANES), 1)
    even = lane < HEAD_DIM
    row_t = t0 + lax.broadcasted_iota(jnp.int32, (Q_BLOCK, LANES), 0)
    has_cmp = row_t >= (CMP_LEN - 1)
    imp = jnp.zeros((Q_BLOCK, LANES), F32)
    for p in range(PAIRS):
        re = slice(p * Q_BLOCK, (p + 1) * Q_BLOCK)
        ro = slice((PAIRS + p) * Q_BLOCK, (PAIRS + p + 1) * Q_BLOCK)
        de = jnp.maximum(r[re, LANES:2 * LANES], 1e-30)
        do = jnp.maximum(r[ro, LANES:2 * LANES], 1e-30)
        o = jnp.where(even, r[re, 0:LANES] / de, r[ro, 0:LANES] / do)
        o_ref[:, p * LANES:(p + 1) * LANES] = jnp.where(has_cmp, o, 0.0)
        imp = imp + r[re, 2 * LANES:3 * LANES] / de + r[ro, 2 * LANES:3 * LANES] / do
    imp = jnp.where(has_cmp, imp, 0.0)

    imp_t = imp.T
    ji = lax.broadcasted_iota(jnp.int32, (LANES, Q_BLOCK), 0)
    qt = lax.broadcasted_iota(jnp.int32, (LANES, Q_BLOCK), 1)
    cur = (t0 + qt) // SEL_LEN
    causal = ji <= cur
    forced = (ji == 0) | (ji == cur) | (ji == cur - 1)
    neg_inf = -jnp.inf
    score = jnp.where(forced, neg_inf, jnp.where(causal, imp_t, MASK_VALUE))
    score = jnp.where(ji < n_sel, score, neg_inf)
    picked = jnp.where(forced, 1.0, 0.0)
    for _ in range(SEL_TOPK - SEL_FORCED):
        mx = jnp.max(score, axis=0, keepdims=True)
        first = jnp.min(jnp.where(score == mx, ji, LANES), axis=0, keepdims=True)
        hit = ji == first
        picked = jnp.where(hit, 1.0, picked)
        score = jnp.where(hit, neg_inf, score)
    picked = jnp.where(ji < cur, picked, 0.0)
    sel_ref[0, 0] = picked.T.astype(sel_ref.dtype)


def nsa_cmp_select(q, kc2, vc_aug, nslopes, batch, seq, q_col_block):
    nblk = seq // Q_BLOCK
    ncp = seq // CMP_STRIDE
    n_cmp = (seq - CMP_LEN) // CMP_STRIDE + 1
    n_sel = seq // SEL_LEN
    gw = PAIRS * LANES
    kern = functools.partial(_nsa_cmp_kernel, ncp=ncp, n_cmp=n_cmp, n_sel=n_sel)
    return pl.pallas_call(
        kern,
        grid=(batch, KV_GROUPS, nblk),
        in_specs=[
            pl.BlockSpec(memory_space=pltpu.SMEM),
            pl.BlockSpec((Q_BLOCK, gw), lambda b, g, c: (b * nblk + c, q_col_block + g)),
            pl.BlockSpec((1, 1, ncp, LANES), lambda b, g, c: (b, g, 0, 0)),
            pl.BlockSpec((1, 1, ncp, 3 * LANES), lambda b, g, c: (b, g, 0, 0)),
        ],
        out_specs=[
            pl.BlockSpec((Q_BLOCK, gw), lambda b, g, c: (b * nblk + c, g)),
            pl.BlockSpec((1, 1, Q_BLOCK, LANES), lambda b, g, c: (b, g, c, 0)),
        ],
        out_shape=[
            jax.ShapeDtypeStruct((batch * seq, N_HEADS * HEAD_DIM), F32),
            jax.ShapeDtypeStruct((batch, KV_GROUPS, seq, LANES), MXU_DTYPE),
        ],
        scratch_shapes=[pltpu.VMEM((ROWS, LANES), MXU_DTYPE), pltpu.VMEM((ROWS, ncp), MXU_DTYPE),
                        pltpu.VMEM((ROWS, 3 * LANES), F32)],
        compiler_params=_params("parallel", "parallel", "arbitrary"),
        name="nsa_cmp_select",
    )(nslopes, q, kc2, vc_aug)


N_FEAT = 6


def sel_query_features():
    s = -jnp.asarray(_alibi_neg_slopes())
    s1 = s.astype(MXU_DTYPE).astype(F32)
    s2 = (s - s1).astype(MXU_DTYPE).astype(F32)
    s3 = (s - s1 - s2).astype(MXU_DTYPE).astype(F32)
    feat = jnp.zeros((N_HEADS, LANES), F32).at[:, HEAD_DIM:HEAD_DIM + N_FEAT].set(
        jnp.stack([s1, s2, s3, s1, s2, s3], axis=1))
    feat = feat.at[:, HEAD_DIM + N_FEAT].set(MASK_VALUE)
    order = np.array([[_head_of_row_block(g, rb) for rb in range(GROUP)] for g in range(KV_GROUPS)])
    return feat[order]


def sel_key_features(seq):
    pos = np.arange(seq)
    kk = pos % SEL_STEP
    f = np.zeros((seq + SEL_STEP, 2 * LANES), np.float32)
    f[:seq, HEAD_DIM:HEAD_DIM + 3] = (SEL_LEN * (kk // SEL_LEN))[:, None]
    f[:seq, HEAD_DIM + 3:HEAD_DIM + 6] = (kk % SEL_LEN)[:, None]
    f[seq:, HEAD_DIM + N_FEAT] = 1.0
    f[pos, LANES + pos // SEL_LEN] = 1.0
    return f


def _nsa_sel_kernel(delta_ref, qfeat_ref, q_ref, sel_ref, k_in, v_in, kfeat_ref, gate_ref, ex_ref, oc_ref, ow_ref,
                    o_ref, qa_ref, m_ref, acc_ref, s_a, s_b, p_a, p_b, al_a, al_b, k_ref, v_ref):
    grp = pl.program_id(1)
    c = pl.program_id(2)
    t0 = c * Q_BLOCK
    seq = k_in.shape[0]

    @pl.when(c == 0)
    def _():
        def put_k(r0, y):
            k_ref[pl.ds(r0, KV_CHUNK), 0:LANES] = y
            k_ref[pl.ds(r0, KV_CHUNK), LANES:2 * LANES] = kfeat_ref[pl.ds(r0, KV_CHUNK), LANES:2 * LANES]

        def put_v(r0, y):
            v_ref[pl.ds(r0, KV_CHUNK), 0:LANES] = y
            v_ref[pl.ds(r0, KV_CHUNK), LANES:2 * LANES] = jnp.ones((KV_CHUNK, LANES), v_ref.dtype)

        _unpack_group(k_in, grp, put_k, fill_ref=kfeat_ref)
        _unpack_group(v_in, grp, put_v)
        k_ref[seq:seq + SEL_STEP, :] = kfeat_ref[seq:seq + SEL_STEP, :]
        v_ref[seq:seq + SEL_STEP, :] = jnp.zeros((SEL_STEP, 2 * LANES), v_ref.dtype)
    lane = lax.broadcasted_iota(jnp.int32, (Q_BLOCK, LANES), 1)
    low = lane < HEAD_DIM
    selneg = ((1.0 - sel_ref[0, 0].astype(F32)) * MASK_VALUE).astype(qa_ref.dtype)
    for p in range(PAIRS):
        qp = q_ref[:, p * LANES:(p + 1) * LANES].astype(F32) * (HEAD_DIM ** -0.5)
        for rb, src in ((p, qp), (PAIRS + p, pltpu.roll(qp, HEAD_DIM, axis=1))):
            qa_ref[_rows(rb), 0:LANES] = jnp.where(low, src, qfeat_ref[0, rb:rb + 1, :]).astype(qa_ref.dtype)
            qa_ref[_rows(rb), LANES:2 * LANES] = selneg
    n_steps = t0 // SEL_STEP + 1
    n_pad_step = seq // SEL_STEP
    dot_nt = (((1,), (1,)), ((), ()))

    def key_start(step):
        return pl.multiple_of(jnp.clip(step, 0, n_pad_step) * SEL_STEP, SEL_STEP)

    def scores(step, s_out):
        k = k_ref[pl.ds(key_start(step), SEL_STEP), :]
        s_out[...] = lax.dot_general(qa_ref[...], k, dot_nt, preferred_element_type=F32)

    def softmax(s_in, p_out, al_out):
        for rb in range(GROUP):
            m_prev = m_ref[_rows(rb), :] - delta_ref[_head_of_row_block(grp, rb)]
            m_new = jnp.maximum(m_prev, jnp.max(s_in[_rows(rb), :], axis=1, keepdims=True))
            al_out[_rows(rb), :] = jnp.exp(m_prev - m_new)
            m_ref[_rows(rb), :] = m_new
        for rb in range(GROUP):
            p_out[_rows(rb), :] = jnp.exp(s_in[_rows(rb), :] - _rep(m_ref[_rows(rb), :], SEL_STEP)).astype(p_out.dtype)

    def values(step, p_in, al_in):
        v = v_ref[pl.ds(key_start(step), SEL_STEP), :]
        pv = jnp.dot(p_in[...], v, preferred_element_type=F32)
        alpha = al_in[...]
        acc_ref[...] = acc_ref[...] * jnp.concatenate([alpha, alpha], axis=1) + pv

    def even_half(t):
        scores(t, s_a)
        softmax(s_b, p_b, al_b)
        values(t - 2, p_a, al_a)

    def odd_half(t):
        scores(t, s_b)
        softmax(s_a, p_a, al_a)
        values(t - 2, p_b, al_b)

    scores(0, s_a)
    own = pl.multiple_of(t0, Q_BLOCK)
    qi = lax.broadcasted_iota(jnp.int32, (Q_BLOCK, Q_BLOCK), 0)
    ki = lax.broadcasted_iota(jnp.int32, (Q_BLOCK, Q_BLOCK), 1)
    own_bias = jnp.where((ki <= qi) & (ki // SEL_LEN == qi // SEL_LEN), 0.0, MASK_VALUE)
    s_own = lax.dot_general(qa_ref[:, 0:LANES], k_ref[pl.ds(own, Q_BLOCK), 0:LANES], dot_nt,
                            preferred_element_type=F32)
    scores(1, s_b)
    back = jnp.full((Q_BLOCK, LANES), n_steps, jnp.int32).astype(F32)
    p_own = []
    for rb in range(GROUP):
        s = s_own[_rows(rb)] + own_bias
        m = jnp.max(s, axis=1, keepdims=True)
        p_own.append(jnp.exp(s - m).astype(MXU_DTYPE))
        m_ref[_rows(rb), :] = m + back * delta_ref[_head_of_row_block(grp, rb)]
    softmax(s_a, p_a, al_a)
    acc_ref[...] = jnp.dot(jnp.concatenate(p_own, axis=0), v_ref[pl.ds(own, Q_BLOCK), :],
                           preferred_element_type=F32)

    def quad(j, carry):
        even_half(4 * j + 2)
        odd_half(4 * j + 3)
        even_half(4 * j + 4)
        odd_half(4 * j + 5)
        return carry

    lax.fori_loop(0, n_steps // 4, quad, 0)
    rest = 4 * (n_steps // 4) + 2

    @pl.when(n_steps % 4 >= 2)
    def _():
        even_half(rest)
        odd_half(rest + 1)

    @pl.when(n_steps % 2 == 1)
    def _():
        even_half(n_steps + 1)
    o_sel = _pairs(lambda rb: acc_ref[_rows(rb), 0:LANES], lambda rb: acc_ref[_rows(rb), LANES:2 * LANES])

    gw = PAIRS * LANES
    sig = jax.nn.sigmoid(gate_ref[...])
    hi = sig.astype(MXU_DTYPE)
    lo = (sig - hi.astype(F32)).astype(MXU_DTYPE)
    ex = ex_ref[0]
    g = jnp.dot(hi, ex, preferred_element_type=F32) + jnp.dot(lo, ex, preferred_element_type=F32)
    for p in range(PAIRS):
        cols = slice(p * LANES, (p + 1) * LANES)
        out = (g[:, cols] * oc_ref[:, cols]
               + g[:, gw + p * LANES:gw + (p + 1) * LANES] * o_sel[p]
               + g[:, 2 * gw + p * LANES:2 * gw + (p + 1) * LANES] * ow_ref[:, cols])
        o_ref[:, cols] = out.astype(o_ref.dtype)


def _gate_expansion():
    gw = PAIRS * LANES
    ex = np.zeros((KV_GROUPS, LANES, 3 * gw), np.float32)
    for g in range(KV_GROUPS):
        for hl in range(GROUP):
            for i in range(3):
                ex[g, 3 * (g * GROUP + hl) + i, i * gw + hl * HEAD_DIM:i * gw + (hl + 1) * HEAD_DIM] = 1.0
    return ex


def nsa_sel_attention(qkv, sel, gate, o_cmp, o_win, batch, seq, k_col, v_col):
    nblk = seq // Q_BLOCK
    gw = PAIRS * LANES
    kd = KV_GROUPS * HEAD_DIM
    deltas = jnp.asarray((-_alibi_neg_slopes() * SEL_STEP).astype(np.float32))
    ex = jnp.asarray(_gate_expansion(), MXU_DTYPE)
    kfeat = jnp.asarray(sel_key_features(seq), MXU_DTYPE)
    blk = lambda b, g, c: (b * nblk + c, g)
    return pl.pallas_call(
        _nsa_sel_kernel,
        grid=(batch, KV_GROUPS, nblk),
        in_specs=[
            pl.BlockSpec(memory_space=pltpu.SMEM),
            pl.BlockSpec((1, GROUP, LANES), lambda b, g, c: (g, 0, 0)),
            pl.BlockSpec((Q_BLOCK, gw), blk),
            pl.BlockSpec((1, 1, Q_BLOCK, LANES), lambda b, g, c: (b, g, c, 0)),
            pl.BlockSpec((seq, kd), lambda b, g, c: (b, k_col)),
            pl.BlockSpec((seq, kd), lambda b, g, c: (b, v_col)),
            pl.BlockSpec((seq + SEL_STEP, 2 * LANES), lambda b, g, c: (0, 0)),
            pl.BlockSpec((Q_BLOCK, LANES), lambda b, g, c: (b * nblk + c, 0)),
            pl.BlockSpec((1, LANES, 3 * gw), lambda b, g, c: (g, 0, 0)),
            pl.BlockSpec((Q_BLOCK, gw), blk),
            pl.BlockSpec((Q_BLOCK, gw), blk),
        ],
        out_specs=pl.BlockSpec((Q_BLOCK, gw), blk),
        out_shape=jax.ShapeDtypeStruct((batch * seq, N_HEADS * HEAD_DIM), MXU_DTYPE),
        scratch_shapes=[
            pltpu.VMEM((ROWS, 2 * LANES), MXU_DTYPE),
            pltpu.VMEM((ROWS, LANES), F32),
            pltpu.VMEM((ROWS, 2 * LANES), F32),
            pltpu.VMEM((ROWS, SEL_STEP), F32),
            pltpu.VMEM((ROWS, SEL_STEP), F32),
            pltpu.VMEM((ROWS, SEL_STEP), MXU_DTYPE),
            pltpu.VMEM((ROWS, SEL_STEP), MXU_DTYPE),
            pltpu.VMEM((ROWS, LANES), F32),
            pltpu.VMEM((ROWS, LANES), F32),
            pltpu.VMEM((seq + SEL_STEP, 2 * LANES), MXU_DTYPE),
            pltpu.VMEM((seq + SEL_STEP, 2 * LANES), MXU_DTYPE),
        ],
        compiler_params=_params("parallel", "parallel", "arbitrary"),
        name="nsa_sel_attention",
    )(deltas, sel_query_features(), qkv, sel, qkv, qkv, kfeat, gate, ex, o_cmp, o_win)


def _nsa_win_kernel(nslope_ref, q_ref, k_in, v_in, o_ref, qs_ref, bias_scr, k_ref, v_ref):
    grp = pl.program_id(1)
    c = pl.program_id(2)
    t0 = c * Q_BLOCK
    span = NSA_WINDOW + Q_BLOCK
    lead = NSA_WINDOW // Q_BLOCK
    _build_q_stack(q_ref, qs_ref)
    nslope = lambda rb: nslope_ref[_head_of_row_block(grp, rb)]

    def write(outs):
        for p, o in enumerate(outs):
            o_ref[:, p * LANES:(p + 1) * LANES] = o

    @pl.when(c == 0)
    def _():
        _unpack_kv(k_in, v_in, grp, k_ref, v_ref)
        d = _pos_tiles(NSA_WINDOW, 0, span)
        for rb in range(GROUP):
            bias_scr[rb] = _band_bias(nslope(rb), d, NSA_WINDOW)

    @pl.when(c < lead)
    def _():
        d = _pos_tiles(t0, 0, span)
        write(_window_attend(qs_ref, k_ref[0:span, :], v_ref[0:span, :],
                             lambda rb: _band_bias(nslope(rb), d, NSA_WINDOW)))

    @pl.when(c >= lead)
    def _():
        start = pl.multiple_of(t0 - NSA_WINDOW, Q_BLOCK)
        write(_window_attend(qs_ref, k_ref[pl.ds(start, span), :], v_ref[pl.ds(start, span), :],
                             lambda rb: bias_scr[rb]))


def nsa_win_attention(qkv, nslopes, batch, seq, k_col, v_col):
    nblk = seq // Q_BLOCK
    gw = PAIRS * LANES
    kd = KV_GROUPS * HEAD_DIM
    span = NSA_WINDOW + Q_BLOCK
    return pl.pallas_call(
        _nsa_win_kernel,
        grid=(batch, KV_GROUPS, nblk),
        in_specs=[
            pl.BlockSpec(memory_space=pltpu.SMEM),
            pl.BlockSpec((Q_BLOCK, gw), lambda b, g, c: (b * nblk + c, g)),
            pl.BlockSpec((seq, kd), lambda b, g, c: (b, k_col)),
            pl.BlockSpec((seq, kd), lambda b, g, c: (b, v_col)),
        ],
        out_specs=pl.BlockSpec((Q_BLOCK, gw), lambda b, g, c: (b * nblk + c, g)),
        out_shape=jax.ShapeDtypeStruct((batch * seq, N_HEADS * HEAD_DIM), F32),
        scratch_shapes=[
            pltpu.VMEM((ROWS, LANES), MXU_DTYPE),
            pltpu.VMEM((GROUP, Q_BLOCK, span), F32),
            pltpu.VMEM((seq, LANES), MXU_DTYPE),
            pltpu.VMEM((seq, 2 * LANES), MXU_DTYPE),
        ],
        compiler_params=_params("parallel", "parallel", "arbitrary"),
        name="nsa_win_attention",
    )(nslopes, qkv, qkv, qkv)


def _overlap_matrix(seq):
    ncp = seq // CMP_STRIDE
    n_cmp = (seq - CMP_LEN) // CMP_STRIDE + 1
    cs = np.arange(n_cmp) * CMP_STRIDE
    ss = np.arange(seq // SEL_LEN) * SEL_LEN
    ov = (cs[:, None] < ss[None, :] + SEL_LEN) & (cs[:, None] + CMP_LEN > ss[None, :])
    out = np.zeros((ncp, LANES), np.float32)
    out[:n_cmp, :seq // SEL_LEN] = ov
    return out


def _swa_layer(h, x, w_in, b_in, sinks, w_o, b_o, g_post, g_next, nslopes, batch, seq):
    hd = N_HEADS * HEAD_DIM
    kd = KV_GROUPS * HEAD_DIM
    qkv = matmul_bias(h, w_in.astype(MXU_DTYPE), b_in, MXU_DTYPE, name="swa_in_proj")
    o = swa_attention(qkv, nslopes, sinks.astype(F32), batch, seq)
    return matmul_norm_res(o, w_o.astype(MXU_DTYPE), b_o, g_post, g_next, x, name="swa_out_proj")


def _nsa_layer(h, x, w_in, cmp_pe, cmp_w1, cmp_b1, cmp_w2, cmp_b2, w_o, g_post, g_next, nslopes, batch, seq):
    hd = N_HEADS * HEAD_DIM
    kd = KV_GROUPS * HEAD_DIM
    t = batch * seq
    ncp = seq // CMP_STRIDE
    qkv = matmul_bias(h, w_in[:, :hd + 6 * kd].astype(MXU_DTYPE), jnp.zeros((hd + 6 * kd,), F32), MXU_DTYPE,
                      name="nsa_in_proj")
    n_gate = 3 * N_HEADS
    w_gate = jnp.pad(w_in[:, hd + 6 * kd:], ((0, 0), (0, LANES - n_gate))).astype(MXU_DTYPE)
    gate = matmul_bias(h, w_gate, jnp.zeros((LANES,), F32), F32, name="nsa_gate_proj")

    def kv(i):
        return qkv[:, hd + i * kd:hd + (i + 1) * kd]

    def slabs(a):
        return a.reshape(batch, seq, KV_GROUPS, HEAD_DIM).transpose(0, 2, 1, 3).reshape(
            batch, KV_GROUPS, ncp, CMP_STRIDE * HEAD_DIM)

    z = jnp.stack([slabs(kv(0)), slabs(kv(1))])
    half = CMP_STRIDE * HEAD_DIM
    cmp_out = compress(z, cmp_pe.reshape(2, 2, half).astype(F32), cmp_w1.astype(MXU_DTYPE),
                       cmp_b1.reshape(2, 1, -1), cmp_w2.astype(MXU_DTYPE), cmp_b2.reshape(2, 1, -1))
    kcm = cmp_out[0].astype(MXU_DTYPE)
    vcm = cmp_out[1].astype(MXU_DTYPE)
    kc2 = jnp.concatenate([kcm, kcm], axis=-1)
    ov = jnp.broadcast_to(jnp.asarray(_overlap_matrix(seq), MXU_DTYPE), (batch, KV_GROUPS, ncp, LANES))
    vc_aug = jnp.concatenate([vcm, vcm, jnp.ones((batch, KV_GROUPS, ncp, LANES), MXU_DTYPE), ov], axis=-1)

    o_cmp, sel = nsa_cmp_select(qkv, kc2, vc_aug, nslopes, batch, seq, 0)
    kv_col = lambda i: hd // kd + i
    o_win = nsa_win_attention(qkv, nslopes, batch, seq, kv_col(4), kv_col(5))
    o = nsa_sel_attention(qkv, sel, gate, o_cmp, o_win, batch, seq, kv_col(2), kv_col(3))
    return matmul_norm_res(o, w_o.astype(MXU_DTYPE), jnp.zeros((w_o.shape[1],), F32), g_post, g_next, x,
                           name="nsa_out_proj")


def kernel(x, norm_g, swa_w_in, swa_b_in, swa_sinks, swa_w_o, swa_b_o, nsa_w_in, nsa_cmp_pe, nsa_cmp_w1, nsa_cmp_b1, nsa_cmp_w2, nsa_cmp_b2, nsa_w_o, ffn_w_gate, ffn_w_up, ffn_conv_w, ffn_conv_b, ffn_w_down):
    batch, seq, d = x.shape
    depth = norm_g.shape[0]
    nslopes = jnp.asarray(_alibi_neg_slopes())
    xf = x.reshape(batch * seq, d)
    h = rms_cast(xf, norm_g[0, 0])
    for i in range(depth):
        g = norm_g[i]
        j = i // 2
        if i % 2 == 0:
            xf, h = _swa_layer(h, xf, swa_w_in[j], swa_b_in[j], swa_sinks[j], swa_w_o[j], swa_b_o[j],
                               g[1], g[2], nslopes, batch, seq)
        else:
            xf, h = _nsa_layer(h, xf, nsa_w_in[j], nsa_cmp_pe[j], nsa_cmp_w1[j], nsa_cmp_b1[j], nsa_cmp_w2[j],
                               nsa_cmp_b2[j], nsa_w_o[j], g[1], g[2], nslopes, batch, seq)
        act = ffn_up(h, ffn_w_gate, ffn_w_up, i, ffn_conv_w[i], ffn_conv_b[i], seq)
        g_next = norm_g[i + 1, 0] if i + 1 < depth else jnp.ones((d,), F32)
        xf, h = matmul_norm_res(act, ffn_w_down[i].astype(MXU_DTYPE), jnp.zeros((d,), F32), g[3], g_next, xf,
                                tn=256, name="ffn_down")
    return xf.reshape(batch, seq, d)
```

```python
import functools

import numpy as np
import jax
import jax.numpy as jnp
from jax import lax
from jax.experimental import pallas as pl
from jax.experimental.pallas import tpu as pltpu

F32 = jnp.float32
MXU_DTYPE = jnp.bfloat16

N_HEADS = 32
HEAD_DIM = 64
KV_GROUPS = 4
GROUP = N_HEADS // KV_GROUPS
PAIRS = GROUP // 2
LANES = 128
Q_BLOCK = 128
ROWS = GROUP * Q_BLOCK
SWA_WINDOW = 128
CMP_LEN = 32
CMP_STRIDE = 16
SEL_LEN = 64
SEL_TOPK = 16
SEL_FORCED = 3
SEL_STEP = 512
NSA_WINDOW = 512
CONV_WIDTH = 3
RMS_EPS = 1e-6
MASK_VALUE = -1e30
VMEM_LIMIT = 56 * 1024 * 1024


def _params(*sem):
    return pltpu.CompilerParams(dimension_semantics=sem, vmem_limit_bytes=VMEM_LIMIT)


def _alibi_neg_slopes():
    return (-np.exp2(-8.0 * np.arange(1, N_HEADS + 1, dtype=np.float64) / N_HEADS)).astype(np.float32)


def _rms_cast_kernel(x_ref, g_ref, o_ref):
    x = x_ref[...]
    ms = jnp.mean(x * x, axis=-1, keepdims=True)
    o_ref[...] = (x * lax.rsqrt(ms + RMS_EPS) * g_ref[...]).astype(o_ref.dtype)


def rms_cast(x, g, tm=512):
    t, d = x.shape
    return pl.pallas_call(
        _rms_cast_kernel,
        grid=(t // tm,),
        in_specs=[pl.BlockSpec((tm, d), lambda i: (i, 0)), pl.BlockSpec((1, d), lambda i: (0, 0))],
        out_specs=pl.BlockSpec((tm, d), lambda i: (i, 0)),
        out_shape=jax.ShapeDtypeStruct((t, d), MXU_DTYPE),
        compiler_params=_params("parallel"),
        name="rms_cast",
    )(x, g.reshape(1, d))


def _matmul_kernel(a_ref, w_ref, b_ref, o_ref):
    acc = jnp.dot(a_ref[...], w_ref[...], preferred_element_type=F32)
    o_ref[...] = (acc + b_ref[...]).astype(o_ref.dtype)


def matmul_bias(a, w, b, out_dtype, tm=1024, tn=512, name="matmul_bias"):
    t, k = a.shape
    n = w.shape[1]
    tn = min(tn, n)
    return pl.pallas_call(
        _matmul_kernel,
        grid=(t // tm, n // tn),
        in_specs=[
            pl.BlockSpec((tm, k), lambda i, j: (i, 0)),
            pl.BlockSpec((k, tn), lambda i, j: (0, j)),
            pl.BlockSpec((1, tn), lambda i, j: (0, j)),
        ],
        out_specs=pl.BlockSpec((tm, tn), lambda i, j: (i, j)),
        out_shape=jax.ShapeDtypeStruct((t, n), out_dtype),
        compiler_params=_params("parallel", "arbitrary"),
        name=name,
    )(a, w, b.reshape(1, n))


def _mm_norm_res_kernel(a_ref, w_ref, b_ref, gpost_ref, gnext_ref, x_ref, xo_ref, ho_ref, y_scr, *, nj, tn, n):
    j = pl.program_id(1)
    y_scr[j] = jnp.dot(a_ref[...], w_ref[...], preferred_element_type=F32) + b_ref[...]

    @pl.when(j == nj - 1)
    def _():
        ss = jnp.sum(y_scr[0] * y_scr[0], axis=1, keepdims=True)
        for jj in range(1, nj):
            ss = ss + jnp.sum(y_scr[jj] * y_scr[jj], axis=1, keepdims=True)
        r = lax.rsqrt(ss / n + RMS_EPS)
        ss2 = jnp.zeros_like(ss)
        for jj in range(nj):
            cols = slice(jj * tn, (jj + 1) * tn)
            xn = x_ref[:, cols] + y_scr[jj] * r * gpost_ref[:, cols]
            xo_ref[:, cols] = xn
            ss2 = ss2 + jnp.sum(xn * xn, axis=1, keepdims=True)
        r2 = lax.rsqrt(ss2 / n + RMS_EPS)
        for jj in range(nj):
            cols = slice(jj * tn, (jj + 1) * tn)
            ho_ref[:, cols] = (xo_ref[:, cols] * r2 * gnext_ref[:, cols]).astype(ho_ref.dtype)


def matmul_norm_res(a, w, b, g_post, g_next, x, tm=512, tn=512, name="matmul_norm_res"):
    t, k = a.shape
    n = w.shape[1]
    nj = n // tn
    kern = functools.partial(_mm_norm_res_kernel, nj=nj, tn=tn, n=n)
    return pl.pallas_call(
        kern,
        grid=(t // tm, nj),
        in_specs=[
            pl.BlockSpec((tm, k), lambda i, j: (i, 0)),
            pl.BlockSpec((k, tn), lambda i, j: (0, j)),
            pl.BlockSpec((1, tn), lambda i, j: (0, j)),
            pl.BlockSpec((1, n), lambda i, j: (0, 0)),
            pl.BlockSpec((1, n), lambda i, j: (0, 0)),
            pl.BlockSpec((tm, n), lambda i, j: (i, 0)),
        ],
        out_specs=[
            pl.BlockSpec((tm, n), lambda i, j: (i, 0)),
            pl.BlockSpec((tm, n), lambda i, j: (i, 0)),
        ],
        out_shape=[jax.ShapeDtypeStruct((t, n), F32), jax.ShapeDtypeStruct((t, n), MXU_DTYPE)],
        scratch_shapes=[pltpu.VMEM((nj, tm, tn), F32)],
        compiler_params=_params("parallel", "arbitrary"),
        name=name,
    )(a, w, b.reshape(1, n), g_post.reshape(1, n), g_next.reshape(1, n), x)


FIX_ROWS = 16
CARRY_ROWS = 8


def _ffn_up_kernel(h_ref, wg_ref, wu_ref, cw_ref, cb_ref, o_ref, wg_scr, wu_scr, carry_scr, *, tm, tiles_per_seq):
    i = pl.program_id(1)

    @pl.when(i == 0)
    def _():
        wg_scr[...] = wg_ref[...].astype(wg_scr.dtype)
        wu_scr[...] = wu_ref[...].astype(wu_scr.dtype)

    h = h_ref[...]
    gate = jnp.dot(h, wg_scr[...], preferred_element_type=F32)
    up = jnp.dot(h, wu_scr[...], preferred_element_type=F32)
    w0 = cw_ref[0:1, :]
    w1 = cw_ref[1:2, :]
    w2 = cw_ref[2:3, :]
    b = cb_ref[...]
    a = b + pltpu.roll(gate, 2, axis=0) * w0
    a = a + pltpu.roll(gate, 1, axis=0) * w1
    a = a + gate * w2
    o_ref[...] = (jax.nn.silu(a) * up).astype(o_ref.dtype)

    seq_start = (i % tiles_per_seq) == 0
    prev = jnp.where(seq_start, 0.0, carry_scr[...])
    head = gate[0:FIX_ROWS]
    ext = jnp.concatenate([prev, head], axis=0)
    af = b + ext[CARRY_ROWS - 2:CARRY_ROWS - 2 + FIX_ROWS] * w0
    af = af + ext[CARRY_ROWS - 1:CARRY_ROWS - 1 + FIX_ROWS] * w1
    af = af + head * w2
    o_ref[0:FIX_ROWS, :] = (jax.nn.silu(af) * up[0:FIX_ROWS]).astype(o_ref.dtype)
    carry_scr[...] = gate[tm - CARRY_ROWS:tm]


def ffn_up(h, wg, wu, layer, conv_w, conv_b, seq, tm=1024, tn=512):
    t, k = h.shape
    n = wg.shape[2]
    kern = functools.partial(_ffn_up_kernel, tm=tm, tiles_per_seq=seq // tm)
    return pl.pallas_call(
        kern,
        grid=(n // tn, t // tm),
        in_specs=[
            pl.BlockSpec((tm, k), lambda j, i: (i, 0)),
            pl.BlockSpec((None, k, tn), lambda j, i: (layer, 0, j)),
            pl.BlockSpec((None, k, tn), lambda j, i: (layer, 0, j)),
            pl.BlockSpec((CONV_WIDTH, tn), lambda j, i: (0, j)),
            pl.BlockSpec((1, tn), lambda j, i: (0, j)),
        ],
        out_specs=pl.BlockSpec((tm, tn), lambda j, i: (i, j)),
        out_shape=jax.ShapeDtypeStruct((t, n), MXU_DTYPE),
        scratch_shapes=[
            pltpu.VMEM((k, tn), MXU_DTYPE),
            pltpu.VMEM((k, tn), MXU_DTYPE),
            pltpu.VMEM((CARRY_ROWS, tn), F32),
        ],
        compiler_params=_params("arbitrary", "arbitrary"),
        name="ffn_up",
    )(h, wg, wu, conv_w, conv_b.reshape(1, n))


def _head_of_row_block(group, rb):
    return group * GROUP + 2 * (rb % PAIRS) + rb // PAIRS


Q_SUB = 2


def _build_q_stack(q_ref, qs_ref, row0=0):
    lane = lax.broadcasted_iota(jnp.int32, (Q_BLOCK, LANES), 1)
    even = lane < HEAD_DIM
    for p in range(PAIRS):
        qp = q_ref[row0:row0 + Q_BLOCK, p * LANES:(p + 1) * LANES].astype(F32) * (HEAD_DIM ** -0.5)
        qs_ref[p * Q_BLOCK:(p + 1) * Q_BLOCK, :] = jnp.where(even, qp, 0.0).astype(qs_ref.dtype)
        qs_ref[(PAIRS + p) * Q_BLOCK:(PAIRS + p + 1) * Q_BLOCK, :] = jnp.where(even, 0.0, qp).astype(qs_ref.dtype)


def _rep(x, size):
    return x if size == LANES else jnp.concatenate([x] * (size // LANES), axis=1)


def _pairs(num, den):
    lane = lax.broadcasted_iota(jnp.int32, (Q_BLOCK, LANES), 1)
    even = lane < HEAD_DIM
    outs = []
    for p in range(PAIRS):
        oe = num(p) / jnp.maximum(den(p), 1e-30)
        oo = num(PAIRS + p) / jnp.maximum(den(PAIRS + p), 1e-30)
        outs.append(jnp.where(even, oe, oo))
    return outs


def _rows(rb):
    return slice(rb * Q_BLOCK, (rb + 1) * Q_BLOCK)


def _band_bias(nslope, d, window):
    return jnp.where((d >= 0) & (d < window), nslope * d.astype(F32), MASK_VALUE)


def _window_attend(qs_ref, k, v, bias, extra_logit=None):
    half = ROWS // 2
    dot_nt = (((1,), (1,)), ((), ()))
    s_halves = [lax.dot_general(qs_ref[hh * half:(hh + 1) * half, :], k, dot_nt, preferred_element_type=F32)
                for hh in range(2)]
    ps, extras = [], []
    for rb in range(GROUP):
        lo = (rb % PAIRS) * Q_BLOCK
        s = s_halves[rb // PAIRS][lo:lo + Q_BLOCK] + bias(rb)
        m = jnp.max(s, axis=1, keepdims=True)
        if extra_logit is not None:
            m = jnp.maximum(m, extra_logit(rb))
            extras.append(jnp.exp(extra_logit(rb) - m))
        ps.append(jnp.exp(s - m).astype(MXU_DTYPE))
    r_halves = [jnp.dot(jnp.concatenate(ps[hh * PAIRS:(hh + 1) * PAIRS], axis=0), v, preferred_element_type=F32)
                for hh in range(2)]

    def part(rb, cols):
        lo = (rb % PAIRS) * Q_BLOCK
        return r_halves[rb // PAIRS][lo:lo + Q_BLOCK, cols]

    num = lambda rb: part(rb, slice(0, LANES))
    if extra_logit is None:
        den = lambda rb: part(rb, slice(LANES, 2 * LANES))
    else:
        den = lambda rb: part(rb, slice(LANES, 2 * LANES)) + extras[rb]
    return _pairs(num, den)


KV_CHUNK = 1024


def _unpack_group(kv_ref, grp, write_chunk, fill_ref=None):
    n = kv_ref.shape[0]
    for gi in range(KV_GROUPS):
        def fill(gi=gi):
            def body(i, carry):
                r0 = pl.multiple_of(i * KV_CHUNK, KV_CHUNK)
                x = kv_ref[pl.ds(r0, KV_CHUNK), (gi // 2) * LANES:(gi // 2 + 1) * LANES]
                xi = pltpu.bitcast(x, jnp.int32)
                xr = pltpu.roll(xi, HEAD_DIM, axis=1)
                low = lax.broadcasted_iota(jnp.int32, xi.shape, 1) < HEAD_DIM
                if fill_ref is None:
                    y = jnp.where(low, xi, xr) if gi % 2 == 0 else jnp.where(low, xr, xi)
                else:
                    f = pltpu.bitcast(fill_ref[pl.ds(r0, KV_CHUNK), 0:LANES], jnp.int32)
                    y = jnp.where(low, xi if gi % 2 == 0 else xr, f)
                write_chunk(r0, pltpu.bitcast(y, x.dtype))
                return carry

            lax.fori_loop(0, n // KV_CHUNK, body, 0)

        pl.when(grp == gi)(fill)


def _pos_tiles(t0, start, size):
    qi = lax.broadcasted_iota(jnp.int32, (Q_BLOCK, size), 0)
    ki = lax.broadcasted_iota(jnp.int32, (Q_BLOCK, size), 1)
    return (t0 - start) + (qi - ki)


def _unpack_kv(k_in, v_in, grp, k_ref, v_ref):
    def put_k(r0, y):
        k_ref[pl.ds(r0, KV_CHUNK), :] = y

    def put_v(r0, y):
        v_ref[pl.ds(r0, KV_CHUNK), 0:LANES] = y
        v_ref[pl.ds(r0, KV_CHUNK), LANES:2 * LANES] = jnp.ones((KV_CHUNK, LANES), v_ref.dtype)

    _unpack_group(k_in, grp, put_k)
    _unpack_group(v_in, grp, put_v)


def _swa_kernel(nslope_ref, sink_ref, q_ref, k_in, v_in, o_ref, qs_ref, bias_scr, k_ref, v_ref):
    grp = pl.program_id(1)
    c = pl.program_id(2)
    span = SWA_WINDOW + Q_BLOCK
    for j in range(Q_SUB):
        _build_q_stack(q_ref, qs_ref.at[j], j * Q_BLOCK)
    sink = lambda rb: sink_ref[_head_of_row_block(grp, rb)]

    def write(j, outs):
        for p, o in enumerate(outs):
            o_ref[j * Q_BLOCK:(j + 1) * Q_BLOCK, p * LANES:(p + 1) * LANES] = o.astype(o_ref.dtype)

    def later_block(j):
        start = pl.multiple_of((c * Q_SUB + j) * Q_BLOCK - SWA_WINDOW, Q_BLOCK)
        write(j, _window_attend(qs_ref.at[j], k_ref[pl.ds(start, span), :], v_ref[pl.ds(start, span), :],
                                lambda rb: bias_scr[rb], sink))

    @pl.when(c == 0)
    def _():
        _unpack_kv(k_in, v_in, grp, k_ref, v_ref)
        d = _pos_tiles(SWA_WINDOW, 0, span)
        for rb in range(GROUP):
            bias_scr[rb] = _band_bias(nslope_ref[_head_of_row_block(grp, rb)], d, SWA_WINDOW)

    @pl.when(c == 0)
    def _():
        write(0, _window_attend(qs_ref.at[0], k_ref[0:Q_BLOCK, :], v_ref[0:Q_BLOCK, :],
                                lambda rb: bias_scr[rb, :, SWA_WINDOW:span], sink))
        for j in range(1, Q_SUB):
            later_block(j)

    @pl.when(c > 0)
    def _():
        for j in range(Q_SUB):
            later_block(j)


def swa_attention(qkv, nslopes, sinks, batch, seq):
    nstep = seq // (Q_SUB * Q_BLOCK)
    rows = Q_SUB * Q_BLOCK
    gw = PAIRS * LANES
    kd = KV_GROUPS * HEAD_DIM
    span = SWA_WINDOW + Q_BLOCK
    k_col = N_HEADS * HEAD_DIM // kd
    return pl.pallas_call(
        _swa_kernel,
        grid=(batch, KV_GROUPS, nstep),
        in_specs=[
            pl.BlockSpec(memory_space=pltpu.SMEM),
            pl.BlockSpec(memory_space=pltpu.SMEM),
            pl.BlockSpec((rows, gw), lambda b, g, c: (b * nstep + c, g)),
            pl.BlockSpec((seq, kd), lambda b, g, c: (b, k_col)),
            pl.BlockSpec((seq, kd), lambda b, g, c: (b, k_col + 1)),
        ],
        out_specs=pl.BlockSpec((rows, gw), lambda b, g, c: (b * nstep + c, g)),
        out_shape=jax.ShapeDtypeStruct((batch * seq, N_HEADS * HEAD_DIM), MXU_DTYPE),
        scratch_shapes=[
            pltpu.VMEM((Q_SUB, ROWS, LANES), MXU_DTYPE),
            pltpu.VMEM((GROUP, Q_BLOCK, span), F32),
            pltpu.VMEM((seq, LANES), MXU_DTYPE),
            pltpu.VMEM((seq, 2 * LANES), MXU_DTYPE),
        ],
        compiler_params=_params("parallel", "parallel", "arbitrary"),
        name="swa_attention",
    )(nslopes, sinks, qkv, qkv, qkv)


def _compress_kernel(z_ref, pe_ref, w1_ref, b1_ref, w2_ref, b2_ref, o_ref, *, ncp):
    half = CMP_STRIDE * HEAD_DIM
    z = z_ref[0, 0, 0].astype(F32)
    top = (z + pe_ref[0, 0:1, :]).astype(MXU_DTYPE)
    bot = (z + pe_ref[0, 1:2, :]).astype(MXU_DTYPE)
    a = jnp.dot(top, w1_ref[0, 0:half, :], preferred_element_type=F32)
    bm = jnp.dot(bot, w1_ref[0, half:2 * half, :], preferred_element_type=F32)
    hid = a + pltpu.roll(bm, ncp - 1, axis=0) + b1_ref[0]
    act = jax.nn.gelu(hid).astype(MXU_DTYPE)
    o_ref[0, 0, 0] = jnp.dot(act, w2_ref[0], preferred_element_type=F32) + b2_ref[0]


def compress(z, pe, w1, b1, w2, b2):
    _, batch, groups, ncp, zw = z.shape
    hid = w1.shape[-1]
    kern = functools.partial(_compress_kernel, ncp=ncp)
    return pl.pallas_call(
        kern,
        grid=(2, batch, groups),
        in_specs=[
            pl.BlockSpec((1, 1, 1, ncp, zw), lambda s, b, g: (s, b, g, 0, 0)),
            pl.BlockSpec((1, 2, zw), lambda s, b, g: (s, 0, 0)),
            pl.BlockSpec((1, 2 * zw, hid), lambda s, b, g: (s, 0, 0)),
            pl.BlockSpec((1, 1, hid), lambda s, b, g: (s, 0, 0)),
            pl.BlockSpec((1, hid, HEAD_DIM), lambda s, b, g: (s, 0, 0)),
            pl.BlockSpec((1, 1, HEAD_DIM), lambda s, b, g: (s, 0, 0)),
        ],
        out_specs=pl.BlockSpec((1, 1, 1, ncp, HEAD_DIM), lambda s, b, g: (s, b, g, 0, 0)),
        out_shape=jax.ShapeDtypeStruct((2, batch, groups, ncp, HEAD_DIM), F32),
        compiler_params=_params("parallel", "parallel", "parallel"),
        name="nsa_compress",
    )(z, pe, w1, b1, w2, b2)


def _nsa_cmp_kernel(nslope_ref, q_ref, kc_ref, vc_ref, o_ref, sel_ref, qs_ref, e_scr, r_scr, *, ncp, n_cmp, n_sel):
    grp = pl.program_id(1)
    c = pl.program_id(2)
    starts = [(c * Q_SUB + j) * Q_BLOCK for j in range(Q_SUB)]
    for j in range(Q_SUB):
        _build_q_stack(q_ref, qs_ref.at[j], j * Q_BLOCK)

    def attend(width):
        for j, t0 in enumerate(starts):
            qi = lax.broadcasted_iota(jnp.int32, (Q_BLOCK, width), 0)
            ni = lax.broadcasted_iota(jnp.int32, (Q_BLOCK, width), 1)
            d = (t0 + qi) - (ni * CMP_STRIDE + (CMP_LEN - 1))
            negb = jnp.where((d >= 0) & (ni < n_cmp), 0.0, MASK_VALUE)
            dist = d.astype(F32)
            s_all = lax.dot_general(qs_ref[j], kc_ref[0, 0, 0:width, :], (((1,), (1,)), ((), ())),
                                    preferred_element_type=F32)
            for rb in range(GROUP):
                s = s_all[_rows(rb)] + nslope_ref[_head_of_row_block(grp, rb)] * dist + negb
                m = jnp.max(s, axis=1, keepdims=True)
                e_scr[j, _rows(rb), 0:width] = jnp.exp(s - m).astype(e_scr.dtype)
            r_scr[j] = jnp.dot(e_scr[j, :, 0:width], vc_ref[0, 0, 0:width, :], preferred_element_type=F32)

    n_chunks = ncp // LANES
    need = jnp.minimum((starts[-1] + Q_BLOCK - CMP_LEN) // CMP_STRIDE // LANES + 1, n_chunks)
    for kq in range(1, n_chunks + 1):
        pl.when(need == kq)(functools.partial(attend, kq * LANES))

    lane = lax.broadcasted_iota(jnp.int32, (Q_BLOCK, LANES), 1)
    even = lane < HEAD_DIM
    ji = lax.broadcasted_iota(jnp.int32, (LANES, Q_BLOCK), 0)
    qt = lax.broadcasted_iota(jnp.int32, (LANES, Q_BLOCK), 1)
    neg_inf = -jnp.inf
    for j, t0 in enumerate(starts):
        orow = slice(j * Q_BLOCK, (j + 1) * Q_BLOCK)
        row_t = t0 + lax.broadcasted_iota(jnp.int32, (Q_BLOCK, LANES), 0)
        has_cmp = row_t >= (CMP_LEN - 1)
        imp = jnp.zeros((Q_BLOCK, LANES), F32)
        for p in range(PAIRS):
            re = _rows(p)
            ro = _rows(PAIRS + p)
            de = jnp.maximum(r_scr[j, re, LANES:2 * LANES], 1e-30)
            do = jnp.maximum(r_scr[j, ro, LANES:2 * LANES], 1e-30)
            o = jnp.where(even, r_scr[j, re, 0:LANES] / de, r_scr[j, ro, 0:LANES] / do)
            o_ref[orow, p * LANES:(p + 1) * LANES] = jnp.where(has_cmp, o, 0.0)
            imp = imp + r_scr[j, re, 2 * LANES:3 * LANES] / de + r_scr[j, ro, 2 * LANES:3 * LANES] / do
        imp = jnp.where(has_cmp, imp, 0.0)

        imp_t = imp.T
        cur = (t0 + qt) // SEL_LEN
        causal = ji <= cur
        forced = (ji == 0) | (ji == cur) | (ji == cur - 1)
        score = jnp.where(forced, neg_inf, jnp.where(causal, imp_t, MASK_VALUE))
        score = jnp.where(ji < n_sel, score, neg_inf)
        picked = jnp.where(forced, 1.0, 0.0)
        for _ in range(SEL_TOPK - SEL_FORCED):
            mx = jnp.max(score, axis=0, keepdims=True)
            first = jnp.min(jnp.where(score == mx, ji, LANES), axis=0, keepdims=True)
            hit = ji == first
            picked = jnp.where(hit, 1.0, picked)
            score = jnp.where(hit, neg_inf, score)
        picked = jnp.where(ji < cur, picked, 0.0)
        sel_ref[0, 0, orow, :] = picked.T.astype(sel_ref.dtype)


def nsa_cmp_select(q, kc2, vc_aug, nslopes, batch, seq, q_col_block):
    nstep = seq // (Q_SUB * Q_BLOCK)
    rows = Q_SUB * Q_BLOCK
    ncp = seq // CMP_STRIDE
    n_cmp = (seq - CMP_LEN) // CMP_STRIDE + 1
    n_sel = seq // SEL_LEN
    gw = PAIRS * LANES
    kern = functools.partial(_nsa_cmp_kernel, ncp=ncp, n_cmp=n_cmp, n_sel=n_sel)
    return pl.pallas_call(
        kern,
        grid=(batch, KV_GROUPS, nstep),
        in_specs=[
            pl.BlockSpec(memory_space=pltpu.SMEM),
            pl.BlockSpec((rows, gw), lambda b, g, c: (b * nstep + c, q_col_block + g)),
            pl.BlockSpec((1, 1, ncp, LANES), lambda b, g, c: (b, g, 0, 0)),
            pl.BlockSpec((1, 1, ncp, 3 * LANES), lambda b, g, c: (b, g, 0, 0)),
        ],
        out_specs=[
            pl.BlockSpec((rows, gw), lambda b, g, c: (b * nstep + c, g)),
            pl.BlockSpec((1, 1, rows, LANES), lambda b, g, c: (b, g, c, 0)),
        ],
        out_shape=[
            jax.ShapeDtypeStruct((batch * seq, N_HEADS * HEAD_DIM), F32),
            jax.ShapeDtypeStruct((batch, KV_GROUPS, seq, LANES), MXU_DTYPE),
        ],
        scratch_shapes=[pltpu.VMEM((Q_SUB, ROWS, LANES), MXU_DTYPE), pltpu.VMEM((Q_SUB, ROWS, ncp), MXU_DTYPE),
                        pltpu.VMEM((Q_SUB, ROWS, 3 * LANES), F32)],
        compiler_params=_params("parallel", "parallel", "arbitrary"),
        name="nsa_cmp_select",
    )(nslopes, q, kc2, vc_aug)


N_FEAT = 6


def sel_query_features():
    s = -jnp.asarray(_alibi_neg_slopes())
    s1 = s.astype(MXU_DTYPE).astype(F32)
    s2 = (s - s1).astype(MXU_DTYPE).astype(F32)
    s3 = (s - s1 - s2).astype(MXU_DTYPE).astype(F32)
    feat = jnp.zeros((N_HEADS, LANES), F32).at[:, HEAD_DIM:HEAD_DIM + N_FEAT].set(
        jnp.stack([s1, s2, s3, s1, s2, s3], axis=1))
    feat = feat.at[:, HEAD_DIM + N_FEAT].set(MASK_VALUE)
    order = np.array([[_head_of_row_block(g, rb) for rb in range(GROUP)] for g in range(KV_GROUPS)])
    return feat[order]


def sel_key_features(seq):
    pos = np.arange(seq)
    kk = pos % SEL_STEP
    f = np.zeros((seq + SEL_STEP, 2 * LANES), np.float32)
    f[:seq, HEAD_DIM:HEAD_DIM + 3] = (SEL_LEN * (kk // SEL_LEN))[:, None]
    f[:seq, HEAD_DIM + 3:HEAD_DIM + 6] = (kk % SEL_LEN)[:, None]
    f[seq:, HEAD_DIM + N_FEAT] = 1.0
    f[pos, LANES + pos // SEL_LEN] = 1.0
    return f


def _nsa_sel_kernel(delta_ref, qfeat_ref, q_ref, sel_ref, k_in, v_in, kfeat_ref, gate_ref, ex_ref, oc_ref, ow_ref,
                    o_ref, qa_ref, m_ref, acc_ref, s_a, s_b, p_a, p_b, al_a, al_b, k_ref, v_ref):
    grp = pl.program_id(1)
    c = pl.program_id(2)
    t0 = c * Q_BLOCK
    seq = k_in.shape[0]

    @pl.when(c == 0)
    def _():
        def put_k(r0, y):
            k_ref[pl.ds(r0, KV_CHUNK), 0:LANES] = y
            k_ref[pl.ds(r0, KV_CHUNK), LANES:2 * LANES] = kfeat_ref[pl.ds(r0, KV_CHUNK), LANES:2 * LANES]

        def put_v(r0, y):
            v_ref[pl.ds(r0, KV_CHUNK), 0:LANES] = y
            v_ref[pl.ds(r0, KV_CHUNK), LANES:2 * LANES] = jnp.ones((KV_CHUNK, LANES), v_ref.dtype)

        _unpack_group(k_in, grp, put_k, fill_ref=kfeat_ref)
        _unpack_group(v_in, grp, put_v)
        k_ref[seq:seq + SEL_STEP, :] = kfeat_ref[seq:seq + SEL_STEP, :]
        v_ref[seq:seq + SEL_STEP, :] = jnp.zeros((SEL_STEP, 2 * LANES), v_ref.dtype)
    lane = lax.broadcasted_iota(jnp.int32, (Q_BLOCK, LANES), 1)
    low = lane < HEAD_DIM
    selneg = ((1.0 - sel_ref[0, 0].astype(F32)) * MASK_VALUE).astype(qa_ref.dtype)
    for p in range(PAIRS):
        qp = q_ref[:, p * LANES:(p + 1) * LANES].astype(F32) * (HEAD_DIM ** -0.5)
        for rb, src in ((p, qp), (PAIRS + p, pltpu.roll(qp, HEAD_DIM, axis=1))):
            qa_ref[_rows(rb), 0:LANES] = jnp.where(low, src, qfeat_ref[0, rb:rb + 1, :]).astype(qa_ref.dtype)
            qa_ref[_rows(rb), LANES:2 * LANES] = selneg
    n_steps = t0 // SEL_STEP + 1
    n_pad_step = seq // SEL_STEP
    dot_nt = (((1,), (1,)), ((), ()))

    def key_start(step):
        return pl.multiple_of(jnp.clip(step, 0, n_pad_step) * SEL_STEP, SEL_STEP)

    def scores(step, s_out):
        k = k_ref[pl.ds(key_start(step), SEL_STEP), :]
        s_out[...] = lax.dot_general(qa_ref[...], k, dot_nt, preferred_element_type=F32)

    def softmax(s_in, p_out, al_out):
        for rb in range(GROUP):
            m_prev = m_ref[_rows(rb), :] - delta_ref[_head_of_row_block(grp, rb)]
            m_new = jnp.maximum(m_prev, jnp.max(s_in[_rows(rb), :], axis=1, keepdims=True))
            al_out[_rows(rb), :] = jnp.exp(m_prev - m_new)
            m_ref[_rows(rb), :] = m_new
        for rb in range(GROUP):
            p_out[_rows(rb), :] = jnp.exp(s_in[_rows(rb), :] - _rep(m_ref[_rows(rb), :], SEL_STEP)).astype(p_out.dtype)

    def values(step, p_in, al_in):
        v = v_ref[pl.ds(key_start(step), SEL_STEP), :]
        pv = jnp.dot(p_in[...], v, preferred_element_type=F32)
        alpha = al_in[...]
        acc_ref[...] = acc_ref[...] * jnp.concatenate([alpha, alpha], axis=1) + pv

    def even_half(t):
        scores(t, s_a)
        softmax(s_b, p_b, al_b)
        values(t - 2, p_a, al_a)

    def odd_half(t):
        scores(t, s_b)
        softmax(s_a, p_a, al_a)
        values(t - 2, p_b, al_b)

    scores(0, s_a)
    own = pl.multiple_of(t0, Q_BLOCK)
    qi = lax.broadcasted_iota(jnp.int32, (Q_BLOCK, Q_BLOCK), 0)
    ki = lax.broadcasted_iota(jnp.int32, (Q_BLOCK, Q_BLOCK), 1)
    own_bias = jnp.where((ki <= qi) & (ki // SEL_LEN == qi // SEL_LEN), 0.0, MASK_VALUE)
    s_own = lax.dot_general(qa_ref[:, 0:LANES], k_ref[pl.ds(own, Q_BLOCK), 0:LANES], dot_nt,
                            preferred_element_type=F32)
    scores(1, s_b)
    back = jnp.full((Q_BLOCK, LANES), n_steps, jnp.int32).astype(F32)
    p_own = []
    for rb in range(GROUP):
        s = s_own[_rows(rb)] + own_bias
        m = jnp.max(s, axis=1, keepdims=True)
        p_own.append(jnp.exp(s - m).astype(MXU_DTYPE))
        m_ref[_rows(rb), :] = m + back * delta_ref[_head_of_row_block(grp, rb)]
    softmax(s_a, p_a, al_a)
    acc_ref[...] = jnp.dot(jnp.concatenate(p_own, axis=0), v_ref[pl.ds(own, Q_BLOCK), :],
                           preferred_element_type=F32)

    def quad(j, carry):
        even_half(4 * j + 2)
        odd_half(4 * j + 3)
        even_half(4 * j + 4)
        odd_half(4 * j + 5)
        return carry

    lax.fori_loop(0, n_steps // 4, quad, 0)
    rest = 4 * (n_steps // 4) + 2

    @pl.when(n_steps % 4 >= 2)
    def _():
        even_half(rest)
        odd_half(rest + 1)

    @pl.when(n_steps % 2 == 1)
    def _():
        even_half(n_steps + 1)
    o_sel = _pairs(lambda rb: acc_ref[_rows(rb), 0:LANES], lambda rb: acc_ref[_rows(rb), LANES:2 * LANES])

    gw = PAIRS * LANES
    sig = jax.nn.sigmoid(gate_ref[...])
    hi = sig.astype(MXU_DTYPE)
    lo = (sig - hi.astype(F32)).astype(MXU_DTYPE)
    ex = ex_ref[0]
    g = jnp.dot(hi, ex, preferred_element_type=F32) + jnp.dot(lo, ex, preferred_element_type=F32)
    for p in range(PAIRS):
        cols = slice(p * LANES, (p + 1) * LANES)
        out = (g[:, cols] * oc_ref[:, cols]
               + g[:, gw + p * LANES:gw + (p + 1) * LANES] * o_sel[p]
               + g[:, 2 * gw + p * LANES:2 * gw + (p + 1) * LANES] * ow_ref[:, cols])
        o_ref[:, cols] = out.astype(o_ref.dtype)


def _gate_expansion():
    gw = PAIRS * LANES
    ex = np.zeros((KV_GROUPS, LANES, 3 * gw), np.float32)
    for g in range(KV_GROUPS):
        for hl in range(GROUP):
            for i in range(3):
                ex[g, 3 * (g * GROUP + hl) + i, i * gw + hl * HEAD_DIM:i * gw + (hl + 1) * HEAD_DIM] = 1.0
    return ex


def nsa_sel_attention(qkv, sel, gate, o_cmp, o_win, batch, seq, k_col, v_col):
    nblk = seq // Q_BLOCK
    gw = PAIRS * LANES
    kd = KV_GROUPS * HEAD_DIM
    deltas = jnp.asarray((-_alibi_neg_slopes() * SEL_STEP).astype(np.float32))
    ex = jnp.asarray(_gate_expansion(), MXU_DTYPE)
    kfeat = jnp.asarray(sel_key_features(seq), MXU_DTYPE)
    blk = lambda b, g, c: (b * nblk + c, g)
    return pl.pallas_call(
        _nsa_sel_kernel,
        grid=(batch, KV_GROUPS, nblk),
        in_specs=[
            pl.BlockSpec(memory_space=pltpu.SMEM),
            pl.BlockSpec((1, GROUP, LANES), lambda b, g, c: (g, 0, 0)),
            pl.BlockSpec((Q_BLOCK, gw), blk),
            pl.BlockSpec((1, 1, Q_BLOCK, LANES), lambda b, g, c: (b, g, c, 0)),
            pl.BlockSpec((seq, kd), lambda b, g, c: (b, k_col)),
            pl.BlockSpec((seq, kd), lambda b, g, c: (b, v_col)),
            pl.BlockSpec((seq + SEL_STEP, 2 * LANES), lambda b, g, c: (0, 0)),
            pl.BlockSpec((Q_BLOCK, LANES), lambda b, g, c: (b * nblk + c, 0)),
            pl.BlockSpec((1, LANES, 3 * gw), lambda b, g, c: (g, 0, 0)),
            pl.BlockSpec((Q_BLOCK, gw), blk),
            pl.BlockSpec((Q_BLOCK, gw), blk),
        ],
        out_specs=pl.BlockSpec((Q_BLOCK, gw), blk),
        out_shape=jax.ShapeDtypeStruct((batch * seq, N_HEADS * HEAD_DIM), MXU_DTYPE),
        scratch_shapes=[
            pltpu.VMEM((ROWS, 2 * LANES), MXU_DTYPE),
            pltpu.VMEM((ROWS, LANES), F32),
            pltpu.VMEM((ROWS, 2 * LANES), F32),
            pltpu.VMEM((ROWS, SEL_STEP), F32),
            pltpu.VMEM((ROWS, SEL_STEP), F32),
            pltpu.VMEM((ROWS, SEL_STEP), MXU_DTYPE),
            pltpu.VMEM((ROWS, SEL_STEP), MXU_DTYPE),
            pltpu.VMEM((ROWS, LANES), F32),
            pltpu.VMEM((ROWS, LANES), F32),
            pltpu.VMEM((seq + SEL_STEP, 2 * LANES), MXU_DTYPE),
            pltpu.VMEM((seq + SEL_STEP, 2 * LANES), MXU_DTYPE),
        ],
        compiler_params=_params("parallel", "parallel", "arbitrary"),
        name="nsa_sel_attention",
    )(deltas, sel_query_features(), qkv, sel, qkv, qkv, kfeat, gate, ex, o_cmp, o_win)


def _nsa_win_kernel(nslope_ref, q_ref, k_in, v_in, o_ref, qs_ref, bias_scr, k_ref, v_ref):
    grp = pl.program_id(1)
    c = pl.program_id(2)
    span = NSA_WINDOW + Q_BLOCK
    lead = NSA_WINDOW // Q_BLOCK
    for j in range(Q_SUB):
        _build_q_stack(q_ref, qs_ref.at[j], j * Q_BLOCK)
    nslope = lambda rb: nslope_ref[_head_of_row_block(grp, rb)]

    def write(j, outs):
        for p, o in enumerate(outs):
            o_ref[j * Q_BLOCK:(j + 1) * Q_BLOCK, p * LANES:(p + 1) * LANES] = o

    @pl.when(c == 0)
    def _():
        _unpack_kv(k_in, v_in, grp, k_ref, v_ref)
        d = _pos_tiles(NSA_WINDOW, 0, span)
        for rb in range(GROUP):
            bias_scr[rb] = _band_bias(nslope(rb), d, NSA_WINDOW)

    @pl.when(c < lead // Q_SUB)
    def _():
        for j in range(Q_SUB):
            d = _pos_tiles((c * Q_SUB + j) * Q_BLOCK, 0, span)
            write(j, _window_attend(qs_ref.at[j], k_ref[0:span, :], v_ref[0:span, :],
                                    lambda rb: _band_bias(nslope(rb), d, NSA_WINDOW)))

    @pl.when(c >= lead // Q_SUB)
    def _():
        for j in range(Q_SUB):
            start = pl.multiple_of((c * Q_SUB + j) * Q_BLOCK - NSA_WINDOW, Q_BLOCK)
            write(j, _window_attend(qs_ref.at[j], k_ref[pl.ds(start, span), :], v_ref[pl.ds(start, span), :],
                                    lambda rb: bias_scr[rb]))


def nsa_win_attention(qkv, nslopes, batch, seq, k_col, v_col):
    nstep = seq // (Q_SUB * Q_BLOCK)
    rows = Q_SUB * Q_BLOCK
    gw = PAIRS * LANES
    kd = KV_GROUPS * HEAD_DIM
    span = NSA_WINDOW + Q_BLOCK
    return pl.pallas_call(
        _nsa_win_kernel,
        grid=(batch, KV_GROUPS, nstep),
        in_specs=[
            pl.BlockSpec(memory_space=pltpu.SMEM),
            pl.BlockSpec((rows, gw), lambda b, g, c: (b * nstep + c, g)),
            pl.BlockSpec((seq, kd), lambda b, g, c: (b, k_col)),
            pl.BlockSpec((seq, kd), lambda b, g, c: (b, v_col)),
        ],
        out_specs=pl.BlockSpec((rows, gw), lambda b, g, c: (b * nstep + c, g)),
        out_shape=jax.ShapeDtypeStruct((batch * seq, N_HEADS * HEAD_DIM), F32),
        scratch_shapes=[
            pltpu.VMEM((Q_SUB, ROWS, LANES), MXU_DTYPE),
            pltpu.VMEM((GROUP, Q_BLOCK, span), F32),
            pltpu.VMEM((seq, LANES), MXU_DTYPE),
            pltpu.VMEM((seq, 2 * LANES), MXU_DTYPE),
        ],
        compiler_params=_params("parallel", "parallel", "arbitrary"),
        name="nsa_win_attention",
    )(nslopes, qkv, qkv, qkv)


def _overlap_matrix(seq):
    ncp = seq // CMP_STRIDE
    n_cmp = (seq - CMP_LEN) // CMP_STRIDE + 1
    cs = np.arange(n_cmp) * CMP_STRIDE
    ss = np.arange(seq // SEL_LEN) * SEL_LEN
    ov = (cs[:, None] < ss[None, :] + SEL_LEN) & (cs[:, None] + CMP_LEN > ss[None, :])
    out = np.zeros((ncp, LANES), np.float32)
    out[:n_cmp, :seq // SEL_LEN] = ov
    return out


def _swa_layer(h, x, w_in, b_in, sinks, w_o, b_o, g_post, g_next, nslopes, batch, seq):
    hd = N_HEADS * HEAD_DIM
    kd = KV_GROUPS * HEAD_DIM
    qkv = matmul_bias(h, w_in.astype(MXU_DTYPE), b_in, MXU_DTYPE, name="swa_in_proj")
    o = swa_attention(qkv, nslopes, sinks.astype(F32), batch, seq)
    return matmul_norm_res(o, w_o.astype(MXU_DTYPE), b_o, g_post, g_next, x, name="swa_out_proj")


def _nsa_layer(h, x, w_in, cmp_pe, cmp_w1, cmp_b1, cmp_w2, cmp_b2, w_o, g_post, g_next, nslopes, batch, seq):
    hd = N_HEADS * HEAD_DIM
    kd = KV_GROUPS * HEAD_DIM
    t = batch * seq
    ncp = seq // CMP_STRIDE
    qkv = matmul_bias(h, w_in[:, :hd + 6 * kd].astype(MXU_DTYPE), jnp.zeros((hd + 6 * kd,), F32), MXU_DTYPE,
                      name="nsa_in_proj")
    n_gate = 3 * N_HEADS
    w_gate = jnp.pad(w_in[:, hd + 6 * kd:], ((0, 0), (0, LANES - n_gate))).astype(MXU_DTYPE)
    gate = matmul_bias(h, w_gate, jnp.zeros((LANES,), F32), F32, name="nsa_gate_proj")

    def kv(i):
        return qkv[:, hd + i * kd:hd + (i + 1) * kd]

    def slabs(a):
        return a.reshape(batch, seq, KV_GROUPS, HEAD_DIM).transpose(0, 2, 1, 3).reshape(
            batch, KV_GROUPS, ncp, CMP_STRIDE * HEAD_DIM)

    z = jnp.stack([slabs(kv(0)), slabs(kv(1))])
    half = CMP_STRIDE * HEAD_DIM
    cmp_out = compress(z, cmp_pe.reshape(2, 2, half).astype(F32), cmp_w1.astype(MXU_DTYPE),
                       cmp_b1.reshape(2, 1, -1), cmp_w2.astype(MXU_DTYPE), cmp_b2.reshape(2, 1, -1))
    kcm = cmp_out[0].astype(MXU_DTYPE)
    vcm = cmp_out[1].astype(MXU_DTYPE)
    kc2 = jnp.concatenate([kcm, kcm], axis=-1)
    ov = jnp.broadcast_to(jnp.asarray(_overlap_matrix(seq), MXU_DTYPE), (batch, KV_GROUPS, ncp, LANES))
    vc_aug = jnp.concatenate([vcm, vcm, jnp.ones((batch, KV_GROUPS, ncp, LANES), MXU_DTYPE), ov], axis=-1)

    o_cmp, sel = nsa_cmp_select(qkv, kc2, vc_aug, nslopes, batch, seq, 0)
    kv_col = lambda i: hd // kd + i
    o_win = nsa_win_attention(qkv, nslopes, batch, seq, kv_col(4), kv_col(5))
    o = nsa_sel_attention(qkv, sel, gate, o_cmp, o_win, batch, seq, kv_col(2), kv_col(3))
    return matmul_norm_res(o, w_o.astype(MXU_DTYPE), jnp.zeros((w_o.shape[1],), F32), g_post, g_next, x,
                           name="nsa_out_proj")


def kernel(x, norm_g, swa_w_in, swa_b_in, swa_sinks, swa_w_o, swa_b_o, nsa_w_in, nsa_cmp_pe, nsa_cmp_w1, nsa_cmp_b1, nsa_cmp_w2, nsa_cmp_b2, nsa_w_o, ffn_w_gate, ffn_w_up, ffn_conv_w, ffn_conv_b, ffn_w_down):
    batch, seq, d = x.shape
    depth = norm_g.shape[0]
    nslopes = jnp.asarray(_alibi_neg_slopes())
    xf = x.reshape(batch * seq, d)
    h = rms_cast(xf, norm_g[0, 0])
    for i in range(depth):
        g = norm_g[i]
        j = i // 2
        if i % 2 == 0:
            xf, h = _swa_layer(h, xf, swa_w_in[j], swa_b_in[j], swa_sinks[j], swa_w_o[j], swa_b_o[j],
                               g[1], g[2], nslopes, batch, seq)
        else:
            xf, h = _nsa_layer(h, xf, nsa_w_in[j], nsa_cmp_pe[j], nsa_cmp_w1[j], nsa_cmp_b1[j], nsa_cmp_w2[j],
                               nsa_cmp_b2[j], nsa_w_o[j], g[1], g[2], nslopes, batch, seq)
        act = ffn_up(h, ffn_w_gate, ffn_w_up, i, ffn_conv_w[i], ffn_conv_b[i], seq)
        g_next = norm_g[i + 1, 0] if i + 1 < depth else jnp.ones((d,), F32)
        xf, h = matmul_norm_res(act, ffn_w_down[i].astype(MXU_DTYPE), jnp.zeros((d,), F32), g[3], g_next, xf,
                                tn=256, name="ffn_down")
    return xf.reshape(batch, seq, d)
```

```python
import functools

import numpy as np
import jax
import jax.numpy as jnp
from jax import lax
from jax.experimental import pallas as pl
from jax.experimental.pallas import tpu as pltpu

F32 = jnp.float32
MXU_DTYPE = jnp.bfloat16

N_HEADS = 32
HEAD_DIM = 64
KV_GROUPS = 4
GROUP = N_HEADS // KV_GROUPS
PAIRS = GROUP // 2
LANES = 128
Q_BLOCK = 128
ROWS = GROUP * Q_BLOCK
SWA_WINDOW = 128
CMP_LEN = 32
CMP_STRIDE = 16
SEL_LEN = 64
SEL_TOPK = 16
SEL_FORCED = 3
SEL_STEP = 512
NSA_WINDOW = 512
CONV_WIDTH = 3
RMS_EPS = 1e-6
MASK_VALUE = -1e30
VMEM_LIMIT = 56 * 1024 * 1024


def _params(*sem):
    return pltpu.CompilerParams(dimension_semantics=sem, vmem_limit_bytes=VMEM_LIMIT)


def _alibi_neg_slopes():
    return (-np.exp2(-8.0 * np.arange(1, N_HEADS + 1, dtype=np.float64) / N_HEADS)).astype(np.float32)


def _rms_cast_kernel(x_ref, g_ref, o_ref):
    x = x_ref[...]
    ms = jnp.mean(x * x, axis=-1, keepdims=True)
    o_ref[...] = (x * lax.rsqrt(ms + RMS_EPS) * g_ref[...]).astype(o_ref.dtype)


def rms_cast(x, g, tm=512):
    t, d = x.shape
    return pl.pallas_call(
        _rms_cast_kernel,
        grid=(t // tm,),
        in_specs=[pl.BlockSpec((tm, d), lambda i: (i, 0)), pl.BlockSpec((1, d), lambda i: (0, 0))],
        out_specs=pl.BlockSpec((tm, d), lambda i: (i, 0)),
        out_shape=jax.ShapeDtypeStruct((t, d), MXU_DTYPE),
        compiler_params=_params("parallel"),
        name="rms_cast",
    )(x, g.reshape(1, d))


def _matmul_kernel(a_ref, w_ref, b_ref, o_ref):
    acc = jnp.dot(a_ref[...], w_ref[...], preferred_element_type=F32)
    o_ref[...] = (acc + b_ref[...]).astype(o_ref.dtype)


def matmul_bias(a, w, b, out_dtype, tm=1024, tn=512, name="matmul_bias"):
    t, k = a.shape
    n = w.shape[1]
    tn = min(tn, n)
    return pl.pallas_call(
        _matmul_kernel,
        grid=(t // tm, n // tn),
        in_specs=[
            pl.BlockSpec((tm, k), lambda i, j: (i, 0)),
            pl.BlockSpec((k, tn), lambda i, j: (0, j)),
            pl.BlockSpec((1, tn), lambda i, j: (0, j)),
        ],
        out_specs=pl.BlockSpec((tm, tn), lambda i, j: (i, j)),
        out_shape=jax.ShapeDtypeStruct((t, n), out_dtype),
        compiler_params=_params("parallel", "arbitrary"),
        name=name,
    )(a, w, b.reshape(1, n))


def _mm_norm_res_kernel(a_ref, w_ref, b_ref, gpost_ref, gnext_ref, x_ref, xo_ref, ho_ref, y_scr, *, nj, tn, n):
    j = pl.program_id(1)
    y_scr[j] = jnp.dot(a_ref[...], w_ref[...], preferred_element_type=F32) + b_ref[...]

    @pl.when(j == nj - 1)
    def _():
        ss = jnp.sum(y_scr[0] * y_scr[0], axis=1, keepdims=True)
        for jj in range(1, nj):
            ss = ss + jnp.sum(y_scr[jj] * y_scr[jj], axis=1, keepdims=True)
        r = lax.rsqrt(ss / n + RMS_EPS)
        ss2 = jnp.zeros_like(ss)
        for jj in range(nj):
            cols = slice(jj * tn, (jj + 1) * tn)
            xn = x_ref[:, cols] + y_scr[jj] * r * gpost_ref[:, cols]
            xo_ref[:, cols] = xn
            ss2 = ss2 + jnp.sum(xn * xn, axis=1, keepdims=True)
        r2 = lax.rsqrt(ss2 / n + RMS_EPS)
        for jj in range(nj):
            cols = slice(jj * tn, (jj + 1) * tn)
            ho_ref[:, cols] = (xo_ref[:, cols] * r2 * gnext_ref[:, cols]).astype(ho_ref.dtype)


def matmul_norm_res(a, w, b, g_post, g_next, x, tm=512, tn=512, name="matmul_norm_res"):
    t, k = a.shape
    n = w.shape[1]
    nj = n // tn
    kern = functools.partial(_mm_norm_res_kernel, nj=nj, tn=tn, n=n)
    return pl.pallas_call(
        kern,
        grid=(t // tm, nj),
        in_specs=[
            pl.BlockSpec((tm, k), lambda i, j: (i, 0)),
            pl.BlockSpec((k, tn), lambda i, j: (0, j)),
            pl.BlockSpec((1, tn), lambda i, j: (0, j)),
            pl.BlockSpec((1, n), lambda i, j: (0, 0)),
            pl.BlockSpec((1, n), lambda i, j: (0, 0)),
            pl.BlockSpec((tm, n), lambda i, j: (i, 0)),
        ],
        out_specs=[
            pl.BlockSpec((tm, n), lambda i, j: (i, 0)),
            pl.BlockSpec((tm, n), lambda i, j: (i, 0)),
        ],
        out_shape=[jax.ShapeDtypeStruct((t, n), F32), jax.ShapeDtypeStruct((t, n), MXU_DTYPE)],
        scratch_shapes=[pltpu.VMEM((nj, tm, tn), F32)],
        compiler_params=_params("parallel", "arbitrary"),
        name=name,
    )(a, w, b.reshape(1, n), g_post.reshape(1, n), g_next.reshape(1, n), x)


FIX_ROWS = 16
CARRY_ROWS = 8


def _ffn_up_kernel(h_ref, wg_ref, wu_ref, cw_ref, cb_ref, o_ref, wg_scr, wu_scr, carry_scr, *, tm, tiles_per_seq):
    i = pl.program_id(1)

    @pl.when(i == 0)
    def _():
        wg_scr[...] = wg_ref[...].astype(wg_scr.dtype)
        wu_scr[...] = wu_ref[...].astype(wu_scr.dtype)

    h = h_ref[...]
    gate = jnp.dot(h, wg_scr[...], preferred_element_type=F32)
    up = jnp.dot(h, wu_scr[...], preferred_element_type=F32)
    w0 = cw_ref[0:1, :]
    w1 = cw_ref[1:2, :]
    w2 = cw_ref[2:3, :]
    b = cb_ref[...]
    a = b + pltpu.roll(gate, 2, axis=0) * w0
    a = a + pltpu.roll(gate, 1, axis=0) * w1
    a = a + gate * w2
    o_ref[...] = (jax.nn.silu(a) * up).astype(o_ref.dtype)

    seq_start = (i % tiles_per_seq) == 0
    prev = jnp.where(seq_start, 0.0, carry_scr[...])
    head = gate[0:FIX_ROWS]
    ext = jnp.concatenate([prev, head], axis=0)
    af = b + ext[CARRY_ROWS - 2:CARRY_ROWS - 2 + FIX_ROWS] * w0
    af = af + ext[CARRY_ROWS - 1:CARRY_ROWS - 1 + FIX_ROWS] * w1
    af = af + head * w2
    o_ref[0:FIX_ROWS, :] = (jax.nn.silu(af) * up[0:FIX_ROWS]).astype(o_ref.dtype)
    carry_scr[...] = gate[tm - CARRY_ROWS:tm]


def ffn_up(h, wg, wu, layer, conv_w, conv_b, seq, tm=1024, tn=512):
    t, k = h.shape
    n = wg.shape[2]
    kern = functools.partial(_ffn_up_kernel, tm=tm, tiles_per_seq=seq // tm)
    return pl.pallas_call(
        kern,
        grid=(n // tn, t // tm),
        in_specs=[
            pl.BlockSpec((tm, k), lambda j, i: (i, 0)),
            pl.BlockSpec((None, k, tn), lambda j, i: (layer, 0, j)),
            pl.BlockSpec((None, k, tn), lambda j, i: (layer, 0, j)),
            pl.BlockSpec((CONV_WIDTH, tn), lambda j, i: (0, j)),
            pl.BlockSpec((1, tn), lambda j, i: (0, j)),
        ],
        out_specs=pl.BlockSpec((tm, tn), lambda j, i: (i, j)),
        out_shape=jax.ShapeDtypeStruct((t, n), MXU_DTYPE),
        scratch_shapes=[
            pltpu.VMEM((k, tn), MXU_DTYPE),
            pltpu.VMEM((k, tn), MXU_DTYPE),
            pltpu.VMEM((CARRY_ROWS, tn), F32),
        ],
        compiler_params=_params("arbitrary", "arbitrary"),
        name="ffn_up",
    )(h, wg, wu, conv_w, conv_b.reshape(1, n))


def _head_of_row_block(group, rb):
    return group * GROUP + 2 * (rb % PAIRS) + rb // PAIRS


Q_SUB = 4


def _build_q_stack(q_ref, qs_ref, row0=0):
    lane = lax.broadcasted_iota(jnp.int32, (Q_BLOCK, LANES), 1)
    even = lane < HEAD_DIM
    for p in range(PAIRS):
        qp = q_ref[row0:row0 + Q_BLOCK, p * LANES:(p + 1) * LANES].astype(F32) * (HEAD_DIM ** -0.5)
        qs_ref[p * Q_BLOCK:(p + 1) * Q_BLOCK, :] = jnp.where(even, qp, 0.0).astype(qs_ref.dtype)
        qs_ref[(PAIRS + p) * Q_BLOCK:(PAIRS + p + 1) * Q_BLOCK, :] = jnp.where(even, 0.0, qp).astype(qs_ref.dtype)


def _rep(x, size):
    return x if size == LANES else jnp.concatenate([x] * (size // LANES), axis=1)


def _pairs(num, den):
    lane = lax.broadcasted_iota(jnp.int32, (Q_BLOCK, LANES), 1)
    even = lane < HEAD_DIM
    outs = []
    for p in range(PAIRS):
        oe = num(p) / jnp.maximum(den(p), 1e-30)
        oo = num(PAIRS + p) / jnp.maximum(den(PAIRS + p), 1e-30)
        outs.append(jnp.where(even, oe, oo))
    return outs


def _rows(rb):
    return slice(rb * Q_BLOCK, (rb + 1) * Q_BLOCK)


def _band_bias(nslope, d, window):
    return jnp.where((d >= 0) & (d < window), nslope * d.astype(F32), MASK_VALUE)


def _window_attend(qs_ref, k, v, bias, extra_logit=None):
    half = ROWS // 2
    dot_nt = (((1,), (1,)), ((), ()))
    s_halves = [lax.dot_general(qs_ref[hh * half:(hh + 1) * half, :], k, dot_nt, preferred_element_type=F32)
                for hh in range(2)]
    ps, extras = [], []
    for rb in range(GROUP):
        lo = (rb % PAIRS) * Q_BLOCK
        s = s_halves[rb // PAIRS][lo:lo + Q_BLOCK] + bias(rb)
        m = jnp.max(s, axis=1, keepdims=True)
        if extra_logit is not None:
            m = jnp.maximum(m, extra_logit(rb))
            extras.append(jnp.exp(extra_logit(rb) - m))
        ps.append(jnp.exp(s - m).astype(MXU_DTYPE))
    r_halves = [jnp.dot(jnp.concatenate(ps[hh * PAIRS:(hh + 1) * PAIRS], axis=0), v, preferred_element_type=F32)
                for hh in range(2)]

    def part(rb, cols):
        lo = (rb % PAIRS) * Q_BLOCK
        return r_halves[rb // PAIRS][lo:lo + Q_BLOCK, cols]

    num = lambda rb: part(rb, slice(0, LANES))
    if extra_logit is None:
        den = lambda rb: part(rb, slice(LANES, 2 * LANES))
    else:
        den = lambda rb: part(rb, slice(LANES, 2 * LANES)) + extras[rb]
    return _pairs(num, den)


KV_CHUNK = 1024


def _unpack_group(kv_ref, grp, write_chunk, fill_ref=None):
    n = kv_ref.shape[0]
    for gi in range(KV_GROUPS):
        def fill(gi=gi):
            def body(i, carry):
                r0 = pl.multiple_of(i * KV_CHUNK, KV_CHUNK)
                x = kv_ref[pl.ds(r0, KV_CHUNK), (gi // 2) * LANES:(gi // 2 + 1) * LANES]
                xi = pltpu.bitcast(x, jnp.int32)
                xr = pltpu.roll(xi, HEAD_DIM, axis=1)
                low = lax.broadcasted_iota(jnp.int32, xi.shape, 1) < HEAD_DIM
                if fill_ref is None:
                    y = jnp.where(low, xi, xr) if gi % 2 == 0 else jnp.where(low, xr, xi)
                else:
                    f = pltpu.bitcast(fill_ref[pl.ds(r0, KV_CHUNK), 0:LANES], jnp.int32)
                    y = jnp.where(low, xi if gi % 2 == 0 else xr, f)
                write_chunk(r0, pltpu.bitcast(y, x.dtype))
                return carry

            lax.fori_loop(0, n // KV_CHUNK, body, 0)

        pl.when(grp == gi)(fill)


def _pos_tiles(t0, start, size):
    qi = lax.broadcasted_iota(jnp.int32, (Q_BLOCK, size), 0)
    ki = lax.broadcasted_iota(jnp.int32, (Q_BLOCK, size), 1)
    return (t0 - start) + (qi - ki)


def _unpack_kv(k_in, v_in, grp, k_ref, v_ref):
    def put_k(r0, y):
        k_ref[pl.ds(r0, KV_CHUNK), :] = y

    def put_v(r0, y):
        v_ref[pl.ds(r0, KV_CHUNK), 0:LANES] = y
        v_ref[pl.ds(r0, KV_CHUNK), LANES:2 * LANES] = jnp.ones((KV_CHUNK, LANES), v_ref.dtype)

    _unpack_group(k_in, grp, put_k)
    _unpack_group(v_in, grp, put_v)


def _swa_kernel(nslope_ref, sink_ref, q_ref, k_in, v_in, o_ref, qs_ref, bias_scr, k_ref, v_ref):
    grp = pl.program_id(1)
    c = pl.program_id(2)
    span = SWA_WINDOW + Q_BLOCK
    for j in range(Q_SUB):
        _build_q_stack(q_ref, qs_ref.at[j], j * Q_BLOCK)
    sink = lambda rb: sink_ref[_head_of_row_block(grp, rb)]

    def write(j, outs):
        for p, o in enumerate(outs):
            o_ref[j * Q_BLOCK:(j + 1) * Q_BLOCK, p * LANES:(p + 1) * LANES] = o.astype(o_ref.dtype)

    def later_block(j):
        start = pl.multiple_of((c * Q_SUB + j) * Q_BLOCK - SWA_WINDOW, Q_BLOCK)
        write(j, _window_attend(qs_ref.at[j], k_ref[pl.ds(start, span), :], v_ref[pl.ds(start, span), :],
                                lambda rb: bias_scr[rb], sink))

    @pl.when(c == 0)
    def _():
        _unpack_kv(k_in, v_in, grp, k_ref, v_ref)
        d = _pos_tiles(SWA_WINDOW, 0, span)
        for rb in range(GROUP):
            bias_scr[rb] = _band_bias(nslope_ref[_head_of_row_block(grp, rb)], d, SWA_WINDOW)

    @pl.when(c == 0)
    def _():
        write(0, _window_attend(qs_ref.at[0], k_ref[0:Q_BLOCK, :], v_ref[0:Q_BLOCK, :],
                                lambda rb: bias_scr[rb, :, SWA_WINDOW:span], sink))
        for j in range(1, Q_SUB):
            later_block(j)

    @pl.when(c > 0)
    def _():
        for j in range(Q_SUB):
            later_block(j)


def swa_attention(qkv, nslopes, sinks, batch, seq):
    nstep = seq // (Q_SUB * Q_BLOCK)
    rows = Q_SUB * Q_BLOCK
    gw = PAIRS * LANES
    kd = KV_GROUPS * HEAD_DIM
    span = SWA_WINDOW + Q_BLOCK
    k_col = N_HEADS * HEAD_DIM // kd
    return pl.pallas_call(
        _swa_kernel,
        grid=(batch, KV_GROUPS, nstep),
        in_specs=[
            pl.BlockSpec(memory_space=pltpu.SMEM),
            pl.BlockSpec(memory_space=pltpu.SMEM),
            pl.BlockSpec((rows, gw), lambda b, g, c: (b * nstep + c, g)),
            pl.BlockSpec((seq, kd), lambda b, g, c: (b, k_col)),
            pl.BlockSpec((seq, kd), lambda b, g, c: (b, k_col + 1)),
        ],
        out_specs=pl.BlockSpec((rows, gw), lambda b, g, c: (b * nstep + c, g)),
        out_shape=jax.ShapeDtypeStruct((batch * seq, N_HEADS * HEAD_DIM), MXU_DTYPE),
        scratch_shapes=[
            pltpu.VMEM((Q_SUB, ROWS, LANES), MXU_DTYPE),
            pltpu.VMEM((GROUP, Q_BLOCK, span), F32),
            pltpu.VMEM((seq, LANES), MXU_DTYPE),
            pltpu.VMEM((seq, 2 * LANES), MXU_DTYPE),
        ],
        compiler_params=_params("parallel", "parallel", "arbitrary"),
        name="swa_attention",
    )(nslopes, sinks, qkv, qkv, qkv)


def _compress_kernel(z_ref, pe_ref, w1_ref, b1_ref, w2_ref, b2_ref, o_ref, *, ncp):
    half = CMP_STRIDE * HEAD_DIM
    z = z_ref[0, 0, 0].astype(F32)
    top = (z + pe_ref[0, 0:1, :]).astype(MXU_DTYPE)
    bot = (z + pe_ref[0, 1:2, :]).astype(MXU_DTYPE)
    a = jnp.dot(top, w1_ref[0, 0:half, :], preferred_element_type=F32)
    bm = jnp.dot(bot, w1_ref[0, half:2 * half, :], preferred_element_type=F32)
    hid = a + pltpu.roll(bm, ncp - 1, axis=0) + b1_ref[0]
    act = jax.nn.gelu(hid).astype(MXU_DTYPE)
    o_ref[0, 0, 0] = jnp.dot(act, w2_ref[0], preferred_element_type=F32) + b2_ref[0]


def compress(z, pe, w1, b1, w2, b2):
    _, batch, groups, ncp, zw = z.shape
    hid = w1.shape[-1]
    kern = functools.partial(_compress_kernel, ncp=ncp)
    return pl.pallas_call(
        kern,
        grid=(2, batch, groups),
        in_specs=[
            pl.BlockSpec((1, 1, 1, ncp, zw), lambda s, b, g: (s, b, g, 0, 0)),
            pl.BlockSpec((1, 2, zw), lambda s, b, g: (s, 0, 0)),
            pl.BlockSpec((1, 2 * zw, hid), lambda s, b, g: (s, 0, 0)),
            pl.BlockSpec((1, 1, hid), lambda s, b, g: (s, 0, 0)),
            pl.BlockSpec((1, hid, HEAD_DIM), lambda s, b, g: (s, 0, 0)),
            pl.BlockSpec((1, 1, HEAD_DIM), lambda s, b, g: (s, 0, 0)),
        ],
        out_specs=pl.BlockSpec((1, 1, 1, ncp, HEAD_DIM), lambda s, b, g: (s, b, g, 0, 0)),
        out_shape=jax.ShapeDtypeStruct((2, batch, groups, ncp, HEAD_DIM), F32),
        compiler_params=_params("parallel", "parallel", "parallel"),
        name="nsa_compress",
    )(z, pe, w1, b1, w2, b2)


def _nsa_cmp_kernel(nslope_ref, q_ref, kc_ref, vc_ref, o_ref, sel_ref, qs_ref, e_scr, r_scr, *, ncp, n_cmp, n_sel):
    grp = pl.program_id(1)
    c = pl.program_id(2)
    starts = [(c * Q_SUB + j) * Q_BLOCK for j in range(Q_SUB)]
    for j in range(Q_SUB):
        _build_q_stack(q_ref, qs_ref.at[j], j * Q_BLOCK)

    def attend(width):
        for j, t0 in enumerate(starts):
            qi = lax.broadcasted_iota(jnp.int32, (Q_BLOCK, width), 0)
            ni = lax.broadcasted_iota(jnp.int32, (Q_BLOCK, width), 1)
            d = (t0 + qi) - (ni * CMP_STRIDE + (CMP_LEN - 1))
            negb = jnp.where((d >= 0) & (ni < n_cmp), 0.0, MASK_VALUE)
            dist = d.astype(F32)
            s_all = lax.dot_general(qs_ref[j], kc_ref[0, 0, 0:width, :], (((1,), (1,)), ((), ())),
                                    preferred_element_type=F32)
            for rb in range(GROUP):
                s = s_all[_rows(rb)] + nslope_ref[_head_of_row_block(grp, rb)] * dist + negb
                m = jnp.max(s, axis=1, keepdims=True)
                e_scr[j, _rows(rb), 0:width] = jnp.exp(s - m).astype(e_scr.dtype)
            r_scr[j] = jnp.dot(e_scr[j, :, 0:width], vc_ref[0, 0, 0:width, :], preferred_element_type=F32)

    n_chunks = ncp // LANES
    need = jnp.minimum((starts[-1] + Q_BLOCK - CMP_LEN) // CMP_STRIDE // LANES + 1, n_chunks)
    for kq in range(1, n_chunks + 1):
        pl.when(need == kq)(functools.partial(attend, kq * LANES))

    lane = lax.broadcasted_iota(jnp.int32, (Q_BLOCK, LANES), 1)
    even = lane < HEAD_DIM
    ji = lax.broadcasted_iota(jnp.int32, (LANES, Q_BLOCK), 0)
    qt = lax.broadcasted_iota(jnp.int32, (LANES, Q_BLOCK), 1)
    neg_inf = -jnp.inf
    for j, t0 in enumerate(starts):
        orow = slice(j * Q_BLOCK, (j + 1) * Q_BLOCK)
        row_t = t0 + lax.broadcasted_iota(jnp.int32, (Q_BLOCK, LANES), 0)
        has_cmp = row_t >= (CMP_LEN - 1)
        imp = jnp.zeros((Q_BLOCK, LANES), F32)
        for p in range(PAIRS):
            re = _rows(p)
            ro = _rows(PAIRS + p)
            de = jnp.maximum(r_scr[j, re, LANES:2 * LANES], 1e-30)
            do = jnp.maximum(r_scr[j, ro, LANES:2 * LANES], 1e-30)
            o = jnp.where(even, r_scr[j, re, 0:LANES] / de, r_scr[j, ro, 0:LANES] / do)
            o_ref[orow, p * LANES:(p + 1) * LANES] = jnp.where(has_cmp, o, 0.0)
            imp = imp + r_scr[j, re, 2 * LANES:3 * LANES] / de + r_scr[j, ro, 2 * LANES:3 * LANES] / do
        imp = jnp.where(has_cmp, imp, 0.0)

        imp_t = imp.T
        cur = (t0 + qt) // SEL_LEN
        causal = ji <= cur
        forced = (ji == 0) | (ji == cur) | (ji == cur - 1)
        score = jnp.where(forced, neg_inf, jnp.where(causal, imp_t, MASK_VALUE))
        score = jnp.where(ji < n_sel, score, neg_inf)
        picked = jnp.where(forced, 1.0, 0.0)
        for _ in range(SEL_TOPK - SEL_FORCED):
            mx = jnp.max(score, axis=0, keepdims=True)
            first = jnp.min(jnp.where(score == mx, ji, LANES), axis=0, keepdims=True)
            hit = ji == first
            picked = jnp.where(hit, 1.0, picked)
            score = jnp.where(hit, neg_inf, score)
        picked = jnp.where(ji < cur, picked, 0.0)
        sel_ref[0, 0, orow, :] = picked.T.astype(sel_ref.dtype)


def nsa_cmp_select(q, kc2, vc_aug, nslopes, batch, seq, q_col_block):
    nstep = seq // (Q_SUB * Q_BLOCK)
    rows = Q_SUB * Q_BLOCK
    ncp = seq // CMP_STRIDE
    n_cmp = (seq - CMP_LEN) // CMP_STRIDE + 1
    n_sel = seq // SEL_LEN
    gw = PAIRS * LANES
    kern = functools.partial(_nsa_cmp_kernel, ncp=ncp, n_cmp=n_cmp, n_sel=n_sel)
    return pl.pallas_call(
        kern,
        grid=(batch, KV_GROUPS, nstep),
        in_specs=[
            pl.BlockSpec(memory_space=pltpu.SMEM),
            pl.BlockSpec((rows, gw), lambda b, g, c: (b * nstep + c, q_col_block + g)),
            pl.BlockSpec((1, 1, ncp, LANES), lambda b, g, c: (b, g, 0, 0)),
            pl.BlockSpec((1, 1, ncp, 3 * LANES), lambda b, g, c: (b, g, 0, 0)),
        ],
        out_specs=[
            pl.BlockSpec((rows, gw), lambda b, g, c: (b * nstep + c, g)),
            pl.BlockSpec((1, 1, rows, LANES), lambda b, g, c: (b, g, c, 0)),
        ],
        out_shape=[
            jax.ShapeDtypeStruct((batch * seq, N_HEADS * HEAD_DIM), F32),
            jax.ShapeDtypeStruct((batch, KV_GROUPS, seq, LANES), MXU_DTYPE),
        ],
        scratch_shapes=[pltpu.VMEM((Q_SUB, ROWS, LANES), MXU_DTYPE), pltpu.VMEM((Q_SUB, ROWS, ncp), MXU_DTYPE),
                        pltpu.VMEM((Q_SUB, ROWS, 3 * LANES), F32)],
        compiler_params=_params("parallel", "parallel", "arbitrary"),
        name="nsa_cmp_select",
    )(nslopes, q, kc2, vc_aug)


N_FEAT = 6


def sel_query_features():
    s = -jnp.asarray(_alibi_neg_slopes())
    s1 = s.astype(MXU_DTYPE).astype(F32)
    s2 = (s - s1).astype(MXU_DTYPE).astype(F32)
    s3 = (s - s1 - s2).astype(MXU_DTYPE).astype(F32)
    feat = jnp.zeros((N_HEADS, LANES), F32).at[:, HEAD_DIM:HEAD_DIM + N_FEAT].set(
        jnp.stack([s1, s2, s3, s1, s2, s3], axis=1))
    feat = feat.at[:, HEAD_DIM + N_FEAT].set(MASK_VALUE)
    order = np.array([[_head_of_row_block(g, rb) for rb in range(GROUP)] for g in range(KV_GROUPS)])
    return feat[order]


def sel_key_features(seq):
    pos = np.arange(seq)
    kk = pos % SEL_STEP
    f = np.zeros((seq + SEL_STEP, 2 * LANES), np.float32)
    f[:seq, HEAD_DIM:HEAD_DIM + 3] = (SEL_LEN * (kk // SEL_LEN))[:, None]
    f[:seq, HEAD_DIM + 3:HEAD_DIM + 6] = (kk % SEL_LEN)[:, None]
    f[seq:, HEAD_DIM + N_FEAT] = 1.0
    f[pos, LANES + pos // SEL_LEN] = 1.0
    return f


def _nsa_sel_kernel(delta_ref, qfeat_ref, q_ref, sel_ref, k_in, v_in, kfeat_ref, gate_ref, ex_ref, oc_ref, ow_ref,
                    o_ref, qa_ref, m_ref, acc_ref, s_a, s_b, p_a, p_b, al_a, al_b, k_ref, v_ref):
    grp = pl.program_id(1)
    c = pl.program_id(2)
    t0 = c * Q_BLOCK
    seq = k_in.shape[0]

    @pl.when(c == 0)
    def _():
        def put_k(r0, y):
            k_ref[pl.ds(r0, KV_CHUNK), 0:LANES] = y
            k_ref[pl.ds(r0, KV_CHUNK), LANES:2 * LANES] = kfeat_ref[pl.ds(r0, KV_CHUNK), LANES:2 * LANES]

        def put_v(r0, y):
            v_ref[pl.ds(r0, KV_CHUNK), 0:LANES] = y
            v_ref[pl.ds(r0, KV_CHUNK), LANES:2 * LANES] = jnp.ones((KV_CHUNK, LANES), v_ref.dtype)

        _unpack_group(k_in, grp, put_k, fill_ref=kfeat_ref)
        _unpack_group(v_in, grp, put_v)
        k_ref[seq:seq + SEL_STEP, :] = kfeat_ref[seq:seq + SEL_STEP, :]
        v_ref[seq:seq + SEL_STEP, :] = jnp.zeros((SEL_STEP, 2 * LANES), v_ref.dtype)
    lane = lax.broadcasted_iota(jnp.int32, (Q_BLOCK, LANES), 1)
    low = lane < HEAD_DIM
    selneg = ((1.0 - sel_ref[0, 0].astype(F32)) * MASK_VALUE).astype(qa_ref.dtype)
    for p in range(PAIRS):
        qp = q_ref[:, p * LANES:(p + 1) * LANES].astype(F32) * (HEAD_DIM ** -0.5)
        for rb, src in ((p, qp), (PAIRS + p, pltpu.roll(qp, HEAD_DIM, axis=1))):
            qa_ref[_rows(rb), 0:LANES] = jnp.where(low, src, qfeat_ref[0, rb:rb + 1, :]).astype(qa_ref.dtype)
            qa_ref[_rows(rb), LANES:2 * LANES] = selneg
    n_steps = t0 // SEL_STEP + 1
    n_pad_step = seq // SEL_STEP
    dot_nt = (((1,), (1,)), ((), ()))

    def key_start(step):
        return pl.multiple_of(jnp.clip(step, 0, n_pad_step) * SEL_STEP, SEL_STEP)

    def scores(step, s_out):
        k = k_ref[pl.ds(key_start(step), SEL_STEP), :]
        s_out[...] = lax.dot_general(qa_ref[...], k, dot_nt, preferred_element_type=F32)

    def softmax(s_in, p_out, al_out):
        for rb in range(GROUP):
            m_prev = m_ref[_rows(rb), :] - delta_ref[_head_of_row_block(grp, rb)]
            m_new = jnp.maximum(m_prev, jnp.max(s_in[_rows(rb), :], axis=1, keepdims=True))
            al_out[_rows(rb), :] = jnp.exp(m_prev - m_new)
            m_ref[_rows(rb), :] = m_new
        for rb in range(GROUP):
            p_out[_rows(rb), :] = jnp.exp(s_in[_rows(rb), :] - _rep(m_ref[_rows(rb), :], SEL_STEP)).astype(p_out.dtype)

    def values(step, p_in, al_in):
        v = v_ref[pl.ds(key_start(step), SEL_STEP), :]
        pv = jnp.dot(p_in[...], v, preferred_element_type=F32)
        alpha = al_in[...]
        acc_ref[...] = acc_ref[...] * jnp.concatenate([alpha, alpha], axis=1) + pv

    def even_half(t):
        scores(t, s_a)
        softmax(s_b, p_b, al_b)
        values(t - 2, p_a, al_a)

    def odd_half(t):
        scores(t, s_b)
        softmax(s_a, p_a, al_a)
        values(t - 2, p_b, al_b)

    scores(0, s_a)
    own = pl.multiple_of(t0, Q_BLOCK)
    qi = lax.broadcasted_iota(jnp.int32, (Q_BLOCK, Q_BLOCK), 0)
    ki = lax.broadcasted_iota(jnp.int32, (Q_BLOCK, Q_BLOCK), 1)
    own_bias = jnp.where((ki <= qi) & (ki // SEL_LEN == qi // SEL_LEN), 0.0, MASK_VALUE)
    s_own = lax.dot_general(qa_ref[:, 0:LANES], k_ref[pl.ds(own, Q_BLOCK), 0:LANES], dot_nt,
                            preferred_element_type=F32)
    scores(1, s_b)
    back = jnp.full((Q_BLOCK, LANES), n_steps, jnp.int32).astype(F32)
    p_own = []
    for rb in range(GROUP):
        s = s_own[_rows(rb)] + own_bias
        m = jnp.max(s, axis=1, keepdims=True)
        p_own.append(jnp.exp(s - m).astype(MXU_DTYPE))
        m_ref[_rows(rb), :] = m + back * delta_ref[_head_of_row_block(grp, rb)]
    softmax(s_a, p_a, al_a)
    acc_ref[...] = jnp.dot(jnp.concatenate(p_own, axis=0), v_ref[pl.ds(own, Q_BLOCK), :],
                           preferred_element_type=F32)

    def quad(j, carry):
        even_half(4 * j + 2)
        odd_half(4 * j + 3)
        even_half(4 * j + 4)
        odd_half(4 * j + 5)
        return carry

    lax.fori_loop(0, n_steps // 4, quad, 0)
    rest = 4 * (n_steps // 4) + 2

    @pl.when(n_steps % 4 >= 2)
    def _():
        even_half(rest)
        odd_half(rest + 1)

    @pl.when(n_steps % 2 == 1)
    def _():
        even_half(n_steps + 1)
    o_sel = _pairs(lambda rb: acc_ref[_rows(rb), 0:LANES], lambda rb: acc_ref[_rows(rb), LANES:2 * LANES])

    gw = PAIRS * LANES
    sig = jax.nn.sigmoid(gate_ref[...])
    hi = sig.astype(MXU_DTYPE)
    lo = (sig - hi.astype(F32)).astype(MXU_DTYPE)
    ex = ex_ref[0]
    g = jnp.dot(hi, ex, preferred_element_type=F32) + jnp.dot(lo, ex, preferred_element_type=F32)
    for p in range(PAIRS):
        cols = slice(p * LANES, (p + 1) * LANES)
        out = (g[:, cols] * oc_ref[:, cols]
               + g[:, gw + p * LANES:gw + (p + 1) * LANES] * o_sel[p]
               + g[:, 2 * gw + p * LANES:2 * gw + (p + 1) * LANES] * ow_ref[:, cols])
        o_ref[:, cols] = out.astype(o_ref.dtype)


def _gate_expansion():
    gw = PAIRS * LANES
    ex = np.zeros((KV_GROUPS, LANES, 3 * gw), np.float32)
    for g in range(KV_GROUPS):
        for hl in range(GROUP):
            for i in range(3):
                ex[g, 3 * (g * GROUP + hl) + i, i * gw + hl * HEAD_DIM:i * gw + (hl + 1) * HEAD_DIM] = 1.0
    return ex


def nsa_sel_attention(qkv, sel, gate, o_cmp, o_win, batch, seq, k_col, v_col):
    nblk = seq // Q_BLOCK
    gw = PAIRS * LANES
    kd = KV_GROUPS * HEAD_DIM
    deltas = jnp.asarray((-_alibi_neg_slopes() * SEL_STEP).astype(np.float32))
    ex = jnp.asarray(_gate_expansion(), MXU_DTYPE)
    kfeat = jnp.asarray(sel_key_features(seq), MXU_DTYPE)
    blk = lambda b, g, c: (b * nblk + c, g)
    return pl.pallas_call(
        _nsa_sel_kernel,
        grid=(batch, KV_GROUPS, nblk),
        in_specs=[
            pl.BlockSpec(memory_space=pltpu.SMEM),
            pl.BlockSpec((1, GROUP, LANES), lambda b, g, c: (g, 0, 0)),
            pl.BlockSpec((Q_BLOCK, gw), blk),
            pl.BlockSpec((1, 1, Q_BLOCK, LANES), lambda b, g, c: (b, g, c, 0)),
            pl.BlockSpec((seq, kd), lambda b, g, c: (b, k_col)),
            pl.BlockSpec((seq, kd), lambda b, g, c: (b, v_col)),
            pl.BlockSpec((seq + SEL_STEP, 2 * LANES), lambda b, g, c: (0, 0)),
            pl.BlockSpec((Q_BLOCK, LANES), lambda b, g, c: (b * nblk + c, 0)),
            pl.BlockSpec((1, LANES, 3 * gw), lambda b, g, c: (g, 0, 0)),
            pl.BlockSpec((Q_BLOCK, gw), blk),
            pl.BlockSpec((Q_BLOCK, gw), blk),
        ],
        out_specs=pl.BlockSpec((Q_BLOCK, gw), blk),
        out_shape=jax.ShapeDtypeStruct((batch * seq, N_HEADS * HEAD_DIM), MXU_DTYPE),
        scratch_shapes=[
            pltpu.VMEM((ROWS, 2 * LANES), MXU_DTYPE),
            pltpu.VMEM((ROWS, LANES), F32),
            pltpu.VMEM((ROWS, 2 * LANES), F32),
            pltpu.VMEM((ROWS, SEL_STEP), F32),
            pltpu.VMEM((ROWS, SEL_STEP), F32),
            pltpu.VMEM((ROWS, SEL_STEP), MXU_DTYPE),
            pltpu.VMEM((ROWS, SEL_STEP), MXU_DTYPE),
            pltpu.VMEM((ROWS, LANES), F32),
            pltpu.VMEM((ROWS, LANES), F32),
            pltpu.VMEM((seq + SEL_STEP, 2 * LANES), MXU_DTYPE),
            pltpu.VMEM((seq + SEL_STEP, 2 * LANES), MXU_DTYPE),
        ],
        compiler_params=_params("parallel", "parallel", "arbitrary"),
        name="nsa_sel_attention",
    )(deltas, sel_query_features(), qkv, sel, qkv, qkv, kfeat, gate, ex, o_cmp, o_win)


def _nsa_win_kernel(nslope_ref, q_ref, k_in, v_in, o_ref, qs_ref, bias_scr, k_ref, v_ref):
    grp = pl.program_id(1)
    c = pl.program_id(2)
    span = NSA_WINDOW + Q_BLOCK
    lead = NSA_WINDOW // Q_BLOCK
    for j in range(Q_SUB):
        _build_q_stack(q_ref, qs_ref.at[j], j * Q_BLOCK)
    nslope = lambda rb: nslope_ref[_head_of_row_block(grp, rb)]

    def write(j, outs):
        for p, o in enumerate(outs):
            o_ref[j * Q_BLOCK:(j + 1) * Q_BLOCK, p * LANES:(p + 1) * LANES] = o

    @pl.when(c == 0)
    def _():
        _unpack_kv(k_in, v_in, grp, k_ref, v_ref)
        d = _pos_tiles(NSA_WINDOW, 0, span)
        for rb in range(GROUP):
            bias_scr[rb] = _band_bias(nslope(rb), d, NSA_WINDOW)

    @pl.when(c < lead // Q_SUB)
    def _():
        for j in range(Q_SUB):
            d = _pos_tiles((c * Q_SUB + j) * Q_BLOCK, 0, span)
            write(j, _window_attend(qs_ref.at[j], k_ref[0:span, :], v_ref[0:span, :],
                                    lambda rb: _band_bias(nslope(rb), d, NSA_WINDOW)))

    @pl.when(c >= lead // Q_SUB)
    def _():
        for j in range(Q_SUB):
            start = pl.multiple_of((c * Q_SUB + j) * Q_BLOCK - NSA_WINDOW, Q_BLOCK)
            write(j, _window_attend(qs_ref.at[j], k_ref[pl.ds(start, span), :], v_ref[pl.ds(start, span), :],
                                    lambda rb: bias_scr[rb]))


def nsa_win_attention(qkv, nslopes, batch, seq, k_col, v_col):
    nstep = seq // (Q_SUB * Q_BLOCK)
    rows = Q_SUB * Q_BLOCK
    gw = PAIRS * LANES
    kd = KV_GROUPS * HEAD_DIM
    span = NSA_WINDOW + Q_BLOCK
    return pl.pallas_call(
        _nsa_win_kernel,
        grid=(batch, KV_GROUPS, nstep),
        in_specs=[
            pl.BlockSpec(memory_space=pltpu.SMEM),
            pl.BlockSpec((rows, gw), lambda b, g, c: (b * nstep + c, g)),
            pl.BlockSpec((seq, kd), lambda b, g, c: (b, k_col)),
            pl.BlockSpec((seq, kd), lambda b, g, c: (b, v_col)),
        ],
        out_specs=pl.BlockSpec((rows, gw), lambda b, g, c: (b * nstep + c, g)),
        out_shape=jax.ShapeDtypeStruct((batch * seq, N_HEADS * HEAD_DIM), F32),
        scratch_shapes=[
            pltpu.VMEM((Q_SUB, ROWS, LANES), MXU_DTYPE),
            pltpu.VMEM((GROUP, Q_BLOCK, span), F32),
            pltpu.VMEM((seq, LANES), MXU_DTYPE),
            pltpu.VMEM((seq, 2 * LANES), MXU_DTYPE),
        ],
        compiler_params=_params("parallel", "parallel", "arbitrary"),
        name="nsa_win_attention",
    )(nslopes, qkv, qkv, qkv)


def _overlap_matrix(seq):
    ncp = seq // CMP_STRIDE
    n_cmp = (seq - CMP_LEN) // CMP_STRIDE + 1
    cs = np.arange(n_cmp) * CMP_STRIDE
    ss = np.arange(seq // SEL_LEN) * SEL_LEN
    ov = (cs[:, None] < ss[None, :] + SEL_LEN) & (cs[:, None] + CMP_LEN > ss[None, :])
    out = np.zeros((ncp, LANES), np.float32)
    out[:n_cmp, :seq // SEL_LEN] = ov
    return out


def _swa_layer(h, x, w_in, b_in, sinks, w_o, b_o, g_post, g_next, nslopes, batch, seq):
    hd = N_HEADS * HEAD_DIM
    kd = KV_GROUPS * HEAD_DIM
    qkv = matmul_bias(h, w_in.astype(MXU_DTYPE), b_in, MXU_DTYPE, name="swa_in_proj")
    o = swa_attention(qkv, nslopes, sinks.astype(F32), batch, seq)
    return matmul_norm_res(o, w_o.astype(MXU_DTYPE), b_o, g_post, g_next, x, name="swa_out_proj")


def _nsa_layer(h, x, w_in, cmp_pe, cmp_w1, cmp_b1, cmp_w2, cmp_b2, w_o, g_post, g_next, nslopes, batch, seq):
    hd = N_HEADS * HEAD_DIM
    kd = KV_GROUPS * HEAD_DIM
    t = batch * seq
    ncp = seq // CMP_STRIDE
    qkv = matmul_bias(h, w_in[:, :hd + 6 * kd].astype(MXU_DTYPE), jnp.zeros((hd + 6 * kd,), F32), MXU_DTYPE,
                      name="nsa_in_proj")
    n_gate = 3 * N_HEADS
    w_gate = jnp.pad(w_in[:, hd + 6 * kd:], ((0, 0), (0, LANES - n_gate))).astype(MXU_DTYPE)
    gate = matmul_bias(h, w_gate, jnp.zeros((LANES,), F32), F32, name="nsa_gate_proj")

    def kv(i):
        return qkv[:, hd + i * kd:hd + (i + 1) * kd]

    def slabs(a):
        return a.reshape(batch, seq, KV_GROUPS, HEAD_DIM).transpose(0, 2, 1, 3).reshape(
            batch, KV_GROUPS, ncp, CMP_STRIDE * HEAD_DIM)

    z = jnp.stack([slabs(kv(0)), slabs(kv(1))])
    half = CMP_STRIDE * HEAD_DIM
    cmp_out = compress(z, cmp_pe.reshape(2, 2, half).astype(F32), cmp_w1.astype(MXU_DTYPE),
                       cmp_b1.reshape(2, 1, -1), cmp_w2.astype(MXU_DTYPE), cmp_b2.reshape(2, 1, -1))
    kcm = cmp_out[0].astype(MXU_DTYPE)
    vcm = cmp_out[1].astype(MXU_DTYPE)
    kc2 = jnp.concatenate([kcm, kcm], axis=-1)
    ov = jnp.broadcast_to(jnp.asarray(_overlap_matrix(seq), MXU_DTYPE), (batch, KV_GROUPS, ncp, LANES))
    vc_aug = jnp.concatenate([vcm, vcm, jnp.ones((batch, KV_GROUPS, ncp, LANES), MXU_DTYPE), ov], axis=-1)

    o_cmp, sel = nsa_cmp_select(qkv, kc2, vc_aug, nslopes, batch, seq, 0)
    kv_col = lambda i: hd // kd + i
    o_win = nsa_win_attention(qkv, nslopes, batch, seq, kv_col(4), kv_col(5))
    o = nsa_sel_attention(qkv, sel, gate, o_cmp, o_win, batch, seq, kv_col(2), kv_col(3))
    return matmul_norm_res(o, w_o.astype(MXU_DTYPE), jnp.zeros((w_o.shape[1],), F32), g_post, g_next, x,
                           name="nsa_out_proj")


def kernel(x, norm_g, swa_w_in, swa_b_in, swa_sinks, swa_w_o, swa_b_o, nsa_w_in, nsa_cmp_pe, nsa_cmp_w1, nsa_cmp_b1, nsa_cmp_w2, nsa_cmp_b2, nsa_w_o, ffn_w_gate, ffn_w_up, ffn_conv_w, ffn_conv_b, ffn_w_down):
    batch, seq, d = x.shape
    depth = norm_g.shape[0]
    nslopes = jnp.asarray(_alibi_neg_slopes())
    xf = x.reshape(batch * seq, d)
    h = rms_cast(xf, norm_g[0, 0])
    for i in range(depth):
        g = norm_g[i]
        j = i // 2
        if i % 2 == 0:
            xf, h = _swa_layer(h, xf, swa_w_in[j], swa_b_in[j], swa_sinks[j], swa_w_o[j], swa_b_o[j],
                               g[1], g[2], nslopes, batch, seq)
        else:
            xf, h = _nsa_layer(h, xf, nsa_w_in[j], nsa_cmp_pe[j], nsa_cmp_w1[j], nsa_cmp_b1[j], nsa_cmp_w2[j],
                               nsa_cmp_b2[j], nsa_w_o[j], g[1], g[2], nslopes, batch, seq)
        act = ffn_up(h, ffn_w_gate, ffn_w_up, i, ffn_conv_w[i], ffn_conv_b[i], seq)
        g_next = norm_g[i + 1, 0] if i + 1 < depth else jnp.ones((d,), F32)
        xf, h = matmul_norm_res(act, ffn_w_down[i].astype(MXU_DTYPE), jnp.zeros((d,), F32), g[3], g_next, xf,
                                tn=512, name="ffn_down")
    return xf.reshape(batch, seq, d)
```

```python
import functools

import numpy as np
import jax
import jax.numpy as jnp
from jax import lax
from jax.experimental import pallas as pl
from jax.experimental.pallas import tpu as pltpu

F32 = jnp.float32
MXU_DTYPE = jnp.bfloat16

N_HEADS = 32
HEAD_DIM = 64
KV_GROUPS = 4
GROUP = N_HEADS // KV_GROUPS
PAIRS = GROUP // 2
LANES = 128
Q_BLOCK = 128
ROWS = GROUP * Q_BLOCK
SWA_WINDOW = 128
CMP_LEN = 32
CMP_STRIDE = 16
SEL_LEN = 64
SEL_TOPK = 16
SEL_FORCED = 3
SEL_STEP = 512
NSA_WINDOW = 512
CONV_WIDTH = 3
RMS_EPS = 1e-6
MASK_VALUE = -1e30
VMEM_LIMIT = 56 * 1024 * 1024


def _params(*sem):
    return pltpu.CompilerParams(dimension_semantics=sem, vmem_limit_bytes=VMEM_LIMIT)


def _alibi_neg_slopes():
    return (-np.exp2(-8.0 * np.arange(1, N_HEADS + 1, dtype=np.float64) / N_HEADS)).astype(np.float32)


def _rms_cast_kernel(x_ref, g_ref, o_ref):
    x = x_ref[...]
    ms = jnp.mean(x * x, axis=-1, keepdims=True)
    o_ref[...] = (x * lax.rsqrt(ms + RMS_EPS) * g_ref[...]).astype(o_ref.dtype)


def rms_cast(x, g, tm=512):
    t, d = x.shape
    return pl.pallas_call(
        _rms_cast_kernel,
        grid=(t // tm,),
        in_specs=[pl.BlockSpec((tm, d), lambda i: (i, 0)), pl.BlockSpec((1, d), lambda i: (0, 0))],
        out_specs=pl.BlockSpec((tm, d), lambda i: (i, 0)),
        out_shape=jax.ShapeDtypeStruct((t, d), MXU_DTYPE),
        compiler_params=_params("parallel"),
        name="rms_cast",
    )(x, g.reshape(1, d))


def _matmul_kernel(a_ref, w_ref, b_ref, o_ref):
    acc = jnp.dot(a_ref[...], w_ref[...], preferred_element_type=F32)
    o_ref[...] = (acc + b_ref[...]).astype(o_ref.dtype)


def matmul_bias(a, w, b, out_dtype, tm=1024, tn=512, name="matmul_bias"):
    t, k = a.shape
    n = w.shape[1]
    tn = min(tn, n)
    return pl.pallas_call(
        _matmul_kernel,
        grid=(t // tm, n // tn),
        in_specs=[
            pl.BlockSpec((tm, k), lambda i, j: (i, 0)),
            pl.BlockSpec((k, tn), lambda i, j: (0, j)),
            pl.BlockSpec((1, tn), lambda i, j: (0, j)),
        ],
        out_specs=pl.BlockSpec((tm, tn), lambda i, j: (i, j)),
        out_shape=jax.ShapeDtypeStruct((t, n), out_dtype),
        compiler_params=_params("parallel", "arbitrary"),
        name=name,
    )(a, w, b.reshape(1, n))


def _mm_norm_res_kernel(a_ref, w_ref, b_ref, gpost_ref, gnext_ref, x_ref, xo_ref, ho_ref, y_scr, *, nj, tn, n):
    j = pl.program_id(1)
    y_scr[j] = jnp.dot(a_ref[...], w_ref[...], preferred_element_type=F32) + b_ref[...]

    @pl.when(j == nj - 1)
    def _():
        ss = jnp.sum(y_scr[0] * y_scr[0], axis=1, keepdims=True)
        for jj in range(1, nj):
            ss = ss + jnp.sum(y_scr[jj] * y_scr[jj], axis=1, keepdims=True)
        r = lax.rsqrt(ss / n + RMS_EPS)
        ss2 = jnp.zeros_like(ss)
        for jj in range(nj):
            cols = slice(jj * tn, (jj + 1) * tn)
            xn = x_ref[:, cols] + y_scr[jj] * r * gpost_ref[:, cols]
            xo_ref[:, cols] = xn
            ss2 = ss2 + jnp.sum(xn * xn, axis=1, keepdims=True)
        r2 = lax.rsqrt(ss2 / n + RMS_EPS)
        for jj in range(nj):
            cols = slice(jj * tn, (jj + 1) * tn)
            ho_ref[:, cols] = (xo_ref[:, cols] * r2 * gnext_ref[:, cols]).astype(ho_ref.dtype)


def matmul_norm_res(a, w, b, g_post, g_next, x, tm=512, tn=512, name="matmul_norm_res"):
    t, k = a.shape
    n = w.shape[1]
    nj = n // tn
    kern = functools.partial(_mm_norm_res_kernel, nj=nj, tn=tn, n=n)
    return pl.pallas_call(
        kern,
        grid=(t // tm, nj),
        in_specs=[
            pl.BlockSpec((tm, k), lambda i, j: (i, 0)),
            pl.BlockSpec((k, tn), lambda i, j: (0, j)),
            pl.BlockSpec((1, tn), lambda i, j: (0, j)),
            pl.BlockSpec((1, n), lambda i, j: (0, 0)),
            pl.BlockSpec((1, n), lambda i, j: (0, 0)),
            pl.BlockSpec((tm, n), lambda i, j: (i, 0)),
        ],
        out_specs=[
            pl.BlockSpec((tm, n), lambda i, j: (i, 0)),
            pl.BlockSpec((tm, n), lambda i, j: (i, 0)),
        ],
        out_shape=[jax.ShapeDtypeStruct((t, n), F32), jax.ShapeDtypeStruct((t, n), MXU_DTYPE)],
        scratch_shapes=[pltpu.VMEM((nj, tm, tn), F32)],
        compiler_params=_params("parallel", "arbitrary"),
        name=name,
    )(a, w, b.reshape(1, n), g_post.reshape(1, n), g_next.reshape(1, n), x)


FIX_ROWS = 16
CARRY_ROWS = 8


def _ffn_up_kernel(h_ref, wg_ref, wu_ref, cw_ref, cb_ref, o_ref, wg_scr, wu_scr, carry_scr, *, tm, tiles_per_seq):
    i = pl.program_id(1)

    @pl.when(i == 0)
    def _():
        wg_scr[...] = wg_ref[...].astype(wg_scr.dtype)
        wu_scr[...] = wu_ref[...].astype(wu_scr.dtype)

    h = h_ref[...]
    gate = jnp.dot(h, wg_scr[...], preferred_element_type=F32)
    up = jnp.dot(h, wu_scr[...], preferred_element_type=F32)
    w0 = cw_ref[0:1, :]
    w1 = cw_ref[1:2, :]
    w2 = cw_ref[2:3, :]
    b = cb_ref[...]
    a = b + pltpu.roll(gate, 2, axis=0) * w0
    a = a + pltpu.roll(gate, 1, axis=0) * w1
    a = a + gate * w2
    o_ref[...] = (jax.nn.silu(a) * up).astype(o_ref.dtype)

    seq_start = (i % tiles_per_seq) == 0
    prev = jnp.where(seq_start, 0.0, carry_scr[...])
    head = gate[0:FIX_ROWS]
    ext = jnp.concatenate([prev, head], axis=0)
    af = b + ext[CARRY_ROWS - 2:CARRY_ROWS - 2 + FIX_ROWS] * w0
    af = af + ext[CARRY_ROWS - 1:CARRY_ROWS - 1 + FIX_ROWS] * w1
    af = af + head * w2
    o_ref[0:FIX_ROWS, :] = (jax.nn.silu(af) * up[0:FIX_ROWS]).astype(o_ref.dtype)
    carry_scr[...] = gate[tm - CARRY_ROWS:tm]


def ffn_up(h, wg, wu, layer, conv_w, conv_b, seq, tm=1024, tn=512):
    t, k = h.shape
    n = wg.shape[2]
    kern = functools.partial(_ffn_up_kernel, tm=tm, tiles_per_seq=seq // tm)
    return pl.pallas_call(
        kern,
        grid=(n // tn, t // tm),
        in_specs=[
            pl.BlockSpec((tm, k), lambda j, i: (i, 0)),
            pl.BlockSpec((None, k, tn), lambda j, i: (layer, 0, j)),
            pl.BlockSpec((None, k, tn), lambda j, i: (layer, 0, j)),
            pl.BlockSpec((CONV_WIDTH, tn), lambda j, i: (0, j)),
            pl.BlockSpec((1, tn), lambda j, i: (0, j)),
        ],
        out_specs=pl.BlockSpec((tm, tn), lambda j, i: (i, j)),
        out_shape=jax.ShapeDtypeStruct((t, n), MXU_DTYPE),
        scratch_shapes=[
            pltpu.VMEM((k, tn), MXU_DTYPE),
            pltpu.VMEM((k, tn), MXU_DTYPE),
            pltpu.VMEM((CARRY_ROWS, tn), F32),
        ],
        compiler_params=_params("arbitrary", "arbitrary"),
        name="ffn_up",
    )(h, wg, wu, conv_w, conv_b.reshape(1, n))


def _head_of_row_block(group, rb):
    return group * GROUP + 2 * (rb % PAIRS) + rb // PAIRS


Q_SUB = 4


def _build_q_stack(q_ref, qs_ref, row0=0):
    lane = lax.broadcasted_iota(jnp.int32, (Q_BLOCK, LANES), 1)
    even = lane < HEAD_DIM
    for p in range(PAIRS):
        qp = q_ref[row0:row0 + Q_BLOCK, p * LANES:(p + 1) * LANES].astype(F32) * (HEAD_DIM ** -0.5)
        qs_ref[p * Q_BLOCK:(p + 1) * Q_BLOCK, :] = jnp.where(even, qp, 0.0).astype(qs_ref.dtype)
        qs_ref[(PAIRS + p) * Q_BLOCK:(PAIRS + p + 1) * Q_BLOCK, :] = jnp.where(even, 0.0, qp).astype(qs_ref.dtype)


def _rep(x, size):
    return x if size == LANES else jnp.concatenate([x] * (size // LANES), axis=1)


def _pairs(num, den):
    lane = lax.broadcasted_iota(jnp.int32, (Q_BLOCK, LANES), 1)
    even = lane < HEAD_DIM
    outs = []
    for p in range(PAIRS):
        oe = num(p) / jnp.maximum(den(p), 1e-30)
        oo = num(PAIRS + p) / jnp.maximum(den(PAIRS + p), 1e-30)
        outs.append(jnp.where(even, oe, oo))
    return outs


def _rows(rb):
    return slice(rb * Q_BLOCK, (rb + 1) * Q_BLOCK)


def _band_bias(nslope, d, window):
    return jnp.where((d >= 0) & (d < window), nslope * d.astype(F32), MASK_VALUE)


def _window_attend(qs_ref, k, v, bias, extra_logit=None):
    half = ROWS // 2
    dot_nt = (((1,), (1,)), ((), ()))
    s_halves = [lax.dot_general(qs_ref[hh * half:(hh + 1) * half, :], k, dot_nt, preferred_element_type=F32)
                for hh in range(2)]
    ps, extras = [], []
    for rb in range(GROUP):
        lo = (rb % PAIRS) * Q_BLOCK
        s = s_halves[rb // PAIRS][lo:lo + Q_BLOCK] + bias(rb)
        m = jnp.max(s, axis=1, keepdims=True)
        if extra_logit is not None:
            m = jnp.maximum(m, extra_logit(rb))
            extras.append(jnp.exp(extra_logit(rb) - m))
        ps.append(jnp.exp(s - m).astype(MXU_DTYPE))
    r_halves = [jnp.dot(jnp.concatenate(ps[hh * PAIRS:(hh + 1) * PAIRS], axis=0), v, preferred_element_type=F32)
                for hh in range(2)]

    def part(rb, cols):
        lo = (rb % PAIRS) * Q_BLOCK
        return r_halves[rb // PAIRS][lo:lo + Q_BLOCK, cols]

    num = lambda rb: part(rb, slice(0, LANES))
    if extra_logit is None:
        den = lambda rb: part(rb, slice(LANES, 2 * LANES))
    else:
        den = lambda rb: part(rb, slice(LANES, 2 * LANES)) + extras[rb]
    return _pairs(num, den)


KV_CHUNK = 1024


def _unpack_group(kv_ref, grp, write_chunk, fill_ref=None):
    n = kv_ref.shape[0]
    for odd in range(2):
        def fill(odd=odd):
            def body(i, carry):
                r0 = pl.multiple_of(i * KV_CHUNK, KV_CHUNK)
                x = kv_ref[pl.ds(r0, KV_CHUNK), :]
                xi = pltpu.bitcast(x, jnp.int32)
                xr = pltpu.roll(xi, HEAD_DIM, axis=1)
                low = lax.broadcasted_iota(jnp.int32, xi.shape, 1) < HEAD_DIM
                if fill_ref is None:
                    y = jnp.where(low, xr, xi) if odd else jnp.where(low, xi, xr)
                else:
                    f = pltpu.bitcast(fill_ref[pl.ds(r0, KV_CHUNK), 0:LANES], jnp.int32)
                    y = jnp.where(low, xr if odd else xi, f)
                write_chunk(r0, pltpu.bitcast(y, x.dtype))
                return carry

            lax.fori_loop(0, n // KV_CHUNK, body, 0)

        pl.when(grp % 2 == odd)(fill)


def _kv_block_spec(seq, col):
    pairs_per_tensor = KV_GROUPS * HEAD_DIM // LANES
    return pl.BlockSpec((seq, LANES), lambda b, g, c: (b, pairs_per_tensor * col + g // 2))


def _pos_tiles(t0, start, size):
    qi = lax.broadcasted_iota(jnp.int32, (Q_BLOCK, size), 0)
    ki = lax.broadcasted_iota(jnp.int32, (Q_BLOCK, size), 1)
    return (t0 - start) + (qi - ki)


def _unpack_kv(k_in, v_in, grp, k_ref, v_ref):
    def put_k(r0, y):
        k_ref[pl.ds(r0, KV_CHUNK), :] = y

    def put_v(r0, y):
        v_ref[pl.ds(r0, KV_CHUNK), 0:LANES] = y
        v_ref[pl.ds(r0, KV_CHUNK), LANES:2 * LANES] = jnp.ones((KV_CHUNK, LANES), v_ref.dtype)

    _unpack_group(k_in, grp, put_k)
    _unpack_group(v_in, grp, put_v)


def _swa_kernel(nslope_ref, sink_ref, q_ref, k_in, v_in, o_ref, qs_ref, bias_scr, k_ref, v_ref):
    grp = pl.program_id(1)
    c = pl.program_id(2)
    span = SWA_WINDOW + Q_BLOCK
    for j in range(Q_SUB):
        _build_q_stack(q_ref, qs_ref.at[j], j * Q_BLOCK)
    sink = lambda rb: sink_ref[_head_of_row_block(grp, rb)]

    def write(j, outs):
        for p, o in enumerate(outs):
            o_ref[j * Q_BLOCK:(j + 1) * Q_BLOCK, p * LANES:(p + 1) * LANES] = o.astype(o_ref.dtype)

    def later_block(j):
        start = pl.multiple_of((c * Q_SUB + j) * Q_BLOCK - SWA_WINDOW, Q_BLOCK)
        write(j, _window_attend(qs_ref.at[j], k_ref[pl.ds(start, span), :], v_ref[pl.ds(start, span), :],
                                lambda rb: bias_scr[rb], sink))

    @pl.when(c == 0)
    def _():
        _unpack_kv(k_in, v_in, grp, k_ref, v_ref)
        d = _pos_tiles(SWA_WINDOW, 0, span)
        for rb in range(GROUP):
            bias_scr[rb] = _band_bias(nslope_ref[_head_of_row_block(grp, rb)], d, SWA_WINDOW)

    @pl.when(c == 0)
    def _():
        write(0, _window_attend(qs_ref.at[0], k_ref[0:Q_BLOCK, :], v_ref[0:Q_BLOCK, :],
                                lambda rb: bias_scr[rb, :, SWA_WINDOW:span], sink))
        for j in range(1, Q_SUB):
            later_block(j)

    @pl.when(c > 0)
    def _():
        for j in range(Q_SUB):
            later_block(j)


def swa_attention(qkv, nslopes, sinks, batch, seq):
    nstep = seq // (Q_SUB * Q_BLOCK)
    rows = Q_SUB * Q_BLOCK
    gw = PAIRS * LANES
    kd = KV_GROUPS * HEAD_DIM
    span = SWA_WINDOW + Q_BLOCK
    k_col = N_HEADS * HEAD_DIM // kd
    return pl.pallas_call(
        _swa_kernel,
        grid=(batch, KV_GROUPS, nstep),
        in_specs=[
            pl.BlockSpec(memory_space=pltpu.SMEM),
            pl.BlockSpec(memory_space=pltpu.SMEM),
            pl.BlockSpec((rows, gw), lambda b, g, c: (b * nstep + c, g)),
            _kv_block_spec(seq, k_col),
            _kv_block_spec(seq, k_col + 1),
        ],
        out_specs=pl.BlockSpec((rows, gw), lambda b, g, c: (b * nstep + c, g)),
        out_shape=jax.ShapeDtypeStruct((batch * seq, N_HEADS * HEAD_DIM), MXU_DTYPE),
        scratch_shapes=[
            pltpu.VMEM((Q_SUB, ROWS, LANES), MXU_DTYPE),
            pltpu.VMEM((GROUP, Q_BLOCK, span), F32),
            pltpu.VMEM((seq, LANES), MXU_DTYPE),
            pltpu.VMEM((seq, 2 * LANES), MXU_DTYPE),
        ],
        compiler_params=_params("parallel", "parallel", "arbitrary"),
        name="swa_attention",
    )(nslopes, sinks, qkv, qkv, qkv)


def _compress_kernel(z_ref, pe_ref, w1_ref, b1_ref, w2_ref, b2_ref, o_ref, *, ncp):
    half = CMP_STRIDE * HEAD_DIM
    z = z_ref[0, 0, 0].astype(F32)
    top = (z + pe_ref[0, 0:1, :]).astype(MXU_DTYPE)
    bot = (z + pe_ref[0, 1:2, :]).astype(MXU_DTYPE)
    a = jnp.dot(top, w1_ref[0, 0:half, :], preferred_element_type=F32)
    bm = jnp.dot(bot, w1_ref[0, half:2 * half, :], preferred_element_type=F32)
    hid = a + pltpu.roll(bm, ncp - 1, axis=0) + b1_ref[0]
    act = jax.nn.gelu(hid).astype(MXU_DTYPE)
    o_ref[0, 0, 0] = jnp.dot(act, w2_ref[0], preferred_element_type=F32) + b2_ref[0]


def compress(z, pe, w1, b1, w2, b2):
    _, batch, groups, ncp, zw = z.shape
    hid = w1.shape[-1]
    kern = functools.partial(_compress_kernel, ncp=ncp)
    return pl.pallas_call(
        kern,
        grid=(2, batch, groups),
        in_specs=[
            pl.BlockSpec((1, 1, 1, ncp, zw), lambda s, b, g: (s, b, g, 0, 0)),
            pl.BlockSpec((1, 2, zw), lambda s, b, g: (s, 0, 0)),
            pl.BlockSpec((1, 2 * zw, hid), lambda s, b, g: (s, 0, 0)),
            pl.BlockSpec((1, 1, hid), lambda s, b, g: (s, 0, 0)),
            pl.BlockSpec((1, hid, HEAD_DIM), lambda s, b, g: (s, 0, 0)),
            pl.BlockSpec((1, 1, HEAD_DIM), lambda s, b, g: (s, 0, 0)),
        ],
        out_specs=pl.BlockSpec((1, 1, 1, ncp, HEAD_DIM), lambda s, b, g: (s, b, g, 0, 0)),
        out_shape=jax.ShapeDtypeStruct((2, batch, groups, ncp, HEAD_DIM), F32),
        compiler_params=_params("parallel", "parallel", "parallel"),
        name="nsa_compress",
    )(z, pe, w1, b1, w2, b2)


def _nsa_cmp_kernel(nslope_ref, q_ref, kc_ref, vc_ref, o_ref, sel_ref, qs_ref, e_scr, r_scr, *, ncp, n_cmp, n_sel):
    grp = pl.program_id(1)
    c = pl.program_id(2)
    starts = [(c * Q_SUB + j) * Q_BLOCK for j in range(Q_SUB)]
    for j in range(Q_SUB):
        _build_q_stack(q_ref, qs_ref.at[j], j * Q_BLOCK)

    def attend(width):
        for j, t0 in enumerate(starts):
            qi = lax.broadcasted_iota(jnp.int32, (Q_BLOCK, width), 0)
            ni = lax.broadcasted_iota(jnp.int32, (Q_BLOCK, width), 1)
            d = (t0 + qi) - (ni * CMP_STRIDE + (CMP_LEN - 1))
            negb = jnp.where((d >= 0) & (ni < n_cmp), 0.0, MASK_VALUE)
            dist = d.astype(F32)
            s_all = lax.dot_general(qs_ref[j], kc_ref[0, 0, 0:width, :], (((1,), (1,)), ((), ())),
                                    preferred_element_type=F32)
            for rb in range(GROUP):
                s = s_all[_rows(rb)] + nslope_ref[_head_of_row_block(grp, rb)] * dist + negb
                m = jnp.max(s, axis=1, keepdims=True)
                e_scr[j, _rows(rb), 0:width] = jnp.exp(s - m).astype(e_scr.dtype)
            r_scr[j] = jnp.dot(e_scr[j, :, 0:width], vc_ref[0, 0, 0:width, :], preferred_element_type=F32)

    n_chunks = ncp // LANES
    need = jnp.minimum((starts[-1] + Q_BLOCK - CMP_LEN) // CMP_STRIDE // LANES + 1, n_chunks)
    for kq in range(1, n_chunks + 1):
        pl.when(need == kq)(functools.partial(attend, kq * LANES))

    lane = lax.broadcasted_iota(jnp.int32, (Q_BLOCK, LANES), 1)
    even = lane < HEAD_DIM
    ji = lax.broadcasted_iota(jnp.int32, (LANES, Q_BLOCK), 0)
    qt = lax.broadcasted_iota(jnp.int32, (LANES, Q_BLOCK), 1)
    neg_inf = -jnp.inf
    for j, t0 in enumerate(starts):
        orow = slice(j * Q_BLOCK, (j + 1) * Q_BLOCK)
        row_t = t0 + lax.broadcasted_iota(jnp.int32, (Q_BLOCK, LANES), 0)
        has_cmp = row_t >= (CMP_LEN - 1)
        imp = jnp.zeros((Q_BLOCK, LANES), F32)
        for p in range(PAIRS):
            re = _rows(p)
            ro = _rows(PAIRS + p)
            de = jnp.maximum(r_scr[j, re, LANES:2 * LANES], 1e-30)
            do = jnp.maximum(r_scr[j, ro, LANES:2 * LANES], 1e-30)
            o = jnp.where(even, r_scr[j, re, 0:LANES] / de, r_scr[j, ro, 0:LANES] / do)
            o_ref[orow, p * LANES:(p + 1) * LANES] = jnp.where(has_cmp, o, 0.0)
            imp = imp + r_scr[j, re, 2 * LANES:3 * LANES] / de + r_scr[j, ro, 2 * LANES:3 * LANES] / do
        imp = jnp.where(has_cmp, imp, 0.0)

        imp_t = imp.T
        cur = (t0 + qt) // SEL_LEN
        causal = ji <= cur
        forced = (ji == 0) | (ji == cur) | (ji == cur - 1)
        score = jnp.where(forced, neg_inf, jnp.where(causal, imp_t, MASK_VALUE))
        score = jnp.where(ji < n_sel, score, neg_inf)
        picked = jnp.where(forced, 1.0, 0.0)
        for _ in range(SEL_TOPK - SEL_FORCED):
            mx = jnp.max(score, axis=0, keepdims=True)
            first = jnp.min(jnp.where(score == mx, ji, LANES), axis=0, keepdims=True)
            hit = ji == first
            picked = jnp.where(hit, 1.0, picked)
            score = jnp.where(hit, neg_inf, score)
        picked = jnp.where(ji < cur, picked, 0.0)
        sel_ref[0, 0, orow, :] = picked.T.astype(sel_ref.dtype)


def nsa_cmp_select(q, kc2, vc_aug, nslopes, batch, seq, q_col_block):
    nstep = seq // (Q_SUB * Q_BLOCK)
    rows = Q_SUB * Q_BLOCK
    ncp = seq // CMP_STRIDE
    n_cmp = (seq - CMP_LEN) // CMP_STRIDE + 1
    n_sel = seq // SEL_LEN
    gw = PAIRS * LANES
    kern = functools.partial(_nsa_cmp_kernel, ncp=ncp, n_cmp=n_cmp, n_sel=n_sel)
    return pl.pallas_call(
        kern,
        grid=(batch, KV_GROUPS, nstep),
        in_specs=[
            pl.BlockSpec(memory_space=pltpu.SMEM),
            pl.BlockSpec((rows, gw), lambda b, g, c: (b * nstep + c, q_col_block + g)),
            pl.BlockSpec((1, 1, ncp, LANES), lambda b, g, c: (b, g, 0, 0)),
            pl.BlockSpec((1, 1, ncp, 3 * LANES), lambda b, g, c: (b, g, 0, 0)),
        ],
        out_specs=[
            pl.BlockSpec((rows, gw), lambda b, g, c: (b * nstep + c, g)),
            pl.BlockSpec((1, 1, rows, LANES), lambda b, g, c: (b, g, c, 0)),
        ],
        out_shape=[
            jax.ShapeDtypeStruct((batch * seq, N_HEADS * HEAD_DIM), F32),
            jax.ShapeDtypeStruct((batch, KV_GROUPS, seq, LANES), MXU_DTYPE),
        ],
        scratch_shapes=[pltpu.VMEM((Q_SUB, ROWS, LANES), MXU_DTYPE), pltpu.VMEM((Q_SUB, ROWS, ncp), MXU_DTYPE),
                        pltpu.VMEM((Q_SUB, ROWS, 3 * LANES), F32)],
        compiler_params=_params("parallel", "parallel", "arbitrary"),
        name="nsa_cmp_select",
    )(nslopes, q, kc2, vc_aug)


N_FEAT = 6


def sel_query_features():
    s = -jnp.asarray(_alibi_neg_slopes())
    s1 = s.astype(MXU_DTYPE).astype(F32)
    s2 = (s - s1).astype(MXU_DTYPE).astype(F32)
    s3 = (s - s1 - s2).astype(MXU_DTYPE).astype(F32)
    feat = jnp.zeros((N_HEADS, LANES), F32).at[:, HEAD_DIM:HEAD_DIM + N_FEAT].set(
        jnp.stack([s1, s2, s3, s1, s2, s3], axis=1))
    feat = feat.at[:, HEAD_DIM + N_FEAT].set(MASK_VALUE)
    order = np.array([[_head_of_row_block(g, rb) for rb in range(GROUP)] for g in range(KV_GROUPS)])
    return feat[order]


def sel_key_features(seq):
    pos = np.arange(seq)
    kk = pos % SEL_STEP
    f = np.zeros((seq + SEL_STEP, 2 * LANES), np.float32)
    f[:seq, HEAD_DIM:HEAD_DIM + 3] = (SEL_LEN * (kk // SEL_LEN))[:, None]
    f[:seq, HEAD_DIM + 3:HEAD_DIM + 6] = (kk % SEL_LEN)[:, None]
    f[seq:, HEAD_DIM + N_FEAT] = 1.0
    f[pos, LANES + pos // SEL_LEN] = 1.0
    return f


def _nsa_sel_kernel(delta_ref, qfeat_ref, q_ref, sel_ref, k_in, v_in, kfeat_ref, gate_ref, ex_ref, oc_ref, ow_ref,
                    o_ref, qa_ref, m_ref, acc_ref, s_a, s_b, p_a, p_b, al_a, al_b, k_ref, v_ref):
    grp = pl.program_id(1)
    c = pl.program_id(2)
    t0 = c * Q_BLOCK
    seq = k_in.shape[0]

    @pl.when(c == 0)
    def _():
        def put_k(r0, y):
            k_ref[pl.ds(r0, KV_CHUNK), 0:LANES] = y
            k_ref[pl.ds(r0, KV_CHUNK), LANES:2 * LANES] = kfeat_ref[pl.ds(r0, KV_CHUNK), LANES:2 * LANES]

        def put_v(r0, y):
            v_ref[pl.ds(r0, KV_CHUNK), 0:LANES] = y
            v_ref[pl.ds(r0, KV_CHUNK), LANES:2 * LANES] = jnp.ones((KV_CHUNK, LANES), v_ref.dtype)

        _unpack_group(k_in, grp, put_k, fill_ref=kfeat_ref)
        _unpack_group(v_in, grp, put_v)
        k_ref[seq:seq + SEL_STEP, :] = kfeat_ref[seq:seq + SEL_STEP, :]
        v_ref[seq:seq + SEL_STEP, :] = jnp.zeros((SEL_STEP, 2 * LANES), v_ref.dtype)
    lane = lax.broadcasted_iota(jnp.int32, (Q_BLOCK, LANES), 1)
    low = lane < HEAD_DIM
    selneg = ((1.0 - sel_ref[0, 0].astype(F32)) * MASK_VALUE).astype(qa_ref.dtype)
    for p in range(PAIRS):
        qp = q_ref[:, p * LANES:(p + 1) * LANES].astype(F32) * (HEAD_DIM ** -0.5)
        for rb, src in ((p, qp), (PAIRS + p, pltpu.roll(qp, HEAD_DIM, axis=1))):
            qa_ref[_rows(rb), 0:LANES] = jnp.where(low, src, qfeat_ref[0, rb:rb + 1, :]).astype(qa_ref.dtype)
            qa_ref[_rows(rb), LANES:2 * LANES] = selneg
    n_steps = t0 // SEL_STEP + 1
    n_pad_step = seq // SEL_STEP
    dot_nt = (((1,), (1,)), ((), ()))

    def key_start(step):
        return pl.multiple_of(jnp.clip(step, 0, n_pad_step) * SEL_STEP, SEL_STEP)

    def scores(step, s_out):
        k = k_ref[pl.ds(key_start(step), SEL_STEP), :]
        s_out[...] = lax.dot_general(qa_ref[...], k, dot_nt, preferred_element_type=F32)

    def softmax(s_in, p_out, al_out):
        for rb in range(GROUP):
            m_prev = m_ref[_rows(rb), :] - delta_ref[_head_of_row_block(grp, rb)]
            m_new = jnp.maximum(m_prev, jnp.max(s_in[_rows(rb), :], axis=1, keepdims=True))
            al_out[_rows(rb), :] = jnp.exp(m_prev - m_new)
            m_ref[_rows(rb), :] = m_new
        for rb in range(GROUP):
            p_out[_rows(rb), :] = jnp.exp(s_in[_rows(rb), :] - _rep(m_ref[_rows(rb), :], SEL_STEP)).astype(p_out.dtype)

    def values(step, p_in, al_in):
        v = v_ref[pl.ds(key_start(step), SEL_STEP), :]
        pv = jnp.dot(p_in[...], v, preferred_element_type=F32)
        alpha = al_in[...]
        acc_ref[...] = acc_ref[...] * jnp.concatenate([alpha, alpha], axis=1) + pv

    def even_half(t):
        scores(t, s_a)
        softmax(s_b, p_b, al_b)
        values(t - 2, p_a, al_a)

    def odd_half(t):
        scores(t, s_b)
        softmax(s_a, p_a, al_a)
        values(t - 2, p_b, al_b)

    scores(0, s_a)
    own = pl.multiple_of(t0, Q_BLOCK)
    qi = lax.broadcasted_iota(jnp.int32, (Q_BLOCK, Q_BLOCK), 0)
    ki = lax.broadcasted_iota(jnp.int32, (Q_BLOCK, Q_BLOCK), 1)
    own_bias = jnp.where((ki <= qi) & (ki // SEL_LEN == qi // SEL_LEN), 0.0, MASK_VALUE)
    s_own = lax.dot_general(qa_ref[:, 0:LANES], k_ref[pl.ds(own, Q_BLOCK), 0:LANES], dot_nt,
                            preferred_element_type=F32)
    scores(1, s_b)
    back = jnp.full((Q_BLOCK, LANES), n_steps, jnp.int32).astype(F32)
    p_own = []
    for rb in range(GROUP):
        s = s_own[_rows(rb)] + own_bias
        m = jnp.max(s, axis=1, keepdims=True)
        p_own.append(jnp.exp(s - m).astype(MXU_DTYPE))
        m_ref[_rows(rb), :] = m + back * delta_ref[_head_of_row_block(grp, rb)]
    softmax(s_a, p_a, al_a)
    acc_ref[...] = jnp.dot(jnp.concatenate(p_own, axis=0), v_ref[pl.ds(own, Q_BLOCK), :],
                           preferred_element_type=F32)

    def quad(j, carry):
        even_half(4 * j + 2)
        odd_half(4 * j + 3)
        even_half(4 * j + 4)
        odd_half(4 * j + 5)
        return carry

    lax.fori_loop(0, n_steps // 4, quad, 0)
    rest = 4 * (n_steps // 4) + 2

    @pl.when(n_steps % 4 >= 2)
    def _():
        even_half(rest)
        odd_half(rest + 1)

    @pl.when(n_steps % 2 == 1)
    def _():
        even_half(n_steps + 1)
    o_sel = _pairs(lambda rb: acc_ref[_rows(rb), 0:LANES], lambda rb: acc_ref[_rows(rb), LANES:2 * LANES])

    gw = PAIRS * LANES
    sig = jax.nn.sigmoid(gate_ref[...])
    hi = sig.astype(MXU_DTYPE)
    lo = (sig - hi.astype(F32)).astype(MXU_DTYPE)
    ex = ex_ref[0]
    g = jnp.dot(hi, ex, preferred_element_type=F32) + jnp.dot(lo, ex, preferred_element_type=F32)
    for p in range(PAIRS):
        cols = slice(p * LANES, (p + 1) * LANES)
        out = (g[:, cols] * oc_ref[:, cols]
               + g[:, gw + p * LANES:gw + (p + 1) * LANES] * o_sel[p]
               + g[:, 2 * gw + p * LANES:2 * gw + (p + 1) * LANES] * ow_ref[:, cols])
        o_ref[:, cols] = out.astype(o_ref.dtype)


def _gate_expansion():
    gw = PAIRS * LANES
    ex = np.zeros((KV_GROUPS, LANES, 3 * gw), np.float32)
    for g in range(KV_GROUPS):
        for hl in range(GROUP):
            for i in range(3):
                ex[g, 3 * (g * GROUP + hl) + i, i * gw + hl * HEAD_DIM:i * gw + (hl + 1) * HEAD_DIM] = 1.0
    return ex


def nsa_sel_attention(qkv, sel, gate, o_cmp, o_win, batch, seq, k_col, v_col):
    nblk = seq // Q_BLOCK
    gw = PAIRS * LANES
    kd = KV_GROUPS * HEAD_DIM
    deltas = jnp.asarray((-_alibi_neg_slopes() * SEL_STEP).astype(np.float32))
    ex = jnp.asarray(_gate_expansion(), MXU_DTYPE)
    kfeat = jnp.asarray(sel_key_features(seq), MXU_DTYPE)
    blk = lambda b, g, c: (b * nblk + c, g)
    return pl.pallas_call(
        _nsa_sel_kernel,
        grid=(batch, KV_GROUPS, nblk),
        in_specs=[
            pl.BlockSpec(memory_space=pltpu.SMEM),
            pl.BlockSpec((1, GROUP, LANES), lambda b, g, c: (g, 0, 0)),
            pl.BlockSpec((Q_BLOCK, gw), blk),
            pl.BlockSpec((1, 1, Q_BLOCK, LANES), lambda b, g, c: (b, g, c, 0)),
            _kv_block_spec(seq, k_col),
            _kv_block_spec(seq, v_col),
            pl.BlockSpec((seq + SEL_STEP, 2 * LANES), lambda b, g, c: (0, 0)),
            pl.BlockSpec((Q_BLOCK, LANES), lambda b, g, c: (b * nblk + c, 0)),
            pl.BlockSpec((1, LANES, 3 * gw), lambda b, g, c: (g, 0, 0)),
            pl.BlockSpec((Q_BLOCK, gw), blk),
            pl.BlockSpec((Q_BLOCK, gw), blk),
        ],
        out_specs=pl.BlockSpec((Q_BLOCK, gw), blk),
        out_shape=jax.ShapeDtypeStruct((batch * seq, N_HEADS * HEAD_DIM), MXU_DTYPE),
        scratch_shapes=[
            pltpu.VMEM((ROWS, 2 * LANES), MXU_DTYPE),
            pltpu.VMEM((ROWS, LANES), F32),
            pltpu.VMEM((ROWS, 2 * LANES), F32),
            pltpu.VMEM((ROWS, SEL_STEP), F32),
            pltpu.VMEM((ROWS, SEL_STEP), F32),
            pltpu.VMEM((ROWS, SEL_STEP), MXU_DTYPE),
            pltpu.VMEM((ROWS, SEL_STEP), MXU_DTYPE),
            pltpu.VMEM((ROWS, LANES), F32),
            pltpu.VMEM((ROWS, LANES), F32),
            pltpu.VMEM((seq + SEL_STEP, 2 * LANES), MXU_DTYPE),
            pltpu.VMEM((seq + SEL_STEP, 2 * LANES), MXU_DTYPE),
        ],
        compiler_params=_params("parallel", "parallel", "arbitrary"),
        name="nsa_sel_attention",
    )(deltas, sel_query_features(), qkv, sel, qkv, qkv, kfeat, gate, ex, o_cmp, o_win)


def _nsa_win_kernel(nslope_ref, q_ref, k_in, v_in, o_ref, qs_ref, bias_scr, k_ref, v_ref):
    grp = pl.program_id(1)
    c = pl.program_id(2)
    span = NSA_WINDOW + Q_BLOCK
    lead = NSA_WINDOW // Q_BLOCK
    for j in range(Q_SUB):
        _build_q_stack(q_ref, qs_ref.at[j], j * Q_BLOCK)
    nslope = lambda rb: nslope_ref[_head_of_row_block(grp, rb)]

    def write(j, outs):
        for p, o in enumerate(outs):
            o_ref[j * Q_BLOCK:(j + 1) * Q_BLOCK, p * LANES:(p + 1) * LANES] = o

    @pl.when(c == 0)
    def _():
        _unpack_kv(k_in, v_in, grp, k_ref, v_ref)
        d = _pos_tiles(NSA_WINDOW, 0, span)
        for rb in range(GROUP):
            bias_scr[rb] = _band_bias(nslope(rb), d, NSA_WINDOW)

    @pl.when(c < lead // Q_SUB)
    def _():
        for j in range(Q_SUB):
            d = _pos_tiles((c * Q_SUB + j) * Q_BLOCK, 0, span)
            write(j, _window_attend(qs_ref.at[j], k_ref[0:span, :], v_ref[0:span, :],
                                    lambda rb: _band_bias(nslope(rb), d, NSA_WINDOW)))

    @pl.when(c >= lead // Q_SUB)
    def _():
        for j in range(Q_SUB):
            start = pl.multiple_of((c * Q_SUB + j) * Q_BLOCK - NSA_WINDOW, Q_BLOCK)
            write(j, _window_attend(qs_ref.at[j], k_ref[pl.ds(start, span), :], v_ref[pl.ds(start, span), :],
                                    lambda rb: bias_scr[rb]))


def nsa_win_attention(qkv, nslopes, batch, seq, k_col, v_col):
    nstep = seq // (Q_SUB * Q_BLOCK)
    rows = Q_SUB * Q_BLOCK
    gw = PAIRS * LANES
    kd = KV_GROUPS * HEAD_DIM
    span = NSA_WINDOW + Q_BLOCK
    return pl.pallas_call(
        _nsa_win_kernel,
        grid=(batch, KV_GROUPS, nstep),
        in_specs=[
            pl.BlockSpec(memory_space=pltpu.SMEM),
            pl.BlockSpec((rows, gw), lambda b, g, c: (b * nstep + c, g)),
            _kv_block_spec(seq, k_col),
            _kv_block_spec(seq, v_col),
        ],
        out_specs=pl.BlockSpec((rows, gw), lambda b, g, c: (b * nstep + c, g)),
        out_shape=jax.ShapeDtypeStruct((batch * seq, N_HEADS * HEAD_DIM), F32),
        scratch_shapes=[
            pltpu.VMEM((Q_SUB, ROWS, LANES), MXU_DTYPE),
            pltpu.VMEM((GROUP, Q_BLOCK, span), F32),
            pltpu.VMEM((seq, LANES), MXU_DTYPE),
            pltpu.VMEM((seq, 2 * LANES), MXU_DTYPE),
        ],
        compiler_params=_params("parallel", "parallel", "arbitrary"),
        name="nsa_win_attention",
    )(nslopes, qkv, qkv, qkv)


def _overlap_matrix(seq):
    ncp = seq // CMP_STRIDE
    n_cmp = (seq - CMP_LEN) // CMP_STRIDE + 1
    cs = np.arange(n_cmp) * CMP_STRIDE
    ss = np.arange(seq // SEL_LEN) * SEL_LEN
    ov = (cs[:, None] < ss[None, :] + SEL_LEN) & (cs[:, None] + CMP_LEN > ss[None, :])
    out = np.zeros((ncp, LANES), np.float32)
    out[:n_cmp, :seq // SEL_LEN] = ov
    return out


def _swa_layer(h, x, w_in, b_in, sinks, w_o, b_o, g_post, g_next, nslopes, batch, seq):
    hd = N_HEADS * HEAD_DIM
    kd = KV_GROUPS * HEAD_DIM
    qkv = matmul_bias(h, w_in.astype(MXU_DTYPE), b_in, MXU_DTYPE, name="swa_in_proj")
    o = swa_attention(qkv, nslopes, sinks.astype(F32), batch, seq)
    return matmul_norm_res(o, w_o.astype(MXU_DTYPE), b_o, g_post, g_next, x, tn=w_o.shape[1], name="swa_out_proj")


def _nsa_layer(h, x, w_in, cmp_pe, cmp_w1, cmp_b1, cmp_w2, cmp_b2, w_o, g_post, g_next, nslopes, batch, seq):
    hd = N_HEADS * HEAD_DIM
    kd = KV_GROUPS * HEAD_DIM
    t = batch * seq
    ncp = seq // CMP_STRIDE
    qkv = matmul_bias(h, w_in[:, :hd + 6 * kd].astype(MXU_DTYPE), jnp.zeros((hd + 6 * kd,), F32), MXU_DTYPE,
                      name="nsa_in_proj")
    n_gate = 3 * N_HEADS
    w_gate = jnp.pad(w_in[:, hd + 6 * kd:], ((0, 0), (0, LANES - n_gate))).astype(MXU_DTYPE)
    gate = matmul_bias(h, w_gate, jnp.zeros((LANES,), F32), F32, name="nsa_gate_proj")

    def kv(i):
        return qkv[:, hd + i * kd:hd + (i + 1) * kd]

    def slabs(a):
        return a.reshape(batch, seq, KV_GROUPS, HEAD_DIM).transpose(0, 2, 1, 3).reshape(
            batch, KV_GROUPS, ncp, CMP_STRIDE * HEAD_DIM)

    z = jnp.stack([slabs(kv(0)), slabs(kv(1))])
    half = CMP_STRIDE * HEAD_DIM
    cmp_out = compress(z, cmp_pe.reshape(2, 2, half).astype(F32), cmp_w1.astype(MXU_DTYPE),
                       cmp_b1.reshape(2, 1, -1), cmp_w2.astype(MXU_DTYPE), cmp_b2.reshape(2, 1, -1))
    kcm = cmp_out[0].astype(MXU_DTYPE)
    vcm = cmp_out[1].astype(MXU_DTYPE)
    kc2 = jnp.concatenate([kcm, kcm], axis=-1)
    ov = jnp.broadcast_to(jnp.asarray(_overlap_matrix(seq), MXU_DTYPE), (batch, KV_GROUPS, ncp, LANES))
    vc_aug = jnp.concatenate([vcm, vcm, jnp.ones((batch, KV_GROUPS, ncp, LANES), MXU_DTYPE), ov], axis=-1)

    o_cmp, sel = nsa_cmp_select(qkv, kc2, vc_aug, nslopes, batch, seq, 0)
    kv_col = lambda i: hd // kd + i
    o_win = nsa_win_attention(qkv, nslopes, batch, seq, kv_col(4), kv_col(5))
    o = nsa_sel_attention(qkv, sel, gate, o_cmp, o_win, batch, seq, kv_col(2), kv_col(3))
    return matmul_norm_res(o, w_o.astype(MXU_DTYPE), jnp.zeros((w_o.shape[1],), F32), g_post, g_next, x,
                           tn=w_o.shape[1], name="nsa_out_proj")


def kernel(x, norm_g, swa_w_in, swa_b_in, swa_sinks, swa_w_o, swa_b_o, nsa_w_in, nsa_cmp_pe, nsa_cmp_w1, nsa_cmp_b1, nsa_cmp_w2, nsa_cmp_b2, nsa_w_o, ffn_w_gate, ffn_w_up, ffn_conv_w, ffn_conv_b, ffn_w_down):
    batch, seq, d = x.shape
    depth = norm_g.shape[0]
    nslopes = jnp.asarray(_alibi_neg_slopes())
    xf = x.reshape(batch * seq, d)
    h = rms_cast(xf, norm_g[0, 0])
    for i in range(depth):
        g = norm_g[i]
        j = i // 2
        if i % 2 == 0:
            xf, h = _swa_layer(h, xf, swa_w_in[j], swa_b_in[j], swa_sinks[j], swa_w_o[j], swa_b_o[j],
                               g[1], g[2], nslopes, batch, seq)
        else:
            xf, h = _nsa_layer(h, xf, nsa_w_in[j], nsa_cmp_pe[j], nsa_cmp_w1[j], nsa_cmp_b1[j], nsa_cmp_w2[j],
                               nsa_cmp_b2[j], nsa_w_o[j], g[1], g[2], nslopes, batch, seq)
        act = ffn_up(h, ffn_w_gate, ffn_w_up, i, ffn_conv_w[i], ffn_conv_b[i], seq)
        g_next = norm_g[i + 1, 0] if i + 1 < depth else jnp.ones((d,), F32)
        xf, h = matmul_norm_res(act, ffn_w_down[i].astype(MXU_DTYPE), jnp.zeros((d,), F32), g[3], g_next, xf,
                                tn=512, name="ffn_down")
    return xf.reshape(batch, seq, d)
```

```python
import functools

import numpy as np
import jax
import jax.numpy as jnp
from jax import lax
from jax.experimental import pallas as pl
from jax.experimental.pallas import tpu as pltpu

F32 = jnp.float32
MXU_DTYPE = jnp.bfloat16

N_HEADS = 32
HEAD_DIM = 64
KV_GROUPS = 4
GROUP = N_HEADS // KV_GROUPS
PAIRS = GROUP // 2
LANES = 128
Q_BLOCK = 128
ROWS = GROUP * Q_BLOCK
SWA_WINDOW = 128
CMP_LEN = 32
CMP_STRIDE = 16
SEL_LEN = 64
SEL_TOPK = 16
SEL_FORCED = 3
SEL_STEP = 512
SEL_Q = 2
NSA_WINDOW = 512
CONV_WIDTH = 3
RMS_EPS = 1e-6
MASK_VALUE = -1e30
VMEM_LIMIT = 56 * 1024 * 1024


def _params(*sem):
    return pltpu.CompilerParams(dimension_semantics=sem, vmem_limit_bytes=VMEM_LIMIT)


def _alibi_neg_slopes():
    return (-np.exp2(-8.0 * np.arange(1, N_HEADS + 1, dtype=np.float64) / N_HEADS)).astype(np.float32)


def _rms_cast_kernel(x_ref, g_ref, o_ref):
    x = x_ref[...]
    ms = jnp.mean(x * x, axis=-1, keepdims=True)
    o_ref[...] = (x * lax.rsqrt(ms + RMS_EPS) * g_ref[...]).astype(o_ref.dtype)


def rms_cast(x, g, tm=512):
    t, d = x.shape
    return pl.pallas_call(
        _rms_cast_kernel,
        grid=(t // tm,),
        in_specs=[pl.BlockSpec((tm, d), lambda i: (i, 0)), pl.BlockSpec((1, d), lambda i: (0, 0))],
        out_specs=pl.BlockSpec((tm, d), lambda i: (i, 0)),
        out_shape=jax.ShapeDtypeStruct((t, d), MXU_DTYPE),
        compiler_params=_params("parallel"),
        name="rms_cast",
    )(x, g.reshape(1, d))


def _matmul_kernel(a_ref, w_ref, b_ref, o_ref):
    acc = jnp.dot(a_ref[...], w_ref[...], preferred_element_type=F32)
    o_ref[...] = (acc + b_ref[...]).astype(o_ref.dtype)


def matmul_bias(a, w, b, out_dtype, tm=1024, tn=512, name="matmul_bias"):
    t, k = a.shape
    n = w.shape[1]
    tn = min(tn, n)
    return pl.pallas_call(
        _matmul_kernel,
        grid=(t // tm, n // tn),
        in_specs=[
            pl.BlockSpec((tm, k), lambda i, j: (i, 0)),
            pl.BlockSpec((k, tn), lambda i, j: (0, j)),
            pl.BlockSpec((1, tn), lambda i, j: (0, j)),
        ],
        out_specs=pl.BlockSpec((tm, tn), lambda i, j: (i, j)),
        out_shape=jax.ShapeDtypeStruct((t, n), out_dtype),
        compiler_params=_params("parallel", "arbitrary"),
        name=name,
    )(a, w, b.reshape(1, n))


def _mm_norm_res_kernel(a_ref, w_ref, b_ref, gpost_ref, gnext_ref, x_ref, xo_ref, ho_ref, y_scr, *, nj, tn, n):
    j = pl.program_id(1)
    y_scr[j] = jnp.dot(a_ref[...], w_ref[...], preferred_element_type=F32) + b_ref[...]

    @pl.when(j == nj - 1)
    def _():
        ss = jnp.sum(y_scr[0] * y_scr[0], axis=1, keepdims=True)
        for jj in range(1, nj):
            ss = ss + jnp.sum(y_scr[jj] * y_scr[jj], axis=1, keepdims=True)
        r = lax.rsqrt(ss / n + RMS_EPS)
        ss2 = jnp.zeros_like(ss)
        for jj in range(nj):
            cols = slice(jj * tn, (jj + 1) * tn)
            xn = x_ref[:, cols] + y_scr[jj] * r * gpost_ref[:, cols]
            xo_ref[:, cols] = xn
            ss2 = ss2 + jnp.sum(xn * xn, axis=1, keepdims=True)
        r2 = lax.rsqrt(ss2 / n + RMS_EPS)
        for jj in range(nj):
            cols = slice(jj * tn, (jj + 1) * tn)
            ho_ref[:, cols] = (xo_ref[:, cols] * r2 * gnext_ref[:, cols]).astype(ho_ref.dtype)


def matmul_norm_res(a, w, b, g_post, g_next, x, tm=512, tn=512, name="matmul_norm_res"):
    t, k = a.shape
    n = w.shape[1]
    nj = n // tn
    kern = functools.partial(_mm_norm_res_kernel, nj=nj, tn=tn, n=n)
    return pl.pallas_call(
        kern,
        grid=(t // tm, nj),
        in_specs=[
            pl.BlockSpec((tm, k), lambda i, j: (i, 0)),
            pl.BlockSpec((k, tn), lambda i, j: (0, j)),
            pl.BlockSpec((1, tn), lambda i, j: (0, j)),
            pl.BlockSpec((1, n), lambda i, j: (0, 0)),
            pl.BlockSpec((1, n), lambda i, j: (0, 0)),
            pl.BlockSpec((tm, n), lambda i, j: (i, 0)),
        ],
        out_specs=[
            pl.BlockSpec((tm, n), lambda i, j: (i, 0)),
            pl.BlockSpec((tm, n), lambda i, j: (i, 0)),
        ],
        out_shape=[jax.ShapeDtypeStruct((t, n), F32), jax.ShapeDtypeStruct((t, n), MXU_DTYPE)],
        scratch_shapes=[pltpu.VMEM((nj, tm, tn), F32)],
        compiler_params=_params("parallel", "arbitrary"),
        name=name,
    )(a, w, b.reshape(1, n), g_post.reshape(1, n), g_next.reshape(1, n), x)


FIX_ROWS = 16
CARRY_ROWS = 8


def _ffn_up_kernel(h_ref, wg_ref, wu_ref, cw_ref, cb_ref, o_ref, wg_scr, wu_scr, carry_scr, *, tm, tiles_per_seq):
    i = pl.program_id(1)

    @pl.when(i == 0)
    def _():
        wg_scr[...] = wg_ref[...].astype(wg_scr.dtype)
        wu_scr[...] = wu_ref[...].astype(wu_scr.dtype)

    h = h_ref[...]
    gate = jnp.dot(h, wg_scr[...], preferred_element_type=F32)
    up = jnp.dot(h, wu_scr[...], preferred_element_type=F32)
    w0 = cw_ref[0:1, :]
    w1 = cw_ref[1:2, :]
    w2 = cw_ref[2:3, :]
    b = cb_ref[...]
    a = b + pltpu.roll(gate, 2, axis=0) * w0
    a = a + pltpu.roll(gate, 1, axis=0) * w1
    a = a + gate * w2
    o_ref[...] = (jax.nn.silu(a) * up).astype(o_ref.dtype)

    seq_start = (i % tiles_per_seq) == 0
    prev = jnp.where(seq_start, 0.0, carry_scr[...])
    head = gate[0:FIX_ROWS]
    ext = jnp.concatenate([prev, head], axis=0)
    af = b + ext[CARRY_ROWS - 2:CARRY_ROWS - 2 + FIX_ROWS] * w0
    af = af + ext[CARRY_ROWS - 1:CARRY_ROWS - 1 + FIX_ROWS] * w1
    af = af + head * w2
    o_ref[0:FIX_ROWS, :] = (jax.nn.silu(af) * up[0:FIX_ROWS]).astype(o_ref.dtype)
    carry_scr[...] = gate[tm - CARRY_ROWS:tm]


def ffn_up(h, wg, wu, layer, conv_w, conv_b, seq, tm=1024, tn=512):
    t, k = h.shape
    n = wg.shape[2]
    kern = functools.partial(_ffn_up_kernel, tm=tm, tiles_per_seq=seq // tm)
    return pl.pallas_call(
        kern,
        grid=(n // tn, t // tm),
        in_specs=[
            pl.BlockSpec((tm, k), lambda j, i: (i, 0)),
            pl.BlockSpec((None, k, tn), lambda j, i: (layer, 0, j)),
            pl.BlockSpec((None, k, tn), lambda j, i: (layer, 0, j)),
            pl.BlockSpec((CONV_WIDTH, tn), lambda j, i: (0, j)),
            pl.BlockSpec((1, tn), lambda j, i: (0, j)),
        ],
        out_specs=pl.BlockSpec((tm, tn), lambda j, i: (i, j)),
        out_shape=jax.ShapeDtypeStruct((t, n), MXU_DTYPE),
        scratch_shapes=[
            pltpu.VMEM((k, tn), MXU_DTYPE),
            pltpu.VMEM((k, tn), MXU_DTYPE),
            pltpu.VMEM((CARRY_ROWS, tn), F32),
        ],
        compiler_params=_params("arbitrary", "arbitrary"),
        name="ffn_up",
    )(h, wg, wu, conv_w, conv_b.reshape(1, n))


def _head_of_row_block(group, rb):
    return group * GROUP + 2 * (rb % PAIRS) + rb // PAIRS


Q_SUB = 4


def _build_q_stack(q_ref, qs_ref, row0=0):
    lane = lax.broadcasted_iota(jnp.int32, (Q_BLOCK, LANES), 1)
    even = lane < HEAD_DIM
    for p in range(PAIRS):
        qp = q_ref[row0:row0 + Q_BLOCK, p * LANES:(p + 1) * LANES].astype(F32) * (HEAD_DIM ** -0.5)
        qs_ref[p * Q_BLOCK:(p + 1) * Q_BLOCK, :] = jnp.where(even, qp, 0.0).astype(qs_ref.dtype)
        qs_ref[(PAIRS + p) * Q_BLOCK:(PAIRS + p + 1) * Q_BLOCK, :] = jnp.where(even, 0.0, qp).astype(qs_ref.dtype)


def _rep(x, size):
    return x if size == LANES else jnp.concatenate([x] * (size // LANES), axis=1)


def _pairs(num, den):
    lane = lax.broadcasted_iota(jnp.int32, (Q_BLOCK, LANES), 1)
    even = lane < HEAD_DIM
    outs = []
    for p in range(PAIRS):
        oe = num(p) / jnp.maximum(den(p), 1e-30)
        oo = num(PAIRS + p) / jnp.maximum(den(PAIRS + p), 1e-30)
        outs.append(jnp.where(even, oe, oo))
    return outs


def _rows(rb):
    return slice(rb * Q_BLOCK, (rb + 1) * Q_BLOCK)


def _band_bias(nslope, d, window):
    return jnp.where((d >= 0) & (d < window), nslope * d.astype(F32), MASK_VALUE)


def _window_attend(qs_ref, k, v, bias, extra_logit=None):
    half = ROWS // 2
    dot_nt = (((1,), (1,)), ((), ()))
    s_halves = [lax.dot_general(qs_ref[hh * half:(hh + 1) * half, :], k, dot_nt, preferred_element_type=F32)
                for hh in range(2)]
    ps, extras = [], []
    for rb in range(GROUP):
        lo = (rb % PAIRS) * Q_BLOCK
        s = s_halves[rb // PAIRS][lo:lo + Q_BLOCK] + bias(rb)
        m = jnp.max(s, axis=1, keepdims=True)
        if extra_logit is not None:
            m = jnp.maximum(m, extra_logit(rb))
            extras.append(jnp.exp(extra_logit(rb) - m))
        ps.append(jnp.exp(s - m).astype(MXU_DTYPE))
    r_halves = [jnp.dot(jnp.concatenate(ps[hh * PAIRS:(hh + 1) * PAIRS], axis=0), v, preferred_element_type=F32)
                for hh in range(2)]

    def part(rb, cols):
        lo = (rb % PAIRS) * Q_BLOCK
        return r_halves[rb // PAIRS][lo:lo + Q_BLOCK, cols]

    num = lambda rb: part(rb, slice(0, LANES))
    if extra_logit is None:
        den = lambda rb: part(rb, slice(LANES, 2 * LANES))
    else:
        den = lambda rb: part(rb, slice(LANES, 2 * LANES)) + extras[rb]
    return _pairs(num, den)


KV_CHUNK = 1024


def _unpack_group(kv_ref, grp, write_chunk, fill_ref=None):
    n = kv_ref.shape[0]
    for odd in range(2):
        def fill(odd=odd):
            def body(i, carry):
                r0 = pl.multiple_of(i * KV_CHUNK, KV_CHUNK)
                x = kv_ref[pl.ds(r0, KV_CHUNK), :]
                xi = pltpu.bitcast(x, jnp.int32)
                xr = pltpu.roll(xi, HEAD_DIM, axis=1)
                low = lax.broadcasted_iota(jnp.int32, xi.shape, 1) < HEAD_DIM
                if fill_ref is None:
                    y = jnp.where(low, xr, xi) if odd else jnp.where(low, xi, xr)
                else:
                    f = pltpu.bitcast(fill_ref[pl.ds(r0, KV_CHUNK), 0:LANES], jnp.int32)
                    y = jnp.where(low, xr if odd else xi, f)
                write_chunk(r0, pltpu.bitcast(y, x.dtype))
                return carry

            lax.fori_loop(0, n // KV_CHUNK, body, 0)

        pl.when(grp % 2 == odd)(fill)


def _kv_block_spec(seq, col):
    pairs_per_tensor = KV_GROUPS * HEAD_DIM // LANES
    return pl.BlockSpec((seq, LANES), lambda b, g, c: (b, pairs_per_tensor * col + g // 2))


def _pos_tiles(t0, start, size):
    qi = lax.broadcasted_iota(jnp.int32, (Q_BLOCK, size), 0)
    ki = lax.broadcasted_iota(jnp.int32, (Q_BLOCK, size), 1)
    return (t0 - start) + (qi - ki)


def _unpack_kv(k_in, v_in, grp, k_ref, v_ref):
    def put_k(r0, y):
        k_ref[pl.ds(r0, KV_CHUNK), :] = y

    def put_v(r0, y):
        v_ref[pl.ds(r0, KV_CHUNK), 0:LANES] = y
        v_ref[pl.ds(r0, KV_CHUNK), LANES:2 * LANES] = jnp.ones((KV_CHUNK, LANES), v_ref.dtype)

    _unpack_group(k_in, grp, put_k)
    _unpack_group(v_in, grp, put_v)


def _swa_kernel(nslope_ref, sink_ref, q_ref, k_in, v_in, o_ref, qs_ref, bias_scr, k_ref, v_ref):
    grp = pl.program_id(1)
    c = pl.program_id(2)
    span = SWA_WINDOW + Q_BLOCK
    for j in range(Q_SUB):
        _build_q_stack(q_ref, qs_ref.at[j], j * Q_BLOCK)
    sink = lambda rb: sink_ref[_head_of_row_block(grp, rb)]

    def write(j, outs):
        for p, o in enumerate(outs):
            o_ref[j * Q_BLOCK:(j + 1) * Q_BLOCK, p * LANES:(p + 1) * LANES] = o.astype(o_ref.dtype)

    def later_block(j):
        start = pl.multiple_of((c * Q_SUB + j) * Q_BLOCK - SWA_WINDOW, Q_BLOCK)
        write(j, _window_attend(qs_ref.at[j], k_ref[pl.ds(start, span), :], v_ref[pl.ds(start, span), :],
                                lambda rb: bias_scr[rb], sink))

    @pl.when(c == 0)
    def _():
        _unpack_kv(k_in, v_in, grp, k_ref, v_ref)
        d = _pos_tiles(SWA_WINDOW, 0, span)
        for rb in range(GROUP):
            bias_scr[rb] = _band_bias(nslope_ref[_head_of_row_block(grp, rb)], d, SWA_WINDOW)

    @pl.when(c == 0)
    def _():
        write(0, _window_attend(qs_ref.at[0], k_ref[0:Q_BLOCK, :], v_ref[0:Q_BLOCK, :],
                                lambda rb: bias_scr[rb, :, SWA_WINDOW:span], sink))
        for j in range(1, Q_SUB):
            later_block(j)

    @pl.when(c > 0)
    def _():
        for j in range(Q_SUB):
            later_block(j)


def swa_attention(qkv, nslopes, sinks, batch, seq):
    nstep = seq // (Q_SUB * Q_BLOCK)
    rows = Q_SUB * Q_BLOCK
    gw = PAIRS * LANES
    kd = KV_GROUPS * HEAD_DIM
    span = SWA_WINDOW + Q_BLOCK
    k_col = N_HEADS * HEAD_DIM // kd
    return pl.pallas_call(
        _swa_kernel,
        grid=(batch, KV_GROUPS, nstep),
        in_specs=[
            pl.BlockSpec(memory_space=pltpu.SMEM),
            pl.BlockSpec(memory_space=pltpu.SMEM),
            pl.BlockSpec((rows, gw), lambda b, g, c: (b * nstep + c, g)),
            _kv_block_spec(seq, k_col),
            _kv_block_spec(seq, k_col + 1),
        ],
        out_specs=pl.BlockSpec((rows, gw), lambda b, g, c: (b * nstep + c, g)),
        out_shape=jax.ShapeDtypeStruct((batch * seq, N_HEADS * HEAD_DIM), MXU_DTYPE),
        scratch_shapes=[
            pltpu.VMEM((Q_SUB, ROWS, LANES), MXU_DTYPE),
            pltpu.VMEM((GROUP, Q_BLOCK, span), F32),
            pltpu.VMEM((seq, LANES), MXU_DTYPE),
            pltpu.VMEM((seq, 2 * LANES), MXU_DTYPE),
        ],
        compiler_params=_params("parallel", "parallel", "arbitrary"),
        name="swa_attention",
    )(nslopes, sinks, qkv, qkv, qkv)


def _compress_kernel(z_ref, pe_ref, w1_ref, b1_ref, w2_ref, b2_ref, o_ref, *, ncp):
    half = CMP_STRIDE * HEAD_DIM
    z = z_ref[0, 0, 0].astype(F32)
    top = (z + pe_ref[0, 0:1, :]).astype(MXU_DTYPE)
    bot = (z + pe_ref[0, 1:2, :]).astype(MXU_DTYPE)
    a = jnp.dot(top, w1_ref[0, 0:half, :], preferred_element_type=F32)
    bm = jnp.dot(bot, w1_ref[0, half:2 * half, :], preferred_element_type=F32)
    hid = a + pltpu.roll(bm, ncp - 1, axis=0) + b1_ref[0]
    act = jax.nn.gelu(hid).astype(MXU_DTYPE)
    o_ref[0, 0, 0] = jnp.dot(act, w2_ref[0], preferred_element_type=F32) + b2_ref[0]


def compress(z, pe, w1, b1, w2, b2):
    _, batch, groups, ncp, zw = z.shape
    hid = w1.shape[-1]
    kern = functools.partial(_compress_kernel, ncp=ncp)
    return pl.pallas_call(
        kern,
        grid=(2, batch, groups),
        in_specs=[
            pl.BlockSpec((1, 1, 1, ncp, zw), lambda s, b, g: (s, b, g, 0, 0)),
            pl.BlockSpec((1, 2, zw), lambda s, b, g: (s, 0, 0)),
            pl.BlockSpec((1, 2 * zw, hid), lambda s, b, g: (s, 0, 0)),
            pl.BlockSpec((1, 1, hid), lambda s, b, g: (s, 0, 0)),
            pl.BlockSpec((1, hid, HEAD_DIM), lambda s, b, g: (s, 0, 0)),
            pl.BlockSpec((1, 1, HEAD_DIM), lambda s, b, g: (s, 0, 0)),
        ],
        out_specs=pl.BlockSpec((1, 1, 1, ncp, HEAD_DIM), lambda s, b, g: (s, b, g, 0, 0)),
        out_shape=jax.ShapeDtypeStruct((2, batch, groups, ncp, HEAD_DIM), F32),
        compiler_params=_params("parallel", "parallel", "parallel"),
        name="nsa_compress",
    )(z, pe, w1, b1, w2, b2)


def _nsa_cmp_kernel(nslope_ref, q_ref, kc_ref, vc_ref, o_ref, sel_ref, qs_ref, e_scr, r_scr, *, ncp, n_cmp, n_sel):
    grp = pl.program_id(1)
    c = pl.program_id(2)
    starts = [(c * Q_SUB + j) * Q_BLOCK for j in range(Q_SUB)]
    for j in range(Q_SUB):
        _build_q_stack(q_ref, qs_ref.at[j], j * Q_BLOCK)

    def attend(width):
        for j, t0 in enumerate(starts):
            qi = lax.broadcasted_iota(jnp.int32, (Q_BLOCK, width), 0)
            ni = lax.broadcasted_iota(jnp.int32, (Q_BLOCK, width), 1)
            d = (t0 + qi) - (ni * CMP_STRIDE + (CMP_LEN - 1))
            negb = jnp.where((d >= 0) & (ni < n_cmp), 0.0, MASK_VALUE)
            dist = d.astype(F32)
            s_all = lax.dot_general(qs_ref[j], kc_ref[0, 0, 0:width, :], (((1,), (1,)), ((), ())),
                                    preferred_element_type=F32)
            for rb in range(GROUP):
                s = s_all[_rows(rb)] + nslope_ref[_head_of_row_block(grp, rb)] * dist + negb
                m = jnp.max(s, axis=1, keepdims=True)
                e_scr[j, _rows(rb), 0:width] = jnp.exp(s - m).astype(e_scr.dtype)
            r_scr[j] = jnp.dot(e_scr[j, :, 0:width], vc_ref[0, 0, 0:width, :], preferred_element_type=F32)

    n_chunks = ncp // LANES
    need = jnp.minimum((starts[-1] + Q_BLOCK - CMP_LEN) // CMP_STRIDE // LANES + 1, n_chunks)
    for kq in range(1, n_chunks + 1):
        pl.when(need == kq)(functools.partial(attend, kq * LANES))

    lane = lax.broadcasted_iota(jnp.int32, (Q_BLOCK, LANES), 1)
    even = lane < HEAD_DIM
    ji = lax.broadcasted_iota(jnp.int32, (LANES, Q_BLOCK), 0)
    qt = lax.broadcasted_iota(jnp.int32, (LANES, Q_BLOCK), 1)
    neg_inf = -jnp.inf
    for j, t0 in enumerate(starts):
        orow = slice(j * Q_BLOCK, (j + 1) * Q_BLOCK)
        row_t = t0 + lax.broadcasted_iota(jnp.int32, (Q_BLOCK, LANES), 0)
        has_cmp = row_t >= (CMP_LEN - 1)
        imp = jnp.zeros((Q_BLOCK, LANES), F32)
        for p in range(PAIRS):
            re = _rows(p)
            ro = _rows(PAIRS + p)
            de = jnp.maximum(r_scr[j, re, LANES:2 * LANES], 1e-30)
            do = jnp.maximum(r_scr[j, ro, LANES:2 * LANES], 1e-30)
            o = jnp.where(even, r_scr[j, re, 0:LANES] / de, r_scr[j, ro, 0:LANES] / do)
            o_ref[orow, p * LANES:(p + 1) * LANES] = jnp.where(has_cmp, o, 0.0)
            imp = imp + r_scr[j, re, 2 * LANES:3 * LANES] / de + r_scr[j, ro, 2 * LANES:3 * LANES] / do
        imp = jnp.where(has_cmp, imp, 0.0)

        imp_t = imp.T
        cur = (t0 + qt) // SEL_LEN
        causal = ji <= cur
        forced = (ji == 0) | (ji == cur) | (ji == cur - 1)
        score = jnp.where(forced, neg_inf, jnp.where(causal, imp_t, MASK_VALUE))
        score = jnp.where(ji < n_sel, score, neg_inf)
        picked = jnp.where(forced, 1.0, 0.0)
        for _ in range(SEL_TOPK - SEL_FORCED):
            mx = jnp.max(score, axis=0, keepdims=True)
            first = jnp.min(jnp.where(score == mx, ji, LANES), axis=0, keepdims=True)
            hit = ji == first
            picked = jnp.where(hit, 1.0, picked)
            score = jnp.where(hit, neg_inf, score)
        picked = jnp.where(ji < cur, picked, 0.0)
        sel_ref[0, 0, orow, :] = picked.T.astype(sel_ref.dtype)


def nsa_cmp_select(q, kc2, vc_aug, nslopes, batch, seq, q_col_block):
    nstep = seq // (Q_SUB * Q_BLOCK)
    rows = Q_SUB * Q_BLOCK
    ncp = seq // CMP_STRIDE
    n_cmp = (seq - CMP_LEN) // CMP_STRIDE + 1
    n_sel = seq // SEL_LEN
    gw = PAIRS * LANES
    kern = functools.partial(_nsa_cmp_kernel, ncp=ncp, n_cmp=n_cmp, n_sel=n_sel)
    return pl.pallas_call(
        kern,
        grid=(batch, KV_GROUPS, nstep),
        in_specs=[
            pl.BlockSpec(memory_space=pltpu.SMEM),
            pl.BlockSpec((rows, gw), lambda b, g, c: (b * nstep + c, q_col_block + g)),
            pl.BlockSpec((1, 1, ncp, LANES), lambda b, g, c: (b, g, 0, 0)),
            pl.BlockSpec((1, 1, ncp, 3 * LANES), lambda b, g, c: (b, g, 0, 0)),
        ],
        out_specs=[
            pl.BlockSpec((rows, gw), lambda b, g, c: (b * nstep + c, g)),
            pl.BlockSpec((1, 1, rows, LANES), lambda b, g, c: (b, g, c, 0)),
        ],
        out_shape=[
            jax.ShapeDtypeStruct((batch * seq, N_HEADS * HEAD_DIM), F32),
            jax.ShapeDtypeStruct((batch, KV_GROUPS, seq, LANES), MXU_DTYPE),
        ],
        scratch_shapes=[pltpu.VMEM((Q_SUB, ROWS, LANES), MXU_DTYPE), pltpu.VMEM((Q_SUB, ROWS, ncp), MXU_DTYPE),
                        pltpu.VMEM((Q_SUB, ROWS, 3 * LANES), F32)],
        compiler_params=_params("parallel", "parallel", "arbitrary"),
        name="nsa_cmp_select",
    )(nslopes, q, kc2, vc_aug)


N_FEAT = 6


def sel_query_features():
    s = -jnp.asarray(_alibi_neg_slopes())
    s1 = s.astype(MXU_DTYPE).astype(F32)
    s2 = (s - s1).astype(MXU_DTYPE).astype(F32)
    s3 = (s - s1 - s2).astype(MXU_DTYPE).astype(F32)
    feat = jnp.zeros((N_HEADS, LANES), F32).at[:, HEAD_DIM:HEAD_DIM + N_FEAT].set(
        jnp.stack([s1, s2, s3, s1, s2, s3], axis=1))
    feat = feat.at[:, HEAD_DIM + N_FEAT].set(MASK_VALUE)
    order = np.array([[_head_of_row_block(g, rb) for rb in range(GROUP)] for g in range(KV_GROUPS)])
    return feat[order]


def sel_key_features(seq):
    pos = np.arange(seq)
    kk = pos % SEL_STEP
    f = np.zeros((seq + SEL_STEP, 2 * LANES), np.float32)
    f[:seq, HEAD_DIM:HEAD_DIM + 3] = (SEL_LEN * (kk // SEL_LEN))[:, None]
    f[:seq, HEAD_DIM + 3:HEAD_DIM + 6] = (kk % SEL_LEN)[:, None]
    f[seq:, HEAD_DIM + N_FEAT] = 1.0
    f[pos, LANES + pos // SEL_LEN] = 1.0
    return f


def _nsa_sel_kernel(delta_ref, qfeat_ref, q_ref, sel_ref, k_in, v_in, kfeat_ref, gate_ref, ex_ref, oc_ref, ow_ref,
                    o_ref, qa_ref, m_ref, acc_ref, s_a, s_b, p_a, p_b, al_a, al_b, k_ref, v_ref):
    grp = pl.program_id(1)
    c = pl.program_id(2)
    t0 = c * (SEL_Q * Q_BLOCK)
    seq = k_in.shape[0]
    srows = lambda j, rb: slice(j * ROWS + rb * Q_BLOCK, j * ROWS + (rb + 1) * Q_BLOCK)
    qrows = lambda j: slice(j * Q_BLOCK, (j + 1) * Q_BLOCK)

    @pl.when(c == 0)
    def _():
        def put_k(r0, y):
            k_ref[pl.ds(r0, KV_CHUNK), 0:LANES] = y
            k_ref[pl.ds(r0, KV_CHUNK), LANES:2 * LANES] = kfeat_ref[pl.ds(r0, KV_CHUNK), LANES:2 * LANES]

        def put_v(r0, y):
            v_ref[pl.ds(r0, KV_CHUNK), 0:LANES] = y
            v_ref[pl.ds(r0, KV_CHUNK), LANES:2 * LANES] = jnp.ones((KV_CHUNK, LANES), v_ref.dtype)

        _unpack_group(k_in, grp, put_k, fill_ref=kfeat_ref)
        _unpack_group(v_in, grp, put_v)
        k_ref[seq:seq + SEL_STEP, :] = kfeat_ref[seq:seq + SEL_STEP, :]
        v_ref[seq:seq + SEL_STEP, :] = jnp.zeros((SEL_STEP, 2 * LANES), v_ref.dtype)
    lane = lax.broadcasted_iota(jnp.int32, (Q_BLOCK, LANES), 1)
    low = lane < HEAD_DIM
    for j in range(SEL_Q):
        selneg = ((1.0 - sel_ref[0, 0, qrows(j), :].astype(F32)) * MASK_VALUE).astype(qa_ref.dtype)
        for p in range(PAIRS):
            qp = q_ref[qrows(j), p * LANES:(p + 1) * LANES].astype(F32) * (HEAD_DIM ** -0.5)
            for rb, src in ((p, qp), (PAIRS + p, pltpu.roll(qp, HEAD_DIM, axis=1))):
                qa_ref[srows(j, rb), 0:LANES] = jnp.where(low, src, qfeat_ref[0, rb:rb + 1, :]).astype(qa_ref.dtype)
                qa_ref[srows(j, rb), LANES:2 * LANES] = selneg
    n_steps = t0 // SEL_STEP + 1
    n_pad_step = seq // SEL_STEP
    dot_nt = (((1,), (1,)), ((), ()))

    def key_start(step):
        return pl.multiple_of(jnp.clip(step, 0, n_pad_step) * SEL_STEP, SEL_STEP)

    def scores(step, s_out):
        k = k_ref[pl.ds(key_start(step), SEL_STEP), :]
        s_out[...] = lax.dot_general(qa_ref[...], k, dot_nt, preferred_element_type=F32)

    def softmax(s_in, p_out, al_out):
        for rb in range(SEL_Q * GROUP):
            m_prev = m_ref[_rows(rb), :] - delta_ref[_head_of_row_block(grp, rb % GROUP)]
            m_new = jnp.maximum(m_prev, jnp.max(s_in[_rows(rb), :], axis=1, keepdims=True))
            al_out[_rows(rb), :] = jnp.exp(m_prev - m_new)
            m_ref[_rows(rb), :] = m_new
        for rb in range(SEL_Q * GROUP):
            p_out[_rows(rb), :] = jnp.exp(s_in[_rows(rb), :] - _rep(m_ref[_rows(rb), :], SEL_STEP)).astype(p_out.dtype)

    def values(step, p_in, al_in):
        v = v_ref[pl.ds(key_start(step), SEL_STEP), :]
        pv = jnp.dot(p_in[...], v, preferred_element_type=F32)
        alpha = al_in[...]
        acc_ref[...] = acc_ref[...] * jnp.concatenate([alpha, alpha], axis=1) + pv

    def even_half(t):
        scores(t, s_a)
        softmax(s_b, p_b, al_b)
        values(t - 2, p_a, al_a)

    def odd_half(t):
        scores(t, s_b)
        softmax(s_a, p_a, al_a)
        values(t - 2, p_b, al_b)

    scores(0, s_a)
    qi = lax.broadcasted_iota(jnp.int32, (Q_BLOCK, Q_BLOCK), 0)
    ki = lax.broadcasted_iota(jnp.int32, (Q_BLOCK, Q_BLOCK), 1)
    own_bias = jnp.where((ki <= qi) & (ki // SEL_LEN == qi // SEL_LEN), 0.0, MASK_VALUE)
    owns = [pl.multiple_of(t0 + j * Q_BLOCK, Q_BLOCK) for j in range(SEL_Q)]
    s_owns = [lax.dot_general(qa_ref[j * ROWS:(j + 1) * ROWS, 0:LANES], k_ref[pl.ds(owns[j], Q_BLOCK), 0:LANES],
                              dot_nt, preferred_element_type=F32) for j in range(SEL_Q)]
    scores(1, s_b)
    back = jnp.full((Q_BLOCK, LANES), n_steps, jnp.int32).astype(F32)
    p_owns = []
    for j in range(SEL_Q):
        p_own = []
        for rb in range(GROUP):
            s = s_owns[j][_rows(rb)] + own_bias
            m = jnp.max(s, axis=1, keepdims=True)
            p_own.append(jnp.exp(s - m).astype(MXU_DTYPE))
            m_ref[srows(j, rb), :] = m + back * delta_ref[_head_of_row_block(grp, rb)]
        p_owns.append(jnp.concatenate(p_own, axis=0))
    softmax(s_a, p_a, al_a)
    for j in range(SEL_Q):
        acc_ref[j * ROWS:(j + 1) * ROWS, :] = jnp.dot(p_owns[j], v_ref[pl.ds(owns[j], Q_BLOCK), :],
                                                      preferred_element_type=F32)

    def quad(j, carry):
        even_half(4 * j + 2)
        odd_half(4 * j + 3)
        even_half(4 * j + 4)
        odd_half(4 * j + 5)
        return carry

    lax.fori_loop(0, n_steps // 4, quad, 0)
    rest = 4 * (n_steps // 4) + 2

    @pl.when(n_steps % 4 >= 2)
    def _():
        even_half(rest)
        odd_half(rest + 1)

    @pl.when(n_steps % 2 == 1)
    def _():
        even_half(n_steps + 1)

    gw = PAIRS * LANES
    sig = jax.nn.sigmoid(gate_ref[...])
    hi = sig.astype(MXU_DTYPE)
    lo = (sig - hi.astype(F32)).astype(MXU_DTYPE)
    ex = ex_ref[0]
    g = jnp.dot(hi, ex, preferred_element_type=F32) + jnp.dot(lo, ex, preferred_element_type=F32)
    for j in range(SEL_Q):
        o_sel = _pairs(lambda rb: acc_ref[srows(j, rb), 0:LANES], lambda rb: acc_ref[srows(j, rb), LANES:2 * LANES])
        for p in range(PAIRS):
            cols = slice(p * LANES, (p + 1) * LANES)
            out = (g[qrows(j), cols] * oc_ref[qrows(j), cols]
                   + g[qrows(j), gw + p * LANES:gw + (p + 1) * LANES] * o_sel[p]
                   + g[qrows(j), 2 * gw + p * LANES:2 * gw + (p + 1) * LANES] * ow_ref[qrows(j), cols])
            o_ref[qrows(j), cols] = out.astype(o_ref.dtype)


def _gate_expansion():
    gw = PAIRS * LANES
    ex = np.zeros((KV_GROUPS, LANES, 3 * gw), np.float32)
    for g in range(KV_GROUPS):
        for hl in range(GROUP):
            for i in range(3):
                ex[g, 3 * (g * GROUP + hl) + i, i * gw + hl * HEAD_DIM:i * gw + (hl + 1) * HEAD_DIM] = 1.0
    return ex


def nsa_sel_attention(qkv, sel, gate, o_cmp, o_win, batch, seq, k_col, v_col):
    nstep = seq // (SEL_Q * Q_BLOCK)
    rows = SEL_Q * Q_BLOCK
    srows = SEL_Q * ROWS
    gw = PAIRS * LANES
    deltas = jnp.asarray((-_alibi_neg_slopes() * SEL_STEP).astype(np.float32))
    ex = jnp.asarray(_gate_expansion(), MXU_DTYPE)
    kfeat = jnp.asarray(sel_key_features(seq), MXU_DTYPE)
    blk = lambda b, g, c: (b * nstep + c, g)
    return pl.pallas_call(
        _nsa_sel_kernel,
        grid=(batch, KV_GROUPS, nstep),
        in_specs=[
            pl.BlockSpec(memory_space=pltpu.SMEM),
            pl.BlockSpec((1, GROUP, LANES), lambda b, g, c: (g, 0, 0)),
            pl.BlockSpec((rows, gw), blk),
            pl.BlockSpec((1, 1, rows, LANES), lambda b, g, c: (b, g, c, 0)),
            _kv_block_spec(seq, k_col),
            _kv_block_spec(seq, v_col),
            pl.BlockSpec((seq + SEL_STEP, 2 * LANES), lambda b, g, c: (0, 0)),
            pl.BlockSpec((rows, LANES), lambda b, g, c: (b * nstep + c, 0)),
            pl.BlockSpec((1, LANES, 3 * gw), lambda b, g, c: (g, 0, 0)),
            pl.BlockSpec((rows, gw), blk),
            pl.BlockSpec((rows, gw), blk),
        ],
        out_specs=pl.BlockSpec((rows, gw), blk),
        out_shape=jax.ShapeDtypeStruct((batch * seq, N_HEADS * HEAD_DIM), MXU_DTYPE),
        scratch_shapes=[
            pltpu.VMEM((srows, 2 * LANES), MXU_DTYPE),
            pltpu.VMEM((srows, LANES), F32),
            pltpu.VMEM((srows, 2 * LANES), F32),
            pltpu.VMEM((srows, SEL_STEP), F32),
            pltpu.VMEM((srows, SEL_STEP), F32),
            pltpu.VMEM((srows, SEL_STEP), MXU_DTYPE),
            pltpu.VMEM((srows, SEL_STEP), MXU_DTYPE),
            pltpu.VMEM((srows, LANES), F32),
            pltpu.VMEM((srows, LANES), F32),
            pltpu.VMEM((seq + SEL_STEP, 2 * LANES), MXU_DTYPE),
            pltpu.VMEM((seq + SEL_STEP, 2 * LANES), MXU_DTYPE),
        ],
        compiler_params=_params("parallel", "parallel", "arbitrary"),
        name="nsa_sel_attention",
    )(deltas, sel_query_features(), qkv, sel, qkv, qkv, kfeat, gate, ex, o_cmp, o_win)


def _nsa_win_kernel(nslope_ref, q_ref, k_in, v_in, o_ref, qs_ref, bias_scr, k_ref, v_ref):
    grp = pl.program_id(1)
    c = pl.program_id(2)
    span = NSA_WINDOW + Q_BLOCK
    lead = NSA_WINDOW // Q_BLOCK
    for j in range(Q_SUB):
        _build_q_stack(q_ref, qs_ref.at[j], j * Q_BLOCK)
    nslope = lambda rb: nslope_ref[_head_of_row_block(grp, rb)]

    def write(j, outs):
        for p, o in enumerate(outs):
            o_ref[j * Q_BLOCK:(j + 1) * Q_BLOCK, p * LANES:(p + 1) * LANES] = o

    @pl.when(c == 0)
    def _():
        _unpack_kv(k_in, v_in, grp, k_ref, v_ref)
        d = _pos_tiles(NSA_WINDOW, 0, span)
        for rb in range(GROUP):
            bias_scr[rb] = _band_bias(nslope(rb), d, NSA_WINDOW)

    @pl.when(c < lead // Q_SUB)
    def _():
        for j in range(Q_SUB):
            d = _pos_tiles((c * Q_SUB + j) * Q_BLOCK, 0, span)
            write(j, _window_attend(qs_ref.at[j], k_ref[0:span, :], v_ref[0:span, :],
                                    lambda rb: _band_bias(nslope(rb), d, NSA_WINDOW)))

    @pl.when(c >= lead // Q_SUB)
    def _():
        for j in range(Q_SUB):
            start = pl.multiple_of((c * Q_SUB + j) * Q_BLOCK - NSA_WINDOW, Q_BLOCK)
            write(j, _window_attend(qs_ref.at[j], k_ref[pl.ds(start, span), :], v_ref[pl.ds(start, span), :],
                                    lambda rb: bias_scr[rb]))


def nsa_win_attention(qkv, nslopes, batch, seq, k_col, v_col):
    nstep = seq // (Q_SUB * Q_BLOCK)
    rows = Q_SUB * Q_BLOCK
    gw = PAIRS * LANES
    kd = KV_GROUPS * HEAD_DIM
    span = NSA_WINDOW + Q_BLOCK
    return pl.pallas_call(
        _nsa_win_kernel,
        grid=(batch, KV_GROUPS, nstep),
        in_specs=[
            pl.BlockSpec(memory_space=pltpu.SMEM),
            pl.BlockSpec((rows, gw), lambda b, g, c: (b * nstep + c, g)),
            _kv_block_spec(seq, k_col),
            _kv_block_spec(seq, v_col),
        ],
        out_specs=pl.BlockSpec((rows, gw), lambda b, g, c: (b * nstep + c, g)),
        out_shape=jax.ShapeDtypeStruct((batch * seq, N_HEADS * HEAD_DIM), F32),
        scratch_shapes=[
            pltpu.VMEM((Q_SUB, ROWS, LANES), MXU_DTYPE),
            pltpu.VMEM((GROUP, Q_BLOCK, span), F32),
            pltpu.VMEM((seq, LANES), MXU_DTYPE),
            pltpu.VMEM((seq, 2 * LANES), MXU_DTYPE),
        ],
        compiler_params=_params("parallel", "parallel", "arbitrary"),
        name="nsa_win_attention",
    )(nslopes, qkv, qkv, qkv)


def _overlap_matrix(seq):
    ncp = seq // CMP_STRIDE
    n_cmp = (seq - CMP_LEN) // CMP_STRIDE + 1
    cs = np.arange(n_cmp) * CMP_STRIDE
    ss = np.arange(seq // SEL_LEN) * SEL_LEN
    ov = (cs[:, None] < ss[None, :] + SEL_LEN) & (cs[:, None] + CMP_LEN > ss[None, :])
    out = np.zeros((ncp, LANES), np.float32)
    out[:n_cmp, :seq // SEL_LEN] = ov
    return out


def _swa_layer(h, x, w_in, b_in, sinks, w_o, b_o, g_post, g_next, nslopes, batch, seq):
    hd = N_HEADS * HEAD_DIM
    kd = KV_GROUPS * HEAD_DIM
    qkv = matmul_bias(h, w_in.astype(MXU_DTYPE), b_in, MXU_DTYPE, name="swa_in_proj")
    o = swa_attention(qkv, nslopes, sinks.astype(F32), batch, seq)
    return matmul_norm_res(o, w_o.astype(MXU_DTYPE), b_o, g_post, g_next, x, tn=w_o.shape[1], name="swa_out_proj")


def _nsa_layer(h, x, w_in, cmp_pe, cmp_w1, cmp_b1, cmp_w2, cmp_b2, w_o, g_post, g_next, nslopes, batch, seq):
    hd = N_HEADS * HEAD_DIM
    kd = KV_GROUPS * HEAD_DIM
    t = batch * seq
    ncp = seq // CMP_STRIDE
    qkv = matmul_bias(h, w_in[:, :hd + 6 * kd].astype(MXU_DTYPE), jnp.zeros((hd + 6 * kd,), F32), MXU_DTYPE,
                      name="nsa_in_proj")
    n_gate = 3 * N_HEADS
    w_gate = jnp.pad(w_in[:, hd + 6 * kd:], ((0, 0), (0, LANES - n_gate))).astype(MXU_DTYPE)
    gate = matmul_bias(h, w_gate, jnp.zeros((LANES,), F32), F32, name="nsa_gate_proj")

    def kv(i):
        return qkv[:, hd + i * kd:hd + (i + 1) * kd]

    def slabs(a):
        return a.reshape(batch, seq, KV_GROUPS, HEAD_DIM).transpose(0, 2, 1, 3).reshape(
            batch, KV_GROUPS, ncp, CMP_STRIDE * HEAD_DIM)

    z = jnp.stack([slabs(kv(0)), slabs(kv(1))])
    half = CMP_STRIDE * HEAD_DIM
    cmp_out = compress(z, cmp_pe.reshape(2, 2, half).astype(F32), cmp_w1.astype(MXU_DTYPE),
                       cmp_b1.reshape(2, 1, -1), cmp_w2.astype(MXU_DTYPE), cmp_b2.reshape(2, 1, -1))
    kcm = cmp_out[0].astype(MXU_DTYPE)
    vcm = cmp_out[1].astype(MXU_DTYPE)
    kc2 = jnp.concatenate([kcm, kcm], axis=-1)
    ov = jnp.broadcast_to(jnp.asarray(_overlap_matrix(seq), MXU_DTYPE), (batch, KV_GROUPS, ncp, LANES))
    vc_aug = jnp.concatenate([vcm, vcm, jnp.ones((batch, KV_GROUPS, ncp, LANES), MXU_DTYPE), ov], axis=-1)

    o_cmp, sel = nsa_cmp_select(qkv, kc2, vc_aug, nslopes, batch, seq, 0)
    kv_col = lambda i: hd // kd + i
    o_win = nsa_win_attention(qkv, nslopes, batch, seq, kv_col(4), kv_col(5))
    o = nsa_sel_attention(qkv, sel, gate, o_cmp, o_win, batch, seq, kv_col(2), kv_col(3))
    return matmul_norm_res(o, w_o.astype(MXU_DTYPE), jnp.zeros((w_o.shape[1],), F32), g_post, g_next, x,
                           tn=w_o.shape[1], name="nsa_out_proj")


def kernel(x, norm_g, swa_w_in, swa_b_in, swa_sinks, swa_w_o, swa_b_o, nsa_w_in, nsa_cmp_pe, nsa_cmp_w1, nsa_cmp_b1, nsa_cmp_w2, nsa_cmp_b2, nsa_w_o, ffn_w_gate, ffn_w_up, ffn_conv_w, ffn_conv_b, ffn_w_down):
    batch, seq, d = x.shape
    depth = norm_g.shape[0]
    nslopes = jnp.asarray(_alibi_neg_slopes())
    xf = x.reshape(batch * seq, d)
    h = rms_cast(xf, norm_g[0, 0])
    for i in range(depth):
        g = norm_g[i]
        j = i // 2
        if i % 2 == 0:
            xf, h = _swa_layer(h, xf, swa_w_in[j], swa_b_in[j], swa_sinks[j], swa_w_o[j], swa_b_o[j],
                               g[1], g[2], nslopes, batch, seq)
        else:
            xf, h = _nsa_layer(h, xf, nsa_w_in[j], nsa_cmp_pe[j], nsa_cmp_w1[j], nsa_cmp_b1[j], nsa_cmp_w2[j],
                               nsa_cmp_b2[j], nsa_w_o[j], g[1], g[2], nslopes, batch, seq)
        act = ffn_up(h, ffn_w_gate, ffn_w_up, i, ffn_conv_w[i], ffn_conv_b[i], seq)
        g_next = norm_g[i + 1, 0] if i + 1 < depth else jnp.ones((d,), F32)
        xf, h = matmul_norm_res(act, ffn_w_down[i].astype(MXU_DTYPE), jnp.zeros((d,), F32), g[3], g_next, xf,
                                tn=512, name="ffn_down")
    return xf.reshape(batch, seq, d)
```

```python
import functools

import numpy as np
import jax
import jax.numpy as jnp
from jax import lax
from jax.experimental import pallas as pl
from jax.experimental.pallas import tpu as pltpu

F32 = jnp.float32
MXU_DTYPE = jnp.bfloat16

N_HEADS = 32
HEAD_DIM = 64
KV_GROUPS = 4
GROUP = N_HEADS // KV_GROUPS
PAIRS = GROUP // 2
LANES = 128
Q_BLOCK = 128
ROWS = GROUP * Q_BLOCK
SWA_WINDOW = 128
CMP_LEN = 32
CMP_STRIDE = 16
SEL_LEN = 64
SEL_TOPK = 16
SEL_FORCED = 3
LOG2E = 1.4426950408889634
SEL_STEP = 512
SEL_Q = 2
NSA_WINDOW = 512
CONV_WIDTH = 3
RMS_EPS = 1e-6
MASK_VALUE = -1e30
VMEM_LIMIT = 60000 * 1024


def _params(*sem):
    return pltpu.CompilerParams(dimension_semantics=sem, vmem_limit_bytes=VMEM_LIMIT)


def _alibi_neg_slopes():
    return (-np.exp2(-8.0 * np.arange(1, N_HEADS + 1, dtype=np.float64) / N_HEADS)).astype(np.float32)


def _rms_cast_kernel(x_ref, g_ref, o_ref):
    x = x_ref[...]
    ms = jnp.mean(x * x, axis=-1, keepdims=True)
    o_ref[...] = (x * lax.rsqrt(ms + RMS_EPS) * g_ref[...]).astype(o_ref.dtype)


def rms_cast(x, g, tm=512):
    t, d = x.shape
    return pl.pallas_call(
        _rms_cast_kernel,
        grid=(t // tm,),
        in_specs=[pl.BlockSpec((tm, d), lambda i: (i, 0)), pl.BlockSpec((1, d), lambda i: (0, 0))],
        out_specs=pl.BlockSpec((tm, d), lambda i: (i, 0)),
        out_shape=jax.ShapeDtypeStruct((t, d), MXU_DTYPE),
        compiler_params=_params("parallel"),
        name="rms_cast",
    )(x, g.reshape(1, d))


def _matmul_kernel(a_ref, w_ref, b_ref, o_ref):
    acc = jnp.dot(a_ref[...], w_ref[...], preferred_element_type=F32)
    o_ref[...] = (acc + b_ref[...]).astype(o_ref.dtype)


def matmul_bias(a, w, b, out_dtype, tm=1024, tn=512, name="matmul_bias"):
    t, k = a.shape
    n = w.shape[1]
    tn = min(tn, n)
    return pl.pallas_call(
        _matmul_kernel,
        grid=(t // tm, n // tn),
        in_specs=[
            pl.BlockSpec((tm, k), lambda i, j: (i, 0)),
            pl.BlockSpec((k, tn), lambda i, j: (0, j)),
            pl.BlockSpec((1, tn), lambda i, j: (0, j)),
        ],
        out_specs=pl.BlockSpec((tm, tn), lambda i, j: (i, j)),
        out_shape=jax.ShapeDtypeStruct((t, n), out_dtype),
        compiler_params=_params("parallel", "arbitrary"),
        name=name,
    )(a, w, b.reshape(1, n))


def _mm_norm_res_kernel(a_ref, w_ref, b_ref, gpost_ref, gnext_ref, x_ref, xo_ref, ho_ref, y_scr, *, nj, tn, n):
    j = pl.program_id(1)
    y_scr[j] = jnp.dot(a_ref[...], w_ref[...], preferred_element_type=F32) + b_ref[...]

    @pl.when(j == nj - 1)
    def _():
        ss = jnp.sum(y_scr[0] * y_scr[0], axis=1, keepdims=True)
        for jj in range(1, nj):
            ss = ss + jnp.sum(y_scr[jj] * y_scr[jj], axis=1, keepdims=True)
        r = lax.rsqrt(ss / n + RMS_EPS)
        ss2 = jnp.zeros_like(ss)
        for jj in range(nj):
            cols = slice(jj * tn, (jj + 1) * tn)
            xn = x_ref[:, cols] + y_scr[jj] * r * gpost_ref[:, cols]
            xo_ref[:, cols] = xn
            ss2 = ss2 + jnp.sum(xn * xn, axis=1, keepdims=True)
        r2 = lax.rsqrt(ss2 / n + RMS_EPS)
        for jj in range(nj):
            cols = slice(jj * tn, (jj + 1) * tn)
            ho_ref[:, cols] = (xo_ref[:, cols] * r2 * gnext_ref[:, cols]).astype(ho_ref.dtype)


def matmul_norm_res(a, w, b, g_post, g_next, x, tm=512, tn=512, name="matmul_norm_res"):
    t, k = a.shape
    n = w.shape[1]
    nj = n // tn
    kern = functools.partial(_mm_norm_res_kernel, nj=nj, tn=tn, n=n)
    return pl.pallas_call(
        kern,
        grid=(t // tm, nj),
        in_specs=[
            pl.BlockSpec((tm, k), lambda i, j: (i, 0)),
            pl.BlockSpec((k, tn), lambda i, j: (0, j)),
            pl.BlockSpec((1, tn), lambda i, j: (0, j)),
            pl.BlockSpec((1, n), lambda i, j: (0, 0)),
            pl.BlockSpec((1, n), lambda i, j: (0, 0)),
            pl.BlockSpec((tm, n), lambda i, j: (i, 0)),
        ],
        out_specs=[
            pl.BlockSpec((tm, n), lambda i, j: (i, 0)),
            pl.BlockSpec((tm, n), lambda i, j: (i, 0)),
        ],
        out_shape=[jax.ShapeDtypeStruct((t, n), F32), jax.ShapeDtypeStruct((t, n), MXU_DTYPE)],
        scratch_shapes=[pltpu.VMEM((nj, tm, tn), F32)],
        compiler_params=_params("parallel", "arbitrary"),
        name=name,
    )(a, w, b.reshape(1, n), g_post.reshape(1, n), g_next.reshape(1, n), x)


FIX_ROWS = 16
CARRY_ROWS = 8


def _ffn_up_kernel(h_ref, wg_ref, wu_ref, cw_ref, cb_ref, o_ref, wg_scr, wu_scr, carry_scr, *, tm, tiles_per_seq):
    i = pl.program_id(1)

    @pl.when(i == 0)
    def _():
        wg_scr[...] = wg_ref[...].astype(wg_scr.dtype)
        wu_scr[...] = wu_ref[...].astype(wu_scr.dtype)

    h = h_ref[...]
    gate = jnp.dot(h, wg_scr[...], preferred_element_type=F32)
    up = jnp.dot(h, wu_scr[...], preferred_element_type=F32)
    w0 = cw_ref[0:1, :]
    w1 = cw_ref[1:2, :]
    w2 = cw_ref[2:3, :]
    b = cb_ref[...]
    a = b + pltpu.roll(gate, 2, axis=0) * w0
    a = a + pltpu.roll(gate, 1, axis=0) * w1
    a = a + gate * w2
    o_ref[...] = (jax.nn.silu(a) * up).astype(o_ref.dtype)

    seq_start = (i % tiles_per_seq) == 0
    prev = jnp.where(seq_start, 0.0, carry_scr[...])
    head = gate[0:FIX_ROWS]
    ext = jnp.concatenate([prev, head], axis=0)
    af = b + ext[CARRY_ROWS - 2:CARRY_ROWS - 2 + FIX_ROWS] * w0
    af = af + ext[CARRY_ROWS - 1:CARRY_ROWS - 1 + FIX_ROWS] * w1
    af = af + head * w2
    o_ref[0:FIX_ROWS, :] = (jax.nn.silu(af) * up[0:FIX_ROWS]).astype(o_ref.dtype)
    carry_scr[...] = gate[tm - CARRY_ROWS:tm]


def ffn_up(h, wg, wu, layer, conv_w, conv_b, seq, tm=1024, tn=512):
    t, k = h.shape
    n = wg.shape[2]
    kern = functools.partial(_ffn_up_kernel, tm=tm, tiles_per_seq=seq // tm)
    return pl.pallas_call(
        kern,
        grid=(n // tn, t // tm),
        in_specs=[
            pl.BlockSpec((tm, k), lambda j, i: (i, 0)),
            pl.BlockSpec((None, k, tn), lambda j, i: (layer, 0, j)),
            pl.BlockSpec((None, k, tn), lambda j, i: (layer, 0, j)),
            pl.BlockSpec((CONV_WIDTH, tn), lambda j, i: (0, j)),
            pl.BlockSpec((1, tn), lambda j, i: (0, j)),
        ],
        out_specs=pl.BlockSpec((tm, tn), lambda j, i: (i, j)),
        out_shape=jax.ShapeDtypeStruct((t, n), MXU_DTYPE),
        scratch_shapes=[
            pltpu.VMEM((k, tn), MXU_DTYPE),
            pltpu.VMEM((k, tn), MXU_DTYPE),
            pltpu.VMEM((CARRY_ROWS, tn), F32),
        ],
        compiler_params=_params("arbitrary", "arbitrary"),
        name="ffn_up",
    )(h, wg, wu, conv_w, conv_b.reshape(1, n))


def _head_of_row_block(group, rb):
    return group * GROUP + 2 * (rb % PAIRS) + rb // PAIRS


Q_SUB = 4


def _build_q_stack(q_ref, qs_ref, row0=0):
    lane = lax.broadcasted_iota(jnp.int32, (Q_BLOCK, LANES), 1)
    even = lane < HEAD_DIM
    for p in range(PAIRS):
        qp = q_ref[row0:row0 + Q_BLOCK, p * LANES:(p + 1) * LANES].astype(F32) * (HEAD_DIM ** -0.5 * LOG2E)
        qs_ref[p * Q_BLOCK:(p + 1) * Q_BLOCK, :] = jnp.where(even, qp, 0.0).astype(qs_ref.dtype)
        qs_ref[(PAIRS + p) * Q_BLOCK:(PAIRS + p + 1) * Q_BLOCK, :] = jnp.where(even, 0.0, qp).astype(qs_ref.dtype)


def _rep(x, size):
    return x if size == LANES else jnp.concatenate([x] * (size // LANES), axis=1)


def _pairs(num, den):
    lane = lax.broadcasted_iota(jnp.int32, (Q_BLOCK, LANES), 1)
    even = lane < HEAD_DIM
    outs = []
    for p in range(PAIRS):
        oe = num(p) / jnp.maximum(den(p), 1e-30)
        oo = num(PAIRS + p) / jnp.maximum(den(PAIRS + p), 1e-30)
        outs.append(jnp.where(even, oe, oo))
    return outs


def _rows(rb):
    return slice(rb * Q_BLOCK, (rb + 1) * Q_BLOCK)


def _band_bias(nslope, d, window):
    return jnp.where((d >= 0) & (d < window), nslope * d.astype(F32), MASK_VALUE)


def _window_attend(qs_ref, k, v, bias, extra_logit=None):
    half = ROWS // 2
    dot_nt = (((1,), (1,)), ((), ()))
    s_halves = [lax.dot_general(qs_ref[hh * half:(hh + 1) * half, :], k, dot_nt, preferred_element_type=F32)
                for hh in range(2)]
    ps, extras = [], []
    for rb in range(GROUP):
        lo = (rb % PAIRS) * Q_BLOCK
        s = s_halves[rb // PAIRS][lo:lo + Q_BLOCK] + bias(rb)
        m = jnp.max(s, axis=1, keepdims=True)
        if extra_logit is not None:
            m = jnp.maximum(m, extra_logit(rb))
            extras.append(jnp.exp2(extra_logit(rb) - m))
        ps.append(jnp.exp2(s - m).astype(MXU_DTYPE))
    r_halves = [jnp.dot(jnp.concatenate(ps[hh * PAIRS:(hh + 1) * PAIRS], axis=0), v, preferred_element_type=F32)
                for hh in range(2)]

    def part(rb, cols):
        lo = (rb % PAIRS) * Q_BLOCK
        return r_halves[rb // PAIRS][lo:lo + Q_BLOCK, cols]

    num = lambda rb: part(rb, slice(0, LANES))
    if extra_logit is None:
        den = lambda rb: part(rb, slice(LANES, 2 * LANES))
    else:
        den = lambda rb: part(rb, slice(LANES, 2 * LANES)) + extras[rb]
    return _pairs(num, den)


KV_CHUNK = 1024


def _unpack_group(kv_ref, grp, write_chunk, fill_ref=None):
    n = kv_ref.shape[0]
    for odd in range(2):
        def fill(odd=odd):
            def body(i, carry):
                r0 = pl.multiple_of(i * KV_CHUNK, KV_CHUNK)
                x = kv_ref[pl.ds(r0, KV_CHUNK), :]
                xi = pltpu.bitcast(x, jnp.int32)
                xr = pltpu.roll(xi, HEAD_DIM, axis=1)
                low = lax.broadcasted_iota(jnp.int32, xi.shape, 1) < HEAD_DIM
                if fill_ref is None:
                    y = jnp.where(low, xr, xi) if odd else jnp.where(low, xi, xr)
                else:
                    f = pltpu.bitcast(fill_ref[pl.ds(r0, KV_CHUNK), 0:LANES], jnp.int32)
                    y = jnp.where(low, xr if odd else xi, f)
                write_chunk(r0, pltpu.bitcast(y, x.dtype))
                return carry

            lax.fori_loop(0, n // KV_CHUNK, body, 0)

        pl.when(grp % 2 == odd)(fill)


def _kv_block_spec(seq, col):
    pairs_per_tensor = KV_GROUPS * HEAD_DIM // LANES
    return pl.BlockSpec((seq, LANES), lambda b, g, c: (b, pairs_per_tensor * col + g // 2))


def _pos_tiles(t0, start, size):
    qi = lax.broadcasted_iota(jnp.int32, (Q_BLOCK, size), 0)
    ki = lax.broadcasted_iota(jnp.int32, (Q_BLOCK, size), 1)
    return (t0 - start) + (qi - ki)


def _unpack_kv(k_in, v_in, grp, k_ref, v_ref):
    def put_k(r0, y):
        k_ref[pl.ds(r0, KV_CHUNK), :] = y

    def put_v(r0, y):
        v_ref[pl.ds(r0, KV_CHUNK), 0:LANES] = y
        v_ref[pl.ds(r0, KV_CHUNK), LANES:2 * LANES] = jnp.ones((KV_CHUNK, LANES), v_ref.dtype)

    _unpack_group(k_in, grp, put_k)
    _unpack_group(v_in, grp, put_v)


def _swa_kernel(nslope_ref, sink_ref, q_ref, k_in, v_in, o_ref, qs_ref, bias_scr, k_ref, v_ref):
    grp = pl.program_id(1)
    c = pl.program_id(2)
    span = SWA_WINDOW + Q_BLOCK
    for j in range(Q_SUB):
        _build_q_stack(q_ref, qs_ref.at[j], j * Q_BLOCK)
    sink = lambda rb: sink_ref[_head_of_row_block(grp, rb)]

    def write(j, outs):
        for p, o in enumerate(outs):
            o_ref[j * Q_BLOCK:(j + 1) * Q_BLOCK, p * LANES:(p + 1) * LANES] = o.astype(o_ref.dtype)

    def later_block(j):
        start = pl.multiple_of((c * Q_SUB + j) * Q_BLOCK - SWA_WINDOW, Q_BLOCK)
        write(j, _window_attend(qs_ref.at[j], k_ref[pl.ds(start, span), :], v_ref[pl.ds(start, span), :],
                                lambda rb: bias_scr[rb], sink))

    @pl.when(c == 0)
    def _():
        _unpack_kv(k_in, v_in, grp, k_ref, v_ref)
        d = _pos_tiles(SWA_WINDOW, 0, span)
        for rb in range(GROUP):
            bias_scr[rb] = _band_bias(nslope_ref[_head_of_row_block(grp, rb)], d, SWA_WINDOW)

    @pl.when(c == 0)
    def _():
        write(0, _window_attend(qs_ref.at[0], k_ref[0:Q_BLOCK, :], v_ref[0:Q_BLOCK, :],
                                lambda rb: bias_scr[rb, :, SWA_WINDOW:span], sink))
        for j in range(1, Q_SUB):
            later_block(j)

    @pl.when(c > 0)
    def _():
        for j in range(Q_SUB):
            later_block(j)


def swa_attention(qkv, nslopes, sinks, batch, seq):
    nstep = seq // (Q_SUB * Q_BLOCK)
    rows = Q_SUB * Q_BLOCK
    gw = PAIRS * LANES
    kd = KV_GROUPS * HEAD_DIM
    span = SWA_WINDOW + Q_BLOCK
    k_col = N_HEADS * HEAD_DIM // kd
    return pl.pallas_call(
        _swa_kernel,
        grid=(batch, KV_GROUPS, nstep),
        in_specs=[
            pl.BlockSpec(memory_space=pltpu.SMEM),
            pl.BlockSpec(memory_space=pltpu.SMEM),
            pl.BlockSpec((rows, gw), lambda b, g, c: (b * nstep + c, g)),
            _kv_block_spec(seq, k_col),
            _kv_block_spec(seq, k_col + 1),
        ],
        out_specs=pl.BlockSpec((rows, gw), lambda b, g, c: (b * nstep + c, g)),
        out_shape=jax.ShapeDtypeStruct((batch * seq, N_HEADS * HEAD_DIM), MXU_DTYPE),
        scratch_shapes=[
            pltpu.VMEM((Q_SUB, ROWS, LANES), MXU_DTYPE),
            pltpu.VMEM((GROUP, Q_BLOCK, span), F32),
            pltpu.VMEM((seq, LANES), MXU_DTYPE),
            pltpu.VMEM((seq, 2 * LANES), MXU_DTYPE),
        ],
        compiler_params=_params("parallel", "parallel", "arbitrary"),
        name="swa_attention",
    )(nslopes, sinks, qkv, qkv, qkv)


def _compress_kernel(z_ref, pe_ref, w1_ref, b1_ref, w2_ref, b2_ref, o_ref, *, ncp):
    half = CMP_STRIDE * HEAD_DIM
    z = z_ref[0, 0, 0].astype(F32)
    top = (z + pe_ref[0, 0:1, :]).astype(MXU_DTYPE)
    bot = (z + pe_ref[0, 1:2, :]).astype(MXU_DTYPE)
    a = jnp.dot(top, w1_ref[0, 0:half, :], preferred_element_type=F32)
    bm = jnp.dot(bot, w1_ref[0, half:2 * half, :], preferred_element_type=F32)
    hid = a + pltpu.roll(bm, ncp - 1, axis=0) + b1_ref[0]
    act = jax.nn.gelu(hid).astype(MXU_DTYPE)
    o_ref[0, 0, 0] = jnp.dot(act, w2_ref[0], preferred_element_type=F32) + b2_ref[0]


def compress(z, pe, w1, b1, w2, b2):
    _, batch, groups, ncp, zw = z.shape
    hid = w1.shape[-1]
    kern = functools.partial(_compress_kernel, ncp=ncp)
    return pl.pallas_call(
        kern,
        grid=(2, batch, groups),
        in_specs=[
            pl.BlockSpec((1, 1, 1, ncp, zw), lambda s, b, g: (s, b, g, 0, 0)),
            pl.BlockSpec((1, 2, zw), lambda s, b, g: (s, 0, 0)),
            pl.BlockSpec((1, 2 * zw, hid), lambda s, b, g: (s, 0, 0)),
            pl.BlockSpec((1, 1, hid), lambda s, b, g: (s, 0, 0)),
            pl.BlockSpec((1, hid, HEAD_DIM), lambda s, b, g: (s, 0, 0)),
            pl.BlockSpec((1, 1, HEAD_DIM), lambda s, b, g: (s, 0, 0)),
        ],
        out_specs=pl.BlockSpec((1, 1, 1, ncp, HEAD_DIM), lambda s, b, g: (s, b, g, 0, 0)),
        out_shape=jax.ShapeDtypeStruct((2, batch, groups, ncp, HEAD_DIM), F32),
        compiler_params=_params("parallel", "parallel", "parallel"),
        name="nsa_compress",
    )(z, pe, w1, b1, w2, b2)


def _nsa_cmp_kernel(nslope_ref, q_ref, kc_ref, vc_ref, o_ref, sel_ref, qs_ref, e_scr, r_scr, *, ncp, n_cmp, n_sel):
    grp = pl.program_id(1)
    c = pl.program_id(2)
    starts = [(c * Q_SUB + j) * Q_BLOCK for j in range(Q_SUB)]
    for j in range(Q_SUB):
        _build_q_stack(q_ref, qs_ref.at[j], j * Q_BLOCK)

    def attend(width):
        for j, t0 in enumerate(starts):
            qi = lax.broadcasted_iota(jnp.int32, (Q_BLOCK, width), 0)
            ni = lax.broadcasted_iota(jnp.int32, (Q_BLOCK, width), 1)
            d = (t0 + qi) - (ni * CMP_STRIDE + (CMP_LEN - 1))
            negb = jnp.where((d >= 0) & (ni < n_cmp), 0.0, MASK_VALUE)
            dist = d.astype(F32)
            s_all = lax.dot_general(qs_ref[j], kc_ref[0, 0, 0:width, :], (((1,), (1,)), ((), ())),
                                    preferred_element_type=F32)
            for rb in range(GROUP):
                s = s_all[_rows(rb)] + nslope_ref[_head_of_row_block(grp, rb)] * dist + negb
                m = jnp.max(s, axis=1, keepdims=True)
                e_scr[j, _rows(rb), 0:width] = jnp.exp2(s - m).astype(e_scr.dtype)
            r_scr[j] = jnp.dot(e_scr[j, :, 0:width], vc_ref[0, 0, 0:width, :], preferred_element_type=F32)

    n_chunks = ncp // LANES
    need = jnp.minimum((starts[-1] + Q_BLOCK - CMP_LEN) // CMP_STRIDE // LANES + 1, n_chunks)
    for kq in range(1, n_chunks + 1):
        pl.when(need == kq)(functools.partial(attend, kq * LANES))

    lane = lax.broadcasted_iota(jnp.int32, (Q_BLOCK, LANES), 1)
    even = lane < HEAD_DIM
    ji = lax.broadcasted_iota(jnp.int32, (LANES, Q_BLOCK), 0)
    qt = lax.broadcasted_iota(jnp.int32, (LANES, Q_BLOCK), 1)
    neg_inf = -jnp.inf
    for j, t0 in enumerate(starts):
        orow = slice(j * Q_BLOCK, (j + 1) * Q_BLOCK)
        row_t = t0 + lax.broadcasted_iota(jnp.int32, (Q_BLOCK, LANES), 0)
        has_cmp = row_t >= (CMP_LEN - 1)
        imp = jnp.zeros((Q_BLOCK, LANES), F32)
        for p in range(PAIRS):
            re = _rows(p)
            ro = _rows(PAIRS + p)
            de = jnp.maximum(r_scr[j, re, LANES:2 * LANES], 1e-30)
            do = jnp.maximum(r_scr[j, ro, LANES:2 * LANES], 1e-30)
            o = jnp.where(even, r_scr[j, re, 0:LANES] / de, r_scr[j, ro, 0:LANES] / do)
            o_ref[orow, p * LANES:(p + 1) * LANES] = jnp.where(has_cmp, o, 0.0)
            imp = imp + r_scr[j, re, 2 * LANES:3 * LANES] / de + r_scr[j, ro, 2 * LANES:3 * LANES] / do
        imp = jnp.where(has_cmp, imp, 0.0)

        imp_t = imp.T
        cur = (t0 + qt) // SEL_LEN
        causal = ji <= cur
        forced = (ji == 0) | (ji == cur) | (ji == cur - 1)
        score = jnp.where(forced, neg_inf, jnp.where(causal, imp_t, MASK_VALUE))
        score = jnp.where(ji < n_sel, score, neg_inf)
        picked = jnp.where(forced, 1.0, 0.0)
        for _ in range(SEL_TOPK - SEL_FORCED):
            mx = jnp.max(score, axis=0, keepdims=True)
            first = jnp.min(jnp.where(score == mx, ji, LANES), axis=0, keepdims=True)
            hit = ji == first
            picked = jnp.where(hit, 1.0, picked)
            score = jnp.where(hit, neg_inf, score)
        picked = jnp.where(ji < cur, picked, 0.0)
        sel_ref[0, 0, orow, :] = picked.T.astype(sel_ref.dtype)


def nsa_cmp_select(q, kc2, vc_aug, nslopes, batch, seq, q_col_block):
    nstep = seq // (Q_SUB * Q_BLOCK)
    rows = Q_SUB * Q_BLOCK
    ncp = seq // CMP_STRIDE
    n_cmp = (seq - CMP_LEN) // CMP_STRIDE + 1
    n_sel = seq // SEL_LEN
    gw = PAIRS * LANES
    kern = functools.partial(_nsa_cmp_kernel, ncp=ncp, n_cmp=n_cmp, n_sel=n_sel)
    return pl.pallas_call(
        kern,
        grid=(batch, KV_GROUPS, nstep),
        in_specs=[
            pl.BlockSpec(memory_space=pltpu.SMEM),
            pl.BlockSpec((rows, gw), lambda b, g, c: (b * nstep + c, q_col_block + g)),
            pl.BlockSpec((1, 1, ncp, LANES), lambda b, g, c: (b, g, 0, 0)),
            pl.BlockSpec((1, 1, ncp, 3 * LANES), lambda b, g, c: (b, g, 0, 0)),
        ],
        out_specs=[
            pl.BlockSpec((rows, gw), lambda b, g, c: (b * nstep + c, g)),
            pl.BlockSpec((1, 1, rows, LANES), lambda b, g, c: (b, g, c, 0)),
        ],
        out_shape=[
            jax.ShapeDtypeStruct((batch * seq, N_HEADS * HEAD_DIM), F32),
            jax.ShapeDtypeStruct((batch, KV_GROUPS, seq, LANES), MXU_DTYPE),
        ],
        scratch_shapes=[pltpu.VMEM((Q_SUB, ROWS, LANES), MXU_DTYPE), pltpu.VMEM((Q_SUB, ROWS, ncp), MXU_DTYPE),
                        pltpu.VMEM((Q_SUB, ROWS, 3 * LANES), F32)],
        compiler_params=_params("parallel", "parallel", "arbitrary"),
        name="nsa_cmp_select",
    )(nslopes, q, kc2, vc_aug)


N_FEAT = 6


def sel_query_features():
    s = jnp.asarray((-_alibi_neg_slopes().astype(np.float64) * LOG2E).astype(np.float32))
    s1 = s.astype(MXU_DTYPE).astype(F32)
    s2 = (s - s1).astype(MXU_DTYPE).astype(F32)
    s3 = (s - s1 - s2).astype(MXU_DTYPE).astype(F32)
    feat = jnp.zeros((N_HEADS, LANES), F32).at[:, HEAD_DIM:HEAD_DIM + N_FEAT].set(
        jnp.stack([s1, s2, s3, s1, s2, s3], axis=1))
    feat = feat.at[:, HEAD_DIM + N_FEAT].set(MASK_VALUE)
    order = np.array([[_head_of_row_block(g, rb) for rb in range(GROUP)] for g in range(KV_GROUPS)])
    return feat[order]


def sel_key_features(seq):
    pos = np.arange(seq)
    kk = pos % SEL_STEP
    f = np.zeros((seq + SEL_STEP, 2 * LANES), np.float32)
    f[:seq, HEAD_DIM:HEAD_DIM + 3] = (SEL_LEN * (kk // SEL_LEN))[:, None]
    f[:seq, HEAD_DIM + 3:HEAD_DIM + 6] = (kk % SEL_LEN)[:, None]
    f[seq:, HEAD_DIM + N_FEAT] = 1.0
    f[pos, LANES + pos // SEL_LEN] = 1.0
    return f


def _nsa_sel_kernel(delta_ref, qfeat_ref, q_ref, sel_ref, k_in, v_in, kfeat_ref, gate_ref, ex_ref, oc_ref, ow_ref,
                    o_ref, qa_ref, m_ref, acc_ref, s_a, s_b, p_a, p_b, al_a, al_b, k_ref, v_ref):
    grp = pl.program_id(1)
    c = pl.program_id(2)
    t0 = c * (SEL_Q * Q_BLOCK)
    seq = k_in.shape[0]
    srows = lambda j, rb: slice(j * ROWS + rb * Q_BLOCK, j * ROWS + (rb + 1) * Q_BLOCK)
    qrows = lambda j: slice(j * Q_BLOCK, (j + 1) * Q_BLOCK)

    @pl.when(c == 0)
    def _():
        def put_k(r0, y):
            k_ref[pl.ds(r0, KV_CHUNK), 0:LANES] = y
            k_ref[pl.ds(r0, KV_CHUNK), LANES:2 * LANES] = kfeat_ref[pl.ds(r0, KV_CHUNK), LANES:2 * LANES]

        def put_v(r0, y):
            v_ref[pl.ds(r0, KV_CHUNK), 0:LANES] = y
            v_ref[pl.ds(r0, KV_CHUNK), LANES:2 * LANES] = jnp.ones((KV_CHUNK, LANES), v_ref.dtype)

        _unpack_group(k_in, grp, put_k, fill_ref=kfeat_ref)
        _unpack_group(v_in, grp, put_v)
        k_ref[seq:seq + SEL_STEP, :] = kfeat_ref[seq:seq + SEL_STEP, :]
        v_ref[seq:seq + SEL_STEP, :] = jnp.zeros((SEL_STEP, 2 * LANES), v_ref.dtype)
    lane = lax.broadcasted_iota(jnp.int32, (Q_BLOCK, LANES), 1)
    low = lane < HEAD_DIM
    for j in range(SEL_Q):
        selneg = ((1.0 - sel_ref[0, 0, qrows(j), :].astype(F32)) * MASK_VALUE).astype(qa_ref.dtype)
        for p in range(PAIRS):
            qp = q_ref[qrows(j), p * LANES:(p + 1) * LANES].astype(F32) * (HEAD_DIM ** -0.5 * LOG2E)
            for rb, src in ((p, qp), (PAIRS + p, pltpu.roll(qp, HEAD_DIM, axis=1))):
                qa_ref[srows(j, rb), 0:LANES] = jnp.where(low, src, qfeat_ref[0, rb:rb + 1, :]).astype(qa_ref.dtype)
                qa_ref[srows(j, rb), LANES:2 * LANES] = selneg
    n_steps = t0 // SEL_STEP + 1
    n_pad_step = seq // SEL_STEP
    dot_nt = (((1,), (1,)), ((), ()))

    def key_start(step):
        return pl.multiple_of(jnp.clip(step, 0, n_pad_step) * SEL_STEP, SEL_STEP)

    def scores(step, s_out):
        k = k_ref[pl.ds(key_start(step), SEL_STEP), :]
        s_out[...] = lax.dot_general(qa_ref[...], k, dot_nt, preferred_element_type=F32)

    def softmax(s_in, p_out, al_out):
        for rb in range(SEL_Q * GROUP):
            m_prev = m_ref[_rows(rb), :] - delta_ref[_head_of_row_block(grp, rb % GROUP)]
            m_new = jnp.maximum(m_prev, jnp.max(s_in[_rows(rb), :], axis=1, keepdims=True))
            al_out[_rows(rb), :] = jnp.exp2(m_prev - m_new)
            m_ref[_rows(rb), :] = m_new
        for rb in range(SEL_Q * GROUP):
            p_out[_rows(rb), :] = jnp.exp2(s_in[_rows(rb), :] - _rep(m_ref[_rows(rb), :], SEL_STEP)).astype(p_out.dtype)

    def values(step, p_in, al_in):
        v = v_ref[pl.ds(key_start(step), SEL_STEP), :]
        pv = jnp.dot(p_in[...], v, preferred_element_type=F32)
        alpha = al_in[...]
        acc_ref[...] = acc_ref[...] * jnp.concatenate([alpha, alpha], axis=1) + pv

    def even_half(t):
        scores(t, s_a)
        softmax(s_b, p_b, al_b)
        values(t - 2, p_a, al_a)

    def odd_half(t):
        scores(t, s_b)
        softmax(s_a, p_a, al_a)
        values(t - 2, p_b, al_b)

    scores(0, s_a)
    qi = lax.broadcasted_iota(jnp.int32, (Q_BLOCK, Q_BLOCK), 0)
    ki = lax.broadcasted_iota(jnp.int32, (Q_BLOCK, Q_BLOCK), 1)
    own_bias = jnp.where((ki <= qi) & (ki // SEL_LEN == qi // SEL_LEN), 0.0, MASK_VALUE)
    owns = [pl.multiple_of(t0 + j * Q_BLOCK, Q_BLOCK) for j in range(SEL_Q)]
    s_owns = [lax.dot_general(qa_ref[j * ROWS:(j + 1) * ROWS, 0:LANES], k_ref[pl.ds(owns[j], Q_BLOCK), 0:LANES],
                              dot_nt, preferred_element_type=F32) for j in range(SEL_Q)]
    scores(1, s_b)
    back = jnp.full((Q_BLOCK, LANES), n_steps, jnp.int32).astype(F32)
    p_owns = []
    for j in range(SEL_Q):
        p_own = []
        for rb in range(GROUP):
            s = s_owns[j][_rows(rb)] + own_bias
            m = jnp.max(s, axis=1, keepdims=True)
            p_own.append(jnp.exp2(s - m).astype(MXU_DTYPE))
            m_ref[srows(j, rb), :] = m + back * delta_ref[_head_of_row_block(grp, rb)]
        p_owns.append(jnp.concatenate(p_own, axis=0))
    softmax(s_a, p_a, al_a)
    for j in range(SEL_Q):
        acc_ref[j * ROWS:(j + 1) * ROWS, :] = jnp.dot(p_owns[j], v_ref[pl.ds(owns[j], Q_BLOCK), :],
                                                      preferred_element_type=F32)

    def quad(j, carry):
        even_half(4 * j + 2)
        odd_half(4 * j + 3)
        even_half(4 * j + 4)
        odd_half(4 * j + 5)
        return carry

    lax.fori_loop(0, n_steps // 4, quad, 0)
    rest = 4 * (n_steps // 4) + 2

    @pl.when(n_steps % 4 >= 2)
    def _():
        even_half(rest)
        odd_half(rest + 1)

    @pl.when(n_steps % 2 == 1)
    def _():
        even_half(n_steps + 1)

    gw = PAIRS * LANES
    sig = jax.nn.sigmoid(gate_ref[...])
    hi = sig.astype(MXU_DTYPE)
    lo = (sig - hi.astype(F32)).astype(MXU_DTYPE)
    ex = ex_ref[0]
    g = jnp.dot(hi, ex, preferred_element_type=F32) + jnp.dot(lo, ex, preferred_element_type=F32)
    for j in range(SEL_Q):
        o_sel = _pairs(lambda rb: acc_ref[srows(j, rb), 0:LANES], lambda rb: acc_ref[srows(j, rb), LANES:2 * LANES])
        for p in range(PAIRS):
            cols = slice(p * LANES, (p + 1) * LANES)
            out = (g[qrows(j), cols] * oc_ref[qrows(j), cols]
                   + g[qrows(j), gw + p * LANES:gw + (p + 1) * LANES] * o_sel[p]
                   + g[qrows(j), 2 * gw + p * LANES:2 * gw + (p + 1) * LANES] * ow_ref[qrows(j), cols])
            o_ref[qrows(j), cols] = out.astype(o_ref.dtype)


def _gate_expansion():
    gw = PAIRS * LANES
    ex = np.zeros((KV_GROUPS, LANES, 3 * gw), np.float32)
    for g in range(KV_GROUPS):
        for hl in range(GROUP):
            for i in range(3):
                ex[g, 3 * (g * GROUP + hl) + i, i * gw + hl * HEAD_DIM:i * gw + (hl + 1) * HEAD_DIM] = 1.0
    return ex


def nsa_sel_attention(qkv, sel, gate, o_cmp, o_win, batch, seq, k_col, v_col):
    nstep = seq // (SEL_Q * Q_BLOCK)
    rows = SEL_Q * Q_BLOCK
    srows = SEL_Q * ROWS
    gw = PAIRS * LANES
    deltas = jnp.asarray((-_alibi_neg_slopes().astype(np.float64) * LOG2E * SEL_STEP).astype(np.float32))
    ex = jnp.asarray(_gate_expansion(), MXU_DTYPE)
    kfeat = jnp.asarray(sel_key_features(seq), MXU_DTYPE)
    blk = lambda b, g, c: (b * nstep + c, g)
    return pl.pallas_call(
        _nsa_sel_kernel,
        grid=(batch, KV_GROUPS, nstep),
        in_specs=[
            pl.BlockSpec(memory_space=pltpu.SMEM),
            pl.BlockSpec((1, GROUP, LANES), lambda b, g, c: (g, 0, 0)),
            pl.BlockSpec((rows, gw), blk),
            pl.BlockSpec((1, 1, rows, LANES), lambda b, g, c: (b, g, c, 0)),
            _kv_block_spec(seq, k_col),
            _kv_block_spec(seq, v_col),
            pl.BlockSpec((seq + SEL_STEP, 2 * LANES), lambda b, g, c: (0, 0), pipeline_mode=pl.Buffered(1)),
            pl.BlockSpec((rows, LANES), lambda b, g, c: (b * nstep + c, 0)),
            pl.BlockSpec((1, LANES, 3 * gw), lambda b, g, c: (g, 0, 0)),
            pl.BlockSpec((rows, gw), blk),
            pl.BlockSpec((rows, gw), blk),
        ],
        out_specs=pl.BlockSpec((rows, gw), blk),
        out_shape=jax.ShapeDtypeStruct((batch * seq, N_HEADS * HEAD_DIM), MXU_DTYPE),
        scratch_shapes=[
            pltpu.VMEM((srows, 2 * LANES), MXU_DTYPE),
            pltpu.VMEM((srows, LANES), F32),
            pltpu.VMEM((srows, 2 * LANES), F32),
            pltpu.VMEM((srows, SEL_STEP), F32),
            pltpu.VMEM((srows, SEL_STEP), F32),
            pltpu.VMEM((srows, SEL_STEP), MXU_DTYPE),
            pltpu.VMEM((srows, SEL_STEP), MXU_DTYPE),
            pltpu.VMEM((srows, LANES), F32),
            pltpu.VMEM((srows, LANES), F32),
            pltpu.VMEM((seq + SEL_STEP, 2 * LANES), MXU_DTYPE),
            pltpu.VMEM((seq + SEL_STEP, 2 * LANES), MXU_DTYPE),
        ],
        compiler_params=_params("parallel", "parallel", "arbitrary"),
        name="nsa_sel_attention",
    )(deltas, sel_query_features(), qkv, sel, qkv, qkv, kfeat, gate, ex, o_cmp, o_win)


def _nsa_win_kernel(nslope_ref, q_ref, k_in, v_in, o_ref, qs_ref, bias_scr, k_ref, v_ref):
    grp = pl.program_id(1)
    c = pl.program_id(2)
    span = NSA_WINDOW + Q_BLOCK
    lead = NSA_WINDOW // Q_BLOCK
    for j in range(Q_SUB):
        _build_q_stack(q_ref, qs_ref.at[j], j * Q_BLOCK)
    nslope = lambda rb: nslope_ref[_head_of_row_block(grp, rb)]

    def write(j, outs):
        for p, o in enumerate(outs):
            o_ref[j * Q_BLOCK:(j + 1) * Q_BLOCK, p * LANES:(p + 1) * LANES] = o

    @pl.when(c == 0)
    def _():
        _unpack_kv(k_in, v_in, grp, k_ref, v_ref)
        d = _pos_tiles(NSA_WINDOW, 0, span)
        for rb in range(GROUP):
            bias_scr[rb] = _band_bias(nslope(rb), d, NSA_WINDOW)

    @pl.when(c < lead // Q_SUB)
    def _():
        for j in range(Q_SUB):
            d = _pos_tiles((c * Q_SUB + j) * Q_BLOCK, 0, span)
            write(j, _window_attend(qs_ref.at[j], k_ref[0:span, :], v_ref[0:span, :],
                                    lambda rb: _band_bias(nslope(rb), d, NSA_WINDOW)))

    @pl.when(c >= lead // Q_SUB)
    def _():
        for j in range(Q_SUB):
            start = pl.multiple_of((c * Q_SUB + j) * Q_BLOCK - NSA_WINDOW, Q_BLOCK)
            write(j, _window_attend(qs_ref.at[j], k_ref[pl.ds(start, span), :], v_ref[pl.ds(start, span), :],
                                    lambda rb: bias_scr[rb]))


def nsa_win_attention(qkv, nslopes, batch, seq, k_col, v_col):
    nstep = seq // (Q_SUB * Q_BLOCK)
    rows = Q_SUB * Q_BLOCK
    gw = PAIRS * LANES
    kd = KV_GROUPS * HEAD_DIM
    span = NSA_WINDOW + Q_BLOCK
    return pl.pallas_call(
        _nsa_win_kernel,
        grid=(batch, KV_GROUPS, nstep),
        in_specs=[
            pl.BlockSpec(memory_space=pltpu.SMEM),
            pl.BlockSpec((rows, gw), lambda b, g, c: (b * nstep + c, g)),
            _kv_block_spec(seq, k_col),
            _kv_block_spec(seq, v_col),
        ],
        out_specs=pl.BlockSpec((rows, gw), lambda b, g, c: (b * nstep + c, g)),
        out_shape=jax.ShapeDtypeStruct((batch * seq, N_HEADS * HEAD_DIM), F32),
        scratch_shapes=[
            pltpu.VMEM((Q_SUB, ROWS, LANES), MXU_DTYPE),
            pltpu.VMEM((GROUP, Q_BLOCK, span), F32),
            pltpu.VMEM((seq, LANES), MXU_DTYPE),
            pltpu.VMEM((seq, 2 * LANES), MXU_DTYPE),
        ],
        compiler_params=_params("parallel", "parallel", "arbitrary"),
        name="nsa_win_attention",
    )(nslopes, qkv, qkv, qkv)


def _overlap_matrix(seq):
    ncp = seq // CMP_STRIDE
    n_cmp = (seq - CMP_LEN) // CMP_STRIDE + 1
    cs = np.arange(n_cmp) * CMP_STRIDE
    ss = np.arange(seq // SEL_LEN) * SEL_LEN
    ov = (cs[:, None] < ss[None, :] + SEL_LEN) & (cs[:, None] + CMP_LEN > ss[None, :])
    out = np.zeros((ncp, LANES), np.float32)
    out[:n_cmp, :seq // SEL_LEN] = ov
    return out


def _swa_layer(h, x, w_in, b_in, sinks, w_o, b_o, g_post, g_next, nslopes, batch, seq):
    hd = N_HEADS * HEAD_DIM
    kd = KV_GROUPS * HEAD_DIM
    qkv = matmul_bias(h, w_in.astype(MXU_DTYPE), b_in, MXU_DTYPE, name="swa_in_proj")
    o = swa_attention(qkv, nslopes, sinks.astype(F32) * LOG2E, batch, seq)
    return matmul_norm_res(o, w_o.astype(MXU_DTYPE), b_o, g_post, g_next, x, tn=w_o.shape[1], name="swa_out_proj")


def _nsa_layer(h, x, w_in, cmp_pe, cmp_w1, cmp_b1, cmp_w2, cmp_b2, w_o, g_post, g_next, nslopes, batch, seq):
    hd = N_HEADS * HEAD_DIM
    kd = KV_GROUPS * HEAD_DIM
    t = batch * seq
    ncp = seq // CMP_STRIDE
    qkv = matmul_bias(h, w_in[:, :hd + 6 * kd].astype(MXU_DTYPE), jnp.zeros((hd + 6 * kd,), F32), MXU_DTYPE,
                      name="nsa_in_proj")
    n_gate = 3 * N_HEADS
    w_gate = jnp.pad(w_in[:, hd + 6 * kd:], ((0, 0), (0, LANES - n_gate))).astype(MXU_DTYPE)
    gate = matmul_bias(h, w_gate, jnp.zeros((LANES,), F32), F32, name="nsa_gate_proj")

    def kv(i):
        return qkv[:, hd + i * kd:hd + (i + 1) * kd]

    def slabs(a):
        return a.reshape(batch, seq, KV_GROUPS, HEAD_DIM).transpose(0, 2, 1, 3).reshape(
            batch, KV_GROUPS, ncp, CMP_STRIDE * HEAD_DIM)

    z = jnp.stack([slabs(kv(0)), slabs(kv(1))])
    half = CMP_STRIDE * HEAD_DIM
    cmp_out = compress(z, cmp_pe.reshape(2, 2, half).astype(F32), cmp_w1.astype(MXU_DTYPE),
                       cmp_b1.reshape(2, 1, -1), cmp_w2.astype(MXU_DTYPE), cmp_b2.reshape(2, 1, -1))
    kcm = cmp_out[0].astype(MXU_DTYPE)
    vcm = cmp_out[1].astype(MXU_DTYPE)
    kc2 = jnp.concatenate([kcm, kcm], axis=-1)
    ov = jnp.broadcast_to(jnp.asarray(_overlap_matrix(seq), MXU_DTYPE), (batch, KV_GROUPS, ncp, LANES))
    vc_aug = jnp.concatenate([vcm, vcm, jnp.ones((batch, KV_GROUPS, ncp, LANES), MXU_DTYPE), ov], axis=-1)

    o_cmp, sel = nsa_cmp_select(qkv, kc2, vc_aug, nslopes, batch, seq, 0)
    kv_col = lambda i: hd // kd + i
    o_win = nsa_win_attention(qkv, nslopes, batch, seq, kv_col(4), kv_col(5))
    o = nsa_sel_attention(qkv, sel, gate, o_cmp, o_win, batch, seq, kv_col(2), kv_col(3))
    return matmul_norm_res(o, w_o.astype(MXU_DTYPE), jnp.zeros((w_o.shape[1],), F32), g_post, g_next, x,
                           tn=w_o.shape[1], name="nsa_out_proj")


def kernel(x, norm_g, swa_w_in, swa_b_in, swa_sinks, swa_w_o, swa_b_o, nsa_w_in, nsa_cmp_pe, nsa_cmp_w1, nsa_cmp_b1, nsa_cmp_w2, nsa_cmp_b2, nsa_w_o, ffn_w_gate, ffn_w_up, ffn_conv_w, ffn_conv_b, ffn_w_down):
    batch, seq, d = x.shape
    depth = norm_g.shape[0]
    nslopes = jnp.asarray((_alibi_neg_slopes().astype(np.float64) * LOG2E).astype(np.float32))
    xf = x.reshape(batch * seq, d)
    h = rms_cast(xf, norm_g[0, 0])
    for i in range(depth):
        g = norm_g[i]
        j = i // 2
        if i % 2 == 0:
            xf, h = _swa_layer(h, xf, swa_w_in[j], swa_b_in[j], swa_sinks[j], swa_w_o[j], swa_b_o[j],
                               g[1], g[2], nslopes, batch, seq)
        else:
            xf, h = _nsa_layer(h, xf, nsa_w_in[j], nsa_cmp_pe[j], nsa_cmp_w1[j], nsa_cmp_b1[j], nsa_cmp_w2[j],
                               nsa_cmp_b2[j], nsa_w_o[j], g[1], g[2], nslopes, batch, seq)
        act = ffn_up(h, ffn_w_gate, ffn_w_up, i, ffn_conv_w[i], ffn_conv_b[i], seq)
        g_next = norm_g[i + 1, 0] if i + 1 < depth else jnp.ones((d,), F32)
        xf, h = matmul_norm_res(act, ffn_w_down[i].astype(MXU_DTYPE), jnp.zeros((d,), F32), g[3], g_next, xf,
                                tn=512, name="ffn_down")
    return xf.reshape(batch, seq, d)
```

```python
import functools

import numpy as np
import jax
import jax.numpy as jnp
from jax import lax
from jax.experimental import pallas as pl
from jax.experimental.pallas import tpu as pltpu

F32 = jnp.float32
MXU_DTYPE = jnp.bfloat16

N_HEADS = 32
HEAD_DIM = 64
KV_GROUPS = 4
GROUP = N_HEADS // KV_GROUPS
PAIRS = GROUP // 2
LANES = 128
Q_BLOCK = 128
ROWS = GROUP * Q_BLOCK
SWA_WINDOW = 128
CMP_LEN = 32
CMP_STRIDE = 16
SEL_LEN = 64
SEL_TOPK = 16
SEL_FORCED = 3
LOG2E = 1.4426950408889634
SEL_STEP = 512
SEL_Q = 2
NSA_WINDOW = 512
CONV_WIDTH = 3
RMS_EPS = 1e-6
MASK_VALUE = -1e30
VMEM_LIMIT = 60000 * 1024


def _params(*sem):
    return pltpu.CompilerParams(dimension_semantics=sem, vmem_limit_bytes=VMEM_LIMIT)


def _alibi_neg_slopes():
    return (-np.exp2(-8.0 * np.arange(1, N_HEADS + 1, dtype=np.float64) / N_HEADS)).astype(np.float32)


def _rms_cast_kernel(x_ref, g_ref, o_ref):
    x = x_ref[...]
    ms = jnp.mean(x * x, axis=-1, keepdims=True)
    o_ref[...] = (x * lax.rsqrt(ms + RMS_EPS) * g_ref[...]).astype(o_ref.dtype)


def rms_cast(x, g, tm=512):
    t, d = x.shape
    return pl.pallas_call(
        _rms_cast_kernel,
        grid=(t // tm,),
        in_specs=[pl.BlockSpec((tm, d), lambda i: (i, 0)), pl.BlockSpec((1, d), lambda i: (0, 0))],
        out_specs=pl.BlockSpec((tm, d), lambda i: (i, 0)),
        out_shape=jax.ShapeDtypeStruct((t, d), MXU_DTYPE),
        compiler_params=_params("parallel"),
        name="rms_cast",
    )(x, g.reshape(1, d))


def _matmul_kernel(a_ref, w_ref, b_ref, o_ref):
    acc = jnp.dot(a_ref[...], w_ref[...], preferred_element_type=F32)
    o_ref[...] = (acc + b_ref[...]).astype(o_ref.dtype)


def matmul_bias(a, w, b, out_dtype, tm=1024, tn=512, name="matmul_bias"):
    t, k = a.shape
    n = w.shape[1]
    tn = min(tn, n)
    return pl.pallas_call(
        _matmul_kernel,
        grid=(t // tm, n // tn),
        in_specs=[
            pl.BlockSpec((tm, k), lambda i, j: (i, 0)),
            pl.BlockSpec((k, tn), lambda i, j: (0, j)),
            pl.BlockSpec((1, tn), lambda i, j: (0, j)),
        ],
        out_specs=pl.BlockSpec((tm, tn), lambda i, j: (i, j)),
        out_shape=jax.ShapeDtypeStruct((t, n), out_dtype),
        compiler_params=_params("parallel", "arbitrary"),
        name=name,
    )(a, w, b.reshape(1, n))


def _mm_norm_res_kernel(a_ref, w_ref, b_ref, gpost_ref, gnext_ref, x_ref, xo_ref, ho_ref, y_scr, *, nj, tn, n):
    j = pl.program_id(1)
    y_scr[j] = jnp.dot(a_ref[...], w_ref[...], preferred_element_type=F32) + b_ref[...]

    @pl.when(j == nj - 1)
    def _():
        ss = jnp.sum(y_scr[0] * y_scr[0], axis=1, keepdims=True)
        for jj in range(1, nj):
            ss = ss + jnp.sum(y_scr[jj] * y_scr[jj], axis=1, keepdims=True)
        r = lax.rsqrt(ss / n + RMS_EPS)
        ss2 = jnp.zeros_like(ss)
        for jj in range(nj):
            cols = slice(jj * tn, (jj + 1) * tn)
            xn = x_ref[:, cols] + y_scr[jj] * r * gpost_ref[:, cols]
            xo_ref[:, cols] = xn
            ss2 = ss2 + jnp.sum(xn * xn, axis=1, keepdims=True)
        r2 = lax.rsqrt(ss2 / n + RMS_EPS)
        for jj in range(nj):
            cols = slice(jj * tn, (jj + 1) * tn)
            ho_ref[:, cols] = (xo_ref[:, cols] * r2 * gnext_ref[:, cols]).astype(ho_ref.dtype)


def matmul_norm_res(a, w, b, g_post, g_next, x, tm=512, tn=512, name="matmul_norm_res"):
    t, k = a.shape
    n = w.shape[1]
    nj = n // tn
    kern = functools.partial(_mm_norm_res_kernel, nj=nj, tn=tn, n=n)
    return pl.pallas_call(
        kern,
        grid=(t // tm, nj),
        in_specs=[
            pl.BlockSpec((tm, k), lambda i, j: (i, 0)),
            pl.BlockSpec((k, tn), lambda i, j: (0, j)),
            pl.BlockSpec((1, tn), lambda i, j: (0, j)),
            pl.BlockSpec((1, n), lambda i, j: (0, 0)),
            pl.BlockSpec((1, n), lambda i, j: (0, 0)),
            pl.BlockSpec((tm, n), lambda i, j: (i, 0)),
        ],
        out_specs=[
            pl.BlockSpec((tm, n), lambda i, j: (i, 0)),
            pl.BlockSpec((tm, n), lambda i, j: (i, 0)),
        ],
        out_shape=[jax.ShapeDtypeStruct((t, n), F32), jax.ShapeDtypeStruct((t, n), MXU_DTYPE)],
        scratch_shapes=[pltpu.VMEM((nj, tm, tn), F32)],
        compiler_params=_params("parallel", "arbitrary"),
        name=name,
    )(a, w, b.reshape(1, n), g_post.reshape(1, n), g_next.reshape(1, n), x)


FIX_ROWS = 16
CARRY_ROWS = 8


def _ffn_up_kernel(h_ref, wg_ref, wu_ref, cw_ref, cb_ref, o_ref, wg_scr, wu_scr, carry_scr, *, tm, tiles_per_seq):
    i = pl.program_id(1)

    @pl.when(i == 0)
    def _():
        wg_scr[...] = wg_ref[...].astype(wg_scr.dtype)
        wu_scr[...] = wu_ref[...].astype(wu_scr.dtype)

    h = h_ref[...]
    gate = jnp.dot(h, wg_scr[...], preferred_element_type=F32)
    up = jnp.dot(h, wu_scr[...], preferred_element_type=F32)
    w0 = cw_ref[0:1, :]
    w1 = cw_ref[1:2, :]
    w2 = cw_ref[2:3, :]
    b = cb_ref[...]
    a = b + pltpu.roll(gate, 2, axis=0) * w0
    a = a + pltpu.roll(gate, 1, axis=0) * w1
    a = a + gate * w2
    o_ref[...] = (jax.nn.silu(a) * up).astype(o_ref.dtype)

    seq_start = (i % tiles_per_seq) == 0
    prev = jnp.where(seq_start, 0.0, carry_scr[...])
    head = gate[0:FIX_ROWS]
    ext = jnp.concatenate([prev, head], axis=0)
    af = b + ext[CARRY_ROWS - 2:CARRY_ROWS - 2 + FIX_ROWS] * w0
    af = af + ext[CARRY_ROWS - 1:CARRY_ROWS - 1 + FIX_ROWS] * w1
    af = af + head * w2
    o_ref[0:FIX_ROWS, :] = (jax.nn.silu(af) * up[0:FIX_ROWS]).astype(o_ref.dtype)
    carry_scr[...] = gate[tm - CARRY_ROWS:tm]


def ffn_up(h, wg, wu, layer, conv_w, conv_b, seq, tm=1024, tn=512):
    t, k = h.shape
    n = wg.shape[2]
    kern = functools.partial(_ffn_up_kernel, tm=tm, tiles_per_seq=seq // tm)
    return pl.pallas_call(
        kern,
        grid=(n // tn, t // tm),
        in_specs=[
            pl.BlockSpec((tm, k), lambda j, i: (i, 0)),
            pl.BlockSpec((None, k, tn), lambda j, i: (layer, 0, j)),
            pl.BlockSpec((None, k, tn), lambda j, i: (layer, 0, j)),
            pl.BlockSpec((CONV_WIDTH, tn), lambda j, i: (0, j)),
            pl.BlockSpec((1, tn), lambda j, i: (0, j)),
        ],
        out_specs=pl.BlockSpec((tm, tn), lambda j, i: (i, j)),
        out_shape=jax.ShapeDtypeStruct((t, n), MXU_DTYPE),
        scratch_shapes=[
            pltpu.VMEM((k, tn), MXU_DTYPE),
            pltpu.VMEM((k, tn), MXU_DTYPE),
            pltpu.VMEM((CARRY_ROWS, tn), F32),
        ],
        compiler_params=_params("arbitrary", "arbitrary"),
        name="ffn_up",
    )(h, wg, wu, conv_w, conv_b.reshape(1, n))


def _head_of_row_block(group, rb):
    return group * GROUP + 2 * (rb % PAIRS) + rb // PAIRS


Q_SUB = 4


def _build_q_stack(q_ref, qs_ref, row0=0):
    lane = lax.broadcasted_iota(jnp.int32, (Q_BLOCK, LANES), 1)
    even = lane < HEAD_DIM
    for p in range(PAIRS):
        qp = q_ref[row0:row0 + Q_BLOCK, p * LANES:(p + 1) * LANES].astype(F32) * (HEAD_DIM ** -0.5 * LOG2E)
        qs_ref[p * Q_BLOCK:(p + 1) * Q_BLOCK, :] = jnp.where(even, qp, 0.0).astype(qs_ref.dtype)
        qs_ref[(PAIRS + p) * Q_BLOCK:(PAIRS + p + 1) * Q_BLOCK, :] = jnp.where(even, 0.0, qp).astype(qs_ref.dtype)


def _rep(x, size):
    return x if size == LANES else jnp.concatenate([x] * (size // LANES), axis=1)


def _pairs(num, den):
    lane = lax.broadcasted_iota(jnp.int32, (Q_BLOCK, LANES), 1)
    even = lane < HEAD_DIM
    outs = []
    for p in range(PAIRS):
        oe = num(p) / jnp.maximum(den(p), 1e-30)
        oo = num(PAIRS + p) / jnp.maximum(den(PAIRS + p), 1e-30)
        outs.append(jnp.where(even, oe, oo))
    return outs


def _rows(rb):
    return slice(rb * Q_BLOCK, (rb + 1) * Q_BLOCK)


def _band_bias(nslope, d, window):
    return jnp.where((d >= 0) & (d < window), nslope * d.astype(F32), MASK_VALUE)


def _window_attend(qs_ref, k, v, bias, extra_logit=None):
    half = ROWS // 2
    dot_nt = (((1,), (1,)), ((), ()))
    s_halves = [lax.dot_general(qs_ref[hh * half:(hh + 1) * half, :], k, dot_nt, preferred_element_type=F32)
                for hh in range(2)]
    ps, extras = [], []
    for rb in range(GROUP):
        lo = (rb % PAIRS) * Q_BLOCK
        s = s_halves[rb // PAIRS][lo:lo + Q_BLOCK] + bias(rb)
        m = jnp.max(s, axis=1, keepdims=True)
        if extra_logit is not None:
            m = jnp.maximum(m, extra_logit(rb))
            extras.append(jnp.exp2(extra_logit(rb) - m))
        ps.append(jnp.exp2(s - m).astype(MXU_DTYPE))
    r_halves = [jnp.dot(jnp.concatenate(ps[hh * PAIRS:(hh + 1) * PAIRS], axis=0), v, preferred_element_type=F32)
                for hh in range(2)]

    def part(rb, cols):
        lo = (rb % PAIRS) * Q_BLOCK
        return r_halves[rb // PAIRS][lo:lo + Q_BLOCK, cols]

    num = lambda rb: part(rb, slice(0, LANES))
    if extra_logit is None:
        den = lambda rb: part(rb, slice(LANES, 2 * LANES))
    else:
        den = lambda rb: part(rb, slice(LANES, 2 * LANES)) + extras[rb]
    return _pairs(num, den)


KV_CHUNK = 1024


def _unpack_group(kv_ref, grp, write_chunk, fill_ref=None):
    n = kv_ref.shape[0]
    for odd in range(2):
        def fill(odd=odd):
            def body(i, carry):
                r0 = pl.multiple_of(i * KV_CHUNK, KV_CHUNK)
                x = kv_ref[pl.ds(r0, KV_CHUNK), :]
                xi = pltpu.bitcast(x, jnp.int32)
                xr = pltpu.roll(xi, HEAD_DIM, axis=1)
                low = lax.broadcasted_iota(jnp.int32, xi.shape, 1) < HEAD_DIM
                if fill_ref is None:
                    y = jnp.where(low, xr, xi) if odd else jnp.where(low, xi, xr)
                else:
                    f = pltpu.bitcast(fill_ref[pl.ds(r0, KV_CHUNK), 0:LANES], jnp.int32)
                    y = jnp.where(low, xr if odd else xi, f)
                write_chunk(r0, pltpu.bitcast(y, x.dtype))
                return carry

            lax.fori_loop(0, n // KV_CHUNK, body, 0)

        pl.when(grp % 2 == odd)(fill)


def _kv_block_spec(seq, col):
    pairs_per_tensor = KV_GROUPS * HEAD_DIM // LANES
    return pl.BlockSpec((seq, LANES), lambda b, g, c: (b, pairs_per_tensor * col + g // 2))


def _pos_tiles(t0, start, size):
    qi = lax.broadcasted_iota(jnp.int32, (Q_BLOCK, size), 0)
    ki = lax.broadcasted_iota(jnp.int32, (Q_BLOCK, size), 1)
    return (t0 - start) + (qi - ki)


def _unpack_kv(k_in, v_in, grp, k_ref, v_ref):
    def put_k(r0, y):
        k_ref[pl.ds(r0, KV_CHUNK), :] = y

    def put_v(r0, y):
        v_ref[pl.ds(r0, KV_CHUNK), 0:LANES] = y
        v_ref[pl.ds(r0, KV_CHUNK), LANES:2 * LANES] = jnp.ones((KV_CHUNK, LANES), v_ref.dtype)

    _unpack_group(k_in, grp, put_k)
    _unpack_group(v_in, grp, put_v)


def _swa_kernel(nslope_ref, sink_ref, q_ref, k_in, v_in, o_ref, qs_ref, bias_scr, k_ref, v_ref):
    grp = pl.program_id(1)
    c = pl.program_id(2)
    span = SWA_WINDOW + Q_BLOCK
    for j in range(Q_SUB):
        _build_q_stack(q_ref, qs_ref.at[j], j * Q_BLOCK)
    sink = lambda rb: sink_ref[_head_of_row_block(grp, rb)]

    def write(j, outs):
        for p, o in enumerate(outs):
            o_ref[j * Q_BLOCK:(j + 1) * Q_BLOCK, p * LANES:(p + 1) * LANES] = o.astype(o_ref.dtype)

    def later_block(j):
        start = pl.multiple_of((c * Q_SUB + j) * Q_BLOCK - SWA_WINDOW, Q_BLOCK)
        write(j, _window_attend(qs_ref.at[j], k_ref[pl.ds(start, span), :], v_ref[pl.ds(start, span), :],
                                lambda rb: bias_scr[rb], sink))

    @pl.when(c == 0)
    def _():
        _unpack_kv(k_in, v_in, grp, k_ref, v_ref)
        d = _pos_tiles(SWA_WINDOW, 0, span)
        for rb in range(GROUP):
            bias_scr[rb] = _band_bias(nslope_ref[_head_of_row_block(grp, rb)], d, SWA_WINDOW)

    @pl.when(c == 0)
    def _():
        write(0, _window_attend(qs_ref.at[0], k_ref[0:Q_BLOCK, :], v_ref[0:Q_BLOCK, :],
                                lambda rb: bias_scr[rb, :, SWA_WINDOW:span], sink))
        for j in range(1, Q_SUB):
            later_block(j)

    @pl.when(c > 0)
    def _():
        for j in range(Q_SUB):
            later_block(j)


def swa_attention(qkv, nslopes, sinks, batch, seq):
    nstep = seq // (Q_SUB * Q_BLOCK)
    rows = Q_SUB * Q_BLOCK
    gw = PAIRS * LANES
    kd = KV_GROUPS * HEAD_DIM
    span = SWA_WINDOW + Q_BLOCK
    k_col = N_HEADS * HEAD_DIM // kd
    return pl.pallas_call(
        _swa_kernel,
        grid=(batch, KV_GROUPS, nstep),
        in_specs=[
            pl.BlockSpec(memory_space=pltpu.SMEM),
            pl.BlockSpec(memory_space=pltpu.SMEM),
            pl.BlockSpec((rows, gw), lambda b, g, c: (b * nstep + c, g)),
            _kv_block_spec(seq, k_col),
            _kv_block_spec(seq, k_col + 1),
        ],
        out_specs=pl.BlockSpec((rows, gw), lambda b, g, c: (b * nstep + c, g)),
        out_shape=jax.ShapeDtypeStruct((batch * seq, N_HEADS * HEAD_DIM), MXU_DTYPE),
        scratch_shapes=[
            pltpu.VMEM((Q_SUB, ROWS, LANES), MXU_DTYPE),
            pltpu.VMEM((GROUP, Q_BLOCK, span), F32),
            pltpu.VMEM((seq, LANES), MXU_DTYPE),
            pltpu.VMEM((seq, 2 * LANES), MXU_DTYPE),
        ],
        compiler_params=_params("parallel", "parallel", "arbitrary"),
        name="swa_attention",
    )(nslopes, sinks, qkv, qkv, qkv)


def _compress_kernel(z_ref, pe_ref, w1_ref, b1_ref, w2_ref, b2_ref, o_ref, *, ncp):
    half = CMP_STRIDE * HEAD_DIM
    z = z_ref[0, 0, 0].astype(F32)
    top = (z + pe_ref[0, 0:1, :]).astype(MXU_DTYPE)
    bot = (z + pe_ref[0, 1:2, :]).astype(MXU_DTYPE)
    a = jnp.dot(top, w1_ref[0, 0:half, :], preferred_element_type=F32)
    bm = jnp.dot(bot, w1_ref[0, half:2 * half, :], preferred_element_type=F32)
    hid = a + pltpu.roll(bm, ncp - 1, axis=0) + b1_ref[0]
    act = jax.nn.gelu(hid).astype(MXU_DTYPE)
    o_ref[0, 0, 0] = jnp.dot(act, w2_ref[0], preferred_element_type=F32) + b2_ref[0]


def compress(z, pe, w1, b1, w2, b2):
    _, batch, groups, ncp, zw = z.shape
    hid = w1.shape[-1]
    kern = functools.partial(_compress_kernel, ncp=ncp)
    return pl.pallas_call(
        kern,
        grid=(2, batch, groups),
        in_specs=[
            pl.BlockSpec((1, 1, 1, ncp, zw), lambda s, b, g: (s, b, g, 0, 0)),
            pl.BlockSpec((1, 2, zw), lambda s, b, g: (s, 0, 0)),
            pl.BlockSpec((1, 2 * zw, hid), lambda s, b, g: (s, 0, 0)),
            pl.BlockSpec((1, 1, hid), lambda s, b, g: (s, 0, 0)),
            pl.BlockSpec((1, hid, HEAD_DIM), lambda s, b, g: (s, 0, 0)),
            pl.BlockSpec((1, 1, HEAD_DIM), lambda s, b, g: (s, 0, 0)),
        ],
        out_specs=pl.BlockSpec((1, 1, 1, ncp, HEAD_DIM), lambda s, b, g: (s, b, g, 0, 0)),
        out_shape=jax.ShapeDtypeStruct((2, batch, groups, ncp, HEAD_DIM), F32),
        compiler_params=_params("parallel", "parallel", "parallel"),
        name="nsa_compress",
    )(z, pe, w1, b1, w2, b2)


def _nsa_cmp_kernel(nslope_ref, q_ref, kc_ref, vc_ref, o_ref, sel_ref, qs_ref, e_scr, r_scr, *, ncp, n_cmp, n_sel):
    grp = pl.program_id(1)
    c = pl.program_id(2)
    starts = [(c * Q_SUB + j) * Q_BLOCK for j in range(Q_SUB)]
    for j in range(Q_SUB):
        _build_q_stack(q_ref, qs_ref.at[j], j * Q_BLOCK)

    def attend(width):
        for j, t0 in enumerate(starts):
            qi = lax.broadcasted_iota(jnp.int32, (Q_BLOCK, width), 0)
            ni = lax.broadcasted_iota(jnp.int32, (Q_BLOCK, width), 1)
            d = (t0 + qi) - (ni * CMP_STRIDE + (CMP_LEN - 1))
            negb = jnp.where((d >= 0) & (ni < n_cmp), 0.0, MASK_VALUE)
            dist = d.astype(F32)
            s_all = lax.dot_general(qs_ref[j], kc_ref[0, 0, 0:width, :], (((1,), (1,)), ((), ())),
                                    preferred_element_type=F32)
            for rb in range(GROUP):
                s = s_all[_rows(rb)] + nslope_ref[_head_of_row_block(grp, rb)] * dist + negb
                m = jnp.max(s, axis=1, keepdims=True)
                e_scr[j, _rows(rb), 0:width] = jnp.exp2(s - m).astype(e_scr.dtype)
            r_scr[j] = jnp.dot(e_scr[j, :, 0:width], vc_ref[0, 0, 0:width, :], preferred_element_type=F32)

    n_chunks = ncp // LANES
    need = jnp.minimum((starts[-1] + Q_BLOCK - CMP_LEN) // CMP_STRIDE // LANES + 1, n_chunks)
    for kq in range(1, n_chunks + 1):
        pl.when(need == kq)(functools.partial(attend, kq * LANES))

    lane = lax.broadcasted_iota(jnp.int32, (Q_BLOCK, LANES), 1)
    even = lane < HEAD_DIM
    ji = lax.broadcasted_iota(jnp.int32, (LANES, Q_BLOCK), 0)
    qt = lax.broadcasted_iota(jnp.int32, (LANES, Q_BLOCK), 1)
    neg_inf = -jnp.inf
    for j, t0 in enumerate(starts):
        orow = slice(j * Q_BLOCK, (j + 1) * Q_BLOCK)
        row_t = t0 + lax.broadcasted_iota(jnp.int32, (Q_BLOCK, LANES), 0)
        has_cmp = row_t >= (CMP_LEN - 1)
        imp = jnp.zeros((Q_BLOCK, LANES), F32)
        for p in range(PAIRS):
            parts = []
            for rb in (p, PAIRS + p):
                nd = r_scr[j, _rows(rb), 0:LANES]
                dn = pltpu.roll(nd, HEAD_DIM, axis=1)
                den = jnp.maximum(jnp.where(even, dn, nd), 1e-30)
                parts.append((nd / den, r_scr[j, _rows(rb), LANES:2 * LANES] / den))
            (oe, ie), (oo, io) = parts
            o = jnp.where(even, oe, pltpu.roll(oo, HEAD_DIM, axis=1))
            o_ref[orow, p * LANES:(p + 1) * LANES] = jnp.where(has_cmp, o, 0.0)
            imp = imp + ie + io
        imp = jnp.where(has_cmp, imp, 0.0)

        imp_t = imp.T
        cur = (t0 + qt) // SEL_LEN
        causal = ji <= cur
        forced = (ji == 0) | (ji == cur) | (ji == cur - 1)
        score = jnp.where(forced, neg_inf, jnp.where(causal, imp_t, MASK_VALUE))
        score = jnp.where(ji < n_sel, score, neg_inf)
        picked = jnp.where(forced, 1.0, 0.0)
        for _ in range(SEL_TOPK - SEL_FORCED):
            mx = jnp.max(score, axis=0, keepdims=True)
            first = jnp.min(jnp.where(score == mx, ji, LANES), axis=0, keepdims=True)
            hit = ji == first
            picked = jnp.where(hit, 1.0, picked)
            score = jnp.where(hit, neg_inf, score)
        picked = jnp.where(ji < cur, picked, 0.0)
        sel_ref[0, 0, orow, :] = picked.T.astype(sel_ref.dtype)


def nsa_cmp_select(q, kc2, vc_aug, nslopes, batch, seq, q_col_block):
    nstep = seq // (Q_SUB * Q_BLOCK)
    rows = Q_SUB * Q_BLOCK
    ncp = seq // CMP_STRIDE
    n_cmp = (seq - CMP_LEN) // CMP_STRIDE + 1
    n_sel = seq // SEL_LEN
    gw = PAIRS * LANES
    kern = functools.partial(_nsa_cmp_kernel, ncp=ncp, n_cmp=n_cmp, n_sel=n_sel)
    return pl.pallas_call(
        kern,
        grid=(batch, KV_GROUPS, nstep),
        in_specs=[
            pl.BlockSpec(memory_space=pltpu.SMEM),
            pl.BlockSpec((rows, gw), lambda b, g, c: (b * nstep + c, q_col_block + g)),
            pl.BlockSpec((1, 1, ncp, LANES), lambda b, g, c: (b, g, 0, 0)),
            pl.BlockSpec((1, 1, ncp, 2 * LANES), lambda b, g, c: (b, g, 0, 0)),
        ],
        out_specs=[
            pl.BlockSpec((rows, gw), lambda b, g, c: (b * nstep + c, g)),
            pl.BlockSpec((1, 1, rows, LANES), lambda b, g, c: (b, g, c, 0)),
        ],
        out_shape=[
            jax.ShapeDtypeStruct((batch * seq, N_HEADS * HEAD_DIM), F32),
            jax.ShapeDtypeStruct((batch, KV_GROUPS, seq, LANES), MXU_DTYPE),
        ],
        scratch_shapes=[pltpu.VMEM((Q_SUB, ROWS, LANES), MXU_DTYPE), pltpu.VMEM((Q_SUB, ROWS, ncp), MXU_DTYPE),
                        pltpu.VMEM((Q_SUB, ROWS, 2 * LANES), F32)],
        compiler_params=_params("parallel", "parallel", "arbitrary"),
        name="nsa_cmp_select",
    )(nslopes, q, kc2, vc_aug)


N_FEAT = 6


def sel_query_features():
    s = jnp.asarray((-_alibi_neg_slopes().astype(np.float64) * LOG2E).astype(np.float32))
    s1 = s.astype(MXU_DTYPE).astype(F32)
    s2 = (s - s1).astype(MXU_DTYPE).astype(F32)
    s3 = (s - s1 - s2).astype(MXU_DTYPE).astype(F32)
    feat = jnp.zeros((N_HEADS, LANES), F32).at[:, HEAD_DIM:HEAD_DIM + N_FEAT].set(
        jnp.stack([s1, s2, s3, s1, s2, s3], axis=1))
    feat = feat.at[:, HEAD_DIM + N_FEAT].set(MASK_VALUE)
    order = np.array([[_head_of_row_block(g, rb) for rb in range(GROUP)] for g in range(KV_GROUPS)])
    return feat[order]


def sel_key_features(seq):
    pos = np.arange(seq)
    kk = pos % SEL_STEP
    f = np.zeros((seq + SEL_STEP, 2 * LANES), np.float32)
    f[:seq, HEAD_DIM:HEAD_DIM + 3] = (SEL_LEN * (kk // SEL_LEN))[:, None]
    f[:seq, HEAD_DIM + 3:HEAD_DIM + 6] = (kk % SEL_LEN)[:, None]
    f[seq:, HEAD_DIM + N_FEAT] = 1.0
    f[pos, LANES + pos // SEL_LEN] = 1.0
    return f


def _nsa_sel_kernel(delta_ref, qfeat_ref, q_ref, sel_ref, k_in, v_in, kfeat_ref, gate_ref, ex_ref, oc_ref, ow_ref,
                    o_ref, qa_ref, m_ref, acc_ref, s_a, s_b, p_a, p_b, al_a, al_b, k_ref, v_ref):
    grp = pl.program_id(1)
    c = pl.program_id(2)
    t0 = c * (SEL_Q * Q_BLOCK)
    seq = k_in.shape[0]
    srows = lambda j, rb: slice(j * ROWS + rb * Q_BLOCK, j * ROWS + (rb + 1) * Q_BLOCK)
    qrows = lambda j: slice(j * Q_BLOCK, (j + 1) * Q_BLOCK)

    @pl.when(c == 0)
    def _():
        def put_k(r0, y):
            k_ref[pl.ds(r0, KV_CHUNK), 0:LANES] = y
            k_ref[pl.ds(r0, KV_CHUNK), LANES:2 * LANES] = kfeat_ref[pl.ds(r0, KV_CHUNK), LANES:2 * LANES]

        def put_v(r0, y):
            v_ref[pl.ds(r0, KV_CHUNK), 0:LANES] = y
            v_ref[pl.ds(r0, KV_CHUNK), LANES:2 * LANES] = jnp.ones((KV_CHUNK, LANES), v_ref.dtype)

        _unpack_group(k_in, grp, put_k, fill_ref=kfeat_ref)
        _unpack_group(v_in, grp, put_v)
        k_ref[seq:seq + SEL_STEP, :] = kfeat_ref[seq:seq + SEL_STEP, :]
        v_ref[seq:seq + SEL_STEP, :] = jnp.zeros((SEL_STEP, 2 * LANES), v_ref.dtype)
    lane = lax.broadcasted_iota(jnp.int32, (Q_BLOCK, LANES), 1)
    low = lane < HEAD_DIM
    for j in range(SEL_Q):
        selneg = ((1.0 - sel_ref[0, 0, qrows(j), :].astype(F32)) * MASK_VALUE).astype(qa_ref.dtype)
        for p in range(PAIRS):
            qp = q_ref[qrows(j), p * LANES:(p + 1) * LANES].astype(F32) * (HEAD_DIM ** -0.5 * LOG2E)
            for rb, src in ((p, qp), (PAIRS + p, pltpu.roll(qp, HEAD_DIM, axis=1))):
                qa_ref[srows(j, rb), 0:LANES] = jnp.where(low, src, qfeat_ref[0, rb:rb + 1, :]).astype(qa_ref.dtype)
                qa_ref[srows(j, rb), LANES:2 * LANES] = selneg
    n_steps = t0 // SEL_STEP + 1
    n_pad_step = seq // SEL_STEP
    dot_nt = (((1,), (1,)), ((), ()))

    def key_start(step):
        return pl.multiple_of(jnp.clip(step, 0, n_pad_step) * SEL_STEP, SEL_STEP)

    def scores(step, s_out):
        k = k_ref[pl.ds(key_start(step), SEL_STEP), :]
        s_out[...] = lax.dot_general(qa_ref[...], k, dot_nt, preferred_element_type=F32)

    def softmax(s_in, p_out, al_out):
        for rb in range(SEL_Q * GROUP):
            m_prev = m_ref[_rows(rb), :] - delta_ref[_head_of_row_block(grp, rb % GROUP)]
            m_new = jnp.maximum(m_prev, jnp.max(s_in[_rows(rb), :], axis=1, keepdims=True))
            al_out[_rows(rb), :] = jnp.exp2(m_prev - m_new)
            m_ref[_rows(rb), :] = m_new
        for rb in range(SEL_Q * GROUP):
            p_out[_rows(rb), :] = jnp.exp2(s_in[_rows(rb), :] - _rep(m_ref[_rows(rb), :], SEL_STEP)).astype(p_out.dtype)

    def values(step, p_in, al_in):
        v = v_ref[pl.ds(key_start(step), SEL_STEP), :]
        pv = jnp.dot(p_in[...], v, preferred_element_type=F32)
        alpha = al_in[...]
        acc_ref[...] = acc_ref[...] * jnp.concatenate([alpha, alpha], axis=1) + pv

    def even_half(t):
        scores(t, s_a)
        softmax(s_b, p_b, al_b)
        values(t - 2, p_a, al_a)

    def odd_half(t):
        scores(t, s_b)
        softmax(s_a, p_a, al_a)
        values(t - 2, p_b, al_b)

    scores(0, s_a)
    qi = lax.broadcasted_iota(jnp.int32, (Q_BLOCK, Q_BLOCK), 0)
    ki = lax.broadcasted_iota(jnp.int32, (Q_BLOCK, Q_BLOCK), 1)
    own_bias = jnp.where((ki <= qi) & (ki // SEL_LEN == qi // SEL_LEN), 0.0, MASK_VALUE)
    owns = [pl.multiple_of(t0 + j * Q_BLOCK, Q_BLOCK) for j in range(SEL_Q)]
    s_owns = [lax.dot_general(qa_ref[j * ROWS:(j + 1) * ROWS, 0:LANES], k_ref[pl.ds(owns[j], Q_BLOCK), 0:LANES],
                              dot_nt, preferred_element_type=F32) for j in range(SEL_Q)]
    scores(1, s_b)
    back = jnp.full((Q_BLOCK, LANES), n_steps, jnp.int32).astype(F32)
    p_owns = []
    for j in range(SEL_Q):
        p_own = []
        for rb in range(GROUP):
            s = s_owns[j][_rows(rb)] + own_bias
            m = jnp.max(s, axis=1, keepdims=True)
            p_own.append(jnp.exp2(s - m).astype(MXU_DTYPE))
            m_ref[srows(j, rb), :] = m + back * delta_ref[_head_of_row_block(grp, rb)]
        p_owns.append(jnp.concatenate(p_own, axis=0))
    softmax(s_a, p_a, al_a)
    for j in range(SEL_Q):
        acc_ref[j * ROWS:(j + 1) * ROWS, :] = jnp.dot(p_owns[j], v_ref[pl.ds(owns[j], Q_BLOCK), :],
                                                      preferred_element_type=F32)

    def quad(j, carry):
        even_half(4 * j + 2)
        odd_half(4 * j + 3)
        even_half(4 * j + 4)
        odd_half(4 * j + 5)
        return carry

    lax.fori_loop(0, n_steps // 4, quad, 0)
    rest = 4 * (n_steps // 4) + 2

    @pl.when(n_steps % 4 >= 2)
    def _():
        even_half(rest)
        odd_half(rest + 1)

    @pl.when(n_steps % 2 == 1)
    def _():
        even_half(n_steps + 1)

    gw = PAIRS * LANES
    sig = jax.nn.sigmoid(gate_ref[...])
    hi = sig.astype(MXU_DTYPE)
    lo = (sig - hi.astype(F32)).astype(MXU_DTYPE)
    ex = ex_ref[0]
    g = jnp.dot(hi, ex, preferred_element_type=F32) + jnp.dot(lo, ex, preferred_element_type=F32)
    for j in range(SEL_Q):
        o_sel = _pairs(lambda rb: acc_ref[srows(j, rb), 0:LANES], lambda rb: acc_ref[srows(j, rb), LANES:2 * LANES])
        for p in range(PAIRS):
            cols = slice(p * LANES, (p + 1) * LANES)
            out = (g[qrows(j), cols] * oc_ref[qrows(j), cols]
                   + g[qrows(j), gw + p * LANES:gw + (p + 1) * LANES] * o_sel[p]
                   + g[qrows(j), 2 * gw + p * LANES:2 * gw + (p + 1) * LANES] * ow_ref[qrows(j), cols])
            o_ref[qrows(j), cols] = out.astype(o_ref.dtype)


def _gate_expansion():
    gw = PAIRS * LANES
    ex = np.zeros((KV_GROUPS, LANES, 3 * gw), np.float32)
    for g in range(KV_GROUPS):
        for hl in range(GROUP):
            for i in range(3):
                ex[g, 3 * (g * GROUP + hl) + i, i * gw + hl * HEAD_DIM:i * gw + (hl + 1) * HEAD_DIM] = 1.0
    return ex


def nsa_sel_attention(qkv, sel, gate, o_cmp, o_win, batch, seq, k_col, v_col):
    nstep = seq // (SEL_Q * Q_BLOCK)
    rows = SEL_Q * Q_BLOCK
    srows = SEL_Q * ROWS
    gw = PAIRS * LANES
    deltas = jnp.asarray((-_alibi_neg_slopes().astype(np.float64) * LOG2E * SEL_STEP).astype(np.float32))
    ex = jnp.asarray(_gate_expansion(), MXU_DTYPE)
    kfeat = jnp.asarray(sel_key_features(seq), MXU_DTYPE)
    blk = lambda b, g, c: (b * nstep + c, g)
    return pl.pallas_call(
        _nsa_sel_kernel,
        grid=(batch, KV_GROUPS, nstep),
        in_specs=[
            pl.BlockSpec(memory_space=pltpu.SMEM),
            pl.BlockSpec((1, GROUP, LANES), lambda b, g, c: (g, 0, 0)),
            pl.BlockSpec((rows, gw), blk),
            pl.BlockSpec((1, 1, rows, LANES), lambda b, g, c: (b, g, c, 0)),
            _kv_block_spec(seq, k_col),
            _kv_block_spec(seq, v_col),
            pl.BlockSpec((seq + SEL_STEP, 2 * LANES), lambda b, g, c: (0, 0), pipeline_mode=pl.Buffered(1)),
            pl.BlockSpec((rows, LANES), lambda b, g, c: (b * nstep + c, 0)),
            pl.BlockSpec((1, LANES, 3 * gw), lambda b, g, c: (g, 0, 0)),
            pl.BlockSpec((rows, gw), blk),
            pl.BlockSpec((rows, gw), blk),
        ],
        out_specs=pl.BlockSpec((rows, gw), blk),
        out_shape=jax.ShapeDtypeStruct((batch * seq, N_HEADS * HEAD_DIM), MXU_DTYPE),
        scratch_shapes=[
            pltpu.VMEM((srows, 2 * LANES), MXU_DTYPE),
            pltpu.VMEM((srows, LANES), F32),
            pltpu.VMEM((srows, 2 * LANES), F32),
            pltpu.VMEM((srows, SEL_STEP), F32),
            pltpu.VMEM((srows, SEL_STEP), F32),
            pltpu.VMEM((srows, SEL_STEP), MXU_DTYPE),
            pltpu.VMEM((srows, SEL_STEP), MXU_DTYPE),
            pltpu.VMEM((srows, LANES), F32),
            pltpu.VMEM((srows, LANES), F32),
            pltpu.VMEM((seq + SEL_STEP, 2 * LANES), MXU_DTYPE),
            pltpu.VMEM((seq + SEL_STEP, 2 * LANES), MXU_DTYPE),
        ],
        compiler_params=_params("parallel", "parallel", "arbitrary"),
        name="nsa_sel_attention",
    )(deltas, sel_query_features(), qkv, sel, qkv, qkv, kfeat, gate, ex, o_cmp, o_win)


def _nsa_win_kernel(nslope_ref, q_ref, k_in, v_in, o_ref, qs_ref, bias_scr, k_ref, v_ref):
    grp = pl.program_id(1)
    c = pl.program_id(2)
    span = NSA_WINDOW + Q_BLOCK
    lead = NSA_WINDOW // Q_BLOCK
    for j in range(Q_SUB):
        _build_q_stack(q_ref, qs_ref.at[j], j * Q_BLOCK)
    nslope = lambda rb: nslope_ref[_head_of_row_block(grp, rb)]

    def write(j, outs):
        for p, o in enumerate(outs):
            o_ref[j * Q_BLOCK:(j + 1) * Q_BLOCK, p * LANES:(p + 1) * LANES] = o

    @pl.when(c == 0)
    def _():
        _unpack_kv(k_in, v_in, grp, k_ref, v_ref)
        d = _pos_tiles(NSA_WINDOW, 0, span)
        for rb in range(GROUP):
            bias_scr[rb] = _band_bias(nslope(rb), d, NSA_WINDOW)

    @pl.when(c < lead // Q_SUB)
    def _():
        for j in range(Q_SUB):
            d = _pos_tiles((c * Q_SUB + j) * Q_BLOCK, 0, span)
            write(j, _window_attend(qs_ref.at[j], k_ref[0:span, :], v_ref[0:span, :],
                                    lambda rb: _band_bias(nslope(rb), d, NSA_WINDOW)))

    @pl.when(c >= lead // Q_SUB)
    def _():
        for j in range(Q_SUB):
            start = pl.multiple_of((c * Q_SUB + j) * Q_BLOCK - NSA_WINDOW, Q_BLOCK)
            write(j, _window_attend(qs_ref.at[j], k_ref[pl.ds(start, span), :], v_ref[pl.ds(start, span), :],
                                    lambda rb: bias_scr[rb]))


def nsa_win_attention(qkv, nslopes, batch, seq, k_col, v_col):
    nstep = seq // (Q_SUB * Q_BLOCK)
    rows = Q_SUB * Q_BLOCK
    gw = PAIRS * LANES
    kd = KV_GROUPS * HEAD_DIM
    span = NSA_WINDOW + Q_BLOCK
    return pl.pallas_call(
        _nsa_win_kernel,
        grid=(batch, KV_GROUPS, nstep),
        in_specs=[
            pl.BlockSpec(memory_space=pltpu.SMEM),
            pl.BlockSpec((rows, gw), lambda b, g, c: (b * nstep + c, g)),
            _kv_block_spec(seq, k_col),
            _kv_block_spec(seq, v_col),
        ],
        out_specs=pl.BlockSpec((rows, gw), lambda b, g, c: (b * nstep + c, g)),
        out_shape=jax.ShapeDtypeStruct((batch * seq, N_HEADS * HEAD_DIM), F32),
        scratch_shapes=[
            pltpu.VMEM((Q_SUB, ROWS, LANES), MXU_DTYPE),
            pltpu.VMEM((GROUP, Q_BLOCK, span), F32),
            pltpu.VMEM((seq, LANES), MXU_DTYPE),
            pltpu.VMEM((seq, 2 * LANES), MXU_DTYPE),
        ],
        compiler_params=_params("parallel", "parallel", "arbitrary"),
        name="nsa_win_attention",
    )(nslopes, qkv, qkv, qkv)


def _overlap_matrix(seq):
    ncp = seq // CMP_STRIDE
    n_cmp = (seq - CMP_LEN) // CMP_STRIDE + 1
    cs = np.arange(n_cmp) * CMP_STRIDE
    ss = np.arange(seq // SEL_LEN) * SEL_LEN
    ov = (cs[:, None] < ss[None, :] + SEL_LEN) & (cs[:, None] + CMP_LEN > ss[None, :])
    out = np.zeros((ncp, LANES), np.float32)
    out[:n_cmp, :seq // SEL_LEN] = ov
    return out


def _swa_layer(h, x, w_in, b_in, sinks, w_o, b_o, g_post, g_next, nslopes, batch, seq):
    hd = N_HEADS * HEAD_DIM
    kd = KV_GROUPS * HEAD_DIM
    qkv = matmul_bias(h, w_in.astype(MXU_DTYPE), b_in, MXU_DTYPE, name="swa_in_proj")
    o = swa_attention(qkv, nslopes, sinks.astype(F32) * LOG2E, batch, seq)
    return matmul_norm_res(o, w_o.astype(MXU_DTYPE), b_o, g_post, g_next, x, tn=w_o.shape[1], name="swa_out_proj")


def _nsa_layer(h, x, w_in, cmp_pe, cmp_w1, cmp_b1, cmp_w2, cmp_b2, w_o, g_post, g_next, nslopes, batch, seq):
    hd = N_HEADS * HEAD_DIM
    kd = KV_GROUPS * HEAD_DIM
    t = batch * seq
    ncp = seq // CMP_STRIDE
    qkv = matmul_bias(h, w_in[:, :hd + 6 * kd].astype(MXU_DTYPE), jnp.zeros((hd + 6 * kd,), F32), MXU_DTYPE,
                      name="nsa_in_proj")
    n_gate = 3 * N_HEADS
    w_gate = jnp.pad(w_in[:, hd + 6 * kd:], ((0, 0), (0, LANES - n_gate))).astype(MXU_DTYPE)
    gate = matmul_bias(h, w_gate, jnp.zeros((LANES,), F32), F32, name="nsa_gate_proj")

    def kv(i):
        return qkv[:, hd + i * kd:hd + (i + 1) * kd]

    def slabs(a):
        return a.reshape(batch, seq, KV_GROUPS, HEAD_DIM).transpose(0, 2, 1, 3).reshape(
            batch, KV_GROUPS, ncp, CMP_STRIDE * HEAD_DIM)

    z = jnp.stack([slabs(kv(0)), slabs(kv(1))])
    half = CMP_STRIDE * HEAD_DIM
    cmp_out = compress(z, cmp_pe.reshape(2, 2, half).astype(F32), cmp_w1.astype(MXU_DTYPE),
                       cmp_b1.reshape(2, 1, -1), cmp_w2.astype(MXU_DTYPE), cmp_b2.reshape(2, 1, -1))
    kcm = cmp_out[0].astype(MXU_DTYPE)
    vcm = cmp_out[1].astype(MXU_DTYPE)
    kc2 = jnp.concatenate([kcm, kcm], axis=-1)
    ov = jnp.broadcast_to(jnp.asarray(_overlap_matrix(seq), MXU_DTYPE), (batch, KV_GROUPS, ncp, LANES))
    vc_aug = jnp.concatenate([vcm, jnp.ones((batch, KV_GROUPS, ncp, HEAD_DIM), MXU_DTYPE), ov], axis=-1)

    o_cmp, sel = nsa_cmp_select(qkv, kc2, vc_aug, nslopes, batch, seq, 0)
    kv_col = lambda i: hd // kd + i
    o_win = nsa_win_attention(qkv, nslopes, batch, seq, kv_col(4), kv_col(5))
    o = nsa_sel_attention(qkv, sel, gate, o_cmp, o_win, batch, seq, kv_col(2), kv_col(3))
    return matmul_norm_res(o, w_o.astype(MXU_DTYPE), jnp.zeros((w_o.shape[1],), F32), g_post, g_next, x,
                           tn=w_o.shape[1], name="nsa_out_proj")


def kernel(x, norm_g, swa_w_in, swa_b_in, swa_sinks, swa_w_o, swa_b_o, nsa_w_in, nsa_cmp_pe, nsa_cmp_w1, nsa_cmp_b1, nsa_cmp_w2, nsa_cmp_b2, nsa_w_o, ffn_w_gate, ffn_w_up, ffn_conv_w, ffn_conv_b, ffn_w_down):
    batch, seq, d = x.shape
    depth = norm_g.shape[0]
    nslopes = jnp.asarray((_alibi_neg_slopes().astype(np.float64) * LOG2E).astype(np.float32))
    xf = x.reshape(batch * seq, d)
    h = rms_cast(xf, norm_g[0, 0])
    for i in range(depth):
        g = norm_g[i]
        j = i // 2
        if i % 2 == 0:
            xf, h = _swa_layer(h, xf, swa_w_in[j], swa_b_in[j], swa_sinks[j], swa_w_o[j], swa_b_o[j],
                               g[1], g[2], nslopes, batch, seq)
        else:
            xf, h = _nsa_layer(h, xf, nsa_w_in[j], nsa_cmp_pe[j], nsa_cmp_w1[j], nsa_cmp_b1[j], nsa_cmp_w2[j],
                               nsa_cmp_b2[j], nsa_w_o[j], g[1], g[2], nslopes, batch, seq)
        act = ffn_up(h, ffn_w_gate, ffn_w_up, i, ffn_conv_w[i], ffn_conv_b[i], seq)
        g_next = norm_g[i + 1, 0] if i + 1 < depth else jnp.ones((d,), F32)
        xf, h = matmul_norm_res(act, ffn_w_down[i].astype(MXU_DTYPE), jnp.zeros((d,), F32), g[3], g_next, xf,
                                tn=512, name="ffn_down")
    return xf.reshape(batch, seq, d)
```

```python
import functools

import numpy as np
import jax
import jax.numpy as jnp
from jax import lax
from jax.experimental import pallas as pl
from jax.experimental.pallas import tpu as pltpu

F32 = jnp.float32
MXU_DTYPE = jnp.bfloat16

N_HEADS = 32
HEAD_DIM = 64
KV_GROUPS = 4
GROUP = N_HEADS // KV_GROUPS
PAIRS = GROUP // 2
LANES = 128
Q_BLOCK = 128
ROWS = GROUP * Q_BLOCK
SWA_WINDOW = 128
CMP_LEN = 32
CMP_STRIDE = 16
SEL_LEN = 64
SEL_TOPK = 16
SEL_FORCED = 3
LOG2E = 1.4426950408889634
SEL_STEP = 512
SEL_Q = 2
NSA_WINDOW = 512
CONV_WIDTH = 3
RMS_EPS = 1e-6
MASK_VALUE = -1e30
VMEM_LIMIT = 60000 * 1024


def _params(*sem):
    return pltpu.CompilerParams(dimension_semantics=sem, vmem_limit_bytes=VMEM_LIMIT)


def _alibi_neg_slopes():
    return (-np.exp2(-8.0 * np.arange(1, N_HEADS + 1, dtype=np.float64) / N_HEADS)).astype(np.float32)


def _rms_cast_kernel(x_ref, g_ref, o_ref):
    x = x_ref[...]
    ms = jnp.mean(x * x, axis=-1, keepdims=True)
    o_ref[...] = (x * lax.rsqrt(ms + RMS_EPS) * g_ref[...]).astype(o_ref.dtype)


def rms_cast(x, g, tm=512):
    t, d = x.shape
    return pl.pallas_call(
        _rms_cast_kernel,
        grid=(t // tm,),
        in_specs=[pl.BlockSpec((tm, d), lambda i: (i, 0)), pl.BlockSpec((1, d), lambda i: (0, 0))],
        out_specs=pl.BlockSpec((tm, d), lambda i: (i, 0)),
        out_shape=jax.ShapeDtypeStruct((t, d), MXU_DTYPE),
        compiler_params=_params("parallel"),
        name="rms_cast",
    )(x, g.reshape(1, d))


def _matmul_kernel(a_ref, w_ref, b_ref, o_ref):
    acc = jnp.dot(a_ref[...], w_ref[...], preferred_element_type=F32)
    o_ref[...] = (acc + b_ref[...]).astype(o_ref.dtype)


def matmul_bias(a, w, b, out_dtype, tm=1024, tn=512, name="matmul_bias"):
    t, k = a.shape
    n = w.shape[1]
    tn = min(tn, n)
    return pl.pallas_call(
        _matmul_kernel,
        grid=(t // tm, n // tn),
        in_specs=[
            pl.BlockSpec((tm, k), lambda i, j: (i, 0)),
            pl.BlockSpec((k, tn), lambda i, j: (0, j)),
            pl.BlockSpec((1, tn), lambda i, j: (0, j)),
        ],
        out_specs=pl.BlockSpec((tm, tn), lambda i, j: (i, j)),
        out_shape=jax.ShapeDtypeStruct((t, n), out_dtype),
        compiler_params=_params("parallel", "arbitrary"),
        name=name,
    )(a, w, b.reshape(1, n))


def _mm_norm_res_kernel(a_ref, w_ref, b_ref, gpost_ref, gnext_ref, x_ref, xo_ref, ho_ref, y_scr, *, nj, tn, n):
    j = pl.program_id(1)
    y_scr[j] = jnp.dot(a_ref[...], w_ref[...], preferred_element_type=F32) + b_ref[...]

    @pl.when(j == nj - 1)
    def _():
        ss = jnp.sum(y_scr[0] * y_scr[0], axis=1, keepdims=True)
        for jj in range(1, nj):
            ss = ss + jnp.sum(y_scr[jj] * y_scr[jj], axis=1, keepdims=True)
        r = lax.rsqrt(ss / n + RMS_EPS)
        ss2 = jnp.zeros_like(ss)
        for jj in range(nj):
            cols = slice(jj * tn, (jj + 1) * tn)
            xn = x_ref[:, cols] + y_scr[jj] * r * gpost_ref[:, cols]
            xo_ref[:, cols] = xn
            ss2 = ss2 + jnp.sum(xn * xn, axis=1, keepdims=True)
        r2 = lax.rsqrt(ss2 / n + RMS_EPS)
        for jj in range(nj):
            cols = slice(jj * tn, (jj + 1) * tn)
            ho_ref[:, cols] = (xo_ref[:, cols] * r2 * gnext_ref[:, cols]).astype(ho_ref.dtype)


def matmul_norm_res(a, w, b, g_post, g_next, x, tm=512, tn=512, name="matmul_norm_res"):
    t, k = a.shape
    n = w.shape[1]
    nj = n // tn
    kern = functools.partial(_mm_norm_res_kernel, nj=nj, tn=tn, n=n)
    return pl.pallas_call(
        kern,
        grid=(t // tm, nj),
        in_specs=[
            pl.BlockSpec((tm, k), lambda i, j: (i, 0)),
            pl.BlockSpec((k, tn), lambda i, j: (0, j)),
            pl.BlockSpec((1, tn), lambda i, j: (0, j)),
            pl.BlockSpec((1, n), lambda i, j: (0, 0)),
            pl.BlockSpec((1, n), lambda i, j: (0, 0)),
            pl.BlockSpec((tm, n), lambda i, j: (i, 0)),
        ],
        out_specs=[
            pl.BlockSpec((tm, n), lambda i, j: (i, 0)),
            pl.BlockSpec((tm, n), lambda i, j: (i, 0)),
        ],
        out_shape=[jax.ShapeDtypeStruct((t, n), F32), jax.ShapeDtypeStruct((t, n), MXU_DTYPE)],
        scratch_shapes=[pltpu.VMEM((nj, tm, tn), F32)],
        compiler_params=_params("parallel", "arbitrary"),
        name=name,
    )(a, w, b.reshape(1, n), g_post.reshape(1, n), g_next.reshape(1, n), x)


FIX_ROWS = 16
CARRY_ROWS = 8


def _ffn_up_kernel(h_ref, wg_ref, wu_ref, cw_ref, cb_ref, o_ref, wg_scr, wu_scr, carry_scr, *, tm, tiles_per_seq):
    i = pl.program_id(1)

    @pl.when(i == 0)
    def _():
        wg_scr[...] = wg_ref[...].astype(wg_scr.dtype)
        wu_scr[...] = wu_ref[...].astype(wu_scr.dtype)

    h = h_ref[...]
    gate = jnp.dot(h, wg_scr[...], preferred_element_type=F32)
    up = jnp.dot(h, wu_scr[...], preferred_element_type=F32)
    w0 = cw_ref[0:1, :]
    w1 = cw_ref[1:2, :]
    w2 = cw_ref[2:3, :]
    b = cb_ref[...]
    a = b + pltpu.roll(gate, 2, axis=0) * w0
    a = a + pltpu.roll(gate, 1, axis=0) * w1
    a = a + gate * w2
    o_ref[...] = (jax.nn.silu(a) * up).astype(o_ref.dtype)

    seq_start = (i % tiles_per_seq) == 0
    prev = jnp.where(seq_start, 0.0, carry_scr[...])
    head = gate[0:FIX_ROWS]
    ext = jnp.concatenate([prev, head], axis=0)
    af = b + ext[CARRY_ROWS - 2:CARRY_ROWS - 2 + FIX_ROWS] * w0
    af = af + ext[CARRY_ROWS - 1:CARRY_ROWS - 1 + FIX_ROWS] * w1
    af = af + head * w2
    o_ref[0:FIX_ROWS, :] = (jax.nn.silu(af) * up[0:FIX_ROWS]).astype(o_ref.dtype)
    carry_scr[...] = gate[tm - CARRY_ROWS:tm]


def ffn_up(h, wg, wu, layer, conv_w, conv_b, seq, tm=1024, tn=512):
    t, k = h.shape
    n = wg.shape[2]
    kern = functools.partial(_ffn_up_kernel, tm=tm, tiles_per_seq=seq // tm)
    return pl.pallas_call(
        kern,
        grid=(n // tn, t // tm),
        in_specs=[
            pl.BlockSpec((tm, k), lambda j, i: (i, 0)),
            pl.BlockSpec((None, k, tn), lambda j, i: (layer, 0, j)),
            pl.BlockSpec((None, k, tn), lambda j, i: (layer, 0, j)),
            pl.BlockSpec((CONV_WIDTH, tn), lambda j, i: (0, j)),
            pl.BlockSpec((1, tn), lambda j, i: (0, j)),
        ],
        out_specs=pl.BlockSpec((tm, tn), lambda j, i: (i, j)),
        out_shape=jax.ShapeDtypeStruct((t, n), MXU_DTYPE),
        scratch_shapes=[
            pltpu.VMEM((k, tn), MXU_DTYPE),
            pltpu.VMEM((k, tn), MXU_DTYPE),
            pltpu.VMEM((CARRY_ROWS, tn), F32),
        ],
        compiler_params=_params("arbitrary", "arbitrary"),
        name="ffn_up",
    )(h, wg, wu, conv_w, conv_b.reshape(1, n))


def _head_of_row_block(group, rb):
    return group * GROUP + 2 * (rb % PAIRS) + rb // PAIRS


Q_SUB = 8


def _build_q_stack(q_ref, qs_ref, row0=0):
    lane = lax.broadcasted_iota(jnp.int32, (Q_BLOCK, LANES), 1)
    even = lane < HEAD_DIM
    for p in range(PAIRS):
        qp = q_ref[row0:row0 + Q_BLOCK, p * LANES:(p + 1) * LANES].astype(F32) * (HEAD_DIM ** -0.5 * LOG2E)
        qs_ref[p * Q_BLOCK:(p + 1) * Q_BLOCK, :] = jnp.where(even, qp, 0.0).astype(qs_ref.dtype)
        qs_ref[(PAIRS + p) * Q_BLOCK:(PAIRS + p + 1) * Q_BLOCK, :] = jnp.where(even, 0.0, qp).astype(qs_ref.dtype)


def _rep(x, size):
    return x if size == LANES else jnp.concatenate([x] * (size // LANES), axis=1)


def _pairs(num, den):
    lane = lax.broadcasted_iota(jnp.int32, (Q_BLOCK, LANES), 1)
    even = lane < HEAD_DIM
    outs = []
    for p in range(PAIRS):
        oe = num(p) / jnp.maximum(den(p), 1e-30)
        oo = num(PAIRS + p) / jnp.maximum(den(PAIRS + p), 1e-30)
        outs.append(jnp.where(even, oe, oo))
    return outs


def _rows(rb):
    return slice(rb * Q_BLOCK, (rb + 1) * Q_BLOCK)


def _band_bias(nslope, d, window):
    return jnp.where((d >= 0) & (d < window), nslope * d.astype(F32), MASK_VALUE)


def _window_attend(qs_ref, k, v, bias, extra_logit=None):
    half = ROWS // 2
    dot_nt = (((1,), (1,)), ((), ()))
    s_halves = [lax.dot_general(qs_ref[hh * half:(hh + 1) * half, :], k, dot_nt, preferred_element_type=F32)
                for hh in range(2)]
    ps, extras = [], []
    for rb in range(GROUP):
        lo = (rb % PAIRS) * Q_BLOCK
        s = s_halves[rb // PAIRS][lo:lo + Q_BLOCK] + bias(rb)
        m = jnp.max(s, axis=1, keepdims=True)
        if extra_logit is not None:
            m = jnp.maximum(m, extra_logit(rb))
            extras.append(jnp.exp2(extra_logit(rb) - m))
        ps.append(jnp.exp2(s - m).astype(MXU_DTYPE))
    r_halves = [jnp.dot(jnp.concatenate(ps[hh * PAIRS:(hh + 1) * PAIRS], axis=0), v, preferred_element_type=F32)
                for hh in range(2)]

    def part(rb, cols):
        lo = (rb % PAIRS) * Q_BLOCK
        return r_halves[rb // PAIRS][lo:lo + Q_BLOCK, cols]

    num = lambda rb: part(rb, slice(0, LANES))
    if extra_logit is None:
        den = lambda rb: part(rb, slice(LANES, 2 * LANES))
    else:
        den = lambda rb: part(rb, slice(LANES, 2 * LANES)) + extras[rb]
    return _pairs(num, den)


KV_CHUNK = 1024


def _unpack_group(kv_ref, grp, write_chunk, fill_ref=None):
    n = kv_ref.shape[0]
    for odd in range(2):
        def fill(odd=odd):
            def body(i, carry):
                r0 = pl.multiple_of(i * KV_CHUNK, KV_CHUNK)
                x = kv_ref[pl.ds(r0, KV_CHUNK), :]
                xi = pltpu.bitcast(x, jnp.int32)
                xr = pltpu.roll(xi, HEAD_DIM, axis=1)
                low = lax.broadcasted_iota(jnp.int32, xi.shape, 1) < HEAD_DIM
                if fill_ref is None:
                    y = jnp.where(low, xr, xi) if odd else jnp.where(low, xi, xr)
                else:
                    f = pltpu.bitcast(fill_ref[pl.ds(r0, KV_CHUNK), 0:LANES], jnp.int32)
                    y = jnp.where(low, xr if odd else xi, f)
                write_chunk(r0, pltpu.bitcast(y, x.dtype))
                return carry

            lax.fori_loop(0, n // KV_CHUNK, body, 0)

        pl.when(grp % 2 == odd)(fill)


def _kv_block_spec(seq, col):
    pairs_per_tensor = KV_GROUPS * HEAD_DIM // LANES
    return pl.BlockSpec((seq, LANES), lambda b, g, c: (b, pairs_per_tensor * col + g // 2))


def _pos_tiles(t0, start, size):
    qi = lax.broadcasted_iota(jnp.int32, (Q_BLOCK, size), 0)
    ki = lax.broadcasted_iota(jnp.int32, (Q_BLOCK, size), 1)
    return (t0 - start) + (qi - ki)


def _unpack_kv(k_in, v_in, grp, k_ref, v_ref):
    def put_k(r0, y):
        k_ref[pl.ds(r0, KV_CHUNK), :] = y

    def put_v(r0, y):
        v_ref[pl.ds(r0, KV_CHUNK), 0:LANES] = y
        v_ref[pl.ds(r0, KV_CHUNK), LANES:2 * LANES] = jnp.ones((KV_CHUNK, LANES), v_ref.dtype)

    _unpack_group(k_in, grp, put_k)
    _unpack_group(v_in, grp, put_v)


def _swa_kernel(nslope_ref, sink_ref, q_ref, k_in, v_in, o_ref, qs_ref, bias_scr, k_ref, v_ref):
    grp = pl.program_id(1)
    c = pl.program_id(2)
    span = SWA_WINDOW + Q_BLOCK
    for j in range(Q_SUB):
        _build_q_stack(q_ref, qs_ref.at[j], j * Q_BLOCK)
    sink = lambda rb: sink_ref[_head_of_row_block(grp, rb)]

    def write(j, outs):
        for p, o in enumerate(outs):
            o_ref[j * Q_BLOCK:(j + 1) * Q_BLOCK, p * LANES:(p + 1) * LANES] = o.astype(o_ref.dtype)

    def later_block(j):
        start = pl.multiple_of((c * Q_SUB + j) * Q_BLOCK - SWA_WINDOW, Q_BLOCK)
        write(j, _window_attend(qs_ref.at[j], k_ref[pl.ds(start, span), :], v_ref[pl.ds(start, span), :],
                                lambda rb: bias_scr[rb], sink))

    @pl.when(c == 0)
    def _():
        _unpack_kv(k_in, v_in, grp, k_ref, v_ref)
        d = _pos_tiles(SWA_WINDOW, 0, span)
        for rb in range(GROUP):
            bias_scr[rb] = _band_bias(nslope_ref[_head_of_row_block(grp, rb)], d, SWA_WINDOW)

    @pl.when(c == 0)
    def _():
        write(0, _window_attend(qs_ref.at[0], k_ref[0:Q_BLOCK, :], v_ref[0:Q_BLOCK, :],
                                lambda rb: bias_scr[rb, :, SWA_WINDOW:span], sink))
        for j in range(1, Q_SUB):
            later_block(j)

    @pl.when(c > 0)
    def _():
        for j in range(Q_SUB):
            later_block(j)


def swa_attention(qkv, nslopes, sinks, batch, seq):
    nstep = seq // (Q_SUB * Q_BLOCK)
    rows = Q_SUB * Q_BLOCK
    gw = PAIRS * LANES
    kd = KV_GROUPS * HEAD_DIM
    span = SWA_WINDOW + Q_BLOCK
    k_col = N_HEADS * HEAD_DIM // kd
    return pl.pallas_call(
        _swa_kernel,
        grid=(batch, KV_GROUPS, nstep),
        in_specs=[
            pl.BlockSpec(memory_space=pltpu.SMEM),
            pl.BlockSpec(memory_space=pltpu.SMEM),
            pl.BlockSpec((rows, gw), lambda b, g, c: (b * nstep + c, g)),
            _kv_block_spec(seq, k_col),
            _kv_block_spec(seq, k_col + 1),
        ],
        out_specs=pl.BlockSpec((rows, gw), lambda b, g, c: (b * nstep + c, g)),
        out_shape=jax.ShapeDtypeStruct((batch * seq, N_HEADS * HEAD_DIM), MXU_DTYPE),
        scratch_shapes=[
            pltpu.VMEM((Q_SUB, ROWS, LANES), MXU_DTYPE),
            pltpu.VMEM((GROUP, Q_BLOCK, span), F32),
            pltpu.VMEM((seq, LANES), MXU_DTYPE),
            pltpu.VMEM((seq, 2 * LANES), MXU_DTYPE),
        ],
        compiler_params=_params("parallel", "parallel", "arbitrary"),
        name="swa_attention",
    )(nslopes, sinks, qkv, qkv, qkv)


def _compress_kernel(z_ref, pe_ref, w1_ref, b1_ref, w2_ref, b2_ref, o_ref, *, ncp):
    half = CMP_STRIDE * HEAD_DIM
    z = z_ref[0, 0, 0].astype(F32)
    top = (z + pe_ref[0, 0:1, :]).astype(MXU_DTYPE)
    bot = (z + pe_ref[0, 1:2, :]).astype(MXU_DTYPE)
    a = jnp.dot(top, w1_ref[0, 0:half, :], preferred_element_type=F32)
    bm = jnp.dot(bot, w1_ref[0, half:2 * half, :], preferred_element_type=F32)
    hid = a + pltpu.roll(bm, ncp - 1, axis=0) + b1_ref[0]
    act = jax.nn.gelu(hid).astype(MXU_DTYPE)
    o_ref[0, 0, 0] = jnp.dot(act, w2_ref[0], preferred_element_type=F32) + b2_ref[0]


def compress(z, pe, w1, b1, w2, b2):
    _, batch, groups, ncp, zw = z.shape
    hid = w1.shape[-1]
    kern = functools.partial(_compress_kernel, ncp=ncp)
    return pl.pallas_call(
        kern,
        grid=(2, batch, groups),
        in_specs=[
            pl.BlockSpec((1, 1, 1, ncp, zw), lambda s, b, g: (s, b, g, 0, 0)),
            pl.BlockSpec((1, 2, zw), lambda s, b, g: (s, 0, 0)),
            pl.BlockSpec((1, 2 * zw, hid), lambda s, b, g: (s, 0, 0)),
            pl.BlockSpec((1, 1, hid), lambda s, b, g: (s, 0, 0)),
            pl.BlockSpec((1, hid, HEAD_DIM), lambda s, b, g: (s, 0, 0)),
            pl.BlockSpec((1, 1, HEAD_DIM), lambda s, b, g: (s, 0, 0)),
        ],
        out_specs=pl.BlockSpec((1, 1, 1, ncp, HEAD_DIM), lambda s, b, g: (s, b, g, 0, 0)),
        out_shape=jax.ShapeDtypeStruct((2, batch, groups, ncp, HEAD_DIM), F32),
        compiler_params=_params("parallel", "parallel", "parallel"),
        name="nsa_compress",
    )(z, pe, w1, b1, w2, b2)


def _nsa_cmp_kernel(nslope_ref, q_ref, kc_ref, vc_ref, o_ref, sel_ref, qs_ref, e_scr, r_scr, *, ncp, n_cmp, n_sel):
    grp = pl.program_id(1)
    c = pl.program_id(2)
    starts = [(c * Q_SUB + j) * Q_BLOCK for j in range(Q_SUB)]
    for j in range(Q_SUB):
        _build_q_stack(q_ref, qs_ref.at[j], j * Q_BLOCK)

    def attend(width):
        for j, t0 in enumerate(starts):
            qi = lax.broadcasted_iota(jnp.int32, (Q_BLOCK, width), 0)
            ni = lax.broadcasted_iota(jnp.int32, (Q_BLOCK, width), 1)
            d = (t0 + qi) - (ni * CMP_STRIDE + (CMP_LEN - 1))
            negb = jnp.where((d >= 0) & (ni < n_cmp), 0.0, MASK_VALUE)
            dist = d.astype(F32)
            s_all = lax.dot_general(qs_ref[j], kc_ref[0, 0, 0:width, :], (((1,), (1,)), ((), ())),
                                    preferred_element_type=F32)
            for rb in range(GROUP):
                s = s_all[_rows(rb)] + nslope_ref[_head_of_row_block(grp, rb)] * dist + negb
                m = jnp.max(s, axis=1, keepdims=True)
                e_scr[j, _rows(rb), 0:width] = jnp.exp2(s - m).astype(e_scr.dtype)
            r_scr[j] = jnp.dot(e_scr[j, :, 0:width], vc_ref[0, 0, 0:width, :], preferred_element_type=F32)

    n_chunks = ncp // LANES
    need = jnp.minimum((starts[-1] + Q_BLOCK - CMP_LEN) // CMP_STRIDE // LANES + 1, n_chunks)
    for kq in range(1, n_chunks + 1):
        pl.when(need == kq)(functools.partial(attend, kq * LANES))

    lane = lax.broadcasted_iota(jnp.int32, (Q_BLOCK, LANES), 1)
    even = lane < HEAD_DIM
    ji = lax.broadcasted_iota(jnp.int32, (LANES, Q_BLOCK), 0)
    qt = lax.broadcasted_iota(jnp.int32, (LANES, Q_BLOCK), 1)
    neg_inf = -jnp.inf
    for j, t0 in enumerate(starts):
        orow = slice(j * Q_BLOCK, (j + 1) * Q_BLOCK)
        row_t = t0 + lax.broadcasted_iota(jnp.int32, (Q_BLOCK, LANES), 0)
        has_cmp = row_t >= (CMP_LEN - 1)
        imp = jnp.zeros((Q_BLOCK, LANES), F32)
        for p in range(PAIRS):
            re = _rows(p)
            ro = _rows(PAIRS + p)
            de = jnp.maximum(r_scr[j, re, LANES:2 * LANES], 1e-30)
            do = jnp.maximum(r_scr[j, ro, LANES:2 * LANES], 1e-30)
            o = jnp.where(even, r_scr[j, re, 0:LANES] / de, r_scr[j, ro, 0:LANES] / do)
            o_ref[orow, p * LANES:(p + 1) * LANES] = jnp.where(has_cmp, o, 0.0)
            imp = imp + r_scr[j, re, 2 * LANES:3 * LANES] / de + r_scr[j, ro, 2 * LANES:3 * LANES] / do
        imp = jnp.where(has_cmp, imp, 0.0)

        imp_t = imp.T
        cur = (t0 + qt) // SEL_LEN
        causal = ji <= cur
        forced = (ji == 0) | (ji == cur) | (ji == cur - 1)
        score = jnp.where(forced, neg_inf, jnp.where(causal, imp_t, MASK_VALUE))
        score = jnp.where(ji < n_sel, score, neg_inf)
        picked = jnp.where(forced, 1.0, 0.0)
        for _ in range(SEL_TOPK - SEL_FORCED):
            mx = jnp.max(score, axis=0, keepdims=True)
            first = jnp.min(jnp.where(score == mx, ji, LANES), axis=0, keepdims=True)
            hit = ji == first
            picked = jnp.where(hit, 1.0, picked)
            score = jnp.where(hit, neg_inf, score)
        picked = jnp.where(ji < cur, picked, 0.0)
        sel_ref[0, 0, orow, :] = picked.T.astype(sel_ref.dtype)


def nsa_cmp_select(q, kc2, vc_aug, nslopes, batch, seq, q_col_block):
    nstep = seq // (Q_SUB * Q_BLOCK)
    rows = Q_SUB * Q_BLOCK
    ncp = seq // CMP_STRIDE
    n_cmp = (seq - CMP_LEN) // CMP_STRIDE + 1
    n_sel = seq // SEL_LEN
    gw = PAIRS * LANES
    kern = functools.partial(_nsa_cmp_kernel, ncp=ncp, n_cmp=n_cmp, n_sel=n_sel)
    return pl.pallas_call(
        kern,
        grid=(batch, KV_GROUPS, nstep),
        in_specs=[
            pl.BlockSpec(memory_space=pltpu.SMEM),
            pl.BlockSpec((rows, gw), lambda b, g, c: (b * nstep + c, q_col_block + g)),
            pl.BlockSpec((1, 1, ncp, LANES), lambda b, g, c: (b, g, 0, 0)),
            pl.BlockSpec((1, 1, ncp, 3 * LANES), lambda b, g, c: (b, g, 0, 0)),
        ],
        out_specs=[
            pl.BlockSpec((rows, gw), lambda b, g, c: (b * nstep + c, g)),
            pl.BlockSpec((1, 1, rows, LANES), lambda b, g, c: (b, g, c, 0)),
        ],
        out_shape=[
            jax.ShapeDtypeStruct((batch * seq, N_HEADS * HEAD_DIM), F32),
            jax.ShapeDtypeStruct((batch, KV_GROUPS, seq, LANES), MXU_DTYPE),
        ],
        scratch_shapes=[pltpu.VMEM((Q_SUB, ROWS, LANES), MXU_DTYPE), pltpu.VMEM((Q_SUB, ROWS, ncp), MXU_DTYPE),
                        pltpu.VMEM((Q_SUB, ROWS, 3 * LANES), F32)],
        compiler_params=_params("parallel", "parallel", "arbitrary"),
        name="nsa_cmp_select",
    )(nslopes, q, kc2, vc_aug)


N_FEAT = 6


def sel_query_features():
    s = jnp.asarray((-_alibi_neg_slopes().astype(np.float64) * LOG2E).astype(np.float32))
    s1 = s.astype(MXU_DTYPE).astype(F32)
    s2 = (s - s1).astype(MXU_DTYPE).astype(F32)
    s3 = (s - s1 - s2).astype(MXU_DTYPE).astype(F32)
    feat = jnp.zeros((N_HEADS, LANES), F32).at[:, HEAD_DIM:HEAD_DIM + N_FEAT].set(
        jnp.stack([s1, s2, s3, s1, s2, s3], axis=1))
    feat = feat.at[:, HEAD_DIM + N_FEAT].set(MASK_VALUE)
    order = np.array([[_head_of_row_block(g, rb) for rb in range(GROUP)] for g in range(KV_GROUPS)])
    return feat[order]


def sel_key_features(seq):
    pos = np.arange(seq)
    kk = pos % SEL_STEP
    f = np.zeros((seq + SEL_STEP, 2 * LANES), np.float32)
    f[:seq, HEAD_DIM:HEAD_DIM + 3] = (SEL_LEN * (kk // SEL_LEN))[:, None]
    f[:seq, HEAD_DIM + 3:HEAD_DIM + 6] = (kk % SEL_LEN)[:, None]
    f[seq:, HEAD_DIM + N_FEAT] = 1.0
    f[pos, LANES + pos // SEL_LEN] = 1.0
    return f


def _nsa_sel_kernel(delta_ref, qfeat_ref, q_ref, sel_ref, k_in, v_in, kfeat_ref, gate_ref, ex_ref, oc_ref, ow_ref,
                    o_ref, qa_ref, m_ref, acc_ref, s_a, s_b, p_a, p_b, al_a, al_b, k_ref, v_ref):
    grp = pl.program_id(1)
    c = pl.program_id(2)
    t0 = c * (SEL_Q * Q_BLOCK)
    seq = k_in.shape[0]
    srows = lambda j, rb: slice(j * ROWS + rb * Q_BLOCK, j * ROWS + (rb + 1) * Q_BLOCK)
    qrows = lambda j: slice(j * Q_BLOCK, (j + 1) * Q_BLOCK)

    @pl.when(c == 0)
    def _():
        def put_k(r0, y):
            k_ref[pl.ds(r0, KV_CHUNK), 0:LANES] = y
            k_ref[pl.ds(r0, KV_CHUNK), LANES:2 * LANES] = kfeat_ref[pl.ds(r0, KV_CHUNK), LANES:2 * LANES]

        def put_v(r0, y):
            v_ref[pl.ds(r0, KV_CHUNK), 0:LANES] = y
            v_ref[pl.ds(r0, KV_CHUNK), LANES:2 * LANES] = jnp.ones((KV_CHUNK, LANES), v_ref.dtype)

        _unpack_group(k_in, grp, put_k, fill_ref=kfeat_ref)
        _unpack_group(v_in, grp, put_v)
        k_ref[seq:seq + SEL_STEP, :] = kfeat_ref[seq:seq + SEL_STEP, :]
        v_ref[seq:seq + SEL_STEP, :] = jnp.zeros((SEL_STEP, 2 * LANES), v_ref.dtype)
    lane = lax.broadcasted_iota(jnp.int32, (Q_BLOCK, LANES), 1)
    low = lane < HEAD_DIM
    for j in range(SEL_Q):
        selneg = ((1.0 - sel_ref[0, 0, qrows(j), :].astype(F32)) * MASK_VALUE).astype(qa_ref.dtype)
        for p in range(PAIRS):
            qp = q_ref[qrows(j), p * LANES:(p + 1) * LANES].astype(F32) * (HEAD_DIM ** -0.5 * LOG2E)
            for rb, src in ((p, qp), (PAIRS + p, pltpu.roll(qp, HEAD_DIM, axis=1))):
                qa_ref[srows(j, rb), 0:LANES] = jnp.where(low, src, qfeat_ref[0, rb:rb + 1, :]).astype(qa_ref.dtype)
                qa_ref[srows(j, rb), LANES:2 * LANES] = selneg
    n_steps = t0 // SEL_STEP + 1
    n_pad_step = seq // SEL_STEP
    dot_nt = (((1,), (1,)), ((), ()))

    def key_start(step):
        return pl.multiple_of(jnp.clip(step, 0, n_pad_step) * SEL_STEP, SEL_STEP)

    def scores(step, s_out):
        k = k_ref[pl.ds(key_start(step), SEL_STEP), :]
        s_out[...] = lax.dot_general(qa_ref[...], k, dot_nt, preferred_element_type=F32)

    def softmax(s_in, p_out, al_out):
        for rb in range(SEL_Q * GROUP):
            m_prev = m_ref[_rows(rb), :] - delta_ref[_head_of_row_block(grp, rb % GROUP)]
            m_new = jnp.maximum(m_prev, jnp.max(s_in[_rows(rb), :], axis=1, keepdims=True))
            al_out[_rows(rb), :] = jnp.exp2(m_prev - m_new)
            m_ref[_rows(rb), :] = m_new
        for rb in range(SEL_Q * GROUP):
            p_out[_rows(rb), :] = jnp.exp2(s_in[_rows(rb), :] - _rep(m_ref[_rows(rb), :], SEL_STEP)).astype(p_out.dtype)

    def values(step, p_in, al_in):
        v = v_ref[pl.ds(key_start(step), SEL_STEP), :]
        pv = jnp.dot(p_in[...], v, preferred_element_type=F32)
        alpha = al_in[...]
        acc_ref[...] = acc_ref[...] * jnp.concatenate([alpha, alpha], axis=1) + pv

    def even_half(t):
        scores(t, s_a)
        softmax(s_b, p_b, al_b)
        values(t - 2, p_a, al_a)

    def odd_half(t):
        scores(t, s_b)
        softmax(s_a, p_a, al_a)
        values(t - 2, p_b, al_b)

    scores(0, s_a)
    qi = lax.broadcasted_iota(jnp.int32, (Q_BLOCK, Q_BLOCK), 0)
    ki = lax.broadcasted_iota(jnp.int32, (Q_BLOCK, Q_BLOCK), 1)
    own_bias = jnp.where((ki <= qi) & (ki // SEL_LEN == qi // SEL_LEN), 0.0, MASK_VALUE)
    owns = [pl.multiple_of(t0 + j * Q_BLOCK, Q_BLOCK) for j in range(SEL_Q)]
    s_owns = [lax.dot_general(qa_ref[j * ROWS:(j + 1) * ROWS, 0:LANES], k_ref[pl.ds(owns[j], Q_BLOCK), 0:LANES],
                              dot_nt, preferred_element_type=F32) for j in range(SEL_Q)]
    scores(1, s_b)
    back = jnp.full((Q_BLOCK, LANES), n_steps, jnp.int32).astype(F32)
    p_owns = []
    for j in range(SEL_Q):
        p_own = []
        for rb in range(GROUP):
            s = s_owns[j][_rows(rb)] + own_bias
            m = jnp.max(s, axis=1, keepdims=True)
            p_own.append(jnp.exp2(s - m).astype(MXU_DTYPE))
            m_ref[srows(j, rb), :] = m + back * delta_ref[_head_of_row_block(grp, rb)]
        p_owns.append(jnp.concatenate(p_own, axis=0))
    softmax(s_a, p_a, al_a)
    for j in range(SEL_Q):
        acc_ref[j * ROWS:(j + 1) * ROWS, :] = jnp.dot(p_owns[j], v_ref[pl.ds(owns[j], Q_BLOCK), :],
                                                      preferred_element_type=F32)

    def quad(j, carry):
        even_half(4 * j + 2)
        odd_half(4 * j + 3)
        even_half(4 * j + 4)
        odd_half(4 * j + 5)
        return carry

    lax.fori_loop(0, n_steps // 4, quad, 0)
    rest = 4 * (n_steps // 4) + 2

    @pl.when(n_steps % 4 >= 2)
    def _():
        even_half(rest)
        odd_half(rest + 1)

    @pl.when(n_steps % 2 == 1)
    def _():
        even_half(n_steps + 1)

    gw = PAIRS * LANES
    sig = jax.nn.sigmoid(gate_ref[...])
    hi = sig.astype(MXU_DTYPE)
    lo = (sig - hi.astype(F32)).astype(MXU_DTYPE)
    ex = ex_ref[0]
    g = jnp.dot(hi, ex, preferred_element_type=F32) + jnp.dot(lo, ex, preferred_element_type=F32)
    for j in range(SEL_Q):
        o_sel = _pairs(lambda rb: acc_ref[srows(j, rb), 0:LANES], lambda rb: acc_ref[srows(j, rb), LANES:2 * LANES])
        for p in range(PAIRS):
            cols = slice(p * LANES, (p + 1) * LANES)
            out = (g[qrows(j), cols] * oc_ref[qrows(j), cols]
                   + g[qrows(j), gw + p * LANES:gw + (p + 1) * LANES] * o_sel[p]
                   + g[qrows(j), 2 * gw + p * LANES:2 * gw + (p + 1) * LANES] * ow_ref[qrows(j), cols])
            o_ref[qrows(j), cols] = out.astype(o_ref.dtype)


def _gate_expansion():
    gw = PAIRS * LANES
    ex = np.zeros((KV_GROUPS, LANES, 3 * gw), np.float32)
    for g in range(KV_GROUPS):
        for hl in range(GROUP):
            for i in range(3):
                ex[g, 3 * (g * GROUP + hl) + i, i * gw + hl * HEAD_DIM:i * gw + (hl + 1) * HEAD_DIM] = 1.0
    return ex


def nsa_sel_attention(qkv, sel, gate, o_cmp, o_win, batch, seq, k_col, v_col):
    nstep = seq // (SEL_Q * Q_BLOCK)
    rows = SEL_Q * Q_BLOCK
    srows = SEL_Q * ROWS
    gw = PAIRS * LANES
    deltas = jnp.asarray((-_alibi_neg_slopes().astype(np.float64) * LOG2E * SEL_STEP).astype(np.float32))
    ex = jnp.asarray(_gate_expansion(), MXU_DTYPE)
    kfeat = jnp.asarray(sel_key_features(seq), MXU_DTYPE)
    blk = lambda b, g, c: (b * nstep + c, g)
    return pl.pallas_call(
        _nsa_sel_kernel,
        grid=(batch, KV_GROUPS, nstep),
        in_specs=[
            pl.BlockSpec(memory_space=pltpu.SMEM),
            pl.BlockSpec((1, GROUP, LANES), lambda b, g, c: (g, 0, 0)),
            pl.BlockSpec((rows, gw), blk),
            pl.BlockSpec((1, 1, rows, LANES), lambda b, g, c: (b, g, c, 0)),
            _kv_block_spec(seq, k_col),
            _kv_block_spec(seq, v_col),
            pl.BlockSpec((seq + SEL_STEP, 2 * LANES), lambda b, g, c: (0, 0), pipeline_mode=pl.Buffered(1)),
            pl.BlockSpec((rows, LANES), lambda b, g, c: (b * nstep + c, 0)),
            pl.BlockSpec((1, LANES, 3 * gw), lambda b, g, c: (g, 0, 0)),
            pl.BlockSpec((rows, gw), blk),
            pl.BlockSpec((rows, gw), blk),
        ],
        out_specs=pl.BlockSpec((rows, gw), blk),
        out_shape=jax.ShapeDtypeStruct((batch * seq, N_HEADS * HEAD_DIM), MXU_DTYPE),
        scratch_shapes=[
            pltpu.VMEM((srows, 2 * LANES), MXU_DTYPE),
            pltpu.VMEM((srows, LANES), F32),
            pltpu.VMEM((srows, 2 * LANES), F32),
            pltpu.VMEM((srows, SEL_STEP), F32),
            pltpu.VMEM((srows, SEL_STEP), F32),
            pltpu.VMEM((srows, SEL_STEP), MXU_DTYPE),
            pltpu.VMEM((srows, SEL_STEP), MXU_DTYPE),
            pltpu.VMEM((srows, LANES), F32),
            pltpu.VMEM((srows, LANES), F32),
            pltpu.VMEM((seq + SEL_STEP, 2 * LANES), MXU_DTYPE),
            pltpu.VMEM((seq + SEL_STEP, 2 * LANES), MXU_DTYPE),
        ],
        compiler_params=_params("parallel", "parallel", "arbitrary"),
        name="nsa_sel_attention",
    )(deltas, sel_query_features(), qkv, sel, qkv, qkv, kfeat, gate, ex, o_cmp, o_win)


def _nsa_win_kernel(nslope_ref, q_ref, k_in, v_in, o_ref, qs_ref, bias_scr, k_ref, v_ref):
    grp = pl.program_id(1)
    c = pl.program_id(2)
    span = NSA_WINDOW + Q_BLOCK
    lead = NSA_WINDOW // Q_BLOCK
    for j in range(Q_SUB):
        _build_q_stack(q_ref, qs_ref.at[j], j * Q_BLOCK)
    nslope = lambda rb: nslope_ref[_head_of_row_block(grp, rb)]

    def write(j, outs):
        for p, o in enumerate(outs):
            o_ref[j * Q_BLOCK:(j + 1) * Q_BLOCK, p * LANES:(p + 1) * LANES] = o

    @pl.when(c == 0)
    def _():
        _unpack_kv(k_in, v_in, grp, k_ref, v_ref)
        d = _pos_tiles(NSA_WINDOW, 0, span)
        for rb in range(GROUP):
            bias_scr[rb] = _band_bias(nslope(rb), d, NSA_WINDOW)

    def leading_block(j):
        d = _pos_tiles((c * Q_SUB + j) * Q_BLOCK, 0, span)
        write(j, _window_attend(qs_ref.at[j], k_ref[0:span, :], v_ref[0:span, :],
                                lambda rb: _band_bias(nslope(rb), d, NSA_WINDOW)))

    def later_block(j):
        start = pl.multiple_of((c * Q_SUB + j) * Q_BLOCK - NSA_WINDOW, Q_BLOCK)
        write(j, _window_attend(qs_ref.at[j], k_ref[pl.ds(start, span), :], v_ref[pl.ds(start, span), :],
                                lambda rb: bias_scr[rb]))

    lead_steps = max(lead // Q_SUB, 1)

    @pl.when(c < lead_steps)
    def _():
        for j in range(Q_SUB):
            if j < lead:
                leading_block(j)
            else:
                later_block(j)

    @pl.when(c >= lead_steps)
    def _():
        for j in range(Q_SUB):
            later_block(j)


def nsa_win_attention(qkv, nslopes, batch, seq, k_col, v_col):
    nstep = seq // (Q_SUB * Q_BLOCK)
    rows = Q_SUB * Q_BLOCK
    gw = PAIRS * LANES
    kd = KV_GROUPS * HEAD_DIM
    span = NSA_WINDOW + Q_BLOCK
    return pl.pallas_call(
        _nsa_win_kernel,
        grid=(batch, KV_GROUPS, nstep),
        in_specs=[
            pl.BlockSpec(memory_space=pltpu.SMEM),
            pl.BlockSpec((rows, gw), lambda b, g, c: (b * nstep + c, g)),
            _kv_block_spec(seq, k_col),
            _kv_block_spec(seq, v_col),
        ],
        out_specs=pl.BlockSpec((rows, gw), lambda b, g, c: (b * nstep + c, g)),
        out_shape=jax.ShapeDtypeStruct((batch * seq, N_HEADS * HEAD_DIM), F32),
        scratch_shapes=[
            pltpu.VMEM((Q_SUB, ROWS, LANES), MXU_DTYPE),
            pltpu.VMEM((GROUP, Q_BLOCK, span), F32),
            pltpu.VMEM((seq, LANES), MXU_DTYPE),
            pltpu.VMEM((seq, 2 * LANES), MXU_DTYPE),
        ],
        compiler_params=_params("parallel", "parallel", "arbitrary"),
        name="nsa_win_attention",
    )(nslopes, qkv, qkv, qkv)


def _overlap_matrix(seq):
    ncp = seq // CMP_STRIDE
    n_cmp = (seq - CMP_LEN) // CMP_STRIDE + 1
    cs = np.arange(n_cmp) * CMP_STRIDE
    ss = np.arange(seq // SEL_LEN) * SEL_LEN
    ov = (cs[:, None] < ss[None, :] + SEL_LEN) & (cs[:, None] + CMP_LEN > ss[None, :])
    out = np.zeros((ncp, LANES), np.float32)
    out[:n_cmp, :seq // SEL_LEN] = ov
    return out


def _swa_layer(h, x, w_in, b_in, sinks, w_o, b_o, g_post, g_next, nslopes, batch, seq):
    hd = N_HEADS * HEAD_DIM
    kd = KV_GROUPS * HEAD_DIM
    qkv = matmul_bias(h, w_in.astype(MXU_DTYPE), b_in, MXU_DTYPE, name="swa_in_proj")
    o = swa_attention(qkv, nslopes, sinks.astype(F32) * LOG2E, batch, seq)
    return matmul_norm_res(o, w_o.astype(MXU_DTYPE), b_o, g_post, g_next, x, tn=w_o.shape[1], name="swa_out_proj")


def _nsa_layer(h, x, w_in, cmp_pe, cmp_w1, cmp_b1, cmp_w2, cmp_b2, w_o, g_post, g_next, nslopes, batch, seq):
    hd = N_HEADS * HEAD_DIM
    kd = KV_GROUPS * HEAD_DIM
    t = batch * seq
    ncp = seq // CMP_STRIDE
    qkv = matmul_bias(h, w_in[:, :hd + 6 * kd].astype(MXU_DTYPE), jnp.zeros((hd + 6 * kd,), F32), MXU_DTYPE,
                      name="nsa_in_proj")
    n_gate = 3 * N_HEADS
    w_gate = jnp.pad(w_in[:, hd + 6 * kd:], ((0, 0), (0, LANES - n_gate))).astype(MXU_DTYPE)
    gate = matmul_bias(h, w_gate, jnp.zeros((LANES,), F32), F32, name="nsa_gate_proj")

    def kv(i):
        return qkv[:, hd + i * kd:hd + (i + 1) * kd]

    def slabs(a):
        return a.reshape(batch, seq, KV_GROUPS, HEAD_DIM).transpose(0, 2, 1, 3).reshape(
            batch, KV_GROUPS, ncp, CMP_STRIDE * HEAD_DIM)

    z = jnp.stack([slabs(kv(0)), slabs(kv(1))])
    half = CMP_STRIDE * HEAD_DIM
    cmp_out = compress(z, cmp_pe.reshape(2, 2, half).astype(F32), cmp_w1.astype(MXU_DTYPE),
                       cmp_b1.reshape(2, 1, -1), cmp_w2.astype(MXU_DTYPE), cmp_b2.reshape(2, 1, -1))
    kcm = cmp_out[0].astype(MXU_DTYPE)
    vcm = cmp_out[1].astype(MXU_DTYPE)
    kc2 = jnp.concatenate([kcm, kcm], axis=-1)
    ov = jnp.broadcast_to(jnp.asarray(_overlap_matrix(seq), MXU_DTYPE), (batch, KV_GROUPS, ncp, LANES))
    vc_aug = jnp.concatenate([vcm, vcm, jnp.ones((batch, KV_GROUPS, ncp, LANES), MXU_DTYPE), ov], axis=-1)

    o_cmp, sel = nsa_cmp_select(qkv, kc2, vc_aug, nslopes, batch, seq, 0)
    kv_col = lambda i: hd // kd + i
    o_win = nsa_win_attention(qkv, nslopes, batch, seq, kv_col(4), kv_col(5))
    o = nsa_sel_attention(qkv, sel, gate, o_cmp, o_win, batch, seq, kv_col(2), kv_col(3))
    return matmul_norm_res(o, w_o.astype(MXU_DTYPE), jnp.zeros((w_o.shape[1],), F32), g_post, g_next, x,
                           tn=w_o.shape[1], name="nsa_out_proj")


def kernel(x, norm_g, swa_w_in, swa_b_in, swa_sinks, swa_w_o, swa_b_o, nsa_w_in, nsa_cmp_pe, nsa_cmp_w1, nsa_cmp_b1, nsa_cmp_w2, nsa_cmp_b2, nsa_w_o, ffn_w_gate, ffn_w_up, ffn_conv_w, ffn_conv_b, ffn_w_down):
    batch, seq, d = x.shape
    depth = norm_g.shape[0]
    nslopes = jnp.asarray((_alibi_neg_slopes().astype(np.float64) * LOG2E).astype(np.float32))
    xf = x.reshape(batch * seq, d)
    h = rms_cast(xf, norm_g[0, 0])
    for i in range(depth):
        g = norm_g[i]
        j = i // 2
        if i % 2 == 0:
            xf, h = _swa_layer(h, xf, swa_w_in[j], swa_b_in[j], swa_sinks[j], swa_w_o[j], swa_b_o[j],
                               g[1], g[2], nslopes, batch, seq)
        else:
            xf, h = _nsa_layer(h, xf, nsa_w_in[j], nsa_cmp_pe[j], nsa_cmp_w1[j], nsa_cmp_b1[j], nsa_cmp_w2[j],
                               nsa_cmp_b2[j], nsa_w_o[j], g[1], g[2], nslopes, batch, seq)
        act = ffn_up(h, ffn_w_gate, ffn_w_up, i, ffn_conv_w[i], ffn_conv_b[i], seq)
        g_next = norm_g[i + 1, 0] if i + 1 < depth else jnp.ones((d,), F32)
        xf, h = matmul_norm_res(act, ffn_w_down[i].astype(MXU_DTYPE), jnp.zeros((d,), F32), g[3], g_next, xf,
                                tn=512, name="ffn_down")
    return xf.reshape(batch, seq, d)
```

```python
import functools

import numpy as np
import jax
import jax.numpy as jnp
from jax import lax
from jax.experimental import pallas as pl
from jax.experimental.pallas import tpu as pltpu

F32 = jnp.float32
MXU_DTYPE = jnp.bfloat16

N_HEADS = 32
HEAD_DIM = 64
KV_GROUPS = 4
GROUP = N_HEADS // KV_GROUPS
PAIRS = GROUP // 2
LANES = 128
Q_BLOCK = 128
ROWS = GROUP * Q_BLOCK
SWA_WINDOW = 128
CMP_LEN = 32
CMP_STRIDE = 16
SEL_LEN = 64
SEL_TOPK = 16
SEL_FORCED = 3
LOG2E = 1.4426950408889634
SEL_STEP = 512
SEL_Q = 2
NSA_WINDOW = 512
CONV_WIDTH = 3
RMS_EPS = 1e-6
MASK_VALUE = -1e30
VMEM_LIMIT = 60000 * 1024


def _params(*sem):
    return pltpu.CompilerParams(dimension_semantics=sem, vmem_limit_bytes=VMEM_LIMIT)


def _alibi_neg_slopes():
    return (-np.exp2(-8.0 * np.arange(1, N_HEADS + 1, dtype=np.float64) / N_HEADS)).astype(np.float32)


def _rms_cast_kernel(x_ref, g_ref, o_ref):
    x = x_ref[...]
    ms = jnp.mean(x * x, axis=-1, keepdims=True)
    o_ref[...] = (x * lax.rsqrt(ms + RMS_EPS) * g_ref[...]).astype(o_ref.dtype)


def rms_cast(x, g, tm=512):
    t, d = x.shape
    return pl.pallas_call(
        _rms_cast_kernel,
        grid=(t // tm,),
        in_specs=[pl.BlockSpec((tm, d), lambda i: (i, 0)), pl.BlockSpec((1, d), lambda i: (0, 0))],
        out_specs=pl.BlockSpec((tm, d), lambda i: (i, 0)),
        out_shape=jax.ShapeDtypeStruct((t, d), MXU_DTYPE),
        compiler_params=_params("parallel"),
        name="rms_cast",
    )(x, g.reshape(1, d))


def _matmul_kernel(a_ref, w_ref, b_ref, o_ref):
    acc = jnp.dot(a_ref[...], w_ref[...], preferred_element_type=F32)
    o_ref[...] = (acc + b_ref[...]).astype(o_ref.dtype)


def matmul_bias(a, w, b, out_dtype, tm=2048, tn=512, name="matmul_bias"):
    t, k = a.shape
    n = w.shape[1]
    tn = min(tn, n)
    return pl.pallas_call(
        _matmul_kernel,
        grid=(t // tm, n // tn),
        in_specs=[
            pl.BlockSpec((tm, k), lambda i, j: (i, 0)),
            pl.BlockSpec((k, tn), lambda i, j: (0, j)),
            pl.BlockSpec((1, tn), lambda i, j: (0, j)),
        ],
        out_specs=pl.BlockSpec((tm, tn), lambda i, j: (i, j)),
        out_shape=jax.ShapeDtypeStruct((t, n), out_dtype),
        compiler_params=_params("parallel", "arbitrary"),
        name=name,
    )(a, w, b.reshape(1, n))


def _mm_norm_res_kernel(a_ref, w_ref, b_ref, gpost_ref, gnext_ref, x_ref, xo_ref, ho_ref, y_scr, *, nj, tn, n):
    j = pl.program_id(1)
    y_scr[j] = jnp.dot(a_ref[...], w_ref[...], preferred_element_type=F32) + b_ref[...]

    @pl.when(j == nj - 1)
    def _():
        ss = jnp.sum(y_scr[0] * y_scr[0], axis=1, keepdims=True)
        for jj in range(1, nj):
            ss = ss + jnp.sum(y_scr[jj] * y_scr[jj], axis=1, keepdims=True)
        r = lax.rsqrt(ss / n + RMS_EPS)
        ss2 = jnp.zeros_like(ss)
        for jj in range(nj):
            cols = slice(jj * tn, (jj + 1) * tn)
            xn = x_ref[:, cols] + y_scr[jj] * r * gpost_ref[:, cols]
            xo_ref[:, cols] = xn
            ss2 = ss2 + jnp.sum(xn * xn, axis=1, keepdims=True)
        r2 = lax.rsqrt(ss2 / n + RMS_EPS)
        for jj in range(nj):
            cols = slice(jj * tn, (jj + 1) * tn)
            ho_ref[:, cols] = (xo_ref[:, cols] * r2 * gnext_ref[:, cols]).astype(ho_ref.dtype)


def matmul_norm_res(a, w, b, g_post, g_next, x, tm=512, tn=512, name="matmul_norm_res"):
    t, k = a.shape
    n = w.shape[1]
    nj = n // tn
    kern = functools.partial(_mm_norm_res_kernel, nj=nj, tn=tn, n=n)
    return pl.pallas_call(
        kern,
        grid=(t // tm, nj),
        in_specs=[
            pl.BlockSpec((tm, k), lambda i, j: (i, 0)),
            pl.BlockSpec((k, tn), lambda i, j: (0, j)),
            pl.BlockSpec((1, tn), lambda i, j: (0, j)),
            pl.BlockSpec((1, n), lambda i, j: (0, 0)),
            pl.BlockSpec((1, n), lambda i, j: (0, 0)),
            pl.BlockSpec((tm, n), lambda i, j: (i, 0)),
        ],
        out_specs=[
            pl.BlockSpec((tm, n), lambda i, j: (i, 0)),
            pl.BlockSpec((tm, n), lambda i, j: (i, 0)),
        ],
        out_shape=[jax.ShapeDtypeStruct((t, n), F32), jax.ShapeDtypeStruct((t, n), MXU_DTYPE)],
        scratch_shapes=[pltpu.VMEM((nj, tm, tn), F32)],
        compiler_params=_params("parallel", "arbitrary"),
        name=name,
    )(a, w, b.reshape(1, n), g_post.reshape(1, n), g_next.reshape(1, n), x)


FIX_ROWS = 16
CARRY_ROWS = 8


def _ffn_up_kernel(h_ref, wg_ref, wu_ref, cw_ref, cb_ref, o_ref, wg_scr, wu_scr, carry_scr, *, tm, tiles_per_seq):
    i = pl.program_id(1)

    @pl.when(i == 0)
    def _():
        wg_scr[...] = wg_ref[...].astype(wg_scr.dtype)
        wu_scr[...] = wu_ref[...].astype(wu_scr.dtype)

    h = h_ref[...]
    gate = jnp.dot(h, wg_scr[...], preferred_element_type=F32)
    up = jnp.dot(h, wu_scr[...], preferred_element_type=F32)
    w0 = cw_ref[0:1, :]
    w1 = cw_ref[1:2, :]
    w2 = cw_ref[2:3, :]
    b = cb_ref[...]
    a = b + pltpu.roll(gate, 2, axis=0) * w0
    a = a + pltpu.roll(gate, 1, axis=0) * w1
    a = a + gate * w2
    o_ref[...] = (jax.nn.silu(a) * up).astype(o_ref.dtype)

    seq_start = (i % tiles_per_seq) == 0
    prev = jnp.where(seq_start, 0.0, carry_scr[...])
    head = gate[0:FIX_ROWS]
    ext = jnp.concatenate([prev, head], axis=0)
    af = b + ext[CARRY_ROWS - 2:CARRY_ROWS - 2 + FIX_ROWS] * w0
    af = af + ext[CARRY_ROWS - 1:CARRY_ROWS - 1 + FIX_ROWS] * w1
    af = af + head * w2
    o_ref[0:FIX_ROWS, :] = (jax.nn.silu(af) * up[0:FIX_ROWS]).astype(o_ref.dtype)
    carry_scr[...] = gate[tm - CARRY_ROWS:tm]


def ffn_up(h, wg, wu, layer, conv_w, conv_b, seq, tm=1024, tn=512):
    t, k = h.shape
    n = wg.shape[2]
    kern = functools.partial(_ffn_up_kernel, tm=tm, tiles_per_seq=seq // tm)
    return pl.pallas_call(
        kern,
        grid=(n // tn, t // tm),
        in_specs=[
            pl.BlockSpec((tm, k), lambda j, i: (i, 0)),
            pl.BlockSpec((None, k, tn), lambda j, i: (layer, 0, j)),
            pl.BlockSpec((None, k, tn), lambda j, i: (layer, 0, j)),
            pl.BlockSpec((CONV_WIDTH, tn), lambda j, i: (0, j)),
            pl.BlockSpec((1, tn), lambda j, i: (0, j)),
        ],
        out_specs=pl.BlockSpec((tm, tn), lambda j, i: (i, j)),
        out_shape=jax.ShapeDtypeStruct((t, n), MXU_DTYPE),
        scratch_shapes=[
            pltpu.VMEM((k, tn), MXU_DTYPE),
            pltpu.VMEM((k, tn), MXU_DTYPE),
            pltpu.VMEM((CARRY_ROWS, tn), F32),
        ],
        compiler_params=_params("arbitrary", "arbitrary"),
        name="ffn_up",
    )(h, wg, wu, conv_w, conv_b.reshape(1, n))


def _head_of_row_block(group, rb):
    return group * GROUP + 2 * (rb % PAIRS) + rb // PAIRS


Q_SUB = 8


def _build_q_stack(q_ref, qs_ref, row0=0):
    lane = lax.broadcasted_iota(jnp.int32, (Q_BLOCK, LANES), 1)
    even = lane < HEAD_DIM
    for p in range(PAIRS):
        qp = q_ref[row0:row0 + Q_BLOCK, p * LANES:(p + 1) * LANES].astype(F32) * (HEAD_DIM ** -0.5 * LOG2E)
        qs_ref[p * Q_BLOCK:(p + 1) * Q_BLOCK, :] = jnp.where(even, qp, 0.0).astype(qs_ref.dtype)
        qs_ref[(PAIRS + p) * Q_BLOCK:(PAIRS + p + 1) * Q_BLOCK, :] = jnp.where(even, 0.0, qp).astype(qs_ref.dtype)


def _rep(x, size):
    return x if size == LANES else jnp.concatenate([x] * (size // LANES), axis=1)


def _pairs(num, den):
    lane = lax.broadcasted_iota(jnp.int32, (Q_BLOCK, LANES), 1)
    even = lane < HEAD_DIM
    outs = []
    for p in range(PAIRS):
        oe = num(p) / jnp.maximum(den(p), 1e-30)
        oo = num(PAIRS + p) / jnp.maximum(den(PAIRS + p), 1e-30)
        outs.append(jnp.where(even, oe, oo))
    return outs


def _rows(rb):
    return slice(rb * Q_BLOCK, (rb + 1) * Q_BLOCK)


def _band_bias(nslope, d, window):
    return jnp.where((d >= 0) & (d < window), nslope * d.astype(F32), MASK_VALUE)


def _window_attend(qs_ref, k, v, bias, extra_logit=None):
    half = ROWS // 2
    dot_nt = (((1,), (1,)), ((), ()))
    s_halves = [lax.dot_general(qs_ref[hh * half:(hh + 1) * half, :], k, dot_nt, preferred_element_type=F32)
                for hh in range(2)]
    ps, extras = [], []
    for rb in range(GROUP):
        lo = (rb % PAIRS) * Q_BLOCK
        s = s_halves[rb // PAIRS][lo:lo + Q_BLOCK] + bias(rb)
        m = jnp.max(s, axis=1, keepdims=True)
        if extra_logit is not None:
            m = jnp.maximum(m, extra_logit(rb))
            extras.append(jnp.exp2(extra_logit(rb) - m))
        ps.append(jnp.exp2(s - m).astype(MXU_DTYPE))
    r_halves = [jnp.dot(jnp.concatenate(ps[hh * PAIRS:(hh + 1) * PAIRS], axis=0), v, preferred_element_type=F32)
                for hh in range(2)]

    def part(rb, cols):
        lo = (rb % PAIRS) * Q_BLOCK
        return r_halves[rb // PAIRS][lo:lo + Q_BLOCK, cols]

    num = lambda rb: part(rb, slice(0, LANES))
    if extra_logit is None:
        den = lambda rb: part(rb, slice(LANES, 2 * LANES))
    else:
        den = lambda rb: part(rb, slice(LANES, 2 * LANES)) + extras[rb]
    return _pairs(num, den)


KV_CHUNK = 1024


def _unpack_group(kv_ref, grp, write_chunk, fill_ref=None):
    n = kv_ref.shape[0]
    for odd in range(2):
        def fill(odd=odd):
            def body(i, carry):
                r0 = pl.multiple_of(i * KV_CHUNK, KV_CHUNK)
                x = kv_ref[pl.ds(r0, KV_CHUNK), :]
                xi = pltpu.bitcast(x, jnp.int32)
                xr = pltpu.roll(xi, HEAD_DIM, axis=1)
                low = lax.broadcasted_iota(jnp.int32, xi.shape, 1) < HEAD_DIM
                if fill_ref is None:
                    y = jnp.where(low, xr, xi) if odd else jnp.where(low, xi, xr)
                else:
                    f = pltpu.bitcast(fill_ref[pl.ds(r0, KV_CHUNK), 0:LANES], jnp.int32)
                    y = jnp.where(low, xr if odd else xi, f)
                write_chunk(r0, pltpu.bitcast(y, x.dtype))
                return carry

            lax.fori_loop(0, n // KV_CHUNK, body, 0)

        pl.when(grp % 2 == odd)(fill)


def _kv_block_spec(seq, col):
    pairs_per_tensor = KV_GROUPS * HEAD_DIM // LANES
    return pl.BlockSpec((seq, LANES), lambda b, g, c: (b, pairs_per_tensor * col + g // 2))


def _pos_tiles(t0, start, size):
    qi = lax.broadcasted_iota(jnp.int32, (Q_BLOCK, size), 0)
    ki = lax.broadcasted_iota(jnp.int32, (Q_BLOCK, size), 1)
    return (t0 - start) + (qi - ki)


def _unpack_kv(k_in, v_in, grp, k_ref, v_ref):
    def put_k(r0, y):
        k_ref[pl.ds(r0, KV_CHUNK), :] = y

    def put_v(r0, y):
        v_ref[pl.ds(r0, KV_CHUNK), 0:LANES] = y
        v_ref[pl.ds(r0, KV_CHUNK), LANES:2 * LANES] = jnp.ones((KV_CHUNK, LANES), v_ref.dtype)

    _unpack_group(k_in, grp, put_k)
    _unpack_group(v_in, grp, put_v)


def _swa_kernel(nslope_ref, sink_ref, q_ref, k_in, v_in, o_ref, qs_ref, bias_scr, k_ref, v_ref):
    grp = pl.program_id(1)
    c = pl.program_id(2)
    span = SWA_WINDOW + Q_BLOCK
    for j in range(Q_SUB):
        _build_q_stack(q_ref, qs_ref.at[j], j * Q_BLOCK)
    sink = lambda rb: sink_ref[_head_of_row_block(grp, rb)]

    def write(j, outs):
        for p, o in enumerate(outs):
            o_ref[j * Q_BLOCK:(j + 1) * Q_BLOCK, p * LANES:(p + 1) * LANES] = o.astype(o_ref.dtype)

    def later_block(j):
        start = pl.multiple_of((c * Q_SUB + j) * Q_BLOCK - SWA_WINDOW, Q_BLOCK)
        write(j, _window_attend(qs_ref.at[j], k_ref[pl.ds(start, span), :], v_ref[pl.ds(start, span), :],
                                lambda rb: bias_scr[rb], sink))

    @pl.when(c == 0)
    def _():
        _unpack_kv(k_in, v_in, grp, k_ref, v_ref)
        d = _pos_tiles(SWA_WINDOW, 0, span)
        for rb in range(GROUP):
            bias_scr[rb] = _band_bias(nslope_ref[_head_of_row_block(grp, rb)], d, SWA_WINDOW)

    @pl.when(c == 0)
    def _():
        write(0, _window_attend(qs_ref.at[0], k_ref[0:Q_BLOCK, :], v_ref[0:Q_BLOCK, :],
                                lambda rb: bias_scr[rb, :, SWA_WINDOW:span], sink))
        for j in range(1, Q_SUB):
            later_block(j)

    @pl.when(c > 0)
    def _():
        for j in range(Q_SUB):
            later_block(j)


def swa_attention(qkv, nslopes, sinks, batch, seq):
    nstep = seq // (Q_SUB * Q_BLOCK)
    rows = Q_SUB * Q_BLOCK
    gw = PAIRS * LANES
    kd = KV_GROUPS * HEAD_DIM
    span = SWA_WINDOW + Q_BLOCK
    k_col = N_HEADS * HEAD_DIM // kd
    return pl.pallas_call(
        _swa_kernel,
        grid=(batch, KV_GROUPS, nstep),
        in_specs=[
            pl.BlockSpec(memory_space=pltpu.SMEM),
            pl.BlockSpec(memory_space=pltpu.SMEM),
            pl.BlockSpec((rows, gw), lambda b, g, c: (b * nstep + c, g)),
            _kv_block_spec(seq, k_col),
            _kv_block_spec(seq, k_col + 1),
        ],
        out_specs=pl.BlockSpec((rows, gw), lambda b, g, c: (b * nstep + c, g)),
        out_shape=jax.ShapeDtypeStruct((batch * seq, N_HEADS * HEAD_DIM), MXU_DTYPE),
        scratch_shapes=[
            pltpu.VMEM((Q_SUB, ROWS, LANES), MXU_DTYPE),
            pltpu.VMEM((GROUP, Q_BLOCK, span), F32),
            pltpu.VMEM((seq, LANES), MXU_DTYPE),
            pltpu.VMEM((seq, 2 * LANES), MXU_DTYPE),
        ],
        compiler_params=_params("parallel", "parallel", "arbitrary"),
        name="swa_attention",
    )(nslopes, sinks, qkv, qkv, qkv)


def _compress_kernel(z_ref, pe_ref, w1_ref, b1_ref, w2_ref, b2_ref, o_ref, *, ncp):
    half = CMP_STRIDE * HEAD_DIM
    z = z_ref[0, 0, 0].astype(F32)
    top = (z + pe_ref[0, 0:1, :]).astype(MXU_DTYPE)
    bot = (z + pe_ref[0, 1:2, :]).astype(MXU_DTYPE)
    a = jnp.dot(top, w1_ref[0, 0:half, :], preferred_element_type=F32)
    bm = jnp.dot(bot, w1_ref[0, half:2 * half, :], preferred_element_type=F32)
    hid = a + pltpu.roll(bm, ncp - 1, axis=0) + b1_ref[0]
    act = jax.nn.gelu(hid).astype(MXU_DTYPE)
    o_ref[0, 0, 0] = jnp.dot(act, w2_ref[0], preferred_element_type=F32) + b2_ref[0]


def compress(z, pe, w1, b1, w2, b2):
    _, batch, groups, ncp, zw = z.shape
    hid = w1.shape[-1]
    kern = functools.partial(_compress_kernel, ncp=ncp)
    return pl.pallas_call(
        kern,
        grid=(2, batch, groups),
        in_specs=[
            pl.BlockSpec((1, 1, 1, ncp, zw), lambda s, b, g: (s, b, g, 0, 0)),
            pl.BlockSpec((1, 2, zw), lambda s, b, g: (s, 0, 0)),
            pl.BlockSpec((1, 2 * zw, hid), lambda s, b, g: (s, 0, 0)),
            pl.BlockSpec((1, 1, hid), lambda s, b, g: (s, 0, 0)),
            pl.BlockSpec((1, hid, HEAD_DIM), lambda s, b, g: (s, 0, 0)),
            pl.BlockSpec((1, 1, HEAD_DIM), lambda s, b, g: (s, 0, 0)),
        ],
        out_specs=pl.BlockSpec((1, 1, 1, ncp, HEAD_DIM), lambda s, b, g: (s, b, g, 0, 0)),
        out_shape=jax.ShapeDtypeStruct((2, batch, groups, ncp, HEAD_DIM), F32),
        compiler_params=_params("parallel", "parallel", "parallel"),
        name="nsa_compress",
    )(z, pe, w1, b1, w2, b2)


def _nsa_cmp_kernel(nslope_ref, q_ref, kc_ref, vc_ref, o_ref, sel_ref, qs_ref, e_scr, r_scr, *, ncp, n_cmp, n_sel):
    grp = pl.program_id(1)
    c = pl.program_id(2)
    starts = [(c * Q_SUB + j) * Q_BLOCK for j in range(Q_SUB)]
    for j in range(Q_SUB):
        _build_q_stack(q_ref, qs_ref.at[j], j * Q_BLOCK)

    def attend(width):
        for j, t0 in enumerate(starts):
            qi = lax.broadcasted_iota(jnp.int32, (Q_BLOCK, width), 0)
            ni = lax.broadcasted_iota(jnp.int32, (Q_BLOCK, width), 1)
            d = (t0 + qi) - (ni * CMP_STRIDE + (CMP_LEN - 1))
            negb = jnp.where((d >= 0) & (ni < n_cmp), 0.0, MASK_VALUE)
            dist = d.astype(F32)
            s_all = lax.dot_general(qs_ref[j], kc_ref[0, 0, 0:width, :], (((1,), (1,)), ((), ())),
                                    preferred_element_type=F32)
            for rb in range(GROUP):
                s = s_all[_rows(rb)] + nslope_ref[_head_of_row_block(grp, rb)] * dist + negb
                m = jnp.max(s, axis=1, keepdims=True)
                e_scr[j, _rows(rb), 0:width] = jnp.exp2(s - m).astype(e_scr.dtype)
            r_scr[j] = jnp.dot(e_scr[j, :, 0:width], vc_ref[0, 0, 0:width, :], preferred_element_type=F32)

    n_chunks = ncp // LANES
    need = jnp.minimum((starts[-1] + Q_BLOCK - CMP_LEN) // CMP_STRIDE // LANES + 1, n_chunks)
    for kq in range(1, n_chunks + 1):
        pl.when(need == kq)(functools.partial(attend, kq * LANES))

    lane = lax.broadcasted_iota(jnp.int32, (Q_BLOCK, LANES), 1)
    even = lane < HEAD_DIM
    ji = lax.broadcasted_iota(jnp.int32, (LANES, Q_BLOCK), 0)
    qt = lax.broadcasted_iota(jnp.int32, (LANES, Q_BLOCK), 1)
    neg_inf = -jnp.inf
    for j, t0 in enumerate(starts):
        orow = slice(j * Q_BLOCK, (j + 1) * Q_BLOCK)
        row_t = t0 + lax.broadcasted_iota(jnp.int32, (Q_BLOCK, LANES), 0)
        has_cmp = row_t >= (CMP_LEN - 1)
        imp = jnp.zeros((Q_BLOCK, LANES), F32)
        for p in range(PAIRS):
            re = _rows(p)
            ro = _rows(PAIRS + p)
            de = jnp.maximum(r_scr[j, re, LANES:2 * LANES], 1e-30)
            do = jnp.maximum(r_scr[j, ro, LANES:2 * LANES], 1e-30)
            o = jnp.where(even, r_scr[j, re, 0:LANES] / de, r_scr[j, ro, 0:LANES] / do)
            o_ref[orow, p * LANES:(p + 1) * LANES] = jnp.where(has_cmp, o, 0.0)
            imp = imp + r_scr[j, re, 2 * LANES:3 * LANES] / de + r_scr[j, ro, 2 * LANES:3 * LANES] / do
        imp = jnp.where(has_cmp, imp, 0.0)

        imp_t = imp.T
        cur = (t0 + qt) // SEL_LEN
        causal = ji <= cur
        forced = (ji == 0) | (ji == cur) | (ji == cur - 1)
        score = jnp.where(forced, neg_inf, jnp.where(causal, imp_t, MASK_VALUE))
        score = jnp.where(ji < n_sel, score, neg_inf)
        picked = jnp.where(forced, 1.0, 0.0)
        for _ in range(SEL_TOPK - SEL_FORCED):
            mx = jnp.max(score, axis=0, keepdims=True)
            first = jnp.min(jnp.where(score == mx, ji, LANES), axis=0, keepdims=True)
            hit = ji == first
            picked = jnp.where(hit, 1.0, picked)
            score = jnp.where(hit, neg_inf, score)
        picked = jnp.where(ji < cur, picked, 0.0)
        sel_ref[0, 0, orow, :] = picked.T.astype(sel_ref.dtype)


def nsa_cmp_select(q, kc2, vc_aug, nslopes, batch, seq, q_col_block):
    nstep = seq // (Q_SUB * Q_BLOCK)
    rows = Q_SUB * Q_BLOCK
    ncp = seq // CMP_STRIDE
    n_cmp = (seq - CMP_LEN) // CMP_STRIDE + 1
    n_sel = seq // SEL_LEN
    gw = PAIRS * LANES
    kern = functools.partial(_nsa_cmp_kernel, ncp=ncp, n_cmp=n_cmp, n_sel=n_sel)
    return pl.pallas_call(
        kern,
        grid=(batch, KV_GROUPS, nstep),
        in_specs=[
            pl.BlockSpec(memory_space=pltpu.SMEM),
            pl.BlockSpec((rows, gw), lambda b, g, c: (b * nstep + c, q_col_block + g)),
            pl.BlockSpec((1, 1, ncp, LANES), lambda b, g, c: (b, g, 0, 0)),
            pl.BlockSpec((1, 1, ncp, 3 * LANES), lambda b, g, c: (b, g, 0, 0)),
        ],
        out_specs=[
            pl.BlockSpec((rows, gw), lambda b, g, c: (b * nstep + c, g)),
            pl.BlockSpec((1, 1, rows, LANES), lambda b, g, c: (b, g, c, 0)),
        ],
        out_shape=[
            jax.ShapeDtypeStruct((batch * seq, N_HEADS * HEAD_DIM), F32),
            jax.ShapeDtypeStruct((batch, KV_GROUPS, seq, LANES), MXU_DTYPE),
        ],
        scratch_shapes=[pltpu.VMEM((Q_SUB, ROWS, LANES), MXU_DTYPE), pltpu.VMEM((Q_SUB, ROWS, ncp), MXU_DTYPE),
                        pltpu.VMEM((Q_SUB, ROWS, 3 * LANES), F32)],
        compiler_params=_params("parallel", "parallel", "arbitrary"),
        name="nsa_cmp_select",
    )(nslopes, q, kc2, vc_aug)


N_FEAT = 6


def sel_query_features():
    s = jnp.asarray((-_alibi_neg_slopes().astype(np.float64) * LOG2E).astype(np.float32))
    s1 = s.astype(MXU_DTYPE).astype(F32)
    s2 = (s - s1).astype(MXU_DTYPE).astype(F32)
    s3 = (s - s1 - s2).astype(MXU_DTYPE).astype(F32)
    feat = jnp.zeros((N_HEADS, LANES), F32).at[:, HEAD_DIM:HEAD_DIM + N_FEAT].set(
        jnp.stack([s1, s2, s3, s1, s2, s3], axis=1))
    feat = feat.at[:, HEAD_DIM + N_FEAT].set(MASK_VALUE)
    order = np.array([[_head_of_row_block(g, rb) for rb in range(GROUP)] for g in range(KV_GROUPS)])
    return feat[order]


def sel_key_features(seq):
    pos = np.arange(seq)
    kk = pos % SEL_STEP
    f = np.zeros((seq + SEL_STEP, 2 * LANES), np.float32)
    f[:seq, HEAD_DIM:HEAD_DIM + 3] = (SEL_LEN * (kk // SEL_LEN))[:, None]
    f[:seq, HEAD_DIM + 3:HEAD_DIM + 6] = (kk % SEL_LEN)[:, None]
    f[seq:, HEAD_DIM + N_FEAT] = 1.0
    f[pos, LANES + pos // SEL_LEN] = 1.0
    return f


def _nsa_sel_kernel(delta_ref, qfeat_ref, q_ref, sel_ref, k_in, v_in, kfeat_ref, gate_ref, ex_ref, oc_ref, ow_ref,
                    o_ref, qa_ref, m_ref, acc_ref, s_a, s_b, p_a, p_b, al_a, al_b, k_ref, v_ref):
    grp = pl.program_id(1)
    c = pl.program_id(2)
    t0 = c * (SEL_Q * Q_BLOCK)
    seq = k_in.shape[0]
    srows = lambda j, rb: slice(j * ROWS + rb * Q_BLOCK, j * ROWS + (rb + 1) * Q_BLOCK)
    qrows = lambda j: slice(j * Q_BLOCK, (j + 1) * Q_BLOCK)

    @pl.when(c == 0)
    def _():
        def put_k(r0, y):
            k_ref[pl.ds(r0, KV_CHUNK), 0:LANES] = y
            k_ref[pl.ds(r0, KV_CHUNK), LANES:2 * LANES] = kfeat_ref[pl.ds(r0, KV_CHUNK), LANES:2 * LANES]

        def put_v(r0, y):
            v_ref[pl.ds(r0, KV_CHUNK), 0:LANES] = y
            v_ref[pl.ds(r0, KV_CHUNK), LANES:2 * LANES] = jnp.ones((KV_CHUNK, LANES), v_ref.dtype)

        _unpack_group(k_in, grp, put_k, fill_ref=kfeat_ref)
        _unpack_group(v_in, grp, put_v)
        k_ref[seq:seq + SEL_STEP, :] = kfeat_ref[seq:seq + SEL_STEP, :]
        v_ref[seq:seq + SEL_STEP, :] = jnp.zeros((SEL_STEP, 2 * LANES), v_ref.dtype)
    lane = lax.broadcasted_iota(jnp.int32, (Q_BLOCK, LANES), 1)
    low = lane < HEAD_DIM
    for j in range(SEL_Q):
        selneg = ((1.0 - sel_ref[0, 0, qrows(j), :].astype(F32)) * MASK_VALUE).astype(qa_ref.dtype)
        for p in range(PAIRS):
            qp = q_ref[qrows(j), p * LANES:(p + 1) * LANES].astype(F32) * (HEAD_DIM ** -0.5 * LOG2E)
            for rb, src in ((p, qp), (PAIRS + p, pltpu.roll(qp, HEAD_DIM, axis=1))):
                qa_ref[srows(j, rb), 0:LANES] = jnp.where(low, src, qfeat_ref[0, rb:rb + 1, :]).astype(qa_ref.dtype)
                qa_ref[srows(j, rb), LANES:2 * LANES] = selneg
    n_steps = t0 // SEL_STEP + 1
    n_pad_step = seq // SEL_STEP
    dot_nt = (((1,), (1,)), ((), ()))

    def key_start(step):
        return pl.multiple_of(jnp.clip(step, 0, n_pad_step) * SEL_STEP, SEL_STEP)

    def scores(step, s_out):
        k = k_ref[pl.ds(key_start(step), SEL_STEP), :]
        s_out[...] = lax.dot_general(qa_ref[...], k, dot_nt, preferred_element_type=F32)

    def softmax(s_in, p_out, al_out):
        for rb in range(SEL_Q * GROUP):
            m_prev = m_ref[_rows(rb), :] - delta_ref[_head_of_row_block(grp, rb % GROUP)]
            m_new = jnp.maximum(m_prev, jnp.max(s_in[_rows(rb), :], axis=1, keepdims=True))
            al_out[_rows(rb), :] = jnp.exp2(m_prev - m_new)
            m_ref[_rows(rb), :] = m_new
        for rb in range(SEL_Q * GROUP):
            p_out[_rows(rb), :] = jnp.exp2(s_in[_rows(rb), :] - _rep(m_ref[_rows(rb), :], SEL_STEP)).astype(p_out.dtype)

    def values(step, p_in, al_in):
        v = v_ref[pl.ds(key_start(step), SEL_STEP), :]
        pv = jnp.dot(p_in[...], v, preferred_element_type=F32)
        alpha = al_in[...]
        acc_ref[...] = acc_ref[...] * jnp.concatenate([alpha, alpha], axis=1) + pv

    def even_half(t):
        scores(t, s_a)
        softmax(s_b, p_b, al_b)
        values(t - 2, p_a, al_a)

    def odd_half(t):
        scores(t, s_b)
        softmax(s_a, p_a, al_a)
        values(t - 2, p_b, al_b)

    scores(0, s_a)
    qi = lax.broadcasted_iota(jnp.int32, (Q_BLOCK, Q_BLOCK), 0)
    ki = lax.broadcasted_iota(jnp.int32, (Q_BLOCK, Q_BLOCK), 1)
    own_bias = jnp.where((ki <= qi) & (ki // SEL_LEN == qi // SEL_LEN), 0.0, MASK_VALUE)
    owns = [pl.multiple_of(t0 + j * Q_BLOCK, Q_BLOCK) for j in range(SEL_Q)]
    s_owns = [lax.dot_general(qa_ref[j * ROWS:(j + 1) * ROWS, 0:LANES], k_ref[pl.ds(owns[j], Q_BLOCK), 0:LANES],
                              dot_nt, preferred_element_type=F32) for j in range(SEL_Q)]
    scores(1, s_b)
    back = jnp.full((Q_BLOCK, LANES), n_steps, jnp.int32).astype(F32)
    p_owns = []
    for j in range(SEL_Q):
        p_own = []
        for rb in range(GROUP):
            s = s_owns[j][_rows(rb)] + own_bias
            m = jnp.max(s, axis=1, keepdims=True)
            p_own.append(jnp.exp2(s - m).astype(MXU_DTYPE))
            m_ref[srows(j, rb), :] = m + back * delta_ref[_head_of_row_block(grp, rb)]
        p_owns.append(jnp.concatenate(p_own, axis=0))
    softmax(s_a, p_a, al_a)
    for j in range(SEL_Q):
        acc_ref[j * ROWS:(j + 1) * ROWS, :] = jnp.dot(p_owns[j], v_ref[pl.ds(owns[j], Q_BLOCK), :],
                                                      preferred_element_type=F32)

    def quad(j, carry):
        even_half(4 * j + 2)
        odd_half(4 * j + 3)
        even_half(4 * j + 4)
        odd_half(4 * j + 5)
        return carry

    lax.fori_loop(0, n_steps // 4, quad, 0)
    rest = 4 * (n_steps // 4) + 2

    @pl.when(n_steps % 4 >= 2)
    def _():
        even_half(rest)
        odd_half(rest + 1)

    @pl.when(n_steps % 2 == 1)
    def _():
        even_half(n_steps + 1)

    gw = PAIRS * LANES
    sig = jax.nn.sigmoid(gate_ref[...])
    hi = sig.astype(MXU_DTYPE)
    lo = (sig - hi.astype(F32)).astype(MXU_DTYPE)
    ex = ex_ref[0]
    g = jnp.dot(hi, ex, preferred_element_type=F32) + jnp.dot(lo, ex, preferred_element_type=F32)
    for j in range(SEL_Q):
        o_sel = _pairs(lambda rb: acc_ref[srows(j, rb), 0:LANES], lambda rb: acc_ref[srows(j, rb), LANES:2 * LANES])
        for p in range(PAIRS):
            cols = slice(p * LANES, (p + 1) * LANES)
            out = (g[qrows(j), cols] * oc_ref[qrows(j), cols]
                   + g[qrows(j), gw + p * LANES:gw + (p + 1) * LANES] * o_sel[p]
                   + g[qrows(j), 2 * gw + p * LANES:2 * gw + (p + 1) * LANES] * ow_ref[qrows(j), cols])
            o_ref[qrows(j), cols] = out.astype(o_ref.dtype)


def _gate_expansion():
    gw = PAIRS * LANES
    ex = np.zeros((KV_GROUPS, LANES, 3 * gw), np.float32)
    for g in range(KV_GROUPS):
        for hl in range(GROUP):
            for i in range(3):
                ex[g, 3 * (g * GROUP + hl) + i, i * gw + hl * HEAD_DIM:i * gw + (hl + 1) * HEAD_DIM] = 1.0
    return ex


def nsa_sel_attention(qkv, sel, gate, o_cmp, o_win, batch, seq, k_col, v_col):
    nstep = seq // (SEL_Q * Q_BLOCK)
    rows = SEL_Q * Q_BLOCK
    srows = SEL_Q * ROWS
    gw = PAIRS * LANES
    deltas = jnp.asarray((-_alibi_neg_slopes().astype(np.float64) * LOG2E * SEL_STEP).astype(np.float32))
    ex = jnp.asarray(_gate_expansion(), MXU_DTYPE)
    kfeat = jnp.asarray(sel_key_features(seq), MXU_DTYPE)
    blk = lambda b, g, c: (b * nstep + c, g)
    return pl.pallas_call(
        _nsa_sel_kernel,
        grid=(batch, KV_GROUPS, nstep),
        in_specs=[
            pl.BlockSpec(memory_space=pltpu.SMEM),
            pl.BlockSpec((1, GROUP, LANES), lambda b, g, c: (g, 0, 0)),
            pl.BlockSpec((rows, gw), blk),
            pl.BlockSpec((1, 1, rows, LANES), lambda b, g, c: (b, g, c, 0)),
            _kv_block_spec(seq, k_col),
            _kv_block_spec(seq, v_col),
            pl.BlockSpec((seq + SEL_STEP, 2 * LANES), lambda b, g, c: (0, 0), pipeline_mode=pl.Buffered(1)),
            pl.BlockSpec((rows, LANES), lambda b, g, c: (b * nstep + c, 0)),
            pl.BlockSpec((1, LANES, 3 * gw), lambda b, g, c: (g, 0, 0)),
            pl.BlockSpec((rows, gw), blk),
            pl.BlockSpec((rows, gw), blk),
        ],
        out_specs=pl.BlockSpec((rows, gw), blk),
        out_shape=jax.ShapeDtypeStruct((batch * seq, N_HEADS * HEAD_DIM), MXU_DTYPE),
        scratch_shapes=[
            pltpu.VMEM((srows, 2 * LANES), MXU_DTYPE),
            pltpu.VMEM((srows, LANES), F32),
            pltpu.VMEM((srows, 2 * LANES), F32),
            pltpu.VMEM((srows, SEL_STEP), F32),
            pltpu.VMEM((srows, SEL_STEP), F32),
            pltpu.VMEM((srows, SEL_STEP), MXU_DTYPE),
            pltpu.VMEM((srows, SEL_STEP), MXU_DTYPE),
            pltpu.VMEM((srows, LANES), F32),
            pltpu.VMEM((srows, LANES), F32),
            pltpu.VMEM((seq + SEL_STEP, 2 * LANES), MXU_DTYPE),
            pltpu.VMEM((seq + SEL_STEP, 2 * LANES), MXU_DTYPE),
        ],
        compiler_params=_params("parallel", "parallel", "arbitrary"),
        name="nsa_sel_attention",
    )(deltas, sel_query_features(), qkv, sel, qkv, qkv, kfeat, gate, ex, o_cmp, o_win)


def _nsa_win_kernel(nslope_ref, q_ref, k_in, v_in, o_ref, qs_ref, bias_scr, k_ref, v_ref):
    grp = pl.program_id(1)
    c = pl.program_id(2)
    span = NSA_WINDOW + Q_BLOCK
    lead = NSA_WINDOW // Q_BLOCK
    for j in range(Q_SUB):
        _build_q_stack(q_ref, qs_ref.at[j], j * Q_BLOCK)
    nslope = lambda rb: nslope_ref[_head_of_row_block(grp, rb)]

    def write(j, outs):
        for p, o in enumerate(outs):
            o_ref[j * Q_BLOCK:(j + 1) * Q_BLOCK, p * LANES:(p + 1) * LANES] = o

    @pl.when(c == 0)
    def _():
        _unpack_kv(k_in, v_in, grp, k_ref, v_ref)
        d = _pos_tiles(NSA_WINDOW, 0, span)
        for rb in range(GROUP):
            bias_scr[rb] = _band_bias(nslope(rb), d, NSA_WINDOW)

    def leading_block(j):
        d = _pos_tiles((c * Q_SUB + j) * Q_BLOCK, 0, span)
        write(j, _window_attend(qs_ref.at[j], k_ref[0:span, :], v_ref[0:span, :],
                                lambda rb: _band_bias(nslope(rb), d, NSA_WINDOW)))

    def later_block(j):
        start = pl.multiple_of((c * Q_SUB + j) * Q_BLOCK - NSA_WINDOW, Q_BLOCK)
        write(j, _window_attend(qs_ref.at[j], k_ref[pl.ds(start, span), :], v_ref[pl.ds(start, span), :],
                                lambda rb: bias_scr[rb]))

    lead_steps = max(lead // Q_SUB, 1)

    @pl.when(c < lead_steps)
    def _():
        for j in range(Q_SUB):
            if j < lead:
                leading_block(j)
            else:
                later_block(j)

    @pl.when(c >= lead_steps)
    def _():
        for j in range(Q_SUB):
            later_block(j)


def nsa_win_attention(qkv, nslopes, batch, seq, k_col, v_col):
    nstep = seq // (Q_SUB * Q_BLOCK)
    rows = Q_SUB * Q_BLOCK
    gw = PAIRS * LANES
    kd = KV_GROUPS * HEAD_DIM
    span = NSA_WINDOW + Q_BLOCK
    return pl.pallas_call(
        _nsa_win_kernel,
        grid=(batch, KV_GROUPS, nstep),
        in_specs=[
            pl.BlockSpec(memory_space=pltpu.SMEM),
            pl.BlockSpec((rows, gw), lambda b, g, c: (b * nstep + c, g)),
            _kv_block_spec(seq, k_col),
            _kv_block_spec(seq, v_col),
        ],
        out_specs=pl.BlockSpec((rows, gw), lambda b, g, c: (b * nstep + c, g)),
        out_shape=jax.ShapeDtypeStruct((batch * seq, N_HEADS * HEAD_DIM), F32),
        scratch_shapes=[
            pltpu.VMEM((Q_SUB, ROWS, LANES), MXU_DTYPE),
            pltpu.VMEM((GROUP, Q_BLOCK, span), F32),
            pltpu.VMEM((seq, LANES), MXU_DTYPE),
            pltpu.VMEM((seq, 2 * LANES), MXU_DTYPE),
        ],
        compiler_params=_params("parallel", "parallel", "arbitrary"),
        name="nsa_win_attention",
    )(nslopes, qkv, qkv, qkv)


def _overlap_matrix(seq):
    ncp = seq // CMP_STRIDE
    n_cmp = (seq - CMP_LEN) // CMP_STRIDE + 1
    cs = np.arange(n_cmp) * CMP_STRIDE
    ss = np.arange(seq // SEL_LEN) * SEL_LEN
    ov = (cs[:, None] < ss[None, :] + SEL_LEN) & (cs[:, None] + CMP_LEN > ss[None, :])
    out = np.zeros((ncp, LANES), np.float32)
    out[:n_cmp, :seq // SEL_LEN] = ov
    return out


def _swa_layer(h, x, w_in, b_in, sinks, w_o, b_o, g_post, g_next, nslopes, batch, seq):
    hd = N_HEADS * HEAD_DIM
    kd = KV_GROUPS * HEAD_DIM
    qkv = matmul_bias(h, w_in.astype(MXU_DTYPE), b_in, MXU_DTYPE, name="swa_in_proj")
    o = swa_attention(qkv, nslopes, sinks.astype(F32) * LOG2E, batch, seq)
    return matmul_norm_res(o, w_o.astype(MXU_DTYPE), b_o, g_post, g_next, x, tn=w_o.shape[1], name="swa_out_proj")


def _nsa_layer(h, x, w_in, cmp_pe, cmp_w1, cmp_b1, cmp_w2, cmp_b2, w_o, g_post, g_next, nslopes, batch, seq):
    hd = N_HEADS * HEAD_DIM
    kd = KV_GROUPS * HEAD_DIM
    t = batch * seq
    ncp = seq // CMP_STRIDE
    qkv = matmul_bias(h, w_in[:, :hd + 6 * kd].astype(MXU_DTYPE), jnp.zeros((hd + 6 * kd,), F32), MXU_DTYPE,
                      name="nsa_in_proj")
    n_gate = 3 * N_HEADS
    w_gate = jnp.pad(w_in[:, hd + 6 * kd:], ((0, 0), (0, LANES - n_gate))).astype(MXU_DTYPE)
    gate = matmul_bias(h, w_gate, jnp.zeros((LANES,), F32), F32, name="nsa_gate_proj")

    def kv(i):
        return qkv[:, hd + i * kd:hd + (i + 1) * kd]

    def slabs(a):
        return a.reshape(batch, seq, KV_GROUPS, HEAD_DIM).transpose(0, 2, 1, 3).reshape(
            batch, KV_GROUPS, ncp, CMP_STRIDE * HEAD_DIM)

    z = jnp.stack([slabs(kv(0)), slabs(kv(1))])
    half = CMP_STRIDE * HEAD_DIM
    cmp_out = compress(z, cmp_pe.reshape(2, 2, half).astype(F32), cmp_w1.astype(MXU_DTYPE),
                       cmp_b1.reshape(2, 1, -1), cmp_w2.astype(MXU_DTYPE), cmp_b2.reshape(2, 1, -1))
    kcm = cmp_out[0].astype(MXU_DTYPE)
    vcm = cmp_out[1].astype(MXU_DTYPE)
    kc2 = jnp.concatenate([kcm, kcm], axis=-1)
    ov = jnp.broadcast_to(jnp.asarray(_overlap_matrix(seq), MXU_DTYPE), (batch, KV_GROUPS, ncp, LANES))
    vc_aug = jnp.concatenate([vcm, vcm, jnp.ones((batch, KV_GROUPS, ncp, LANES), MXU_DTYPE), ov], axis=-1)

    o_cmp, sel = nsa_cmp_select(qkv, kc2, vc_aug, nslopes, batch, seq, 0)
    kv_col = lambda i: hd // kd + i
    o_win = nsa_win_attention(qkv, nslopes, batch, seq, kv_col(4), kv_col(5))
    o = nsa_sel_attention(qkv, sel, gate, o_cmp, o_win, batch, seq, kv_col(2), kv_col(3))
    return matmul_norm_res(o, w_o.astype(MXU_DTYPE), jnp.zeros((w_o.shape[1],), F32), g_post, g_next, x,
                           tn=w_o.shape[1], name="nsa_out_proj")


def kernel(x, norm_g, swa_w_in, swa_b_in, swa_sinks, swa_w_o, swa_b_o, nsa_w_in, nsa_cmp_pe, nsa_cmp_w1, nsa_cmp_b1, nsa_cmp_w2, nsa_cmp_b2, nsa_w_o, ffn_w_gate, ffn_w_up, ffn_conv_w, ffn_conv_b, ffn_w_down):
    batch, seq, d = x.shape
    depth = norm_g.shape[0]
    nslopes = jnp.asarray((_alibi_neg_slopes().astype(np.float64) * LOG2E).astype(np.float32))
    xf = x.reshape(batch * seq, d)
    h = rms_cast(xf, norm_g[0, 0])
    for i in range(depth):
        g = norm_g[i]
        j = i // 2
        if i % 2 == 0:
            xf, h = _swa_layer(h, xf, swa_w_in[j], swa_b_in[j], swa_sinks[j], swa_w_o[j], swa_b_o[j],
                               g[1], g[2], nslopes, batch, seq)
        else:
            xf, h = _nsa_layer(h, xf, nsa_w_in[j], nsa_cmp_pe[j], nsa_cmp_w1[j], nsa_cmp_b1[j], nsa_cmp_w2[j],
                               nsa_cmp_b2[j], nsa_w_o[j], g[1], g[2], nslopes, batch, seq)
        act = ffn_up(h, ffn_w_gate, ffn_w_up, i, ffn_conv_w[i], ffn_conv_b[i], seq)
        g_next = norm_g[i + 1, 0] if i + 1 < depth else jnp.ones((d,), F32)
        xf, h = matmul_norm_res(act, ffn_w_down[i].astype(MXU_DTYPE), jnp.zeros((d,), F32), g[3], g_next, xf,
                                tn=512, name="ffn_down")
    return xf.reshape(batch, seq, d)
```

```python
import functools

import numpy as np
import jax
import jax.numpy as jnp
from jax import lax
from jax.experimental import pallas as pl
from jax.experimental.pallas import tpu as pltpu

F32 = jnp.float32
MXU_DTYPE = jnp.bfloat16

N_HEADS = 32
HEAD_DIM = 64
KV_GROUPS = 4
GROUP = N_HEADS // KV_GROUPS
PAIRS = GROUP // 2
LANES = 128
Q_BLOCK = 128
ROWS = GROUP * Q_BLOCK
SWA_WINDOW = 128
CMP_LEN = 32
CMP_STRIDE = 16
SEL_LEN = 64
SEL_TOPK = 16
SEL_FORCED = 3
LOG2E = 1.4426950408889634
SEL_STEP = 512
SEL_Q = 2
NSA_WINDOW = 512
CONV_WIDTH = 3
RMS_EPS = 1e-6
MASK_VALUE = -1e30
VMEM_LIMIT = 60000 * 1024


def _params(*sem):
    return pltpu.CompilerParams(dimension_semantics=sem, vmem_limit_bytes=VMEM_LIMIT)


def _alibi_neg_slopes():
    return (-np.exp2(-8.0 * np.arange(1, N_HEADS + 1, dtype=np.float64) / N_HEADS)).astype(np.float32)


def _rms_cast_kernel(x_ref, g_ref, o_ref):
    x = x_ref[...]
    ms = jnp.mean(x * x, axis=-1, keepdims=True)
    o_ref[...] = (x * lax.rsqrt(ms + RMS_EPS) * g_ref[...]).astype(o_ref.dtype)


def rms_cast(x, g, tm=512):
    t, d = x.shape
    return pl.pallas_call(
        _rms_cast_kernel,
        grid=(t // tm,),
        in_specs=[pl.BlockSpec((tm, d), lambda i: (i, 0)), pl.BlockSpec((1, d), lambda i: (0, 0))],
        out_specs=pl.BlockSpec((tm, d), lambda i: (i, 0)),
        out_shape=jax.ShapeDtypeStruct((t, d), MXU_DTYPE),
        compiler_params=_params("parallel"),
        name="rms_cast",
    )(x, g.reshape(1, d))


def _matmul_kernel(a_ref, w_ref, b_ref, o_ref):
    acc = jnp.dot(a_ref[...], w_ref[...], preferred_element_type=F32)
    o_ref[...] = (acc + b_ref[...]).astype(o_ref.dtype)


def matmul_bias(a, w, b, out_dtype, tm=2048, tn=512, name="matmul_bias"):
    t, k = a.shape
    n = w.shape[1]
    tn = min(tn, n)
    return pl.pallas_call(
        _matmul_kernel,
        grid=(t // tm, n // tn),
        in_specs=[
            pl.BlockSpec((tm, k), lambda i, j: (i, 0)),
            pl.BlockSpec((k, tn), lambda i, j: (0, j)),
            pl.BlockSpec((1, tn), lambda i, j: (0, j)),
        ],
        out_specs=pl.BlockSpec((tm, tn), lambda i, j: (i, j)),
        out_shape=jax.ShapeDtypeStruct((t, n), out_dtype),
        compiler_params=_params("parallel", "arbitrary"),
        name=name,
    )(a, w, b.reshape(1, n))


def _mm_norm_res_kernel(a_ref, w_ref, b_ref, gpost_ref, gnext_ref, x_ref, xo_ref, ho_ref, y_scr, *, nj, tn, n):
    j = pl.program_id(1)
    y_scr[j] = jnp.dot(a_ref[...], w_ref[...], preferred_element_type=F32) + b_ref[...]

    @pl.when(j == nj - 1)
    def _():
        ss = jnp.sum(y_scr[0] * y_scr[0], axis=1, keepdims=True)
        for jj in range(1, nj):
            ss = ss + jnp.sum(y_scr[jj] * y_scr[jj], axis=1, keepdims=True)
        r = lax.rsqrt(ss / n + RMS_EPS)
        ss2 = jnp.zeros_like(ss)
        for jj in range(nj):
            cols = slice(jj * tn, (jj + 1) * tn)
            xn = x_ref[:, cols] + y_scr[jj] * r * gpost_ref[:, cols]
            xo_ref[:, cols] = xn
            ss2 = ss2 + jnp.sum(xn * xn, axis=1, keepdims=True)
        r2 = lax.rsqrt(ss2 / n + RMS_EPS)
        for jj in range(nj):
            cols = slice(jj * tn, (jj + 1) * tn)
            ho_ref[:, cols] = (xo_ref[:, cols] * r2 * gnext_ref[:, cols]).astype(ho_ref.dtype)


def matmul_norm_res(a, w, b, g_post, g_next, x, tm=512, tn=512, name="matmul_norm_res"):
    t, k = a.shape
    n = w.shape[1]
    nj = n // tn
    kern = functools.partial(_mm_norm_res_kernel, nj=nj, tn=tn, n=n)
    return pl.pallas_call(
        kern,
        grid=(t // tm, nj),
        in_specs=[
            pl.BlockSpec((tm, k), lambda i, j: (i, 0)),
            pl.BlockSpec((k, tn), lambda i, j: (0, j)),
            pl.BlockSpec((1, tn), lambda i, j: (0, j)),
            pl.BlockSpec((1, n), lambda i, j: (0, 0)),
            pl.BlockSpec((1, n), lambda i, j: (0, 0)),
            pl.BlockSpec((tm, n), lambda i, j: (i, 0)),
        ],
        out_specs=[
            pl.BlockSpec((tm, n), lambda i, j: (i, 0)),
            pl.BlockSpec((tm, n), lambda i, j: (i, 0)),
        ],
        out_shape=[jax.ShapeDtypeStruct((t, n), F32), jax.ShapeDtypeStruct((t, n), MXU_DTYPE)],
        scratch_shapes=[pltpu.VMEM((nj, tm, tn), F32)],
        compiler_params=_params("parallel", "arbitrary"),
        name=name,
    )(a, w, b.reshape(1, n), g_post.reshape(1, n), g_next.reshape(1, n), x)


FIX_ROWS = 16
CARRY_ROWS = 8


def _ffn_up_kernel(h_ref, wg_ref, wu_ref, cw_ref, cb_ref, o_ref, wg_scr, wu_scr, carry_scr, *, tm, tiles_per_seq):
    i = pl.program_id(1)

    @pl.when(i == 0)
    def _():
        wg_scr[...] = wg_ref[...].astype(wg_scr.dtype)
        wu_scr[...] = wu_ref[...].astype(wu_scr.dtype)

    h = h_ref[...]
    gate = jnp.dot(h, wg_scr[...], preferred_element_type=F32)
    up = jnp.dot(h, wu_scr[...], preferred_element_type=F32)
    w0 = cw_ref[0:1, :]
    w1 = cw_ref[1:2, :]
    w2 = cw_ref[2:3, :]
    b = cb_ref[...]
    a = b + pltpu.roll(gate, 2, axis=0) * w0
    a = a + pltpu.roll(gate, 1, axis=0) * w1
    a = a + gate * w2
    o_ref[...] = (jax.nn.silu(a) * up).astype(o_ref.dtype)

    seq_start = (i % tiles_per_seq) == 0
    prev = jnp.where(seq_start, 0.0, carry_scr[...])
    head = gate[0:FIX_ROWS]
    ext = jnp.concatenate([prev, head], axis=0)
    af = b + ext[CARRY_ROWS - 2:CARRY_ROWS - 2 + FIX_ROWS] * w0
    af = af + ext[CARRY_ROWS - 1:CARRY_ROWS - 1 + FIX_ROWS] * w1
    af = af + head * w2
    o_ref[0:FIX_ROWS, :] = (jax.nn.silu(af) * up[0:FIX_ROWS]).astype(o_ref.dtype)
    carry_scr[...] = gate[tm - CARRY_ROWS:tm]


def ffn_up(h, wg, wu, layer, conv_w, conv_b, seq, tm=1024, tn=512):
    t, k = h.shape
    n = wg.shape[2]
    kern = functools.partial(_ffn_up_kernel, tm=tm, tiles_per_seq=seq // tm)
    return pl.pallas_call(
        kern,
        grid=(n // tn, t // tm),
        in_specs=[
            pl.BlockSpec((tm, k), lambda j, i: (i, 0)),
            pl.BlockSpec((None, k, tn), lambda j, i: (layer, 0, j)),
            pl.BlockSpec((None, k, tn), lambda j, i: (layer, 0, j)),
            pl.BlockSpec((CONV_WIDTH, tn), lambda j, i: (0, j)),
            pl.BlockSpec((1, tn), lambda j, i: (0, j)),
        ],
        out_specs=pl.BlockSpec((tm, tn), lambda j, i: (i, j)),
        out_shape=jax.ShapeDtypeStruct((t, n), MXU_DTYPE),
        scratch_shapes=[
            pltpu.VMEM((k, tn), MXU_DTYPE),
            pltpu.VMEM((k, tn), MXU_DTYPE),
            pltpu.VMEM((CARRY_ROWS, tn), F32),
        ],
        compiler_params=_params("arbitrary", "arbitrary"),
        name="ffn_up",
    )(h, wg, wu, conv_w, conv_b.reshape(1, n))


def _head_of_row_block(group, rb):
    return group * GROUP + 2 * (rb % PAIRS) + rb // PAIRS


Q_SUB = 8


def _build_q_stack(q_ref, qs_ref, row0=0):
    lane = lax.broadcasted_iota(jnp.int32, (Q_BLOCK, LANES), 1)
    even = lane < HEAD_DIM
    for p in range(PAIRS):
        qp = q_ref[row0:row0 + Q_BLOCK, p * LANES:(p + 1) * LANES].astype(F32) * (HEAD_DIM ** -0.5 * LOG2E)
        qs_ref[p * Q_BLOCK:(p + 1) * Q_BLOCK, :] = jnp.where(even, qp, 0.0).astype(qs_ref.dtype)
        qs_ref[(PAIRS + p) * Q_BLOCK:(PAIRS + p + 1) * Q_BLOCK, :] = jnp.where(even, 0.0, qp).astype(qs_ref.dtype)


def _rep(x, size):
    return x if size == LANES else jnp.concatenate([x] * (size // LANES), axis=1)


def _pairs(num, den):
    lane = lax.broadcasted_iota(jnp.int32, (Q_BLOCK, LANES), 1)
    even = lane < HEAD_DIM
    outs = []
    for p in range(PAIRS):
        oe = num(p) / jnp.maximum(den(p), 1e-30)
        oo = num(PAIRS + p) / jnp.maximum(den(PAIRS + p), 1e-30)
        outs.append(jnp.where(even, oe, oo))
    return outs


def _rows(rb):
    return slice(rb * Q_BLOCK, (rb + 1) * Q_BLOCK)


def _band_bias(nslope, d, window):
    return jnp.where((d >= 0) & (d < window), nslope * d.astype(F32), MASK_VALUE)


def _window_attend(qs_ref, k, v, bias, extra_logit=None):
    half = ROWS // 2
    dot_nt = (((1,), (1,)), ((), ()))
    s_halves = [lax.dot_general(qs_ref[hh * half:(hh + 1) * half, :], k, dot_nt, preferred_element_type=F32)
                for hh in range(2)]
    ps, extras = [], []
    for rb in range(GROUP):
        lo = (rb % PAIRS) * Q_BLOCK
        s = s_halves[rb // PAIRS][lo:lo + Q_BLOCK] + bias(rb)
        m = jnp.max(s, axis=1, keepdims=True)
        if extra_logit is not None:
            m = jnp.maximum(m, extra_logit(rb))
            extras.append(jnp.exp2(extra_logit(rb) - m))
        ps.append(jnp.exp2(s - m).astype(MXU_DTYPE))
    r_halves = [jnp.dot(jnp.concatenate(ps[hh * PAIRS:(hh + 1) * PAIRS], axis=0), v, preferred_element_type=F32)
                for hh in range(2)]

    def part(rb, cols):
        lo = (rb % PAIRS) * Q_BLOCK
        return r_halves[rb // PAIRS][lo:lo + Q_BLOCK, cols]

    num = lambda rb: part(rb, slice(0, LANES))
    if extra_logit is None:
        den = lambda rb: part(rb, slice(LANES, 2 * LANES))
    else:
        den = lambda rb: part(rb, slice(LANES, 2 * LANES)) + extras[rb]
    return _pairs(num, den)


KV_CHUNK = 1024


def _unpack_group(kv_ref, grp, write_chunk, fill_ref=None):
    n = kv_ref.shape[0]
    for odd in range(2):
        def fill(odd=odd):
            def body(i, carry):
                r0 = pl.multiple_of(i * KV_CHUNK, KV_CHUNK)
                x = kv_ref[pl.ds(r0, KV_CHUNK), :]
                xi = pltpu.bitcast(x, jnp.int32)
                xr = pltpu.roll(xi, HEAD_DIM, axis=1)
                low = lax.broadcasted_iota(jnp.int32, xi.shape, 1) < HEAD_DIM
                if fill_ref is None:
                    y = jnp.where(low, xr, xi) if odd else jnp.where(low, xi, xr)
                else:
                    f = pltpu.bitcast(fill_ref[pl.ds(r0, KV_CHUNK), 0:LANES], jnp.int32)
                    y = jnp.where(low, xr if odd else xi, f)
                write_chunk(r0, pltpu.bitcast(y, x.dtype))
                return carry

            lax.fori_loop(0, n // KV_CHUNK, body, 0)

        pl.when(grp % 2 == odd)(fill)


def _kv_block_spec(seq, col):
    pairs_per_tensor = KV_GROUPS * HEAD_DIM // LANES
    return pl.BlockSpec((seq, LANES), lambda b, g, c: (b, pairs_per_tensor * col + g // 2))


def _pos_tiles(t0, start, size):
    qi = lax.broadcasted_iota(jnp.int32, (Q_BLOCK, size), 0)
    ki = lax.broadcasted_iota(jnp.int32, (Q_BLOCK, size), 1)
    return (t0 - start) + (qi - ki)


def _unpack_kv(k_in, v_in, grp, k_ref, v_ref):
    def put_k(r0, y):
        k_ref[pl.ds(r0, KV_CHUNK), :] = y

    def put_v(r0, y):
        v_ref[pl.ds(r0, KV_CHUNK), 0:LANES] = y
        v_ref[pl.ds(r0, KV_CHUNK), LANES:2 * LANES] = jnp.ones((KV_CHUNK, LANES), v_ref.dtype)

    _unpack_group(k_in, grp, put_k)
    _unpack_group(v_in, grp, put_v)


def _swa_kernel(nslope_ref, sink_ref, q_ref, k_in, v_in, o_ref, qs_ref, bias_scr, k_ref, v_ref):
    grp = pl.program_id(1)
    c = pl.program_id(2)
    span = SWA_WINDOW + Q_BLOCK
    for j in range(Q_SUB):
        _build_q_stack(q_ref, qs_ref.at[j], j * Q_BLOCK)
    sink = lambda rb: sink_ref[_head_of_row_block(grp, rb)]

    def write(j, outs):
        for p, o in enumerate(outs):
            o_ref[j * Q_BLOCK:(j + 1) * Q_BLOCK, p * LANES:(p + 1) * LANES] = o.astype(o_ref.dtype)

    def later_block(j):
        start = pl.multiple_of((c * Q_SUB + j) * Q_BLOCK - SWA_WINDOW, Q_BLOCK)
        write(j, _window_attend(qs_ref.at[j], k_ref[pl.ds(start, span), :], v_ref[pl.ds(start, span), :],
                                lambda rb: bias_scr[rb], sink))

    @pl.when(c == 0)
    def _():
        _unpack_kv(k_in, v_in, grp, k_ref, v_ref)
        d = _pos_tiles(SWA_WINDOW, 0, span)
        for rb in range(GROUP):
            bias_scr[rb] = _band_bias(nslope_ref[_head_of_row_block(grp, rb)], d, SWA_WINDOW)

    @pl.when(c == 0)
    def _():
        write(0, _window_attend(qs_ref.at[0], k_ref[0:Q_BLOCK, :], v_ref[0:Q_BLOCK, :],
                                lambda rb: bias_scr[rb, :, SWA_WINDOW:span], sink))
        for j in range(1, Q_SUB):
            later_block(j)

    @pl.when(c > 0)
    def _():
        for j in range(Q_SUB):
            later_block(j)


def swa_attention(qkv, nslopes, sinks, batch, seq):
    nstep = seq // (Q_SUB * Q_BLOCK)
    rows = Q_SUB * Q_BLOCK
    gw = PAIRS * LANES
    kd = KV_GROUPS * HEAD_DIM
    span = SWA_WINDOW + Q_BLOCK
    k_col = N_HEADS * HEAD_DIM // kd
    return pl.pallas_call(
        _swa_kernel,
        grid=(batch, KV_GROUPS, nstep),
        in_specs=[
            pl.BlockSpec(memory_space=pltpu.SMEM),
            pl.BlockSpec(memory_space=pltpu.SMEM),
            pl.BlockSpec((rows, gw), lambda b, g, c: (b * nstep + c, g)),
            _kv_block_spec(seq, k_col),
            _kv_block_spec(seq, k_col + 1),
        ],
        out_specs=pl.BlockSpec((rows, gw), lambda b, g, c: (b * nstep + c, g)),
        out_shape=jax.ShapeDtypeStruct((batch * seq, N_HEADS * HEAD_DIM), MXU_DTYPE),
        scratch_shapes=[
            pltpu.VMEM((Q_SUB, ROWS, LANES), MXU_DTYPE),
            pltpu.VMEM((GROUP, Q_BLOCK, span), F32),
            pltpu.VMEM((seq, LANES), MXU_DTYPE),
            pltpu.VMEM((seq, 2 * LANES), MXU_DTYPE),
        ],
        compiler_params=_params("parallel", "parallel", "arbitrary"),
        name="swa_attention",
    )(nslopes, sinks, qkv, qkv, qkv)


def _compress_kernel(z_ref, pe_ref, w1_ref, b1_ref, w2_ref, b2_ref, o_ref, *, ncp):
    half = CMP_STRIDE * HEAD_DIM
    z = z_ref[0, 0, 0].astype(F32)
    top = (z + pe_ref[0, 0:1, :]).astype(MXU_DTYPE)
    bot = (z + pe_ref[0, 1:2, :]).astype(MXU_DTYPE)
    a = jnp.dot(top, w1_ref[0, 0:half, :], preferred_element_type=F32)
    bm = jnp.dot(bot, w1_ref[0, half:2 * half, :], preferred_element_type=F32)
    hid = a + pltpu.roll(bm, ncp - 1, axis=0) + b1_ref[0]
    act = jax.nn.gelu(hid).astype(MXU_DTYPE)
    o_ref[0, 0, 0] = jnp.dot(act, w2_ref[0], preferred_element_type=F32) + b2_ref[0]


def compress(z, pe, w1, b1, w2, b2):
    _, batch, groups, ncp, zw = z.shape
    hid = w1.shape[-1]
    kern = functools.partial(_compress_kernel, ncp=ncp)
    return pl.pallas_call(
        kern,
        grid=(2, batch, groups),
        in_specs=[
            pl.BlockSpec((1, 1, 1, ncp, zw), lambda s, b, g: (s, b, g, 0, 0)),
            pl.BlockSpec((1, 2, zw), lambda s, b, g: (s, 0, 0)),
            pl.BlockSpec((1, 2 * zw, hid), lambda s, b, g: (s, 0, 0)),
            pl.BlockSpec((1, 1, hid), lambda s, b, g: (s, 0, 0)),
            pl.BlockSpec((1, hid, HEAD_DIM), lambda s, b, g: (s, 0, 0)),
            pl.BlockSpec((1, 1, HEAD_DIM), lambda s, b, g: (s, 0, 0)),
        ],
        out_specs=pl.BlockSpec((1, 1, 1, ncp, HEAD_DIM), lambda s, b, g: (s, b, g, 0, 0)),
        out_shape=jax.ShapeDtypeStruct((2, batch, groups, ncp, HEAD_DIM), F32),
        compiler_params=_params("parallel", "parallel", "parallel"),
        name="nsa_compress",
    )(z, pe, w1, b1, w2, b2)


def _nsa_cmp_kernel(nslope_ref, q_ref, kc_ref, vc_ref, o_ref, sel_ref, qs_ref, e_scr, r_scr, *, ncp, n_cmp, n_sel):
    grp = pl.program_id(1)
    c = pl.program_id(2)
    starts = [(c * Q_SUB + j) * Q_BLOCK for j in range(Q_SUB)]
    for j in range(Q_SUB):
        _build_q_stack(q_ref, qs_ref.at[j], j * Q_BLOCK)

    def attend(width):
        for j, t0 in enumerate(starts):
            qi = lax.broadcasted_iota(jnp.int32, (Q_BLOCK, width), 0)
            ni = lax.broadcasted_iota(jnp.int32, (Q_BLOCK, width), 1)
            d = (t0 + qi) - (ni * CMP_STRIDE + (CMP_LEN - 1))
            negb = jnp.where((d >= 0) & (ni < n_cmp), 0.0, MASK_VALUE)
            dist = d.astype(F32)
            s_all = lax.dot_general(qs_ref[j], kc_ref[0, 0, 0:width, :], (((1,), (1,)), ((), ())),
                                    preferred_element_type=F32)
            for rb in range(GROUP):
                s = s_all[_rows(rb)] + nslope_ref[_head_of_row_block(grp, rb)] * dist + negb
                m = jnp.max(s, axis=1, keepdims=True)
                e_scr[j, _rows(rb), 0:width] = jnp.exp2(s - m).astype(e_scr.dtype)
            r_scr[j] = jnp.dot(e_scr[j, :, 0:width], vc_ref[0, 0, 0:width, :], preferred_element_type=F32)

    n_chunks = ncp // LANES
    need = jnp.minimum((starts[-1] + Q_BLOCK - CMP_LEN) // CMP_STRIDE // LANES + 1, n_chunks)
    for kq in range(1, n_chunks + 1):
        pl.when(need == kq)(functools.partial(attend, kq * LANES))

    lane = lax.broadcasted_iota(jnp.int32, (Q_BLOCK, LANES), 1)
    even = lane < HEAD_DIM
    ji = lax.broadcasted_iota(jnp.int32, (LANES, Q_BLOCK), 0)
    qt = lax.broadcasted_iota(jnp.int32, (LANES, Q_BLOCK), 1)
    neg_inf = -jnp.inf
    for j, t0 in enumerate(starts):
        orow = slice(j * Q_BLOCK, (j + 1) * Q_BLOCK)
        row_t = t0 + lax.broadcasted_iota(jnp.int32, (Q_BLOCK, LANES), 0)
        has_cmp = row_t >= (CMP_LEN - 1)
        imp = jnp.zeros((Q_BLOCK, LANES), F32)
        for p in range(PAIRS):
            re = _rows(p)
            ro = _rows(PAIRS + p)
            de = jnp.maximum(r_scr[j, re, LANES:2 * LANES], 1e-30)
            do = jnp.maximum(r_scr[j, ro, LANES:2 * LANES], 1e-30)
            o = jnp.where(even, r_scr[j, re, 0:LANES] / de, r_scr[j, ro, 0:LANES] / do)
            o_ref[orow, p * LANES:(p + 1) * LANES] = jnp.where(has_cmp, o, 0.0)
            imp = imp + r_scr[j, re, 2 * LANES:3 * LANES] / de + r_scr[j, ro, 2 * LANES:3 * LANES] / do
        imp = jnp.where(has_cmp, imp, 0.0)

        imp_t = imp.T
        cur = (t0 + qt) // SEL_LEN
        causal = ji <= cur
        forced = (ji == 0) | (ji == cur) | (ji == cur - 1)
        score = jnp.where(forced, neg_inf, jnp.where(causal, imp_t, MASK_VALUE))
        score = jnp.where(ji < n_sel, score, neg_inf)
        picked = jnp.where(forced, 1.0, 0.0)
        for _ in range(SEL_TOPK - SEL_FORCED):
            mx = jnp.max(score, axis=0, keepdims=True)
            first = jnp.min(jnp.where(score == mx, ji, LANES), axis=0, keepdims=True)
            hit = ji == first
            picked = jnp.where(hit, 1.0, picked)
            score = jnp.where(hit, neg_inf, score)
        picked = jnp.where(ji < cur, picked, 0.0)
        sel_ref[0, 0, orow, :] = picked.T.astype(sel_ref.dtype)


def nsa_cmp_select(q, kc2, vc_aug, nslopes, batch, seq, q_col_block):
    nstep = seq // (Q_SUB * Q_BLOCK)
    rows = Q_SUB * Q_BLOCK
    ncp = seq // CMP_STRIDE
    n_cmp = (seq - CMP_LEN) // CMP_STRIDE + 1
    n_sel = seq // SEL_LEN
    gw = PAIRS * LANES
    kern = functools.partial(_nsa_cmp_kernel, ncp=ncp, n_cmp=n_cmp, n_sel=n_sel)
    return pl.pallas_call(
        kern,
        grid=(batch, KV_GROUPS, nstep),
        in_specs=[
            pl.BlockSpec(memory_space=pltpu.SMEM),
            pl.BlockSpec((rows, gw), lambda b, g, c: (b * nstep + c, q_col_block + g)),
            pl.BlockSpec((1, 1, ncp, LANES), lambda b, g, c: (b, g, 0, 0)),
            pl.BlockSpec((1, 1, ncp, 3 * LANES), lambda b, g, c: (b, g, 0, 0)),
        ],
        out_specs=[
            pl.BlockSpec((rows, gw), lambda b, g, c: (b * nstep + c, g)),
            pl.BlockSpec((1, 1, rows, LANES), lambda b, g, c: (b, g, c, 0)),
        ],
        out_shape=[
            jax.ShapeDtypeStruct((batch * seq, N_HEADS * HEAD_DIM), F32),
            jax.ShapeDtypeStruct((batch, KV_GROUPS, seq, LANES), MXU_DTYPE),
        ],
        scratch_shapes=[pltpu.VMEM((Q_SUB, ROWS, LANES), MXU_DTYPE), pltpu.VMEM((Q_SUB, ROWS, ncp), MXU_DTYPE),
                        pltpu.VMEM((Q_SUB, ROWS, 3 * LANES), F32)],
        compiler_params=_params("parallel", "parallel", "arbitrary"),
        name="nsa_cmp_select",
    )(nslopes, q, kc2, vc_aug)


N_FEAT = 6


def sel_query_features():
    s = jnp.asarray((-_alibi_neg_slopes().astype(np.float64) * LOG2E).astype(np.float32))
    s1 = s.astype(MXU_DTYPE).astype(F32)
    s2 = (s - s1).astype(MXU_DTYPE).astype(F32)
    s3 = (s - s1 - s2).astype(MXU_DTYPE).astype(F32)
    feat = jnp.zeros((N_HEADS, LANES), F32).at[:, HEAD_DIM:HEAD_DIM + N_FEAT].set(
        jnp.stack([s1, s2, s3, s1, s2, s3], axis=1))
    feat = feat.at[:, HEAD_DIM + N_FEAT].set(MASK_VALUE)
    order = np.array([[_head_of_row_block(g, rb) for rb in range(GROUP)] for g in range(KV_GROUPS)])
    return feat[order]


def sel_key_features(seq):
    pos = np.arange(seq)
    kk = pos % SEL_STEP
    f = np.zeros((seq + SEL_STEP, 2 * LANES), np.float32)
    f[:seq, HEAD_DIM:HEAD_DIM + 3] = (SEL_LEN * (kk // SEL_LEN))[:, None]
    f[:seq, HEAD_DIM + 3:HEAD_DIM + 6] = (kk % SEL_LEN)[:, None]
    f[seq:, HEAD_DIM + N_FEAT] = 1.0
    f[pos, LANES + pos // SEL_LEN] = 1.0
    return f


def _nsa_sel_kernel(delta_ref, qfeat_ref, q_ref, sel_ref, k_in, v_in, kfeat_ref, gate_ref, ex_ref, oc_ref, ow_ref,
                    o_ref, qa_ref, m_ref, acc_ref, s_a, s_b, p_a, p_b, al_a, al_b, k_ref, v_ref):
    grp = pl.program_id(1)
    c = pl.program_id(2)
    t0 = c * (SEL_Q * Q_BLOCK)
    seq = k_in.shape[0]
    srows = lambda j, rb: slice(j * ROWS + rb * Q_BLOCK, j * ROWS + (rb + 1) * Q_BLOCK)
    qrows = lambda j: slice(j * Q_BLOCK, (j + 1) * Q_BLOCK)

    @pl.when(c == 0)
    def _():
        def put_k(r0, y):
            k_ref[pl.ds(r0, KV_CHUNK), 0:LANES] = y
            k_ref[pl.ds(r0, KV_CHUNK), LANES:2 * LANES] = kfeat_ref[pl.ds(r0, KV_CHUNK), LANES:2 * LANES]

        def put_v(r0, y):
            v_ref[pl.ds(r0, KV_CHUNK), 0:LANES] = y
            v_ref[pl.ds(r0, KV_CHUNK), LANES:2 * LANES] = jnp.ones((KV_CHUNK, LANES), v_ref.dtype)

        _unpack_group(k_in, grp, put_k, fill_ref=kfeat_ref)
        _unpack_group(v_in, grp, put_v)
        k_ref[seq:seq + SEL_STEP, :] = kfeat_ref[seq:seq + SEL_STEP, :]
        v_ref[seq:seq + SEL_STEP, :] = jnp.zeros((SEL_STEP, 2 * LANES), v_ref.dtype)
    lane = lax.broadcasted_iota(jnp.int32, (Q_BLOCK, LANES), 1)
    low = lane < HEAD_DIM
    for j in range(SEL_Q):
        selneg = ((1.0 - sel_ref[0, 0, qrows(j), :].astype(F32)) * MASK_VALUE).astype(qa_ref.dtype)
        for p in range(PAIRS):
            qp = q_ref[qrows(j), p * LANES:(p + 1) * LANES].astype(F32) * (HEAD_DIM ** -0.5 * LOG2E)
            for rb, src in ((p, qp), (PAIRS + p, pltpu.roll(qp, HEAD_DIM, axis=1))):
                qa_ref[srows(j, rb), 0:LANES] = jnp.where(low, src, qfeat_ref[0, rb:rb + 1, :]).astype(qa_ref.dtype)
                qa_ref[srows(j, rb), LANES:2 * LANES] = selneg
    n_steps = t0 // SEL_STEP + 1
    n_pad_step = seq // SEL_STEP
    dot_nt = (((1,), (1,)), ((), ()))

    def key_start(step):
        return pl.multiple_of(jnp.clip(step, 0, n_pad_step) * SEL_STEP, SEL_STEP)

    def scores(step, s_out):
        k = k_ref[pl.ds(key_start(step), SEL_STEP), :]
        s_out[...] = lax.dot_general(qa_ref[...], k, dot_nt, preferred_element_type=F32)

    def softmax(s_in, p_out, al_out):
        for rb in range(SEL_Q * GROUP):
            m_prev = m_ref[_rows(rb), :] - delta_ref[_head_of_row_block(grp, rb % GROUP)]
            m_new = jnp.maximum(m_prev, jnp.max(s_in[_rows(rb), :], axis=1, keepdims=True))
            al_out[_rows(rb), :] = jnp.exp2(m_prev - m_new)
            m_ref[_rows(rb), :] = m_new
        for rb in range(SEL_Q * GROUP):
            p_out[_rows(rb), :] = jnp.exp2(s_in[_rows(rb), :] - _rep(m_ref[_rows(rb), :], SEL_STEP)).astype(p_out.dtype)

    def values(step, p_in, al_in):
        v = v_ref[pl.ds(key_start(step), SEL_STEP), :]
        pv = jnp.dot(p_in[...], v, preferred_element_type=F32)
        alpha = al_in[...]
        acc_ref[...] = acc_ref[...] * jnp.concatenate([alpha, alpha], axis=1) + pv

    def even_half(t):
        scores(t, s_a)
        softmax(s_b, p_b, al_b)
        values(t - 2, p_a, al_a)

    def odd_half(t):
        scores(t, s_b)
        softmax(s_a, p_a, al_a)
        values(t - 2, p_b, al_b)

    scores(0, s_a)
    qi = lax.broadcasted_iota(jnp.int32, (Q_BLOCK, Q_BLOCK), 0)
    ki = lax.broadcasted_iota(jnp.int32, (Q_BLOCK, Q_BLOCK), 1)
    own_bias = jnp.where((ki <= qi) & (ki // SEL_LEN == qi // SEL_LEN), 0.0, MASK_VALUE)
    owns = [pl.multiple_of(t0 + j * Q_BLOCK, Q_BLOCK) for j in range(SEL_Q)]
    s_owns = [lax.dot_general(qa_ref[j * ROWS:(j + 1) * ROWS, 0:LANES], k_ref[pl.ds(owns[j], Q_BLOCK), 0:LANES],
                              dot_nt, preferred_element_type=F32) for j in range(SEL_Q)]
    scores(1, s_b)
    back = jnp.full((Q_BLOCK, LANES), n_steps, jnp.int32).astype(F32)
    p_owns = []
    for j in range(SEL_Q):
        p_own = []
        for rb in range(GROUP):
            s = s_owns[j][_rows(rb)] + own_bias
            m = jnp.max(s, axis=1, keepdims=True)
            p_own.append(jnp.exp2(s - m).astype(MXU_DTYPE))
            m_ref[srows(j, rb), :] = m + back * delta_ref[_head_of_row_block(grp, rb)]
        p_owns.append(jnp.concatenate(p_own, axis=0))
    softmax(s_a, p_a, al_a)
    for j in range(SEL_Q):
        acc_ref[j * ROWS:(j + 1) * ROWS, :] = jnp.dot(p_owns[j], v_ref[pl.ds(owns[j], Q_BLOCK), :],
                                                      preferred_element_type=F32)

    def quad(j, carry):
        even_half(4 * j + 2)
        odd_half(4 * j + 3)
        even_half(4 * j + 4)
        odd_half(4 * j + 5)
        return carry

    lax.fori_loop(0, n_steps // 4, quad, 0)
    rest = 4 * (n_steps // 4) + 2

    @pl.when(n_steps % 4 >= 2)
    def _():
        even_half(rest)
        odd_half(rest + 1)

    @pl.when(n_steps % 2 == 1)
    def _():
        even_half(n_steps + 1)

    gw = PAIRS * LANES
    sig = jax.nn.sigmoid(gate_ref[...])
    hi = sig.astype(MXU_DTYPE)
    lo = (sig - hi.astype(F32)).astype(MXU_DTYPE)
    ex = ex_ref[0]
    g = jnp.dot(hi, ex, preferred_element_type=F32) + jnp.dot(lo, ex, preferred_element_type=F32)
    for j in range(SEL_Q):
        o_sel = _pairs(lambda rb: acc_ref[srows(j, rb), 0:LANES], lambda rb: acc_ref[srows(j, rb), LANES:2 * LANES])
        for p in range(PAIRS):
            cols = slice(p * LANES, (p + 1) * LANES)
            out = (g[qrows(j), cols] * oc_ref[qrows(j), cols]
                   + g[qrows(j), gw + p * LANES:gw + (p + 1) * LANES] * o_sel[p]
                   + g[qrows(j), 2 * gw + p * LANES:2 * gw + (p + 1) * LANES] * ow_ref[qrows(j), cols])
            o_ref[qrows(j), cols] = out.astype(o_ref.dtype)


def _gate_expansion():
    gw = PAIRS * LANES
    ex = np.zeros((KV_GROUPS, LANES, 3 * gw), np.float32)
    for g in range(KV_GROUPS):
        for hl in range(GROUP):
            for i in range(3):
                ex[g, 3 * (g * GROUP + hl) + i, i * gw + hl * HEAD_DIM:i * gw + (hl + 1) * HEAD_DIM] = 1.0
    return ex


def nsa_sel_attention(qkv, sel, gate, o_cmp, o_win, batch, seq, k_col, v_col):
    nstep = seq // (SEL_Q * Q_BLOCK)
    rows = SEL_Q * Q_BLOCK
    srows = SEL_Q * ROWS
    gw = PAIRS * LANES
    deltas = jnp.asarray((-_alibi_neg_slopes().astype(np.float64) * LOG2E * SEL_STEP).astype(np.float32))
    ex = jnp.asarray(_gate_expansion(), MXU_DTYPE)
    kfeat = jnp.asarray(sel_key_features(seq), MXU_DTYPE)
    blk = lambda b, g, c: (b * nstep + c, g)
    return pl.pallas_call(
        _nsa_sel_kernel,
        grid=(batch, KV_GROUPS, nstep),
        in_specs=[
            pl.BlockSpec(memory_space=pltpu.SMEM),
            pl.BlockSpec((1, GROUP, LANES), lambda b, g, c: (g, 0, 0)),
            pl.BlockSpec((rows, gw), blk),
            pl.BlockSpec((1, 1, rows, LANES), lambda b, g, c: (b, g, c, 0)),
            _kv_block_spec(seq, k_col),
            _kv_block_spec(seq, v_col),
            pl.BlockSpec((seq + SEL_STEP, 2 * LANES), lambda b, g, c: (0, 0), pipeline_mode=pl.Buffered(1)),
            pl.BlockSpec((rows, LANES), lambda b, g, c: (b * nstep + c, 0)),
            pl.BlockSpec((1, LANES, 3 * gw), lambda b, g, c: (g, 0, 0)),
            pl.BlockSpec((rows, gw), blk),
            pl.BlockSpec((rows, gw), blk),
        ],
        out_specs=pl.BlockSpec((rows, gw), blk),
        out_shape=jax.ShapeDtypeStruct((batch * seq, N_HEADS * HEAD_DIM), MXU_DTYPE),
        scratch_shapes=[
            pltpu.VMEM((srows, 2 * LANES), MXU_DTYPE),
            pltpu.VMEM((srows, LANES), F32),
            pltpu.VMEM((srows, 2 * LANES), F32),
            pltpu.VMEM((srows, SEL_STEP), F32),
            pltpu.VMEM((srows, SEL_STEP), F32),
            pltpu.VMEM((srows, SEL_STEP), MXU_DTYPE),
            pltpu.VMEM((srows, SEL_STEP), MXU_DTYPE),
            pltpu.VMEM((srows, LANES), F32),
            pltpu.VMEM((srows, LANES), F32),
            pltpu.VMEM((seq + SEL_STEP, 2 * LANES), MXU_DTYPE),
            pltpu.VMEM((seq + SEL_STEP, 2 * LANES), MXU_DTYPE),
        ],
        compiler_params=_params("parallel", "parallel", "arbitrary"),
        name="nsa_sel_attention",
    )(deltas, sel_query_features(), qkv, sel, qkv, qkv, kfeat, gate, ex, o_cmp, o_win)


def _nsa_win_kernel(nslope_ref, q_ref, k_in, v_in, o_ref, qs_ref, bias_scr, k_ref, v_ref):
    grp = pl.program_id(1)
    c = pl.program_id(2)
    span = NSA_WINDOW + Q_BLOCK
    lead = NSA_WINDOW // Q_BLOCK
    for j in range(Q_SUB):
        _build_q_stack(q_ref, qs_ref.at[j], j * Q_BLOCK)
    nslope = lambda rb: nslope_ref[_head_of_row_block(grp, rb)]

    def write(j, outs):
        for p, o in enumerate(outs):
            o_ref[j * Q_BLOCK:(j + 1) * Q_BLOCK, p * LANES:(p + 1) * LANES] = o

    @pl.when(c == 0)
    def _():
        _unpack_kv(k_in, v_in, grp, k_ref, v_ref)
        d = _pos_tiles(NSA_WINDOW, 0, span)
        for rb in range(GROUP):
            bias_scr[rb] = _band_bias(nslope(rb), d, NSA_WINDOW)

    def leading_block(j):
        d = _pos_tiles((c * Q_SUB + j) * Q_BLOCK, 0, span)
        write(j, _window_attend(qs_ref.at[j], k_ref[0:span, :], v_ref[0:span, :],
                                lambda rb: _band_bias(nslope(rb), d, NSA_WINDOW)))

    def later_block(j):
        start = pl.multiple_of((c * Q_SUB + j) * Q_BLOCK - NSA_WINDOW, Q_BLOCK)
        write(j, _window_attend(qs_ref.at[j], k_ref[pl.ds(start, span), :], v_ref[pl.ds(start, span), :],
                                lambda rb: bias_scr[rb]))

    lead_steps = max(lead // Q_SUB, 1)

    @pl.when(c < lead_steps)
    def _():
        for j in range(Q_SUB):
            if j < lead:
                leading_block(j)
            else:
                later_block(j)

    @pl.when(c >= lead_steps)
    def _():
        for j in range(Q_SUB):
            later_block(j)


def nsa_win_attention(qkv, nslopes, batch, seq, k_col, v_col):
    nstep = seq // (Q_SUB * Q_BLOCK)
    rows = Q_SUB * Q_BLOCK
    gw = PAIRS * LANES
    kd = KV_GROUPS * HEAD_DIM
    span = NSA_WINDOW + Q_BLOCK
    return pl.pallas_call(
        _nsa_win_kernel,
        grid=(batch, KV_GROUPS, nstep),
        in_specs=[
            pl.BlockSpec(memory_space=pltpu.SMEM),
            pl.BlockSpec((rows, gw), lambda b, g, c: (b * nstep + c, g)),
            _kv_block_spec(seq, k_col),
            _kv_block_spec(seq, v_col),
        ],
        out_specs=pl.BlockSpec((rows, gw), lambda b, g, c: (b * nstep + c, g)),
        out_shape=jax.ShapeDtypeStruct((batch * seq, N_HEADS * HEAD_DIM), F32),
        scratch_shapes=[
            pltpu.VMEM((Q_SUB, ROWS, LANES), MXU_DTYPE),
            pltpu.VMEM((GROUP, Q_BLOCK, span), F32),
            pltpu.VMEM((seq, LANES), MXU_DTYPE),
            pltpu.VMEM((seq, 2 * LANES), MXU_DTYPE),
        ],
        compiler_params=_params("parallel", "parallel", "arbitrary"),
        name="nsa_win_attention",
    )(nslopes, qkv, qkv, qkv)


def _overlap_matrix(seq):
    ncp = seq // CMP_STRIDE
    n_cmp = (seq - CMP_LEN) // CMP_STRIDE + 1
    cs = np.arange(n_cmp) * CMP_STRIDE
    ss = np.arange(seq // SEL_LEN) * SEL_LEN
    ov = (cs[:, None] < ss[None, :] + SEL_LEN) & (cs[:, None] + CMP_LEN > ss[None, :])
    out = np.zeros((ncp, LANES), np.float32)
    out[:n_cmp, :seq // SEL_LEN] = ov
    return out


def _swa_layer(h, x, w_in, b_in, sinks, w_o, b_o, g_post, g_next, nslopes, batch, seq):
    hd = N_HEADS * HEAD_DIM
    kd = KV_GROUPS * HEAD_DIM
    qkv = matmul_bias(h, w_in.astype(MXU_DTYPE), b_in, MXU_DTYPE, name="swa_in_proj")
    o = swa_attention(qkv, nslopes, sinks.astype(F32) * LOG2E, batch, seq)
    return matmul_norm_res(o, w_o.astype(MXU_DTYPE), b_o, g_post, g_next, x, tn=w_o.shape[1], name="swa_out_proj")


def _nsa_layer(h, x, w_in, cmp_pe, cmp_w1, cmp_b1, cmp_w2, cmp_b2, w_o, g_post, g_next, nslopes, batch, seq):
    hd = N_HEADS * HEAD_DIM
    kd = KV_GROUPS * HEAD_DIM
    t = batch * seq
    ncp = seq // CMP_STRIDE
    qkv = matmul_bias(h, w_in[:, :hd + 6 * kd].astype(MXU_DTYPE), jnp.zeros((hd + 6 * kd,), F32), MXU_DTYPE,
                      name="nsa_in_proj")
    n_gate = 3 * N_HEADS
    w_gate = jnp.pad(w_in[:, hd + 6 * kd:], ((0, 0), (0, LANES - n_gate))).astype(MXU_DTYPE)
    gate = matmul_bias(h, w_gate, jnp.zeros((LANES,), F32), F32, name="nsa_gate_proj")

    half = CMP_STRIDE * HEAD_DIM
    z = qkv[:, hd:hd + 2 * kd].reshape(batch, seq, 2, KV_GROUPS, HEAD_DIM).transpose(2, 0, 3, 1, 4).reshape(
        2, batch, KV_GROUPS, ncp, half)
    cmp_out = compress(z, cmp_pe.reshape(2, 2, half).astype(F32), cmp_w1.astype(MXU_DTYPE),
                       cmp_b1.reshape(2, 1, -1), cmp_w2.astype(MXU_DTYPE), cmp_b2.reshape(2, 1, -1))
    kcm = cmp_out[0].astype(MXU_DTYPE)
    vcm = cmp_out[1].astype(MXU_DTYPE)
    kc2 = jnp.concatenate([kcm, kcm], axis=-1)
    ov = jnp.broadcast_to(jnp.asarray(_overlap_matrix(seq), MXU_DTYPE), (batch, KV_GROUPS, ncp, LANES))
    vc_aug = jnp.concatenate([vcm, vcm, jnp.ones((batch, KV_GROUPS, ncp, LANES), MXU_DTYPE), ov], axis=-1)

    o_cmp, sel = nsa_cmp_select(qkv, kc2, vc_aug, nslopes, batch, seq, 0)
    kv_col = lambda i: hd // kd + i
    o_win = nsa_win_attention(qkv, nslopes, batch, seq, kv_col(4), kv_col(5))
    o = nsa_sel_attention(qkv, sel, gate, o_cmp, o_win, batch, seq, kv_col(2), kv_col(3))
    return matmul_norm_res(o, w_o.astype(MXU_DTYPE), jnp.zeros((w_o.shape[1],), F32), g_post, g_next, x,
                           tn=w_o.shape[1], name="nsa_out_proj")


def kernel(x, norm_g, swa_w_in, swa_b_in, swa_sinks, swa_w_o, swa_b_o, nsa_w_in, nsa_cmp_pe, nsa_cmp_w1, nsa_cmp_b1, nsa_cmp_w2, nsa_cmp_b2, nsa_w_o, ffn_w_gate, ffn_w_up, ffn_conv_w, ffn_conv_b, ffn_w_down):
    batch, seq, d = x.shape
    depth = norm_g.shape[0]
    nslopes = jnp.asarray((_alibi_neg_slopes().astype(np.float64) * LOG2E).astype(np.float32))
    xf = x.reshape(batch * seq, d)
    h = rms_cast(xf, norm_g[0, 0])
    for i in range(depth):
        g = norm_g[i]
        j = i // 2
        if i % 2 == 0:
            xf, h = _swa_layer(h, xf, swa_w_in[j], swa_b_in[j], swa_sinks[j], swa_w_o[j], swa_b_o[j],
                               g[1], g[2], nslopes, batch, seq)
        else:
            xf, h = _nsa_layer(h, xf, nsa_w_in[j], nsa_cmp_pe[j], nsa_cmp_w1[j], nsa_cmp_b1[j], nsa_cmp_w2[j],
                               nsa_cmp_b2[j], nsa_w_o[j], g[1], g[2], nslopes, batch, seq)
        act = ffn_up(h, ffn_w_gate, ffn_w_up, i, ffn_conv_w[i], ffn_conv_b[i], seq)
        g_next = norm_g[i + 1, 0] if i + 1 < depth else jnp.ones((d,), F32)
        xf, h = matmul_norm_res(act, ffn_w_down[i].astype(MXU_DTYPE), jnp.zeros((d,), F32), g[3], g_next, xf,
                                tn=512, name="ffn_down")
    return xf.reshape(batch, seq, d)
```

```python
import functools

import numpy as np
import jax
import jax.numpy as jnp
from jax import lax
from jax.experimental import pallas as pl
from jax.experimental.pallas import tpu as pltpu

F32 = jnp.float32
MXU_DTYPE = jnp.bfloat16

N_HEADS = 32
HEAD_DIM = 64
KV_GROUPS = 4
GROUP = N_HEADS // KV_GROUPS
PAIRS = GROUP // 2
LANES = 128
Q_BLOCK = 128
ROWS = GROUP * Q_BLOCK
SWA_WINDOW = 128
CMP_LEN = 32
CMP_STRIDE = 16
SEL_LEN = 64
SEL_TOPK = 16
SEL_FORCED = 3
LOG2E = 1.4426950408889634
SEL_STEP = 512
SEL_Q = 2
NSA_WINDOW = 512
CONV_WIDTH = 3
RMS_EPS = 1e-6
MASK_VALUE = -1e30
VMEM_LIMIT = 60000 * 1024


def _params(*sem):
    return pltpu.CompilerParams(dimension_semantics=sem, vmem_limit_bytes=VMEM_LIMIT)


def _alibi_neg_slopes():
    return (-np.exp2(-8.0 * np.arange(1, N_HEADS + 1, dtype=np.float64) / N_HEADS)).astype(np.float32)


def _rms_cast_kernel(x_ref, g_ref, o_ref):
    x = x_ref[...]
    ms = jnp.mean(x * x, axis=-1, keepdims=True)
    o_ref[...] = (x * lax.rsqrt(ms + RMS_EPS) * g_ref[...]).astype(o_ref.dtype)


def rms_cast(x, g, tm=512):
    t, d = x.shape
    return pl.pallas_call(
        _rms_cast_kernel,
        grid=(t // tm,),
        in_specs=[pl.BlockSpec((tm, d), lambda i: (i, 0)), pl.BlockSpec((1, d), lambda i: (0, 0))],
        out_specs=pl.BlockSpec((tm, d), lambda i: (i, 0)),
        out_shape=jax.ShapeDtypeStruct((t, d), MXU_DTYPE),
        compiler_params=_params("parallel"),
        name="rms_cast",
    )(x, g.reshape(1, d))


def _matmul_kernel(a_ref, w_ref, b_ref, o_ref):
    acc = jnp.dot(a_ref[...], w_ref[...], preferred_element_type=F32)
    o_ref[...] = (acc + b_ref[...]).astype(o_ref.dtype)


def matmul_bias(a, w, b, out_dtype, tm=2048, tn=512, name="matmul_bias"):
    t, k = a.shape
    n = w.shape[1]
    tn = min(tn, n)
    return pl.pallas_call(
        _matmul_kernel,
        grid=(t // tm, n // tn),
        in_specs=[
            pl.BlockSpec((tm, k), lambda i, j: (i, 0)),
            pl.BlockSpec((k, tn), lambda i, j: (0, j)),
            pl.BlockSpec((1, tn), lambda i, j: (0, j)),
        ],
        out_specs=pl.BlockSpec((tm, tn), lambda i, j: (i, j)),
        out_shape=jax.ShapeDtypeStruct((t, n), out_dtype),
        compiler_params=_params("parallel", "arbitrary"),
        name=name,
    )(a, w, b.reshape(1, n))


def _mm_norm_res_kernel(a_ref, w_ref, b_ref, gpost_ref, gnext_ref, x_ref, xo_ref, ho_ref, y_scr, *, nj, tn, n):
    j = pl.program_id(1)
    y_scr[j] = jnp.dot(a_ref[...], w_ref[...], preferred_element_type=F32) + b_ref[...]

    @pl.when(j == nj - 1)
    def _():
        ss = jnp.sum(y_scr[0] * y_scr[0], axis=1, keepdims=True)
        for jj in range(1, nj):
            ss = ss + jnp.sum(y_scr[jj] * y_scr[jj], axis=1, keepdims=True)
        r = lax.rsqrt(ss / n + RMS_EPS)
        ss2 = jnp.zeros_like(ss)
        for jj in range(nj):
            cols = slice(jj * tn, (jj + 1) * tn)
            xn = x_ref[:, cols] + y_scr[jj] * r * gpost_ref[:, cols]
            xo_ref[:, cols] = xn
            ss2 = ss2 + jnp.sum(xn * xn, axis=1, keepdims=True)
        r2 = lax.rsqrt(ss2 / n + RMS_EPS)
        for jj in range(nj):
            cols = slice(jj * tn, (jj + 1) * tn)
            ho_ref[:, cols] = (xo_ref[:, cols] * r2 * gnext_ref[:, cols]).astype(ho_ref.dtype)


def matmul_norm_res(a, w, b, g_post, g_next, x, tm=512, tn=512, name="matmul_norm_res"):
    t, k = a.shape
    n = w.shape[1]
    nj = n // tn
    kern = functools.partial(_mm_norm_res_kernel, nj=nj, tn=tn, n=n)
    return pl.pallas_call(
        kern,
        grid=(t // tm, nj),
        in_specs=[
            pl.BlockSpec((tm, k), lambda i, j: (i, 0)),
            pl.BlockSpec((k, tn), lambda i, j: (0, j)),
            pl.BlockSpec((1, tn), lambda i, j: (0, j)),
            pl.BlockSpec((1, n), lambda i, j: (0, 0)),
            pl.BlockSpec((1, n), lambda i, j: (0, 0)),
            pl.BlockSpec((tm, n), lambda i, j: (i, 0)),
        ],
        out_specs=[
            pl.BlockSpec((tm, n), lambda i, j: (i, 0)),
            pl.BlockSpec((tm, n), lambda i, j: (i, 0)),
        ],
        out_shape=[jax.ShapeDtypeStruct((t, n), F32), jax.ShapeDtypeStruct((t, n), MXU_DTYPE)],
        scratch_shapes=[pltpu.VMEM((nj, tm, tn), F32)],
        compiler_params=_params("parallel", "arbitrary"),
        name=name,
    )(a, w, b.reshape(1, n), g_post.reshape(1, n), g_next.reshape(1, n), x)


FIX_ROWS = 16
CARRY_ROWS = 8


def _ffn_up_kernel(h_ref, wg_ref, wu_ref, cw_ref, cb_ref, o_ref, wg_scr, wu_scr, carry_scr, *, tm, tiles_per_seq):
    i = pl.program_id(1)

    @pl.when(i == 0)
    def _():
        wg_scr[...] = wg_ref[...].astype(wg_scr.dtype)
        wu_scr[...] = wu_ref[...].astype(wu_scr.dtype)

    h = h_ref[...]
    gate = jnp.dot(h, wg_scr[...], preferred_element_type=F32)
    up = jnp.dot(h, wu_scr[...], preferred_element_type=F32)
    w0 = cw_ref[0:1, :]
    w1 = cw_ref[1:2, :]
    w2 = cw_ref[2:3, :]
    b = cb_ref[...]
    a = b + pltpu.roll(gate, 2, axis=0) * w0
    a = a + pltpu.roll(gate, 1, axis=0) * w1
    a = a + gate * w2
    o_ref[...] = (jax.nn.silu(a) * up).astype(o_ref.dtype)

    seq_start = (i % tiles_per_seq) == 0
    prev = jnp.where(seq_start, 0.0, carry_scr[...])
    head = gate[0:FIX_ROWS]
    ext = jnp.concatenate([prev, head], axis=0)
    af = b + ext[CARRY_ROWS - 2:CARRY_ROWS - 2 + FIX_ROWS] * w0
    af = af + ext[CARRY_ROWS - 1:CARRY_ROWS - 1 + FIX_ROWS] * w1
    af = af + head * w2
    o_ref[0:FIX_ROWS, :] = (jax.nn.silu(af) * up[0:FIX_ROWS]).astype(o_ref.dtype)
    carry_scr[...] = gate[tm - CARRY_ROWS:tm]


def ffn_up(h, wg, wu, layer, conv_w, conv_b, seq, tm=1024, tn=512):
    t, k = h.shape
    n = wg.shape[2]
    kern = functools.partial(_ffn_up_kernel, tm=tm, tiles_per_seq=seq // tm)
    return pl.pallas_call(
        kern,
        grid=(n // tn, t // tm),
        in_specs=[
            pl.BlockSpec((tm, k), lambda j, i: (i, 0)),
            pl.BlockSpec((None, k, tn), lambda j, i: (layer, 0, j)),
            pl.BlockSpec((None, k, tn), lambda j, i: (layer, 0, j)),
            pl.BlockSpec((CONV_WIDTH, tn), lambda j, i: (0, j)),
            pl.BlockSpec((1, tn), lambda j, i: (0, j)),
        ],
        out_specs=pl.BlockSpec((tm, tn), lambda j, i: (i, j)),
        out_shape=jax.ShapeDtypeStruct((t, n), MXU_DTYPE),
        scratch_shapes=[
            pltpu.VMEM((k, tn), MXU_DTYPE),
            pltpu.VMEM((k, tn), MXU_DTYPE),
            pltpu.VMEM((CARRY_ROWS, tn), F32),
        ],
        compiler_params=_params("arbitrary", "arbitrary"),
        name="ffn_up",
    )(h, wg, wu, conv_w, conv_b.reshape(1, n))


def _head_of_row_block(group, rb):
    return group * GROUP + 2 * (rb % PAIRS) + rb // PAIRS


Q_SUB = 8


def _build_q_stack(q_ref, qs_ref, row0=0):
    lane = lax.broadcasted_iota(jnp.int32, (Q_BLOCK, LANES), 1)
    even = lane < HEAD_DIM
    for p in range(PAIRS):
        qp = q_ref[row0:row0 + Q_BLOCK, p * LANES:(p + 1) * LANES].astype(F32) * (HEAD_DIM ** -0.5 * LOG2E)
        qs_ref[p * Q_BLOCK:(p + 1) * Q_BLOCK, :] = jnp.where(even, qp, 0.0).astype(qs_ref.dtype)
        qs_ref[(PAIRS + p) * Q_BLOCK:(PAIRS + p + 1) * Q_BLOCK, :] = jnp.where(even, 0.0, qp).astype(qs_ref.dtype)


def _rep(x, size):
    return x if size == LANES else jnp.concatenate([x] * (size // LANES), axis=1)


def _pairs(num, den):
    lane = lax.broadcasted_iota(jnp.int32, (Q_BLOCK, LANES), 1)
    even = lane < HEAD_DIM
    outs = []
    for p in range(PAIRS):
        oe = num(p) / jnp.maximum(den(p), 1e-30)
        oo = num(PAIRS + p) / jnp.maximum(den(PAIRS + p), 1e-30)
        outs.append(jnp.where(even, oe, oo))
    return outs


def _rows(rb):
    return slice(rb * Q_BLOCK, (rb + 1) * Q_BLOCK)


def _band_bias(nslope, d, window):
    return jnp.where((d >= 0) & (d < window), nslope * d.astype(F32), MASK_VALUE)


def _window_attend(qs_ref, k, v, bias, extra_logit=None):
    half = ROWS // 2
    dot_nt = (((1,), (1,)), ((), ()))
    s_halves = [lax.dot_general(qs_ref[hh * half:(hh + 1) * half, :], k, dot_nt, preferred_element_type=F32)
                for hh in range(2)]
    ps, extras = [], []
    for rb in range(GROUP):
        lo = (rb % PAIRS) * Q_BLOCK
        s = s_halves[rb // PAIRS][lo:lo + Q_BLOCK] + bias(rb)
        m = jnp.max(s, axis=1, keepdims=True)
        if extra_logit is not None:
            m = jnp.maximum(m, extra_logit(rb))
            extras.append(jnp.exp2(extra_logit(rb) - m))
        ps.append(jnp.exp2(s - m).astype(MXU_DTYPE))
    r_halves = [jnp.dot(jnp.concatenate(ps[hh * PAIRS:(hh + 1) * PAIRS], axis=0), v, preferred_element_type=F32)
                for hh in range(2)]

    def part(rb, cols):
        lo = (rb % PAIRS) * Q_BLOCK
        return r_halves[rb // PAIRS][lo:lo + Q_BLOCK, cols]

    num = lambda rb: part(rb, slice(0, LANES))
    if extra_logit is None:
        den = lambda rb: part(rb, slice(LANES, 2 * LANES))
    else:
        den = lambda rb: part(rb, slice(LANES, 2 * LANES)) + extras[rb]
    return _pairs(num, den)


KV_CHUNK = 1024


def _unpack_group(kv_ref, grp, write_chunk, fill_ref=None):
    n = kv_ref.shape[0]
    for odd in range(2):
        def fill(odd=odd):
            def body(i, carry):
                r0 = pl.multiple_of(i * KV_CHUNK, KV_CHUNK)
                x = kv_ref[pl.ds(r0, KV_CHUNK), :]
                xi = pltpu.bitcast(x, jnp.int32)
                xr = pltpu.roll(xi, HEAD_DIM, axis=1)
                low = lax.broadcasted_iota(jnp.int32, xi.shape, 1) < HEAD_DIM
                if fill_ref is None:
                    y = jnp.where(low, xr, xi) if odd else jnp.where(low, xi, xr)
                else:
                    f = pltpu.bitcast(fill_ref[pl.ds(r0, KV_CHUNK), 0:LANES], jnp.int32)
                    y = jnp.where(low, xr if odd else xi, f)
                write_chunk(r0, pltpu.bitcast(y, x.dtype))
                return carry

            lax.fori_loop(0, n // KV_CHUNK, body, 0)

        pl.when(grp % 2 == odd)(fill)


def _kv_block_spec(seq, col):
    pairs_per_tensor = KV_GROUPS * HEAD_DIM // LANES
    return pl.BlockSpec((seq, LANES), lambda b, g, c: (b, pairs_per_tensor * col + g // 2))


def _pos_tiles(t0, start, size):
    qi = lax.broadcasted_iota(jnp.int32, (Q_BLOCK, size), 0)
    ki = lax.broadcasted_iota(jnp.int32, (Q_BLOCK, size), 1)
    return (t0 - start) + (qi - ki)


def _unpack_kv(k_in, v_in, grp, k_ref, v_ref):
    def put_k(r0, y):
        k_ref[pl.ds(r0, KV_CHUNK), :] = y

    def put_v(r0, y):
        v_ref[pl.ds(r0, KV_CHUNK), 0:LANES] = y
        v_ref[pl.ds(r0, KV_CHUNK), LANES:2 * LANES] = jnp.ones((KV_CHUNK, LANES), v_ref.dtype)

    _unpack_group(k_in, grp, put_k)
    _unpack_group(v_in, grp, put_v)


def _swa_kernel(nslope_ref, sink_ref, q_ref, k_in, v_in, o_ref, qs_ref, bias_scr, k_ref, v_ref):
    grp = pl.program_id(1)
    c = pl.program_id(2)
    span = SWA_WINDOW + Q_BLOCK
    for j in range(Q_SUB):
        _build_q_stack(q_ref, qs_ref.at[j], j * Q_BLOCK)
    sink = lambda rb: sink_ref[_head_of_row_block(grp, rb)]

    def write(j, outs):
        for p, o in enumerate(outs):
            o_ref[j * Q_BLOCK:(j + 1) * Q_BLOCK, p * LANES:(p + 1) * LANES] = o.astype(o_ref.dtype)

    def later_block(j):
        start = pl.multiple_of((c * Q_SUB + j) * Q_BLOCK - SWA_WINDOW, Q_BLOCK)
        write(j, _window_attend(qs_ref.at[j], k_ref[pl.ds(start, span), :], v_ref[pl.ds(start, span), :],
                                lambda rb: bias_scr[rb], sink))

    @pl.when(c == 0)
    def _():
        _unpack_kv(k_in, v_in, grp, k_ref, v_ref)
        d = _pos_tiles(SWA_WINDOW, 0, span)
        for rb in range(GROUP):
            bias_scr[rb] = _band_bias(nslope_ref[_head_of_row_block(grp, rb)], d, SWA_WINDOW)

    @pl.when(c == 0)
    def _():
        write(0, _window_attend(qs_ref.at[0], k_ref[0:Q_BLOCK, :], v_ref[0:Q_BLOCK, :],
                                lambda rb: bias_scr[rb, :, SWA_WINDOW:span], sink))
        for j in range(1, Q_SUB):
            later_block(j)

    @pl.when(c > 0)
    def _():
        for j in range(Q_SUB):
            later_block(j)


def swa_attention(qkv, nslopes, sinks, batch, seq):
    nstep = seq // (Q_SUB * Q_BLOCK)
    rows = Q_SUB * Q_BLOCK
    gw = PAIRS * LANES
    kd = KV_GROUPS * HEAD_DIM
    span = SWA_WINDOW + Q_BLOCK
    k_col = N_HEADS * HEAD_DIM // kd
    return pl.pallas_call(
        _swa_kernel,
        grid=(batch, KV_GROUPS, nstep),
        in_specs=[
            pl.BlockSpec(memory_space=pltpu.SMEM),
            pl.BlockSpec(memory_space=pltpu.SMEM),
            pl.BlockSpec((rows, gw), lambda b, g, c: (b * nstep + c, g)),
            _kv_block_spec(seq, k_col),
            _kv_block_spec(seq, k_col + 1),
        ],
        out_specs=pl.BlockSpec((rows, gw), lambda b, g, c: (b * nstep + c, g)),
        out_shape=jax.ShapeDtypeStruct((batch * seq, N_HEADS * HEAD_DIM), MXU_DTYPE),
        scratch_shapes=[
            pltpu.VMEM((Q_SUB, ROWS, LANES), MXU_DTYPE),
            pltpu.VMEM((GROUP, Q_BLOCK, span), F32),
            pltpu.VMEM((seq, LANES), MXU_DTYPE),
            pltpu.VMEM((seq, 2 * LANES), MXU_DTYPE),
        ],
        compiler_params=_params("parallel", "parallel", "arbitrary"),
        name="swa_attention",
    )(nslopes, sinks, qkv, qkv, qkv)


def _compress_kernel(z_ref, pe_ref, w1_ref, b1_ref, w2_ref, b2_ref, o_ref, *, ncp):
    half = CMP_STRIDE * HEAD_DIM
    z = z_ref[0, 0, 0].astype(F32)
    top = (z + pe_ref[0, 0:1, :]).astype(MXU_DTYPE)
    bot = (z + pe_ref[0, 1:2, :]).astype(MXU_DTYPE)
    a = jnp.dot(top, w1_ref[0, 0:half, :], preferred_element_type=F32)
    bm = jnp.dot(bot, w1_ref[0, half:2 * half, :], preferred_element_type=F32)
    hid = a + pltpu.roll(bm, ncp - 1, axis=0) + b1_ref[0]
    act = jax.nn.gelu(hid).astype(MXU_DTYPE)
    o_ref[0, 0, 0] = jnp.dot(act, w2_ref[0], preferred_element_type=F32) + b2_ref[0]


def compress(z, pe, w1, b1, w2, b2):
    _, batch, groups, ncp, zw = z.shape
    hid = w1.shape[-1]
    kern = functools.partial(_compress_kernel, ncp=ncp)
    return pl.pallas_call(
        kern,
        grid=(2, batch, groups),
        in_specs=[
            pl.BlockSpec((1, 1, 1, ncp, zw), lambda s, b, g: (s, b, g, 0, 0)),
            pl.BlockSpec((1, 2, zw), lambda s, b, g: (s, 0, 0)),
            pl.BlockSpec((1, 2 * zw, hid), lambda s, b, g: (s, 0, 0)),
            pl.BlockSpec((1, 1, hid), lambda s, b, g: (s, 0, 0)),
            pl.BlockSpec((1, hid, HEAD_DIM), lambda s, b, g: (s, 0, 0)),
            pl.BlockSpec((1, 1, HEAD_DIM), lambda s, b, g: (s, 0, 0)),
        ],
        out_specs=pl.BlockSpec((1, 1, 1, ncp, HEAD_DIM), lambda s, b, g: (s, b, g, 0, 0)),
        out_shape=jax.ShapeDtypeStruct((2, batch, groups, ncp, HEAD_DIM), F32),
        compiler_params=_params("parallel", "parallel", "parallel"),
        name="nsa_compress",
    )(z, pe, w1, b1, w2, b2)


def _nsa_cmp_kernel(nslope_ref, q_ref, kc_ref, vc_ref, o_ref, sel_ref, qs_ref, e_scr, r_scr, *, ncp, n_cmp, n_sel):
    grp = pl.program_id(1)
    c = pl.program_id(2)
    starts = [(c * Q_SUB + j) * Q_BLOCK for j in range(Q_SUB)]
    for j in range(Q_SUB):
        _build_q_stack(q_ref, qs_ref.at[j], j * Q_BLOCK)

    def attend(width):
        for j, t0 in enumerate(starts):
            qi = lax.broadcasted_iota(jnp.int32, (Q_BLOCK, width), 0)
            ni = lax.broadcasted_iota(jnp.int32, (Q_BLOCK, width), 1)
            d = (t0 + qi) - (ni * CMP_STRIDE + (CMP_LEN - 1))
            negb = jnp.where((d >= 0) & (ni < n_cmp), 0.0, MASK_VALUE)
            dist = d.astype(F32)
            s_all = lax.dot_general(qs_ref[j], kc_ref[0, 0, 0:width, :], (((1,), (1,)), ((), ())),
                                    preferred_element_type=F32)
            for rb in range(GROUP):
                s = s_all[_rows(rb)] + nslope_ref[_head_of_row_block(grp, rb)] * dist + negb
                m = jnp.max(s, axis=1, keepdims=True)
                e_scr[j, _rows(rb), 0:width] = jnp.exp2(s - m).astype(e_scr.dtype)
            r_scr[j] = jnp.dot(e_scr[j, :, 0:width], vc_ref[0, 0, 0:width, :], preferred_element_type=F32)

    n_chunks = ncp // LANES
    need = jnp.minimum((starts[-1] + Q_BLOCK - CMP_LEN) // CMP_STRIDE // LANES + 1, n_chunks)
    for kq in range(1, n_chunks + 1):
        pl.when(need == kq)(functools.partial(attend, kq * LANES))

    lane = lax.broadcasted_iota(jnp.int32, (Q_BLOCK, LANES), 1)
    even = lane < HEAD_DIM
    ji = lax.broadcasted_iota(jnp.int32, (LANES, Q_BLOCK), 0)
    qt = lax.broadcasted_iota(jnp.int32, (LANES, Q_BLOCK), 1)
    neg_inf = -jnp.inf
    for j, t0 in enumerate(starts):
        orow = slice(j * Q_BLOCK, (j + 1) * Q_BLOCK)
        row_t = t0 + lax.broadcasted_iota(jnp.int32, (Q_BLOCK, LANES), 0)
        has_cmp = row_t >= (CMP_LEN - 1)
        imp = jnp.zeros((Q_BLOCK, LANES), F32)
        for p in range(PAIRS):
            re = _rows(p)
            ro = _rows(PAIRS + p)
            de = jnp.maximum(r_scr[j, re, LANES:2 * LANES], 1e-30)
            do = jnp.maximum(r_scr[j, ro, LANES:2 * LANES], 1e-30)
            o = jnp.where(even, r_scr[j, re, 0:LANES] / de, r_scr[j, ro, 0:LANES] / do)
            o_ref[orow, p * LANES:(p + 1) * LANES] = jnp.where(has_cmp, o, 0.0)
            imp = imp + r_scr[j, re, 2 * LANES:3 * LANES] / de + r_scr[j, ro, 2 * LANES:3 * LANES] / do
        imp = jnp.where(has_cmp, imp, 0.0)

        imp_t = imp.T
        cur = (t0 + qt) // SEL_LEN
        causal = ji <= cur
        forced = (ji == 0) | (ji == cur) | (ji == cur - 1)
        score = jnp.where(forced, neg_inf, jnp.where(causal, imp_t, MASK_VALUE))
        score = jnp.where(ji < n_sel, score, neg_inf)
        picked = jnp.where(forced, 1.0, 0.0)
        for _ in range(SEL_TOPK - SEL_FORCED):
            mx = jnp.max(score, axis=0, keepdims=True)
            first = jnp.min(jnp.where(score == mx, ji, LANES), axis=0, keepdims=True)
            hit = ji == first
            picked = jnp.where(hit, 1.0, picked)
            score = jnp.where(hit, neg_inf, score)
        picked = jnp.where(ji < cur, picked, 0.0)
        sel_ref[0, 0, orow, :] = picked.T.astype(sel_ref.dtype)


def nsa_cmp_select(q, kc2, vc_aug, nslopes, batch, seq, q_col_block):
    nstep = seq // (Q_SUB * Q_BLOCK)
    rows = Q_SUB * Q_BLOCK
    ncp = seq // CMP_STRIDE
    n_cmp = (seq - CMP_LEN) // CMP_STRIDE + 1
    n_sel = seq // SEL_LEN
    gw = PAIRS * LANES
    kern = functools.partial(_nsa_cmp_kernel, ncp=ncp, n_cmp=n_cmp, n_sel=n_sel)
    return pl.pallas_call(
        kern,
        grid=(batch, KV_GROUPS, nstep),
        in_specs=[
            pl.BlockSpec(memory_space=pltpu.SMEM),
            pl.BlockSpec((rows, gw), lambda b, g, c: (b * nstep + c, q_col_block + g)),
            pl.BlockSpec((1, 1, ncp, LANES), lambda b, g, c: (b, g, 0, 0)),
            pl.BlockSpec((1, 1, ncp, 3 * LANES), lambda b, g, c: (b, g, 0, 0)),
        ],
        out_specs=[
            pl.BlockSpec((rows, gw), lambda b, g, c: (b * nstep + c, g)),
            pl.BlockSpec((1, 1, rows, LANES), lambda b, g, c: (b, g, c, 0)),
        ],
        out_shape=[
            jax.ShapeDtypeStruct((batch * seq, N_HEADS * HEAD_DIM), F32),
            jax.ShapeDtypeStruct((batch, KV_GROUPS, seq, LANES), MXU_DTYPE),
        ],
        scratch_shapes=[pltpu.VMEM((Q_SUB, ROWS, LANES), MXU_DTYPE), pltpu.VMEM((Q_SUB, ROWS, ncp), MXU_DTYPE),
                        pltpu.VMEM((Q_SUB, ROWS, 3 * LANES), F32)],
        compiler_params=_params("parallel", "parallel", "arbitrary"),
        name="nsa_cmp_select",
    )(nslopes, q, kc2, vc_aug)


N_FEAT = 6


def sel_query_features():
    s = jnp.asarray((-_alibi_neg_slopes().astype(np.float64) * LOG2E).astype(np.float32))
    s1 = s.astype(MXU_DTYPE).astype(F32)
    s2 = (s - s1).astype(MXU_DTYPE).astype(F32)
    s3 = (s - s1 - s2).astype(MXU_DTYPE).astype(F32)
    feat = jnp.zeros((N_HEADS, LANES), F32).at[:, HEAD_DIM:HEAD_DIM + N_FEAT].set(
        jnp.stack([s1, s2, s3, s1, s2, s3], axis=1))
    feat = feat.at[:, HEAD_DIM + N_FEAT].set(MASK_VALUE)
    order = np.array([[_head_of_row_block(g, rb) for rb in range(GROUP)] for g in range(KV_GROUPS)])
    return feat[order]


def sel_key_features(seq):
    pos = np.arange(seq)
    kk = pos % SEL_STEP
    f = np.zeros((seq + SEL_STEP, 2 * LANES), np.float32)
    f[:seq, HEAD_DIM:HEAD_DIM + 3] = (SEL_LEN * (kk // SEL_LEN))[:, None]
    f[:seq, HEAD_DIM + 3:HEAD_DIM + 6] = (kk % SEL_LEN)[:, None]
    f[seq:, HEAD_DIM + N_FEAT] = 1.0
    f[pos, LANES + pos // SEL_LEN] = 1.0
    return f


def _nsa_sel_kernel(delta_ref, qfeat_ref, q_ref, sel_ref, k_in, v_in, kfeat_ref, gate_ref, ex_ref, oc_ref, ow_ref,
                    o_ref, qa_ref, m_ref, acc_ref, s_a, s_b, p_a, p_b, al_a, al_b, k_ref, v_ref):
    grp = pl.program_id(1)
    c = pl.program_id(2)
    t0 = c * (SEL_Q * Q_BLOCK)
    seq = k_in.shape[0]
    srows = lambda j, rb: slice(j * ROWS + rb * Q_BLOCK, j * ROWS + (rb + 1) * Q_BLOCK)
    qrows = lambda j: slice(j * Q_BLOCK, (j + 1) * Q_BLOCK)

    @pl.when(c == 0)
    def _():
        def put_k(r0, y):
            k_ref[pl.ds(r0, KV_CHUNK), 0:LANES] = y
            k_ref[pl.ds(r0, KV_CHUNK), LANES:2 * LANES] = kfeat_ref[pl.ds(r0, KV_CHUNK), LANES:2 * LANES]

        def put_v(r0, y):
            v_ref[pl.ds(r0, KV_CHUNK), 0:LANES] = y
            v_ref[pl.ds(r0, KV_CHUNK), LANES:2 * LANES] = jnp.ones((KV_CHUNK, LANES), v_ref.dtype)

        _unpack_group(k_in, grp, put_k, fill_ref=kfeat_ref)
        _unpack_group(v_in, grp, put_v)
        k_ref[seq:seq + SEL_STEP, :] = kfeat_ref[seq:seq + SEL_STEP, :]
        v_ref[seq:seq + SEL_STEP, :] = jnp.zeros((SEL_STEP, 2 * LANES), v_ref.dtype)
    lane = lax.broadcasted_iota(jnp.int32, (Q_BLOCK, LANES), 1)
    low = lane < HEAD_DIM
    for j in range(SEL_Q):
        selneg = ((1.0 - sel_ref[0, 0, qrows(j), :].astype(F32)) * MASK_VALUE).astype(qa_ref.dtype)
        for p in range(PAIRS):
            qp = q_ref[qrows(j), p * LANES:(p + 1) * LANES].astype(F32) * (HEAD_DIM ** -0.5 * LOG2E)
            for rb, src in ((p, qp), (PAIRS + p, pltpu.roll(qp, HEAD_DIM, axis=1))):
                qa_ref[srows(j, rb), 0:LANES] = jnp.where(low, src, qfeat_ref[0, rb:rb + 1, :]).astype(qa_ref.dtype)
                qa_ref[srows(j, rb), LANES:2 * LANES] = selneg
    n_steps = t0 // SEL_STEP + 1
    n_pad_step = seq // SEL_STEP
    dot_nt = (((1,), (1,)), ((), ()))

    def key_start(step):
        return pl.multiple_of(jnp.clip(step, 0, n_pad_step) * SEL_STEP, SEL_STEP)

    def scores(step, s_out):
        k = k_ref[pl.ds(key_start(step), SEL_STEP), :]
        s_out[...] = lax.dot_general(qa_ref[...], k, dot_nt, preferred_element_type=F32)

    def softmax(s_in, p_out, al_out):
        for rb in range(SEL_Q * GROUP):
            m_prev = m_ref[_rows(rb), :] - delta_ref[_head_of_row_block(grp, rb % GROUP)]
            m_new = jnp.maximum(m_prev, jnp.max(s_in[_rows(rb), :], axis=1, keepdims=True))
            al_out[_rows(rb), :] = jnp.exp2(m_prev - m_new)
            m_ref[_rows(rb), :] = m_new
        for rb in range(SEL_Q * GROUP):
            p_out[_rows(rb), :] = jnp.exp2(s_in[_rows(rb), :] - _rep(m_ref[_rows(rb), :], SEL_STEP)).astype(p_out.dtype)

    def values(step, p_in, al_in):
        v = v_ref[pl.ds(key_start(step), SEL_STEP), :]
        pv = jnp.dot(p_in[...], v, preferred_element_type=F32)
        alpha = al_in[...]
        acc_ref[...] = acc_ref[...] * jnp.concatenate([alpha, alpha], axis=1) + pv

    def even_half(t):
        scores(t, s_a)
        softmax(s_b, p_b, al_b)
        values(t - 2, p_a, al_a)

    def odd_half(t):
        scores(t, s_b)
        softmax(s_a, p_a, al_a)
        values(t - 2, p_b, al_b)

    scores(0, s_a)
    qi = lax.broadcasted_iota(jnp.int32, (Q_BLOCK, Q_BLOCK), 0)
    ki = lax.broadcasted_iota(jnp.int32, (Q_BLOCK, Q_BLOCK), 1)
    own_bias = jnp.where((ki <= qi) & (ki // SEL_LEN == qi // SEL_LEN), 0.0, MASK_VALUE)
    owns = [pl.multiple_of(t0 + j * Q_BLOCK, Q_BLOCK) for j in range(SEL_Q)]
    s_owns = [lax.dot_general(qa_ref[j * ROWS:(j + 1) * ROWS, 0:LANES], k_ref[pl.ds(owns[j], Q_BLOCK), 0:LANES],
                              dot_nt, preferred_element_type=F32) for j in range(SEL_Q)]
    scores(1, s_b)
    back = jnp.full((Q_BLOCK, LANES), n_steps, jnp.int32).astype(F32)
    p_owns = []
    for j in range(SEL_Q):
        p_own = []
        for rb in range(GROUP):
            s = s_owns[j][_rows(rb)] + own_bias
            m = jnp.max(s, axis=1, keepdims=True)
            p_own.append(jnp.exp2(s - m).astype(MXU_DTYPE))
            m_ref[srows(j, rb), :] = m + back * delta_ref[_head_of_row_block(grp, rb)]
        p_owns.append(jnp.concatenate(p_own, axis=0))
    softmax(s_a, p_a, al_a)
    for j in range(SEL_Q):
        acc_ref[j * ROWS:(j + 1) * ROWS, :] = jnp.dot(p_owns[j], v_ref[pl.ds(owns[j], Q_BLOCK), :],
                                                      preferred_element_type=F32)

    def run(t, count):
        for i in range(0, count, 2):
            even_half(t + i)
            odd_half(t + i + 1)

    def octo(j, carry):
        run(8 * j + 2, 8)
        return carry

    lax.fori_loop(0, n_steps // 8, octo, 0)
    rest4 = 8 * (n_steps // 8) + 2

    @pl.when(n_steps % 8 >= 4)
    def _():
        run(rest4, 4)

    rest = 4 * (n_steps // 4) + 2

    @pl.when(n_steps % 4 >= 2)
    def _():
        run(rest, 2)

    @pl.when(n_steps % 2 == 1)
    def _():
        even_half(n_steps + 1)

    gw = PAIRS * LANES
    sig = jax.nn.sigmoid(gate_ref[...])
    hi = sig.astype(MXU_DTYPE)
    lo = (sig - hi.astype(F32)).astype(MXU_DTYPE)
    ex = ex_ref[0]
    g = jnp.dot(hi, ex, preferred_element_type=F32) + jnp.dot(lo, ex, preferred_element_type=F32)
    for j in range(SEL_Q):
        o_sel = _pairs(lambda rb: acc_ref[srows(j, rb), 0:LANES], lambda rb: acc_ref[srows(j, rb), LANES:2 * LANES])
        for p in range(PAIRS):
            cols = slice(p * LANES, (p + 1) * LANES)
            out = (g[qrows(j), cols] * oc_ref[qrows(j), cols]
                   + g[qrows(j), gw + p * LANES:gw + (p + 1) * LANES] * o_sel[p]
                   + g[qrows(j), 2 * gw + p * LANES:2 * gw + (p + 1) * LANES] * ow_ref[qrows(j), cols])
            o_ref[qrows(j), cols] = out.astype(o_ref.dtype)


def _gate_expansion():
    gw = PAIRS * LANES
    ex = np.zeros((KV_GROUPS, LANES, 3 * gw), np.float32)
    for g in range(KV_GROUPS):
        for hl in range(GROUP):
            for i in range(3):
                ex[g, 3 * (g * GROUP + hl) + i, i * gw + hl * HEAD_DIM:i * gw + (hl + 1) * HEAD_DIM] = 1.0
    return ex


def nsa_sel_attention(qkv, sel, gate, o_cmp, o_win, batch, seq, k_col, v_col):
    nstep = seq // (SEL_Q * Q_BLOCK)
    rows = SEL_Q * Q_BLOCK
    srows = SEL_Q * ROWS
    gw = PAIRS * LANES
    deltas = jnp.asarray((-_alibi_neg_slopes().astype(np.float64) * LOG2E * SEL_STEP).astype(np.float32))
    ex = jnp.asarray(_gate_expansion(), MXU_DTYPE)
    kfeat = jnp.asarray(sel_key_features(seq), MXU_DTYPE)
    blk = lambda b, g, c: (b * nstep + c, g)
    return pl.pallas_call(
        _nsa_sel_kernel,
        grid=(batch, KV_GROUPS, nstep),
        in_specs=[
            pl.BlockSpec(memory_space=pltpu.SMEM),
            pl.BlockSpec((1, GROUP, LANES), lambda b, g, c: (g, 0, 0)),
            pl.BlockSpec((rows, gw), blk),
            pl.BlockSpec((1, 1, rows, LANES), lambda b, g, c: (b, g, c, 0)),
            _kv_block_spec(seq, k_col),
            _kv_block_spec(seq, v_col),
            pl.BlockSpec((seq + SEL_STEP, 2 * LANES), lambda b, g, c: (0, 0), pipeline_mode=pl.Buffered(1)),
            pl.BlockSpec((rows, LANES), lambda b, g, c: (b * nstep + c, 0)),
            pl.BlockSpec((1, LANES, 3 * gw), lambda b, g, c: (g, 0, 0)),
            pl.BlockSpec((rows, gw), blk),
            pl.BlockSpec((rows, gw), blk),
        ],
        out_specs=pl.BlockSpec((rows, gw), blk),
        out_shape=jax.ShapeDtypeStruct((batch * seq, N_HEADS * HEAD_DIM), MXU_DTYPE),
        scratch_shapes=[
            pltpu.VMEM((srows, 2 * LANES), MXU_DTYPE),
            pltpu.VMEM((srows, LANES), F32),
            pltpu.VMEM((srows, 2 * LANES), F32),
            pltpu.VMEM((srows, SEL_STEP), F32),
            pltpu.VMEM((srows, SEL_STEP), F32),
            pltpu.VMEM((srows, SEL_STEP), MXU_DTYPE),
            pltpu.VMEM((srows, SEL_STEP), MXU_DTYPE),
            pltpu.VMEM((srows, LANES), F32),
            pltpu.VMEM((srows, LANES), F32),
            pltpu.VMEM((seq + SEL_STEP, 2 * LANES), MXU_DTYPE),
            pltpu.VMEM((seq + SEL_STEP, 2 * LANES), MXU_DTYPE),
        ],
        compiler_params=_params("parallel", "parallel", "arbitrary"),
        name="nsa_sel_attention",
    )(deltas, sel_query_features(), qkv, sel, qkv, qkv, kfeat, gate, ex, o_cmp, o_win)


def _nsa_win_kernel(nslope_ref, q_ref, k_in, v_in, o_ref, qs_ref, bias_scr, k_ref, v_ref):
    grp = pl.program_id(1)
    c = pl.program_id(2)
    span = NSA_WINDOW + Q_BLOCK
    lead = NSA_WINDOW // Q_BLOCK
    for j in range(Q_SUB):
        _build_q_stack(q_ref, qs_ref.at[j], j * Q_BLOCK)
    nslope = lambda rb: nslope_ref[_head_of_row_block(grp, rb)]

    def write(j, outs):
        for p, o in enumerate(outs):
            o_ref[j * Q_BLOCK:(j + 1) * Q_BLOCK, p * LANES:(p + 1) * LANES] = o

    @pl.when(c == 0)
    def _():
        _unpack_kv(k_in, v_in, grp, k_ref, v_ref)
        d = _pos_tiles(NSA_WINDOW, 0, span)
        for rb in range(GROUP):
            bias_scr[rb] = _band_bias(nslope(rb), d, NSA_WINDOW)

    def leading_block(j):
        d = _pos_tiles((c * Q_SUB + j) * Q_BLOCK, 0, span)
        write(j, _window_attend(qs_ref.at[j], k_ref[0:span, :], v_ref[0:span, :],
                                lambda rb: _band_bias(nslope(rb), d, NSA_WINDOW)))

    def later_block(j):
        start = pl.multiple_of((c * Q_SUB + j) * Q_BLOCK - NSA_WINDOW, Q_BLOCK)
        write(j, _window_attend(qs_ref.at[j], k_ref[pl.ds(start, span), :], v_ref[pl.ds(start, span), :],
                                lambda rb: bias_scr[rb]))

    lead_steps = max(lead // Q_SUB, 1)

    @pl.when(c < lead_steps)
    def _():
        for j in range(Q_SUB):
            if j < lead:
                leading_block(j)
            else:
                later_block(j)

    @pl.when(c >= lead_steps)
    def _():
        for j in range(Q_SUB):
            later_block(j)


def nsa_win_attention(qkv, nslopes, batch, seq, k_col, v_col):
    nstep = seq // (Q_SUB * Q_BLOCK)
    rows = Q_SUB * Q_BLOCK
    gw = PAIRS * LANES
    kd = KV_GROUPS * HEAD_DIM
    span = NSA_WINDOW + Q_BLOCK
    return pl.pallas_call(
        _nsa_win_kernel,
        grid=(batch, KV_GROUPS, nstep),
        in_specs=[
            pl.BlockSpec(memory_space=pltpu.SMEM),
            pl.BlockSpec((rows, gw), lambda b, g, c: (b * nstep + c, g)),
            _kv_block_spec(seq, k_col),
            _kv_block_spec(seq, v_col),
        ],
        out_specs=pl.BlockSpec((rows, gw), lambda b, g, c: (b * nstep + c, g)),
        out_shape=jax.ShapeDtypeStruct((batch * seq, N_HEADS * HEAD_DIM), F32),
        scratch_shapes=[
            pltpu.VMEM((Q_SUB, ROWS, LANES), MXU_DTYPE),
            pltpu.VMEM((GROUP, Q_BLOCK, span), F32),
            pltpu.VMEM((seq, LANES), MXU_DTYPE),
            pltpu.VMEM((seq, 2 * LANES), MXU_DTYPE),
        ],
        compiler_params=_params("parallel", "parallel", "arbitrary"),
        name="nsa_win_attention",
    )(nslopes, qkv, qkv, qkv)


def _overlap_matrix(seq):
    ncp = seq // CMP_STRIDE
    n_cmp = (seq - CMP_LEN) // CMP_STRIDE + 1
    cs = np.arange(n_cmp) * CMP_STRIDE
    ss = np.arange(seq // SEL_LEN) * SEL_LEN
    ov = (cs[:, None] < ss[None, :] + SEL_LEN) & (cs[:, None] + CMP_LEN > ss[None, :])
    out = np.zeros((ncp, LANES), np.float32)
    out[:n_cmp, :seq // SEL_LEN] = ov
    return out


def _swa_layer(h, x, w_in, b_in, sinks, w_o, b_o, g_post, g_next, nslopes, batch, seq):
    hd = N_HEADS * HEAD_DIM
    kd = KV_GROUPS * HEAD_DIM
    qkv = matmul_bias(h, w_in.astype(MXU_DTYPE), b_in, MXU_DTYPE, name="swa_in_proj")
    o = swa_attention(qkv, nslopes, sinks.astype(F32) * LOG2E, batch, seq)
    return matmul_norm_res(o, w_o.astype(MXU_DTYPE), b_o, g_post, g_next, x, tn=w_o.shape[1], name="swa_out_proj")


def _nsa_layer(h, x, w_in, cmp_pe, cmp_w1, cmp_b1, cmp_w2, cmp_b2, w_o, g_post, g_next, nslopes, batch, seq):
    hd = N_HEADS * HEAD_DIM
    kd = KV_GROUPS * HEAD_DIM
    t = batch * seq
    ncp = seq // CMP_STRIDE
    qkv = matmul_bias(h, w_in[:, :hd + 6 * kd].astype(MXU_DTYPE), jnp.zeros((hd + 6 * kd,), F32), MXU_DTYPE,
                      name="nsa_in_proj")
    n_gate = 3 * N_HEADS
    w_gate = jnp.pad(w_in[:, hd + 6 * kd:], ((0, 0), (0, LANES - n_gate))).astype(MXU_DTYPE)
    gate = matmul_bias(h, w_gate, jnp.zeros((LANES,), F32), F32, name="nsa_gate_proj")

    half = CMP_STRIDE * HEAD_DIM
    z = qkv[:, hd:hd + 2 * kd].reshape(batch, seq, 2, KV_GROUPS, HEAD_DIM).transpose(2, 0, 3, 1, 4).reshape(
        2, batch, KV_GROUPS, ncp, half)
    cmp_out = compress(z, cmp_pe.reshape(2, 2, half).astype(F32), cmp_w1.astype(MXU_DTYPE),
                       cmp_b1.reshape(2, 1, -1), cmp_w2.astype(MXU_DTYPE), cmp_b2.reshape(2, 1, -1))
    kcm = cmp_out[0].astype(MXU_DTYPE)
    vcm = cmp_out[1].astype(MXU_DTYPE)
    kc2 = jnp.concatenate([kcm, kcm], axis=-1)
    ov = jnp.broadcast_to(jnp.asarray(_overlap_matrix(seq), MXU_DTYPE), (batch, KV_GROUPS, ncp, LANES))
    vc_aug = jnp.concatenate([vcm, vcm, jnp.ones((batch, KV_GROUPS, ncp, LANES), MXU_DTYPE), ov], axis=-1)

    o_cmp, sel = nsa_cmp_select(qkv, kc2, vc_aug, nslopes, batch, seq, 0)
    kv_col = lambda i: hd // kd + i
    o_win = nsa_win_attention(qkv, nslopes, batch, seq, kv_col(4), kv_col(5))
    o = nsa_sel_attention(qkv, sel, gate, o_cmp, o_win, batch, seq, kv_col(2), kv_col(3))
    return matmul_norm_res(o, w_o.astype(MXU_DTYPE), jnp.zeros((w_o.shape[1],), F32), g_post, g_next, x,
                           tn=w_o.shape[1], name="nsa_out_proj")


def kernel(x, norm_g, swa_w_in, swa_b_in, swa_sinks, swa_w_o, swa_b_o, nsa_w_in, nsa_cmp_pe, nsa_cmp_w1, nsa_cmp_b1, nsa_cmp_w2, nsa_cmp_b2, nsa_w_o, ffn_w_gate, ffn_w_up, ffn_conv_w, ffn_conv_b, ffn_w_down):
    batch, seq, d = x.shape
    depth = norm_g.shape[0]
    nslopes = jnp.asarray((_alibi_neg_slopes().astype(np.float64) * LOG2E).astype(np.float32))
    xf = x.reshape(batch * seq, d)
    h = rms_cast(xf, norm_g[0, 0])
    for i in range(depth):
        g = norm_g[i]
        j = i // 2
        if i % 2 == 0:
            xf, h = _swa_layer(h, xf, swa_w_in[j], swa_b_in[j], swa_sinks[j], swa_w_o[j], swa_b_o[j],
                               g[1], g[2], nslopes, batch, seq)
        else:
            xf, h = _nsa_layer(h, xf, nsa_w_in[j], nsa_cmp_pe[j], nsa_cmp_w1[j], nsa_cmp_b1[j], nsa_cmp_w2[j],
                               nsa_cmp_b2[j], nsa_w_o[j], g[1], g[2], nslopes, batch, seq)
        act = ffn_up(h, ffn_w_gate, ffn_w_up, i, ffn_conv_w[i], ffn_conv_b[i], seq)
        g_next = norm_g[i + 1, 0] if i + 1 < depth else jnp.ones((d,), F32)
        xf, h = matmul_norm_res(act, ffn_w_down[i].astype(MXU_DTYPE), jnp.zeros((d,), F32), g[3], g_next, xf,
                                tn=512, name="ffn_down")
    return xf.reshape(batch, seq, d)
```

```python
import functools

import numpy as np
import jax
import jax.numpy as jnp
from jax import lax
from jax.experimental import pallas as pl
from jax.experimental.pallas import tpu as pltpu

F32 = jnp.float32
MXU_DTYPE = jnp.bfloat16

N_HEADS = 32
HEAD_DIM = 64
KV_GROUPS = 4
GROUP = N_HEADS // KV_GROUPS
PAIRS = GROUP // 2
LANES = 128
Q_BLOCK = 128
ROWS = GROUP * Q_BLOCK
SWA_WINDOW = 128
CMP_LEN = 32
CMP_STRIDE = 16
SEL_LEN = 64
SEL_TOPK = 16
SEL_FORCED = 3
LOG2E = 1.4426950408889634
SEL_STEP = 512
SEL_Q = 2
NSA_WINDOW = 512
CONV_WIDTH = 3
RMS_EPS = 1e-6
MASK_VALUE = -1e30
VMEM_LIMIT = 60000 * 1024


def _params(*sem):
    return pltpu.CompilerParams(dimension_semantics=sem, vmem_limit_bytes=VMEM_LIMIT)


def _alibi_neg_slopes():
    return (-np.exp2(-8.0 * np.arange(1, N_HEADS + 1, dtype=np.float64) / N_HEADS)).astype(np.float32)


def _rms_cast_kernel(x_ref, g_ref, o_ref):
    x = x_ref[...]
    ms = jnp.mean(x * x, axis=-1, keepdims=True)
    o_ref[...] = (x * lax.rsqrt(ms + RMS_EPS) * g_ref[...]).astype(o_ref.dtype)


def rms_cast(x, g, tm=512):
    t, d = x.shape
    return pl.pallas_call(
        _rms_cast_kernel,
        grid=(t // tm,),
        in_specs=[pl.BlockSpec((tm, d), lambda i: (i, 0)), pl.BlockSpec((1, d), lambda i: (0, 0))],
        out_specs=pl.BlockSpec((tm, d), lambda i: (i, 0)),
        out_shape=jax.ShapeDtypeStruct((t, d), MXU_DTYPE),
        compiler_params=_params("parallel"),
        name="rms_cast",
    )(x, g.reshape(1, d))


def _matmul_kernel(a_ref, w_ref, b_ref, o_ref):
    acc = jnp.dot(a_ref[...], w_ref[...], preferred_element_type=F32)
    o_ref[...] = (acc + b_ref[...]).astype(o_ref.dtype)


def matmul_bias(a, w, b, out_dtype, tm=2048, tn=512, name="matmul_bias"):
    t, k = a.shape
    n = w.shape[1]
    tn = min(tn, n)
    return pl.pallas_call(
        _matmul_kernel,
        grid=(t // tm, n // tn),
        in_specs=[
            pl.BlockSpec((tm, k), lambda i, j: (i, 0)),
            pl.BlockSpec((k, tn), lambda i, j: (0, j)),
            pl.BlockSpec((1, tn), lambda i, j: (0, j)),
        ],
        out_specs=pl.BlockSpec((tm, tn), lambda i, j: (i, j)),
        out_shape=jax.ShapeDtypeStruct((t, n), out_dtype),
        compiler_params=_params("parallel", "arbitrary"),
        name=name,
    )(a, w, b.reshape(1, n))


def _mm_norm_res_kernel(a_ref, w_ref, b_ref, gpost_ref, gnext_ref, x_ref, xo_ref, ho_ref, y_scr, *, nj, tn, n):
    j = pl.program_id(1)
    y_scr[j] = jnp.dot(a_ref[...], w_ref[...], preferred_element_type=F32) + b_ref[...]

    @pl.when(j == nj - 1)
    def _():
        ss = jnp.sum(y_scr[0] * y_scr[0], axis=1, keepdims=True)
        for jj in range(1, nj):
            ss = ss + jnp.sum(y_scr[jj] * y_scr[jj], axis=1, keepdims=True)
        r = lax.rsqrt(ss / n + RMS_EPS)
        ss2 = jnp.zeros_like(ss)
        for jj in range(nj):
            cols = slice(jj * tn, (jj + 1) * tn)
            xn = x_ref[:, cols] + y_scr[jj] * r * gpost_ref[:, cols]
            xo_ref[:, cols] = xn
            ss2 = ss2 + jnp.sum(xn * xn, axis=1, keepdims=True)
        r2 = lax.rsqrt(ss2 / n + RMS_EPS)
        for jj in range(nj):
            cols = slice(jj * tn, (jj + 1) * tn)
            ho_ref[:, cols] = (xo_ref[:, cols] * r2 * gnext_ref[:, cols]).astype(ho_ref.dtype)


def matmul_norm_res(a, w, b, g_post, g_next, x, tm=512, tn=512, name="matmul_norm_res"):
    t, k = a.shape
    n = w.shape[1]
    nj = n // tn
    kern = functools.partial(_mm_norm_res_kernel, nj=nj, tn=tn, n=n)
    return pl.pallas_call(
        kern,
        grid=(t // tm, nj),
        in_specs=[
            pl.BlockSpec((tm, k), lambda i, j: (i, 0)),
            pl.BlockSpec((k, tn), lambda i, j: (0, j)),
            pl.BlockSpec((1, tn), lambda i, j: (0, j)),
            pl.BlockSpec((1, n), lambda i, j: (0, 0)),
            pl.BlockSpec((1, n), lambda i, j: (0, 0)),
            pl.BlockSpec((tm, n), lambda i, j: (i, 0)),
        ],
        out_specs=[
            pl.BlockSpec((tm, n), lambda i, j: (i, 0)),
            pl.BlockSpec((tm, n), lambda i, j: (i, 0)),
        ],
        out_shape=[jax.ShapeDtypeStruct((t, n), F32), jax.ShapeDtypeStruct((t, n), MXU_DTYPE)],
        scratch_shapes=[pltpu.VMEM((nj, tm, tn), F32)],
        compiler_params=_params("parallel", "arbitrary"),
        name=name,
    )(a, w, b.reshape(1, n), g_post.reshape(1, n), g_next.reshape(1, n), x)


FIX_ROWS = 16
CARRY_ROWS = 8


def _ffn_up_kernel(h_ref, wg_ref, wu_ref, cw_ref, cb_ref, o_ref, wg_scr, wu_scr, carry_scr, *, tm, tiles_per_seq):
    i = pl.program_id(1)

    @pl.when(i == 0)
    def _():
        wg_scr[...] = wg_ref[...].astype(wg_scr.dtype)
        wu_scr[...] = wu_ref[...].astype(wu_scr.dtype)

    h = h_ref[...]
    gate = jnp.dot(h, wg_scr[...], preferred_element_type=F32)
    up = jnp.dot(h, wu_scr[...], preferred_element_type=F32)
    w0 = cw_ref[0:1, :]
    w1 = cw_ref[1:2, :]
    w2 = cw_ref[2:3, :]
    b = cb_ref[...]
    a = b + pltpu.roll(gate, 2, axis=0) * w0
    a = a + pltpu.roll(gate, 1, axis=0) * w1
    a = a + gate * w2
    o_ref[...] = (jax.nn.silu(a) * up).astype(o_ref.dtype)

    seq_start = (i % tiles_per_seq) == 0
    prev = jnp.where(seq_start, 0.0, carry_scr[...])
    head = gate[0:FIX_ROWS]
    ext = jnp.concatenate([prev, head], axis=0)
    af = b + ext[CARRY_ROWS - 2:CARRY_ROWS - 2 + FIX_ROWS] * w0
    af = af + ext[CARRY_ROWS - 1:CARRY_ROWS - 1 + FIX_ROWS] * w1
    af = af + head * w2
    o_ref[0:FIX_ROWS, :] = (jax.nn.silu(af) * up[0:FIX_ROWS]).astype(o_ref.dtype)
    carry_scr[...] = gate[tm - CARRY_ROWS:tm]


def ffn_up(h, wg, wu, layer, conv_w, conv_b, seq, tm=1024, tn=512):
    t, k = h.shape
    n = wg.shape[2]
    kern = functools.partial(_ffn_up_kernel, tm=tm, tiles_per_seq=seq // tm)
    return pl.pallas_call(
        kern,
        grid=(n // tn, t // tm),
        in_specs=[
            pl.BlockSpec((tm, k), lambda j, i: (i, 0)),
            pl.BlockSpec((None, k, tn), lambda j, i: (layer, 0, j)),
            pl.BlockSpec((None, k, tn), lambda j, i: (layer, 0, j)),
            pl.BlockSpec((CONV_WIDTH, tn), lambda j, i: (0, j)),
            pl.BlockSpec((1, tn), lambda j, i: (0, j)),
        ],
        out_specs=pl.BlockSpec((tm, tn), lambda j, i: (i, j)),
        out_shape=jax.ShapeDtypeStruct((t, n), MXU_DTYPE),
        scratch_shapes=[
            pltpu.VMEM((k, tn), MXU_DTYPE),
            pltpu.VMEM((k, tn), MXU_DTYPE),
            pltpu.VMEM((CARRY_ROWS, tn), F32),
        ],
        compiler_params=_params("arbitrary", "arbitrary"),
        name="ffn_up",
    )(h, wg, wu, conv_w, conv_b.reshape(1, n))


def _head_of_row_block(group, rb):
    return group * GROUP + 2 * (rb % PAIRS) + rb // PAIRS


Q_SUB = 8


def _build_q_stack(q_ref, qs_ref, row0=0):
    lane = lax.broadcasted_iota(jnp.int32, (Q_BLOCK, LANES), 1)
    even = lane < HEAD_DIM
    for p in range(PAIRS):
        qp = q_ref[row0:row0 + Q_BLOCK, p * LANES:(p + 1) * LANES].astype(F32) * (HEAD_DIM ** -0.5 * LOG2E)
        qs_ref[p * Q_BLOCK:(p + 1) * Q_BLOCK, :] = jnp.where(even, qp, 0.0).astype(qs_ref.dtype)
        qs_ref[(PAIRS + p) * Q_BLOCK:(PAIRS + p + 1) * Q_BLOCK, :] = jnp.where(even, 0.0, qp).astype(qs_ref.dtype)


def _rep(x, size):
    return x if size == LANES else jnp.concatenate([x] * (size // LANES), axis=1)


def _pairs(num, den):
    lane = lax.broadcasted_iota(jnp.int32, (Q_BLOCK, LANES), 1)
    even = lane < HEAD_DIM
    outs = []
    for p in range(PAIRS):
        oe = num(p) / jnp.maximum(den(p), 1e-30)
        oo = num(PAIRS + p) / jnp.maximum(den(PAIRS + p), 1e-30)
        outs.append(jnp.where(even, oe, oo))
    return outs


def _rows(rb):
    return slice(rb * Q_BLOCK, (rb + 1) * Q_BLOCK)


def _band_bias(nslope, d, window):
    return jnp.where((d >= 0) & (d < window), nslope * d.astype(F32), MASK_VALUE)


def _window_attend(qs_ref, k, v, bias, extra_logit=None):
    half = ROWS // 2
    dot_nt = (((1,), (1,)), ((), ()))
    s_halves = [lax.dot_general(qs_ref[hh * half:(hh + 1) * half, :], k, dot_nt, preferred_element_type=F32)
                for hh in range(2)]
    ps, extras = [], []
    for rb in range(GROUP):
        lo = (rb % PAIRS) * Q_BLOCK
        s = s_halves[rb // PAIRS][lo:lo + Q_BLOCK] + bias(rb)
        m = jnp.max(s, axis=1, keepdims=True)
        if extra_logit is not None:
            m = jnp.maximum(m, extra_logit(rb))
            extras.append(jnp.exp2(extra_logit(rb) - m))
        ps.append(jnp.exp2(s - m).astype(MXU_DTYPE))
    r_halves = [jnp.dot(jnp.concatenate(ps[hh * PAIRS:(hh + 1) * PAIRS], axis=0), v, preferred_element_type=F32)
                for hh in range(2)]

    def part(rb, cols):
        lo = (rb % PAIRS) * Q_BLOCK
        return r_halves[rb // PAIRS][lo:lo + Q_BLOCK, cols]

    num = lambda rb: part(rb, slice(0, LANES))
    if extra_logit is None:
        den = lambda rb: part(rb, slice(LANES, 2 * LANES))
    else:
        den = lambda rb: part(rb, slice(LANES, 2 * LANES)) + extras[rb]
    return _pairs(num, den)


KV_CHUNK = 1024


def _unpack_group(kv_ref, grp, write_chunk, fill_ref=None):
    n = kv_ref.shape[0]
    for odd in range(2):
        def fill(odd=odd):
            def body(i, carry):
                r0 = pl.multiple_of(i * KV_CHUNK, KV_CHUNK)
                x = kv_ref[pl.ds(r0, KV_CHUNK), :]
                xi = pltpu.bitcast(x, jnp.int32)
                xr = pltpu.roll(xi, HEAD_DIM, axis=1)
                low = lax.broadcasted_iota(jnp.int32, xi.shape, 1) < HEAD_DIM
                if fill_ref is None:
                    y = jnp.where(low, xr, xi) if odd else jnp.where(low, xi, xr)
                else:
                    f = pltpu.bitcast(fill_ref[pl.ds(r0, KV_CHUNK), 0:LANES], jnp.int32)
                    y = jnp.where(low, xr if odd else xi, f)
                write_chunk(r0, pltpu.bitcast(y, x.dtype))
                return carry

            lax.fori_loop(0, n // KV_CHUNK, body, 0)

        pl.when(grp % 2 == odd)(fill)


def _kv_block_spec(seq, col):
    pairs_per_tensor = KV_GROUPS * HEAD_DIM // LANES
    return pl.BlockSpec((seq, LANES), lambda b, g, c: (b, pairs_per_tensor * col + g // 2))


def _pos_tiles(t0, start, size):
    qi = lax.broadcasted_iota(jnp.int32, (Q_BLOCK, size), 0)
    ki = lax.broadcasted_iota(jnp.int32, (Q_BLOCK, size), 1)
    return (t0 - start) + (qi - ki)


def _unpack_kv(k_in, v_in, grp, k_ref, v_ref):
    def put_k(r0, y):
        k_ref[pl.ds(r0, KV_CHUNK), :] = y

    def put_v(r0, y):
        v_ref[pl.ds(r0, KV_CHUNK), 0:LANES] = y
        v_ref[pl.ds(r0, KV_CHUNK), LANES:2 * LANES] = jnp.ones((KV_CHUNK, LANES), v_ref.dtype)

    _unpack_group(k_in, grp, put_k)
    _unpack_group(v_in, grp, put_v)


def _swa_kernel(nslope_ref, sink_ref, q_ref, k_in, v_in, o_ref, qs_ref, bias_scr, k_ref, v_ref):
    grp = pl.program_id(1)
    c = pl.program_id(2)
    span = SWA_WINDOW + Q_BLOCK
    for j in range(Q_SUB):
        _build_q_stack(q_ref, qs_ref.at[j], j * Q_BLOCK)
    sink = lambda rb: sink_ref[_head_of_row_block(grp, rb)]

    def write(j, outs):
        for p, o in enumerate(outs):
            o_ref[j * Q_BLOCK:(j + 1) * Q_BLOCK, p * LANES:(p + 1) * LANES] = o.astype(o_ref.dtype)

    def later_block(j):
        start = pl.multiple_of((c * Q_SUB + j) * Q_BLOCK - SWA_WINDOW, Q_BLOCK)
        write(j, _window_attend(qs_ref.at[j], k_ref[pl.ds(start, span), :], v_ref[pl.ds(start, span), :],
                                lambda rb: bias_scr[rb], sink))

    @pl.when(c == 0)
    def _():
        _unpack_kv(k_in, v_in, grp, k_ref, v_ref)
        d = _pos_tiles(SWA_WINDOW, 0, span)
        for rb in range(GROUP):
            bias_scr[rb] = _band_bias(nslope_ref[_head_of_row_block(grp, rb)], d, SWA_WINDOW)

    @pl.when(c == 0)
    def _():
        write(0, _window_attend(qs_ref.at[0], k_ref[0:Q_BLOCK, :], v_ref[0:Q_BLOCK, :],
                                lambda rb: bias_scr[rb, :, SWA_WINDOW:span], sink))
        for j in range(1, Q_SUB):
            later_block(j)

    @pl.when(c > 0)
    def _():
        for j in range(Q_SUB):
            later_block(j)


def swa_attention(qkv, nslopes, sinks, batch, seq):
    nstep = seq // (Q_SUB * Q_BLOCK)
    rows = Q_SUB * Q_BLOCK
    gw = PAIRS * LANES
    kd = KV_GROUPS * HEAD_DIM
    span = SWA_WINDOW + Q_BLOCK
    k_col = N_HEADS * HEAD_DIM // kd
    return pl.pallas_call(
        _swa_kernel,
        grid=(batch, KV_GROUPS, nstep),
        in_specs=[
            pl.BlockSpec(memory_space=pltpu.SMEM),
            pl.BlockSpec(memory_space=pltpu.SMEM),
            pl.BlockSpec((rows, gw), lambda b, g, c: (b * nstep + c, g)),
            _kv_block_spec(seq, k_col),
            _kv_block_spec(seq, k_col + 1),
        ],
        out_specs=pl.BlockSpec((rows, gw), lambda b, g, c: (b * nstep + c, g)),
        out_shape=jax.ShapeDtypeStruct((batch * seq, N_HEADS * HEAD_DIM), MXU_DTYPE),
        scratch_shapes=[
            pltpu.VMEM((Q_SUB, ROWS, LANES), MXU_DTYPE),
            pltpu.VMEM((GROUP, Q_BLOCK, span), F32),
            pltpu.VMEM((seq, LANES), MXU_DTYPE),
            pltpu.VMEM((seq, 2 * LANES), MXU_DTYPE),
        ],
        compiler_params=_params("parallel", "parallel", "arbitrary"),
        name="swa_attention",
    )(nslopes, sinks, qkv, qkv, qkv)


def _compress_kernel(z_ref, pe_ref, w1_ref, b1_ref, w2_ref, b2_ref, o_ref, *, ncp):
    half = CMP_STRIDE * HEAD_DIM
    z = z_ref[0, 0, 0].astype(F32)
    top = (z + pe_ref[0, 0:1, :]).astype(MXU_DTYPE)
    bot = (z + pe_ref[0, 1:2, :]).astype(MXU_DTYPE)
    a = jnp.dot(top, w1_ref[0, 0:half, :], preferred_element_type=F32)
    bm = jnp.dot(bot, w1_ref[0, half:2 * half, :], preferred_element_type=F32)
    hid = a + pltpu.roll(bm, ncp - 1, axis=0) + b1_ref[0]
    act = jax.nn.gelu(hid).astype(MXU_DTYPE)
    o_ref[0, 0, 0] = jnp.dot(act, w2_ref[0], preferred_element_type=F32) + b2_ref[0]


def compress(z, pe, w1, b1, w2, b2):
    _, batch, groups, ncp, zw = z.shape
    hid = w1.shape[-1]
    kern = functools.partial(_compress_kernel, ncp=ncp)
    return pl.pallas_call(
        kern,
        grid=(2, batch, groups),
        in_specs=[
            pl.BlockSpec((1, 1, 1, ncp, zw), lambda s, b, g: (s, b, g, 0, 0)),
            pl.BlockSpec((1, 2, zw), lambda s, b, g: (s, 0, 0)),
            pl.BlockSpec((1, 2 * zw, hid), lambda s, b, g: (s, 0, 0)),
            pl.BlockSpec((1, 1, hid), lambda s, b, g: (s, 0, 0)),
            pl.BlockSpec((1, hid, HEAD_DIM), lambda s, b, g: (s, 0, 0)),
            pl.BlockSpec((1, 1, HEAD_DIM), lambda s, b, g: (s, 0, 0)),
        ],
        out_specs=pl.BlockSpec((1, 1, 1, ncp, HEAD_DIM), lambda s, b, g: (s, b, g, 0, 0)),
        out_shape=jax.ShapeDtypeStruct((2, batch, groups, ncp, HEAD_DIM), F32),
        compiler_params=_params("parallel", "parallel", "parallel"),
        name="nsa_compress",
    )(z, pe, w1, b1, w2, b2)


def _nsa_cmp_kernel(nslope_ref, q_ref, kc_ref, vc_ref, o_ref, sel_ref, qs_ref, e_scr, r_scr, *, ncp, n_cmp, n_sel):
    grp = pl.program_id(1)
    c = pl.program_id(2)
    starts = [(c * Q_SUB + j) * Q_BLOCK for j in range(Q_SUB)]
    for j in range(Q_SUB):
        _build_q_stack(q_ref, qs_ref.at[j], j * Q_BLOCK)

    def attend(width):
        for j, t0 in enumerate(starts):
            qi = lax.broadcasted_iota(jnp.int32, (Q_BLOCK, width), 0)
            ni = lax.broadcasted_iota(jnp.int32, (Q_BLOCK, width), 1)
            d = (t0 + qi) - (ni * CMP_STRIDE + (CMP_LEN - 1))
            negb = jnp.where((d >= 0) & (ni < n_cmp), 0.0, MASK_VALUE)
            dist = d.astype(F32)
            s_all = lax.dot_general(qs_ref[j], kc_ref[0, 0, 0:width, :], (((1,), (1,)), ((), ())),
                                    preferred_element_type=F32)
            for rb in range(GROUP):
                s = s_all[_rows(rb)] + nslope_ref[_head_of_row_block(grp, rb)] * dist + negb
                m = jnp.max(s, axis=1, keepdims=True)
                e_scr[j, _rows(rb), 0:width] = jnp.exp2(s - m).astype(e_scr.dtype)
            r_scr[j] = jnp.dot(e_scr[j, :, 0:width], vc_ref[0, 0, 0:width, :], preferred_element_type=F32)

    n_chunks = ncp // LANES
    need = jnp.minimum((starts[-1] + Q_BLOCK - CMP_LEN) // CMP_STRIDE // LANES + 1, n_chunks)
    for kq in range(1, n_chunks + 1):
        pl.when(need == kq)(functools.partial(attend, kq * LANES))

    lane = lax.broadcasted_iota(jnp.int32, (Q_BLOCK, LANES), 1)
    even = lane < HEAD_DIM
    ji = lax.broadcasted_iota(jnp.int32, (LANES, Q_BLOCK), 0)
    qt = lax.broadcasted_iota(jnp.int32, (LANES, Q_BLOCK), 1)
    neg_inf = -jnp.inf
    for j, t0 in enumerate(starts):
        orow = slice(j * Q_BLOCK, (j + 1) * Q_BLOCK)
        row_t = t0 + lax.broadcasted_iota(jnp.int32, (Q_BLOCK, LANES), 0)
        has_cmp = row_t >= (CMP_LEN - 1)
        imp = jnp.zeros((Q_BLOCK, LANES), F32)
        for p in range(PAIRS):
            re = _rows(p)
            ro = _rows(PAIRS + p)
            de = jnp.maximum(r_scr[j, re, LANES:2 * LANES], 1e-30)
            do = jnp.maximum(r_scr[j, ro, LANES:2 * LANES], 1e-30)
            o = jnp.where(even, r_scr[j, re, 0:LANES] / de, r_scr[j, ro, 0:LANES] / do)
            o_ref[orow, p * LANES:(p + 1) * LANES] = jnp.where(has_cmp, o, 0.0)
            imp = imp + r_scr[j, re, 2 * LANES:3 * LANES] / de + r_scr[j, ro, 2 * LANES:3 * LANES] / do
        imp = jnp.where(has_cmp, imp, 0.0)

        imp_t = imp.T
        cur = (t0 + qt) // SEL_LEN
        causal = ji <= cur
        forced = (ji == 0) | (ji == cur) | (ji == cur - 1)
        score = jnp.where(forced, neg_inf, jnp.where(causal, imp_t, MASK_VALUE))
        if n_sel < LANES:
            score = jnp.where(ji < n_sel, score, neg_inf)
        for _ in range(SEL_TOPK - SEL_FORCED):
            mx = jnp.max(score, axis=0, keepdims=True)
            first = jnp.min(jnp.where(score == mx, ji, LANES), axis=0, keepdims=True)
            score = jnp.where(ji == first, neg_inf, score)
        picked = jnp.where((score == neg_inf) & (ji < cur), 1.0, 0.0)
        sel_ref[0, 0, orow, :] = picked.T.astype(sel_ref.dtype)


def nsa_cmp_select(q, kc2, vc_aug, nslopes, batch, seq, q_col_block):
    nstep = seq // (Q_SUB * Q_BLOCK)
    rows = Q_SUB * Q_BLOCK
    ncp = seq // CMP_STRIDE
    n_cmp = (seq - CMP_LEN) // CMP_STRIDE + 1
    n_sel = seq // SEL_LEN
    gw = PAIRS * LANES
    kern = functools.partial(_nsa_cmp_kernel, ncp=ncp, n_cmp=n_cmp, n_sel=n_sel)
    return pl.pallas_call(
        kern,
        grid=(batch, KV_GROUPS, nstep),
        in_specs=[
            pl.BlockSpec(memory_space=pltpu.SMEM),
            pl.BlockSpec((rows, gw), lambda b, g, c: (b * nstep + c, q_col_block + g)),
            pl.BlockSpec((1, 1, ncp, LANES), lambda b, g, c: (b, g, 0, 0)),
            pl.BlockSpec((1, 1, ncp, 3 * LANES), lambda b, g, c: (b, g, 0, 0)),
        ],
        out_specs=[
            pl.BlockSpec((rows, gw), lambda b, g, c: (b * nstep + c, g)),
            pl.BlockSpec((1, 1, rows, LANES), lambda b, g, c: (b, g, c, 0)),
        ],
        out_shape=[
            jax.ShapeDtypeStruct((batch * seq, N_HEADS * HEAD_DIM), F32),
            jax.ShapeDtypeStruct((batch, KV_GROUPS, seq, LANES), MXU_DTYPE),
        ],
        scratch_shapes=[pltpu.VMEM((Q_SUB, ROWS, LANES), MXU_DTYPE), pltpu.VMEM((Q_SUB, ROWS, ncp), MXU_DTYPE),
                        pltpu.VMEM((Q_SUB, ROWS, 3 * LANES), F32)],
        compiler_params=_params("parallel", "parallel", "arbitrary"),
        name="nsa_cmp_select",
    )(nslopes, q, kc2, vc_aug)


N_FEAT = 6


def sel_query_features():
    s = jnp.asarray((-_alibi_neg_slopes().astype(np.float64) * LOG2E).astype(np.float32))
    s1 = s.astype(MXU_DTYPE).astype(F32)
    s2 = (s - s1).astype(MXU_DTYPE).astype(F32)
    s3 = (s - s1 - s2).astype(MXU_DTYPE).astype(F32)
    feat = jnp.zeros((N_HEADS, LANES), F32).at[:, HEAD_DIM:HEAD_DIM + N_FEAT].set(
        jnp.stack([s1, s2, s3, s1, s2, s3], axis=1))
    feat = feat.at[:, HEAD_DIM + N_FEAT].set(MASK_VALUE)
    order = np.array([[_head_of_row_block(g, rb) for rb in range(GROUP)] for g in range(KV_GROUPS)])
    return feat[order]


def sel_key_features(seq):
    pos = np.arange(seq)
    kk = pos % SEL_STEP
    f = np.zeros((seq + SEL_STEP, 2 * LANES), np.float32)
    f[:seq, HEAD_DIM:HEAD_DIM + 3] = (SEL_LEN * (kk // SEL_LEN))[:, None]
    f[:seq, HEAD_DIM + 3:HEAD_DIM + 6] = (kk % SEL_LEN)[:, None]
    f[seq:, HEAD_DIM + N_FEAT] = 1.0
    f[pos, LANES + pos // SEL_LEN] = 1.0
    return f


def _nsa_sel_kernel(delta_ref, qfeat_ref, q_ref, sel_ref, k_in, v_in, kfeat_ref, gate_ref, ex_ref, oc_ref, ow_ref,
                    o_ref, qa_ref, m_ref, acc_ref, s_a, s_b, p_a, p_b, al_a, al_b, k_ref, v_ref):
    grp = pl.program_id(1)
    c = pl.program_id(2)
    t0 = c * (SEL_Q * Q_BLOCK)
    seq = k_in.shape[0]
    srows = lambda j, rb: slice(j * ROWS + rb * Q_BLOCK, j * ROWS + (rb + 1) * Q_BLOCK)
    qrows = lambda j: slice(j * Q_BLOCK, (j + 1) * Q_BLOCK)

    @pl.when(c == 0)
    def _():
        def put_k(r0, y):
            k_ref[pl.ds(r0, KV_CHUNK), 0:LANES] = y
            k_ref[pl.ds(r0, KV_CHUNK), LANES:2 * LANES] = kfeat_ref[pl.ds(r0, KV_CHUNK), LANES:2 * LANES]

        def put_v(r0, y):
            v_ref[pl.ds(r0, KV_CHUNK), 0:LANES] = y
            v_ref[pl.ds(r0, KV_CHUNK), LANES:2 * LANES] = jnp.ones((KV_CHUNK, LANES), v_ref.dtype)

        _unpack_group(k_in, grp, put_k, fill_ref=kfeat_ref)
        _unpack_group(v_in, grp, put_v)
        k_ref[seq:seq + SEL_STEP, :] = kfeat_ref[seq:seq + SEL_STEP, :]
        v_ref[seq:seq + SEL_STEP, :] = jnp.zeros((SEL_STEP, 2 * LANES), v_ref.dtype)
    lane = lax.broadcasted_iota(jnp.int32, (Q_BLOCK, LANES), 1)
    low = lane < HEAD_DIM
    for j in range(SEL_Q):
        selneg = ((1.0 - sel_ref[0, 0, qrows(j), :].astype(F32)) * MASK_VALUE).astype(qa_ref.dtype)
        for p in range(PAIRS):
            qp = q_ref[qrows(j), p * LANES:(p + 1) * LANES].astype(F32) * (HEAD_DIM ** -0.5 * LOG2E)
            for rb, src in ((p, qp), (PAIRS + p, pltpu.roll(qp, HEAD_DIM, axis=1))):
                qa_ref[srows(j, rb), 0:LANES] = jnp.where(low, src, qfeat_ref[0, rb:rb + 1, :]).astype(qa_ref.dtype)
                qa_ref[srows(j, rb), LANES:2 * LANES] = selneg
    n_steps = t0 // SEL_STEP + 1
    n_pad_step = seq // SEL_STEP
    dot_nt = (((1,), (1,)), ((), ()))

    def key_start(step):
        return pl.multiple_of(jnp.clip(step, 0, n_pad_step) * SEL_STEP, SEL_STEP)

    def scores(step, s_out):
        k = k_ref[pl.ds(key_start(step), SEL_STEP), :]
        s_out[...] = lax.dot_general(qa_ref[...], k, dot_nt, preferred_element_type=F32)

    def softmax(s_in, p_out, al_out):
        for rb in range(SEL_Q * GROUP):
            m_prev = m_ref[_rows(rb), :] - delta_ref[_head_of_row_block(grp, rb % GROUP)]
            m_new = jnp.maximum(m_prev, jnp.max(s_in[_rows(rb), :], axis=1, keepdims=True))
            al_out[_rows(rb), :] = jnp.exp2(m_prev - m_new)
            m_ref[_rows(rb), :] = m_new
        for rb in range(SEL_Q * GROUP):
            p_out[_rows(rb), :] = jnp.exp2(s_in[_rows(rb), :] - _rep(m_ref[_rows(rb), :], SEL_STEP)).astype(p_out.dtype)

    def values(step, p_in, al_in):
        v = v_ref[pl.ds(key_start(step), SEL_STEP), :]
        pv = jnp.dot(p_in[...], v, preferred_element_type=F32)
        alpha = al_in[...]
        acc_ref[...] = acc_ref[...] * jnp.concatenate([alpha, alpha], axis=1) + pv

    def even_half(t):
        scores(t, s_a)
        softmax(s_b, p_b, al_b)
        values(t - 2, p_a, al_a)

    def odd_half(t):
        scores(t, s_b)
        softmax(s_a, p_a, al_a)
        values(t - 2, p_b, al_b)

    scores(0, s_a)
    qi = lax.broadcasted_iota(jnp.int32, (Q_BLOCK, Q_BLOCK), 0)
    ki = lax.broadcasted_iota(jnp.int32, (Q_BLOCK, Q_BLOCK), 1)
    own_bias = jnp.where((ki <= qi) & (ki // SEL_LEN == qi // SEL_LEN), 0.0, MASK_VALUE)
    owns = [pl.multiple_of(t0 + j * Q_BLOCK, Q_BLOCK) for j in range(SEL_Q)]
    s_owns = [lax.dot_general(qa_ref[j * ROWS:(j + 1) * ROWS, 0:LANES], k_ref[pl.ds(owns[j], Q_BLOCK), 0:LANES],
                              dot_nt, preferred_element_type=F32) for j in range(SEL_Q)]
    scores(1, s_b)
    back = jnp.full((Q_BLOCK, LANES), n_steps, jnp.int32).astype(F32)
    p_owns = []
    for j in range(SEL_Q):
        p_own = []
        for rb in range(GROUP):
            s = s_owns[j][_rows(rb)] + own_bias
            m = jnp.max(s, axis=1, keepdims=True)
            p_own.append(jnp.exp2(s - m).astype(MXU_DTYPE))
            m_ref[srows(j, rb), :] = m + back * delta_ref[_head_of_row_block(grp, rb)]
        p_owns.append(jnp.concatenate(p_own, axis=0))
    softmax(s_a, p_a, al_a)
    for j in range(SEL_Q):
        acc_ref[j * ROWS:(j + 1) * ROWS, :] = jnp.dot(p_owns[j], v_ref[pl.ds(owns[j], Q_BLOCK), :],
                                                      preferred_element_type=F32)

    def quad(j, carry):
        even_half(4 * j + 2)
        odd_half(4 * j + 3)
        even_half(4 * j + 4)
        odd_half(4 * j + 5)
        return carry

    lax.fori_loop(0, n_steps // 4, quad, 0)
    rest = 4 * (n_steps // 4) + 2

    @pl.when(n_steps % 4 >= 2)
    def _():
        even_half(rest)
        odd_half(rest + 1)

    @pl.when(n_steps % 2 == 1)
    def _():
        even_half(n_steps + 1)

    gw = PAIRS * LANES
    sig = jax.nn.sigmoid(gate_ref[...])
    hi = sig.astype(MXU_DTYPE)
    lo = (sig - hi.astype(F32)).astype(MXU_DTYPE)
    ex = ex_ref[0]
    g = jnp.dot(hi, ex, preferred_element_type=F32) + jnp.dot(lo, ex, preferred_element_type=F32)
    for j in range(SEL_Q):
        o_sel = _pairs(lambda rb: acc_ref[srows(j, rb), 0:LANES], lambda rb: acc_ref[srows(j, rb), LANES:2 * LANES])
        for p in range(PAIRS):
            cols = slice(p * LANES, (p + 1) * LANES)
            out = (g[qrows(j), cols] * oc_ref[qrows(j), cols]
                   + g[qrows(j), gw + p * LANES:gw + (p + 1) * LANES] * o_sel[p]
                   + g[qrows(j), 2 * gw + p * LANES:2 * gw + (p + 1) * LANES] * ow_ref[qrows(j), cols])
            o_ref[qrows(j), cols] = out.astype(o_ref.dtype)


def _gate_expansion():
    gw = PAIRS * LANES
    ex = np.zeros((KV_GROUPS, LANES, 3 * gw), np.float32)
    for g in range(KV_GROUPS):
        for hl in range(GROUP):
            for i in range(3):
                ex[g, 3 * (g * GROUP + hl) + i, i * gw + hl * HEAD_DIM:i * gw + (hl + 1) * HEAD_DIM] = 1.0
    return ex


def nsa_sel_attention(qkv, sel, gate, o_cmp, o_win, batch, seq, k_col, v_col):
    nstep = seq // (SEL_Q * Q_BLOCK)
    rows = SEL_Q * Q_BLOCK
    srows = SEL_Q * ROWS
    gw = PAIRS * LANES
    deltas = jnp.asarray((-_alibi_neg_slopes().astype(np.float64) * LOG2E * SEL_STEP).astype(np.float32))
    ex = jnp.asarray(_gate_expansion(), MXU_DTYPE)
    kfeat = jnp.asarray(sel_key_features(seq), MXU_DTYPE)
    blk = lambda b, g, c: (b * nstep + c, g)
    return pl.pallas_call(
        _nsa_sel_kernel,
        grid=(batch, KV_GROUPS, nstep),
        in_specs=[
            pl.BlockSpec(memory_space=pltpu.SMEM),
            pl.BlockSpec((1, GROUP, LANES), lambda b, g, c: (g, 0, 0)),
            pl.BlockSpec((rows, gw), blk),
            pl.BlockSpec((1, 1, rows, LANES), lambda b, g, c: (b, g, c, 0)),
            _kv_block_spec(seq, k_col),
            _kv_block_spec(seq, v_col),
            pl.BlockSpec((seq + SEL_STEP, 2 * LANES), lambda b, g, c: (0, 0), pipeline_mode=pl.Buffered(1)),
            pl.BlockSpec((rows, LANES), lambda b, g, c: (b * nstep + c, 0)),
            pl.BlockSpec((1, LANES, 3 * gw), lambda b, g, c: (g, 0, 0)),
            pl.BlockSpec((rows, gw), blk),
            pl.BlockSpec((rows, gw), blk),
        ],
        out_specs=pl.BlockSpec((rows, gw), blk),
        out_shape=jax.ShapeDtypeStruct((batch * seq, N_HEADS * HEAD_DIM), MXU_DTYPE),
        scratch_shapes=[
            pltpu.VMEM((srows, 2 * LANES), MXU_DTYPE),
            pltpu.VMEM((srows, LANES), F32),
            pltpu.VMEM((srows, 2 * LANES), F32),
            pltpu.VMEM((srows, SEL_STEP), F32),
            pltpu.VMEM((srows, SEL_STEP), F32),
            pltpu.VMEM((srows, SEL_STEP), MXU_DTYPE),
            pltpu.VMEM((srows, SEL_STEP), MXU_DTYPE),
            pltpu.VMEM((srows, LANES), F32),
            pltpu.VMEM((srows, LANES), F32),
            pltpu.VMEM((seq + SEL_STEP, 2 * LANES), MXU_DTYPE),
            pltpu.VMEM((seq + SEL_STEP, 2 * LANES), MXU_DTYPE),
        ],
        compiler_params=_params("parallel", "parallel", "arbitrary"),
        name="nsa_sel_attention",
    )(deltas, sel_query_features(), qkv, sel, qkv, qkv, kfeat, gate, ex, o_cmp, o_win)


def _nsa_win_kernel(nslope_ref, q_ref, k_in, v_in, o_ref, qs_ref, bias_scr, k_ref, v_ref):
    grp = pl.program_id(1)
    c = pl.program_id(2)
    span = NSA_WINDOW + Q_BLOCK
    lead = NSA_WINDOW // Q_BLOCK
    for j in range(Q_SUB):
        _build_q_stack(q_ref, qs_ref.at[j], j * Q_BLOCK)
    nslope = lambda rb: nslope_ref[_head_of_row_block(grp, rb)]

    def write(j, outs):
        for p, o in enumerate(outs):
            o_ref[j * Q_BLOCK:(j + 1) * Q_BLOCK, p * LANES:(p + 1) * LANES] = o

    @pl.when(c == 0)
    def _():
        _unpack_kv(k_in, v_in, grp, k_ref, v_ref)
        d = _pos_tiles(NSA_WINDOW, 0, span)
        for rb in range(GROUP):
            bias_scr[rb] = _band_bias(nslope(rb), d, NSA_WINDOW)

    def leading_block(j):
        d = _pos_tiles((c * Q_SUB + j) * Q_BLOCK, 0, span)
        write(j, _window_attend(qs_ref.at[j], k_ref[0:span, :], v_ref[0:span, :],
                                lambda rb: _band_bias(nslope(rb), d, NSA_WINDOW)))

    def later_block(j):
        start = pl.multiple_of((c * Q_SUB + j) * Q_BLOCK - NSA_WINDOW, Q_BLOCK)
        write(j, _window_attend(qs_ref.at[j], k_ref[pl.ds(start, span), :], v_ref[pl.ds(start, span), :],
                                lambda rb: bias_scr[rb]))

    lead_steps = max(lead // Q_SUB, 1)

    @pl.when(c < lead_steps)
    def _():
        for j in range(Q_SUB):
            if j < lead:
                leading_block(j)
            else:
                later_block(j)

    @pl.when(c >= lead_steps)
    def _():
        for j in range(Q_SUB):
            later_block(j)


def nsa_win_attention(qkv, nslopes, batch, seq, k_col, v_col):
    nstep = seq // (Q_SUB * Q_BLOCK)
    rows = Q_SUB * Q_BLOCK
    gw = PAIRS * LANES
    kd = KV_GROUPS * HEAD_DIM
    span = NSA_WINDOW + Q_BLOCK
    return pl.pallas_call(
        _nsa_win_kernel,
        grid=(batch, KV_GROUPS, nstep),
        in_specs=[
            pl.BlockSpec(memory_space=pltpu.SMEM),
            pl.BlockSpec((rows, gw), lambda b, g, c: (b * nstep + c, g)),
            _kv_block_spec(seq, k_col),
            _kv_block_spec(seq, v_col),
        ],
        out_specs=pl.BlockSpec((rows, gw), lambda b, g, c: (b * nstep + c, g)),
        out_shape=jax.ShapeDtypeStruct((batch * seq, N_HEADS * HEAD_DIM), F32),
        scratch_shapes=[
            pltpu.VMEM((Q_SUB, ROWS, LANES), MXU_DTYPE),
            pltpu.VMEM((GROUP, Q_BLOCK, span), F32),
            pltpu.VMEM((seq, LANES), MXU_DTYPE),
            pltpu.VMEM((seq, 2 * LANES), MXU_DTYPE),
        ],
        compiler_params=_params("parallel", "parallel", "arbitrary"),
        name="nsa_win_attention",
    )(nslopes, qkv, qkv, qkv)


def _overlap_matrix(seq):
    ncp = seq // CMP_STRIDE
    n_cmp = (seq - CMP_LEN) // CMP_STRIDE + 1
    cs = np.arange(n_cmp) * CMP_STRIDE
    ss = np.arange(seq // SEL_LEN) * SEL_LEN
    ov = (cs[:, None] < ss[None, :] + SEL_LEN) & (cs[:, None] + CMP_LEN > ss[None, :])
    out = np.zeros((ncp, LANES), np.float32)
    out[:n_cmp, :seq // SEL_LEN] = ov
    return out


def _swa_layer(h, x, w_in, b_in, sinks, w_o, b_o, g_post, g_next, nslopes, batch, seq):
    hd = N_HEADS * HEAD_DIM
    kd = KV_GROUPS * HEAD_DIM
    qkv = matmul_bias(h, w_in.astype(MXU_DTYPE), b_in, MXU_DTYPE, name="swa_in_proj")
    o = swa_attention(qkv, nslopes, sinks.astype(F32) * LOG2E, batch, seq)
    return matmul_norm_res(o, w_o.astype(MXU_DTYPE), b_o, g_post, g_next, x, tn=w_o.shape[1], name="swa_out_proj")


def _nsa_layer(h, x, w_in, cmp_pe, cmp_w1, cmp_b1, cmp_w2, cmp_b2, w_o, g_post, g_next, nslopes, batch, seq):
    hd = N_HEADS * HEAD_DIM
    kd = KV_GROUPS * HEAD_DIM
    t = batch * seq
    ncp = seq // CMP_STRIDE
    qkv = matmul_bias(h, w_in[:, :hd + 6 * kd].astype(MXU_DTYPE), jnp.zeros((hd + 6 * kd,), F32), MXU_DTYPE,
                      name="nsa_in_proj")
    n_gate = 3 * N_HEADS
    w_gate = jnp.pad(w_in[:, hd + 6 * kd:], ((0, 0), (0, LANES - n_gate))).astype(MXU_DTYPE)
    gate = matmul_bias(h, w_gate, jnp.zeros((LANES,), F32), F32, name="nsa_gate_proj")

    def kv(i):
        return qkv[:, hd + i * kd:hd + (i + 1) * kd]

    def slabs(a):
        return a.reshape(batch, seq, KV_GROUPS, HEAD_DIM).transpose(0, 2, 1, 3).reshape(
            batch, KV_GROUPS, ncp, CMP_STRIDE * HEAD_DIM)

    z = jnp.stack([slabs(kv(0)), slabs(kv(1))])
    half = CMP_STRIDE * HEAD_DIM
    cmp_out = compress(z, cmp_pe.reshape(2, 2, half).astype(F32), cmp_w1.astype(MXU_DTYPE),
                       cmp_b1.reshape(2, 1, -1), cmp_w2.astype(MXU_DTYPE), cmp_b2.reshape(2, 1, -1))
    kcm = cmp_out[0].astype(MXU_DTYPE)
    vcm = cmp_out[1].astype(MXU_DTYPE)
    kc2 = jnp.concatenate([kcm, kcm], axis=-1)
    ov = jnp.broadcast_to(jnp.asarray(_overlap_matrix(seq), MXU_DTYPE), (batch, KV_GROUPS, ncp, LANES))
    vc_aug = jnp.concatenate([vcm, vcm, jnp.ones((batch, KV_GROUPS, ncp, LANES), MXU_DTYPE), ov], axis=-1)

    o_cmp, sel = nsa_cmp_select(qkv, kc2, vc_aug, nslopes, batch, seq, 0)
    kv_col = lambda i: hd // kd + i
    o_win = nsa_win_attention(qkv, nslopes, batch, seq, kv_col(4), kv_col(5))
    o = nsa_sel_attention(qkv, sel, gate, o_cmp, o_win, batch, seq, kv_col(2), kv_col(3))
    return matmul_norm_res(o, w_o.astype(MXU_DTYPE), jnp.zeros((w_o.shape[1],), F32), g_post, g_next, x,
                           tn=w_o.shape[1], name="nsa_out_proj")


def kernel(x, norm_g, swa_w_in, swa_b_in, swa_sinks, swa_w_o, swa_b_o, nsa_w_in, nsa_cmp_pe, nsa_cmp_w1, nsa_cmp_b1, nsa_cmp_w2, nsa_cmp_b2, nsa_w_o, ffn_w_gate, ffn_w_up, ffn_conv_w, ffn_conv_b, ffn_w_down):
    batch, seq, d = x.shape
    depth = norm_g.shape[0]
    nslopes = jnp.asarray((_alibi_neg_slopes().astype(np.float64) * LOG2E).astype(np.float32))
    xf = x.reshape(batch * seq, d)
    h = rms_cast(xf, norm_g[0, 0])
    for i in range(depth):
        g = norm_g[i]
        j = i // 2
        if i % 2 == 0:
            xf, h = _swa_layer(h, xf, swa_w_in[j], swa_b_in[j], swa_sinks[j], swa_w_o[j], swa_b_o[j],
                               g[1], g[2], nslopes, batch, seq)
        else:
            xf, h = _nsa_layer(h, xf, nsa_w_in[j], nsa_cmp_pe[j], nsa_cmp_w1[j], nsa_cmp_b1[j], nsa_cmp_w2[j],
                               nsa_cmp_b2[j], nsa_w_o[j], g[1], g[2], nslopes, batch, seq)
        act = ffn_up(h, ffn_w_gate, ffn_w_up, i, ffn_conv_w[i], ffn_conv_b[i], seq)
        g_next = norm_g[i + 1, 0] if i + 1 < depth else jnp.ones((d,), F32)
        xf, h = matmul_norm_res(act, ffn_w_down[i].astype(MXU_DTYPE), jnp.zeros((d,), F32), g[3], g_next, xf,
                                tn=512, name="ffn_down")
    return xf.reshape(batch, seq, d)
```

```python
import functools

import numpy as np
import jax
import jax.numpy as jnp
from jax import lax
from jax.experimental import pallas as pl
from jax.experimental.pallas import tpu as pltpu

F32 = jnp.float32
MXU_DTYPE = jnp.bfloat16

N_HEADS = 32
HEAD_DIM = 64
KV_GROUPS = 4
GROUP = N_HEADS // KV_GROUPS
PAIRS = GROUP // 2
LANES = 128
Q_BLOCK = 128
ROWS = GROUP * Q_BLOCK
SWA_WINDOW = 128
CMP_LEN = 32
CMP_STRIDE = 16
SEL_LEN = 64
SEL_TOPK = 16
SEL_FORCED = 3
LOG2E = 1.4426950408889634
SEL_STEP = 512
SEL_Q = 2
NSA_WINDOW = 512
CONV_WIDTH = 3
RMS_EPS = 1e-6
MASK_VALUE = -1e30
VMEM_LIMIT = 60000 * 1024


def _params(*sem):
    return pltpu.CompilerParams(dimension_semantics=sem, vmem_limit_bytes=VMEM_LIMIT)


def _alibi_neg_slopes():
    return (-np.exp2(-8.0 * np.arange(1, N_HEADS + 1, dtype=np.float64) / N_HEADS)).astype(np.float32)


def _rms_cast_kernel(x_ref, g_ref, o_ref):
    x = x_ref[...]
    ms = jnp.mean(x * x, axis=-1, keepdims=True)
    o_ref[...] = (x * lax.rsqrt(ms + RMS_EPS) * g_ref[...]).astype(o_ref.dtype)


def rms_cast(x, g, tm=512):
    t, d = x.shape
    return pl.pallas_call(
        _rms_cast_kernel,
        grid=(t // tm,),
        in_specs=[pl.BlockSpec((tm, d), lambda i: (i, 0)), pl.BlockSpec((1, d), lambda i: (0, 0))],
        out_specs=pl.BlockSpec((tm, d), lambda i: (i, 0)),
        out_shape=jax.ShapeDtypeStruct((t, d), MXU_DTYPE),
        compiler_params=_params("parallel"),
        name="rms_cast",
    )(x, g.reshape(1, d))


def _matmul_kernel(a_ref, w_ref, b_ref, o_ref):
    acc = jnp.dot(a_ref[...], w_ref[...], preferred_element_type=F32)
    o_ref[...] = (acc + b_ref[...]).astype(o_ref.dtype)


def matmul_bias(a, w, b, out_dtype, tm=2048, tn=512, name="matmul_bias"):
    t, k = a.shape
    n = w.shape[1]
    tn = min(tn, n)
    return pl.pallas_call(
        _matmul_kernel,
        grid=(t // tm, n // tn),
        in_specs=[
            pl.BlockSpec((tm, k), lambda i, j: (i, 0)),
            pl.BlockSpec((k, tn), lambda i, j: (0, j)),
            pl.BlockSpec((1, tn), lambda i, j: (0, j)),
        ],
        out_specs=pl.BlockSpec((tm, tn), lambda i, j: (i, j)),
        out_shape=jax.ShapeDtypeStruct((t, n), out_dtype),
        compiler_params=_params("parallel", "arbitrary"),
        name=name,
    )(a, w, b.reshape(1, n))


def _mm_norm_res_kernel(a_ref, w_ref, b_ref, gpost_ref, gnext_ref, x_ref, xo_ref, ho_ref, y_scr, *, nj, tn, n):
    j = pl.program_id(1)
    y_scr[j] = jnp.dot(a_ref[...], w_ref[...], preferred_element_type=F32) + b_ref[...]

    @pl.when(j == nj - 1)
    def _():
        ss = jnp.sum(y_scr[0] * y_scr[0], axis=1, keepdims=True)
        for jj in range(1, nj):
            ss = ss + jnp.sum(y_scr[jj] * y_scr[jj], axis=1, keepdims=True)
        r = lax.rsqrt(ss / n + RMS_EPS)
        ss2 = jnp.zeros_like(ss)
        for jj in range(nj):
            cols = slice(jj * tn, (jj + 1) * tn)
            xn = x_ref[:, cols] + y_scr[jj] * r * gpost_ref[:, cols]
            xo_ref[:, cols] = xn
            ss2 = ss2 + jnp.sum(xn * xn, axis=1, keepdims=True)
        r2 = lax.rsqrt(ss2 / n + RMS_EPS)
        for jj in range(nj):
            cols = slice(jj * tn, (jj + 1) * tn)
            ho_ref[:, cols] = (xo_ref[:, cols] * r2 * gnext_ref[:, cols]).astype(ho_ref.dtype)


def matmul_norm_res(a, w, b, g_post, g_next, x, tm=512, tn=512, name="matmul_norm_res"):
    t, k = a.shape
    n = w.shape[1]
    nj = n // tn
    kern = functools.partial(_mm_norm_res_kernel, nj=nj, tn=tn, n=n)
    return pl.pallas_call(
        kern,
        grid=(t // tm, nj),
        in_specs=[
            pl.BlockSpec((tm, k), lambda i, j: (i, 0)),
            pl.BlockSpec((k, tn), lambda i, j: (0, j)),
            pl.BlockSpec((1, tn), lambda i, j: (0, j)),
            pl.BlockSpec((1, n), lambda i, j: (0, 0)),
            pl.BlockSpec((1, n), lambda i, j: (0, 0)),
            pl.BlockSpec((tm, n), lambda i, j: (i, 0)),
        ],
        out_specs=[
            pl.BlockSpec((tm, n), lambda i, j: (i, 0)),
            pl.BlockSpec((tm, n), lambda i, j: (i, 0)),
        ],
        out_shape=[jax.ShapeDtypeStruct((t, n), F32), jax.ShapeDtypeStruct((t, n), MXU_DTYPE)],
        scratch_shapes=[pltpu.VMEM((nj, tm, tn), F32)],
        compiler_params=_params("parallel", "arbitrary"),
        name=name,
    )(a, w, b.reshape(1, n), g_post.reshape(1, n), g_next.reshape(1, n), x)


FIX_ROWS = 16
CARRY_ROWS = 8


def _ffn_up_kernel(h_ref, wg_ref, wu_ref, cw_ref, cb_ref, o_ref, wg_scr, wu_scr, carry_scr, *, tm, tiles_per_seq):
    i = pl.program_id(1)

    @pl.when(i == 0)
    def _():
        wg_scr[...] = wg_ref[...].astype(wg_scr.dtype)
        wu_scr[...] = wu_ref[...].astype(wu_scr.dtype)

    h = h_ref[...]
    gate = jnp.dot(h, wg_scr[...], preferred_element_type=F32)
    up = jnp.dot(h, wu_scr[...], preferred_element_type=F32)
    w0 = cw_ref[0:1, :]
    w1 = cw_ref[1:2, :]
    w2 = cw_ref[2:3, :]
    b = cb_ref[...]
    a = b + pltpu.roll(gate, 2, axis=0) * w0
    a = a + pltpu.roll(gate, 1, axis=0) * w1
    a = a + gate * w2
    o_ref[...] = (jax.nn.silu(a) * up).astype(o_ref.dtype)

    seq_start = (i % tiles_per_seq) == 0
    prev = jnp.where(seq_start, 0.0, carry_scr[...])
    head = gate[0:FIX_ROWS]
    ext = jnp.concatenate([prev, head], axis=0)
    af = b + ext[CARRY_ROWS - 2:CARRY_ROWS - 2 + FIX_ROWS] * w0
    af = af + ext[CARRY_ROWS - 1:CARRY_ROWS - 1 + FIX_ROWS] * w1
    af = af + head * w2
    o_ref[0:FIX_ROWS, :] = (jax.nn.silu(af) * up[0:FIX_ROWS]).astype(o_ref.dtype)
    carry_scr[...] = gate[tm - CARRY_ROWS:tm]


def ffn_up(h, wg, wu, layer, conv_w, conv_b, seq, tm=1024, tn=512):
    t, k = h.shape
    n = wg.shape[2]
    kern = functools.partial(_ffn_up_kernel, tm=tm, tiles_per_seq=seq // tm)
    return pl.pallas_call(
        kern,
        grid=(n // tn, t // tm),
        in_specs=[
            pl.BlockSpec((tm, k), lambda j, i: (i, 0)),
            pl.BlockSpec((None, k, tn), lambda j, i: (layer, 0, j)),
            pl.BlockSpec((None, k, tn), lambda j, i: (layer, 0, j)),
            pl.BlockSpec((CONV_WIDTH, tn), lambda j, i: (0, j)),
            pl.BlockSpec((1, tn), lambda j, i: (0, j)),
        ],
        out_specs=pl.BlockSpec((tm, tn), lambda j, i: (i, j)),
        out_shape=jax.ShapeDtypeStruct((t, n), MXU_DTYPE),
        scratch_shapes=[
            pltpu.VMEM((k, tn), MXU_DTYPE),
            pltpu.VMEM((k, tn), MXU_DTYPE),
            pltpu.VMEM((CARRY_ROWS, tn), F32),
        ],
        compiler_params=_params("arbitrary", "arbitrary"),
        name="ffn_up",
    )(h, wg, wu, conv_w, conv_b.reshape(1, n))


def _head_of_row_block(group, rb):
    return group * GROUP + 2 * (rb % PAIRS) + rb // PAIRS


Q_SUB = 8


def _build_q_stack(q_ref, qs_ref, row0=0):
    lane = lax.broadcasted_iota(jnp.int32, (Q_BLOCK, LANES), 1)
    even = lane < HEAD_DIM
    for p in range(PAIRS):
        qp = q_ref[row0:row0 + Q_BLOCK, p * LANES:(p + 1) * LANES].astype(F32) * (HEAD_DIM ** -0.5 * LOG2E)
        qs_ref[p * Q_BLOCK:(p + 1) * Q_BLOCK, :] = jnp.where(even, qp, 0.0).astype(qs_ref.dtype)
        qs_ref[(PAIRS + p) * Q_BLOCK:(PAIRS + p + 1) * Q_BLOCK, :] = jnp.where(even, 0.0, qp).astype(qs_ref.dtype)


def _rep(x, size):
    return x if size == LANES else jnp.concatenate([x] * (size // LANES), axis=1)


def _pairs(num, den):
    lane = lax.broadcasted_iota(jnp.int32, (Q_BLOCK, LANES), 1)
    even = lane < HEAD_DIM
    outs = []
    for p in range(PAIRS):
        oe = num(p) / jnp.maximum(den(p), 1e-30)
        oo = num(PAIRS + p) / jnp.maximum(den(PAIRS + p), 1e-30)
        outs.append(jnp.where(even, oe, oo))
    return outs


def _rows(rb):
    return slice(rb * Q_BLOCK, (rb + 1) * Q_BLOCK)


def _band_bias(nslope, d, window):
    return jnp.where((d >= 0) & (d < window), nslope * d.astype(F32), MASK_VALUE)


def _window_attend(qs_ref, k, v, bias, extra_logit=None):
    half = ROWS // 2
    dot_nt = (((1,), (1,)), ((), ()))
    s_halves = [lax.dot_general(qs_ref[hh * half:(hh + 1) * half, :], k, dot_nt, preferred_element_type=F32)
                for hh in range(2)]
    ps, extras = [], []
    for rb in range(GROUP):
        lo = (rb % PAIRS) * Q_BLOCK
        s = s_halves[rb // PAIRS][lo:lo + Q_BLOCK] + bias(rb)
        m = jnp.max(s, axis=1, keepdims=True)
        if extra_logit is not None:
            m = jnp.maximum(m, extra_logit(rb))
            extras.append(jnp.exp2(extra_logit(rb) - m))
        ps.append(jnp.exp2(s - m).astype(MXU_DTYPE))
    r_halves = [jnp.dot(jnp.concatenate(ps[hh * PAIRS:(hh + 1) * PAIRS], axis=0), v, preferred_element_type=F32)
                for hh in range(2)]

    def part(rb, cols):
        lo = (rb % PAIRS) * Q_BLOCK
        return r_halves[rb // PAIRS][lo:lo + Q_BLOCK, cols]

    num = lambda rb: part(rb, slice(0, LANES))
    if extra_logit is None:
        den = lambda rb: part(rb, slice(LANES, 2 * LANES))
    else:
        den = lambda rb: part(rb, slice(LANES, 2 * LANES)) + extras[rb]
    return _pairs(num, den)


KV_CHUNK = 1024


def _unpack_group(kv_ref, grp, write_chunk, fill_ref=None):
    n = kv_ref.shape[0]
    for odd in range(2):
        def fill(odd=odd):
            def body(i, carry):
                r0 = pl.multiple_of(i * KV_CHUNK, KV_CHUNK)
                x = kv_ref[pl.ds(r0, KV_CHUNK), :]
                xi = pltpu.bitcast(x, jnp.int32)
                xr = pltpu.roll(xi, HEAD_DIM, axis=1)
                low = lax.broadcasted_iota(jnp.int32, xi.shape, 1) < HEAD_DIM
                if fill_ref is None:
                    y = jnp.where(low, xr, xi) if odd else jnp.where(low, xi, xr)
                else:
                    f = pltpu.bitcast(fill_ref[pl.ds(r0, KV_CHUNK), 0:LANES], jnp.int32)
                    y = jnp.where(low, xr if odd else xi, f)
                write_chunk(r0, pltpu.bitcast(y, x.dtype))
                return carry

            lax.fori_loop(0, n // KV_CHUNK, body, 0)

        pl.when(grp % 2 == odd)(fill)


def _kv_block_spec(seq, col):
    pairs_per_tensor = KV_GROUPS * HEAD_DIM // LANES
    return pl.BlockSpec((seq, LANES), lambda b, g, c: (b, pairs_per_tensor * col + g // 2))


def _pos_tiles(t0, start, size):
    qi = lax.broadcasted_iota(jnp.int32, (Q_BLOCK, size), 0)
    ki = lax.broadcasted_iota(jnp.int32, (Q_BLOCK, size), 1)
    return (t0 - start) + (qi - ki)


def _unpack_kv(k_in, v_in, grp, k_ref, v_ref):
    def put_k(r0, y):
        k_ref[pl.ds(r0, KV_CHUNK), :] = y

    def put_v(r0, y):
        v_ref[pl.ds(r0, KV_CHUNK), 0:LANES] = y
        v_ref[pl.ds(r0, KV_CHUNK), LANES:2 * LANES] = jnp.ones((KV_CHUNK, LANES), v_ref.dtype)

    _unpack_group(k_in, grp, put_k)
    _unpack_group(v_in, grp, put_v)


def _swa_kernel(nslope_ref, sink_ref, q_ref, k_in, v_in, o_ref, qs_ref, bias_scr, k_ref, v_ref):
    grp = pl.program_id(1)
    c = pl.program_id(2)
    span = SWA_WINDOW + Q_BLOCK
    for j in range(Q_SUB):
        _build_q_stack(q_ref, qs_ref.at[j], j * Q_BLOCK)
    sink = lambda rb: sink_ref[_head_of_row_block(grp, rb)]

    def write(j, outs):
        for p, o in enumerate(outs):
            o_ref[j * Q_BLOCK:(j + 1) * Q_BLOCK, p * LANES:(p + 1) * LANES] = o.astype(o_ref.dtype)

    def later_block(j):
        start = pl.multiple_of((c * Q_SUB + j) * Q_BLOCK - SWA_WINDOW, Q_BLOCK)
        write(j, _window_attend(qs_ref.at[j], k_ref[pl.ds(start, span), :], v_ref[pl.ds(start, span), :],
                                lambda rb: bias_scr[rb], sink))

    @pl.when(c == 0)
    def _():
        _unpack_kv(k_in, v_in, grp, k_ref, v_ref)
        d = _pos_tiles(SWA_WINDOW, 0, span)
        for rb in range(GROUP):
            bias_scr[rb] = _band_bias(nslope_ref[_head_of_row_block(grp, rb)], d, SWA_WINDOW)

    @pl.when(c == 0)
    def _():
        write(0, _window_attend(qs_ref.at[0], k_ref[0:Q_BLOCK, :], v_ref[0:Q_BLOCK, :],
                                lambda rb: bias_scr[rb, :, SWA_WINDOW:span], sink))
        for j in range(1, Q_SUB):
            later_block(j)

    @pl.when(c > 0)
    def _():
        for j in range(Q_SUB):
            later_block(j)


def swa_attention(qkv, nslopes, sinks, batch, seq):
    nstep = seq // (Q_SUB * Q_BLOCK)
    rows = Q_SUB * Q_BLOCK
    gw = PAIRS * LANES
    kd = KV_GROUPS * HEAD_DIM
    span = SWA_WINDOW + Q_BLOCK
    k_col = N_HEADS * HEAD_DIM // kd
    return pl.pallas_call(
        _swa_kernel,
        grid=(batch, KV_GROUPS, nstep),
        in_specs=[
            pl.BlockSpec(memory_space=pltpu.SMEM),
            pl.BlockSpec(memory_space=pltpu.SMEM),
            pl.BlockSpec((rows, gw), lambda b, g, c: (b * nstep + c, g)),
            _kv_block_spec(seq, k_col),
            _kv_block_spec(seq, k_col + 1),
        ],
        out_specs=pl.BlockSpec((rows, gw), lambda b, g, c: (b * nstep + c, g)),
        out_shape=jax.ShapeDtypeStruct((batch * seq, N_HEADS * HEAD_DIM), MXU_DTYPE),
        scratch_shapes=[
            pltpu.VMEM((Q_SUB, ROWS, LANES), MXU_DTYPE),
            pltpu.VMEM((GROUP, Q_BLOCK, span), F32),
            pltpu.VMEM((seq, LANES), MXU_DTYPE),
            pltpu.VMEM((seq, 2 * LANES), MXU_DTYPE),
        ],
        compiler_params=_params("parallel", "parallel", "arbitrary"),
        name="swa_attention",
    )(nslopes, sinks, qkv, qkv, qkv)


def _compress_kernel(z_ref, pe_ref, w1_ref, b1_ref, w2_ref, b2_ref, o_ref, *, ncp):
    half = CMP_STRIDE * HEAD_DIM
    z = z_ref[0, 0, 0].astype(F32)
    top = (z + pe_ref[0, 0:1, :]).astype(MXU_DTYPE)
    bot = (z + pe_ref[0, 1:2, :]).astype(MXU_DTYPE)
    a = jnp.dot(top, w1_ref[0, 0:half, :], preferred_element_type=F32)
    bm = jnp.dot(bot, w1_ref[0, half:2 * half, :], preferred_element_type=F32)
    hid = a + pltpu.roll(bm, ncp - 1, axis=0) + b1_ref[0]
    act = jax.nn.gelu(hid).astype(MXU_DTYPE)
    o_ref[0, 0, 0] = jnp.dot(act, w2_ref[0], preferred_element_type=F32) + b2_ref[0]


def compress(z, pe, w1, b1, w2, b2):
    _, batch, groups, ncp, zw = z.shape
    hid = w1.shape[-1]
    kern = functools.partial(_compress_kernel, ncp=ncp)
    return pl.pallas_call(
        kern,
        grid=(2, batch, groups),
        in_specs=[
            pl.BlockSpec((1, 1, 1, ncp, zw), lambda s, b, g: (s, b, g, 0, 0)),
            pl.BlockSpec((1, 2, zw), lambda s, b, g: (s, 0, 0)),
            pl.BlockSpec((1, 2 * zw, hid), lambda s, b, g: (s, 0, 0)),
            pl.BlockSpec((1, 1, hid), lambda s, b, g: (s, 0, 0)),
            pl.BlockSpec((1, hid, HEAD_DIM), lambda s, b, g: (s, 0, 0)),
            pl.BlockSpec((1, 1, HEAD_DIM), lambda s, b, g: (s, 0, 0)),
        ],
        out_specs=pl.BlockSpec((1, 1, 1, ncp, HEAD_DIM), lambda s, b, g: (s, b, g, 0, 0)),
        out_shape=jax.ShapeDtypeStruct((2, batch, groups, ncp, HEAD_DIM), F32),
        compiler_params=_params("parallel", "parallel", "parallel"),
        name="nsa_compress",
    )(z, pe, w1, b1, w2, b2)


def _nsa_cmp_kernel(nslope_ref, q_ref, kc_ref, vc_ref, o_ref, sel_ref, qs_ref, e_scr, r_scr, *, ncp, n_cmp, n_sel):
    grp = pl.program_id(1)
    c = pl.program_id(2)
    starts = [(c * Q_SUB + j) * Q_BLOCK for j in range(Q_SUB)]
    for j in range(Q_SUB):
        _build_q_stack(q_ref, qs_ref.at[j], j * Q_BLOCK)

    def attend(width):
        for j, t0 in enumerate(starts):
            qi = lax.broadcasted_iota(jnp.int32, (Q_BLOCK, width), 0)
            ni = lax.broadcasted_iota(jnp.int32, (Q_BLOCK, width), 1)
            d = (t0 + qi) - (ni * CMP_STRIDE + (CMP_LEN - 1))
            negb = jnp.where((d >= 0) & (ni < n_cmp), 0.0, MASK_VALUE)
            dist = d.astype(F32)
            s_all = lax.dot_general(qs_ref[j], kc_ref[0, 0, 0:width, :], (((1,), (1,)), ((), ())),
                                    preferred_element_type=F32)
            for rb in range(GROUP):
                s = s_all[_rows(rb)] + nslope_ref[_head_of_row_block(grp, rb)] * dist + negb
                m = jnp.max(s, axis=1, keepdims=True)
                e_scr[j, _rows(rb), 0:width] = jnp.exp2(s - m).astype(e_scr.dtype)
            r_scr[j] = jnp.dot(e_scr[j, :, 0:width], vc_ref[0, 0, 0:width, :], preferred_element_type=F32)

    n_chunks = ncp // LANES
    need = jnp.minimum((starts[-1] + Q_BLOCK - CMP_LEN) // CMP_STRIDE // LANES + 1, n_chunks)
    for kq in range(1, n_chunks + 1):
        pl.when(need == kq)(functools.partial(attend, kq * LANES))

    lane = lax.broadcasted_iota(jnp.int32, (Q_BLOCK, LANES), 1)
    even = lane < HEAD_DIM
    ji = lax.broadcasted_iota(jnp.int32, (LANES, Q_BLOCK), 0)
    qt = lax.broadcasted_iota(jnp.int32, (LANES, Q_BLOCK), 1)
    neg_inf = -jnp.inf
    for j, t0 in enumerate(starts):
        orow = slice(j * Q_BLOCK, (j + 1) * Q_BLOCK)
        row_t = t0 + lax.broadcasted_iota(jnp.int32, (Q_BLOCK, LANES), 0)
        has_cmp = row_t >= (CMP_LEN - 1)
        imp = jnp.zeros((Q_BLOCK, LANES), F32)
        for p in range(PAIRS):
            re = _rows(p)
            ro = _rows(PAIRS + p)
            de = jnp.maximum(r_scr[j, re, LANES:2 * LANES], 1e-30)
            do = jnp.maximum(r_scr[j, ro, LANES:2 * LANES], 1e-30)
            o = jnp.where(even, r_scr[j, re, 0:LANES] / de, r_scr[j, ro, 0:LANES] / do)
            o_ref[orow, p * LANES:(p + 1) * LANES] = jnp.where(has_cmp, o, 0.0)
            imp = imp + r_scr[j, re, 2 * LANES:3 * LANES] / de + r_scr[j, ro, 2 * LANES:3 * LANES] / do
        imp = jnp.where(has_cmp, imp, 0.0)

        imp_t = imp.T
        cur = (t0 + qt) // SEL_LEN
        causal = ji <= cur
        forced = (ji == 0) | (ji == cur) | (ji == cur - 1)
        score = jnp.where(forced, neg_inf, jnp.where(causal, imp_t, MASK_VALUE))
        if n_sel < LANES:
            score = jnp.where(ji < n_sel, score, neg_inf)
        for _ in range(SEL_TOPK - SEL_FORCED):
            mx = jnp.max(score, axis=0, keepdims=True)
            first = jnp.min(jnp.where(score == mx, ji, LANES), axis=0, keepdims=True)
            score = jnp.where(ji == first, neg_inf, score)
        picked = jnp.where((score == neg_inf) & (ji < cur), 1.0, 0.0)
        sel_ref[0, 0, orow, :] = picked.T.astype(sel_ref.dtype)


def nsa_cmp_select(q, kc2, vc_aug, nslopes, batch, seq, q_col_block):
    nstep = seq // (Q_SUB * Q_BLOCK)
    rows = Q_SUB * Q_BLOCK
    ncp = seq // CMP_STRIDE
    n_cmp = (seq - CMP_LEN) // CMP_STRIDE + 1
    n_sel = seq // SEL_LEN
    gw = PAIRS * LANES
    kern = functools.partial(_nsa_cmp_kernel, ncp=ncp, n_cmp=n_cmp, n_sel=n_sel)
    return pl.pallas_call(
        kern,
        grid=(batch, KV_GROUPS, nstep),
        in_specs=[
            pl.BlockSpec(memory_space=pltpu.SMEM),
            pl.BlockSpec((rows, gw), lambda b, g, c: (b * nstep + c, q_col_block + g)),
            pl.BlockSpec((1, 1, ncp, LANES), lambda b, g, c: (b, g, 0, 0)),
            pl.BlockSpec((1, 1, ncp, 3 * LANES), lambda b, g, c: (b, g, 0, 0)),
        ],
        out_specs=[
            pl.BlockSpec((rows, gw), lambda b, g, c: (b * nstep + c, g)),
            pl.BlockSpec((1, 1, rows, LANES), lambda b, g, c: (b, g, c, 0)),
        ],
        out_shape=[
            jax.ShapeDtypeStruct((batch * seq, N_HEADS * HEAD_DIM), F32),
            jax.ShapeDtypeStruct((batch, KV_GROUPS, seq, LANES), MXU_DTYPE),
        ],
        scratch_shapes=[pltpu.VMEM((Q_SUB, ROWS, LANES), MXU_DTYPE), pltpu.VMEM((Q_SUB, ROWS, ncp), MXU_DTYPE),
                        pltpu.VMEM((Q_SUB, ROWS, 3 * LANES), F32)],
        compiler_params=_params("parallel", "parallel", "arbitrary"),
        name="nsa_cmp_select",
    )(nslopes, q, kc2, vc_aug)


N_FEAT = 6


def sel_query_features():
    s = jnp.asarray((-_alibi_neg_slopes().astype(np.float64) * LOG2E).astype(np.float32))
    s1 = s.astype(MXU_DTYPE).astype(F32)
    s2 = (s - s1).astype(MXU_DTYPE).astype(F32)
    s3 = (s - s1 - s2).astype(MXU_DTYPE).astype(F32)
    feat = jnp.zeros((N_HEADS, LANES), F32).at[:, HEAD_DIM:HEAD_DIM + N_FEAT].set(
        jnp.stack([s1, s2, s3, s1, s2, s3], axis=1))
    feat = feat.at[:, HEAD_DIM + N_FEAT].set(MASK_VALUE)
    order = np.array([[_head_of_row_block(g, rb) for rb in range(GROUP)] for g in range(KV_GROUPS)])
    return feat[order]


def sel_key_features(seq):
    pos = np.arange(seq)
    kk = pos % SEL_STEP
    f = np.zeros((seq + SEL_STEP, 2 * LANES), np.float32)
    f[:seq, HEAD_DIM:HEAD_DIM + 3] = (SEL_LEN * (kk // SEL_LEN))[:, None]
    f[:seq, HEAD_DIM + 3:HEAD_DIM + 6] = (kk % SEL_LEN)[:, None]
    f[seq:, HEAD_DIM + N_FEAT] = 1.0
    f[pos, LANES + pos // SEL_LEN] = 1.0
    return f


def _nsa_sel_kernel(delta_ref, qfeat_ref, q_ref, sel_ref, k_in, v_in, kfeat_ref, gate_ref, ex_ref, oc_ref, ow_ref,
                    o_ref, qa_ref, m_ref, acc_ref, s_a, s_b, p_a, p_b, al_a, al_b, k_ref, v_ref):
    grp = pl.program_id(1)
    c = pl.program_id(2)
    t0 = c * (SEL_Q * Q_BLOCK)
    seq = k_in.shape[0]
    srows = lambda j, rb: slice(j * ROWS + rb * Q_BLOCK, j * ROWS + (rb + 1) * Q_BLOCK)
    qrows = lambda j: slice(j * Q_BLOCK, (j + 1) * Q_BLOCK)

    @pl.when(c == 0)
    def _():
        def put_k(r0, y):
            k_ref[pl.ds(r0, KV_CHUNK), 0:LANES] = y
            k_ref[pl.ds(r0, KV_CHUNK), LANES:2 * LANES] = kfeat_ref[pl.ds(r0, KV_CHUNK), LANES:2 * LANES]

        def put_v(r0, y):
            v_ref[pl.ds(r0, KV_CHUNK), 0:LANES] = y
            v_ref[pl.ds(r0, KV_CHUNK), LANES:2 * LANES] = jnp.ones((KV_CHUNK, LANES), v_ref.dtype)

        _unpack_group(k_in, grp, put_k, fill_ref=kfeat_ref)
        _unpack_group(v_in, grp, put_v)
        k_ref[seq:seq + SEL_STEP, :] = kfeat_ref[seq:seq + SEL_STEP, :]
        v_ref[seq:seq + SEL_STEP, :] = jnp.zeros((SEL_STEP, 2 * LANES), v_ref.dtype)
    lane = lax.broadcasted_iota(jnp.int32, (Q_BLOCK, LANES), 1)
    low = lane < HEAD_DIM
    for j in range(SEL_Q):
        selneg = ((1.0 - sel_ref[0, 0, qrows(j), :].astype(F32)) * MASK_VALUE).astype(qa_ref.dtype)
        for p in range(PAIRS):
            qp = q_ref[qrows(j), p * LANES:(p + 1) * LANES].astype(F32) * (HEAD_DIM ** -0.5 * LOG2E)
            for rb, src in ((p, qp), (PAIRS + p, pltpu.roll(qp, HEAD_DIM, axis=1))):
                qa_ref[srows(j, rb), 0:LANES] = jnp.where(low, src, qfeat_ref[0, rb:rb + 1, :]).astype(qa_ref.dtype)
                qa_ref[srows(j, rb), LANES:2 * LANES] = selneg
    n_steps = t0 // SEL_STEP + 1
    n_pad_step = seq // SEL_STEP
    dot_nt = (((1,), (1,)), ((), ()))

    def key_start(step):
        return pl.multiple_of(jnp.clip(step, 0, n_pad_step) * SEL_STEP, SEL_STEP)

    def scores(step, s_out):
        k = k_ref[pl.ds(key_start(step), SEL_STEP), :]
        s_out[...] = lax.dot_general(qa_ref[...], k, dot_nt, preferred_element_type=F32)

    def softmax(s_in, p_out, al_out):
        for rb in range(SEL_Q * GROUP):
            m_prev = m_ref[_rows(rb), :] - delta_ref[_head_of_row_block(grp, rb % GROUP)]
            m_new = jnp.maximum(m_prev, jnp.max(s_in[_rows(rb), :], axis=1, keepdims=True))
            al_out[_rows(rb), :] = jnp.exp2(m_prev - m_new)
            m_ref[_rows(rb), :] = m_new
        for rb in range(SEL_Q * GROUP):
            p_out[_rows(rb), :] = jnp.exp2(s_in[_rows(rb), :] - _rep(m_ref[_rows(rb), :], SEL_STEP)).astype(p_out.dtype)

    def values(step, p_in, al_in):
        v = v_ref[pl.ds(key_start(step), SEL_STEP), :]
        pv = jnp.dot(p_in[...], v, preferred_element_type=F32)
        alpha = al_in[...]
        acc_ref[...] = acc_ref[...] * jnp.concatenate([alpha, alpha], axis=1) + pv

    def even_half(t):
        scores(t, s_a)
        softmax(s_b, p_b, al_b)
        values(t - 2, p_a, al_a)

    def odd_half(t):
        scores(t, s_b)
        softmax(s_a, p_a, al_a)
        values(t - 2, p_b, al_b)

    scores(0, s_a)
    qi = lax.broadcasted_iota(jnp.int32, (Q_BLOCK, Q_BLOCK), 0)
    ki = lax.broadcasted_iota(jnp.int32, (Q_BLOCK, Q_BLOCK), 1)
    own_bias = jnp.where((ki <= qi) & (ki // SEL_LEN == qi // SEL_LEN), 0.0, MASK_VALUE)
    owns = [pl.multiple_of(t0 + j * Q_BLOCK, Q_BLOCK) for j in range(SEL_Q)]
    s_owns = [lax.dot_general(qa_ref[j * ROWS:(j + 1) * ROWS, 0:LANES], k_ref[pl.ds(owns[j], Q_BLOCK), 0:LANES],
                              dot_nt, preferred_element_type=F32) for j in range(SEL_Q)]
    scores(1, s_b)
    back = jnp.full((Q_BLOCK, LANES), n_steps, jnp.int32).astype(F32)
    p_owns = []
    for j in range(SEL_Q):
        p_own = []
        for rb in range(GROUP):
            s = s_owns[j][_rows(rb)] + own_bias
            m = jnp.max(s, axis=1, keepdims=True)
            p_own.append(jnp.exp2(s - m).astype(MXU_DTYPE))
            m_ref[srows(j, rb), :] = m + back * delta_ref[_head_of_row_block(grp, rb)]
        p_owns.append(jnp.concatenate(p_own, axis=0))
    softmax(s_a, p_a, al_a)
    for j in range(SEL_Q):
        acc_ref[j * ROWS:(j + 1) * ROWS, :] = jnp.dot(p_owns[j], v_ref[pl.ds(owns[j], Q_BLOCK), :],
                                                      preferred_element_type=F32)

    def quad(j, carry):
        even_half(4 * j + 2)
        odd_half(4 * j + 3)
        even_half(4 * j + 4)
        odd_half(4 * j + 5)
        return carry

    lax.fori_loop(0, n_steps // 4, quad, 0)
    rest = 4 * (n_steps // 4) + 2

    @pl.when(n_steps % 4 >= 2)
    def _():
        even_half(rest)
        odd_half(rest + 1)

    @pl.when(n_steps % 2 == 1)
    def _():
        even_half(n_steps + 1)

    gw = PAIRS * LANES
    sig = jax.nn.sigmoid(gate_ref[...])
    hi = sig.astype(MXU_DTYPE)
    lo = (sig - hi.astype(F32)).astype(MXU_DTYPE)
    g = jnp.dot(jnp.concatenate([hi, lo], axis=1), ex_ref[0], preferred_element_type=F32)
    for j in range(SEL_Q):
        o_sel = _pairs(lambda rb: acc_ref[srows(j, rb), 0:LANES], lambda rb: acc_ref[srows(j, rb), LANES:2 * LANES])
        for p in range(PAIRS):
            cols = slice(p * LANES, (p + 1) * LANES)
            out = (g[qrows(j), cols] * oc_ref[qrows(j), cols]
                   + g[qrows(j), gw + p * LANES:gw + (p + 1) * LANES] * o_sel[p]
                   + g[qrows(j), 2 * gw + p * LANES:2 * gw + (p + 1) * LANES] * ow_ref[qrows(j), cols])
            o_ref[qrows(j), cols] = out.astype(o_ref.dtype)


def _gate_expansion():
    gw = PAIRS * LANES
    ex = np.zeros((KV_GROUPS, LANES, 3 * gw), np.float32)
    for g in range(KV_GROUPS):
        for hl in range(GROUP):
            for i in range(3):
                ex[g, 3 * (g * GROUP + hl) + i, i * gw + hl * HEAD_DIM:i * gw + (hl + 1) * HEAD_DIM] = 1.0
    return np.concatenate([ex, ex], axis=1)


def nsa_sel_attention(qkv, sel, gate, o_cmp, o_win, batch, seq, k_col, v_col):
    nstep = seq // (SEL_Q * Q_BLOCK)
    rows = SEL_Q * Q_BLOCK
    srows = SEL_Q * ROWS
    gw = PAIRS * LANES
    deltas = jnp.asarray((-_alibi_neg_slopes().astype(np.float64) * LOG2E * SEL_STEP).astype(np.float32))
    ex = jnp.asarray(_gate_expansion(), MXU_DTYPE)
    kfeat = jnp.asarray(sel_key_features(seq), MXU_DTYPE)
    blk = lambda b, g, c: (b * nstep + c, g)
    return pl.pallas_call(
        _nsa_sel_kernel,
        grid=(batch, KV_GROUPS, nstep),
        in_specs=[
            pl.BlockSpec(memory_space=pltpu.SMEM),
            pl.BlockSpec((1, GROUP, LANES), lambda b, g, c: (g, 0, 0)),
            pl.BlockSpec((rows, gw), blk),
            pl.BlockSpec((1, 1, rows, LANES), lambda b, g, c: (b, g, c, 0)),
            _kv_block_spec(seq, k_col),
            _kv_block_spec(seq, v_col),
            pl.BlockSpec((seq + SEL_STEP, 2 * LANES), lambda b, g, c: (0, 0), pipeline_mode=pl.Buffered(1)),
            pl.BlockSpec((rows, LANES), lambda b, g, c: (b * nstep + c, 0)),
            pl.BlockSpec((1, 2 * LANES, 3 * gw), lambda b, g, c: (g, 0, 0)),
            pl.BlockSpec((rows, gw), blk),
            pl.BlockSpec((rows, gw), blk),
        ],
        out_specs=pl.BlockSpec((rows, gw), blk),
        out_shape=jax.ShapeDtypeStruct((batch * seq, N_HEADS * HEAD_DIM), MXU_DTYPE),
        scratch_shapes=[
            pltpu.VMEM((srows, 2 * LANES), MXU_DTYPE),
            pltpu.VMEM((srows, LANES), F32),
            pltpu.VMEM((srows, 2 * LANES), F32),
            pltpu.VMEM((srows, SEL_STEP), F32),
            pltpu.VMEM((srows, SEL_STEP), F32),
            pltpu.VMEM((srows, SEL_STEP), MXU_DTYPE),
            pltpu.VMEM((srows, SEL_STEP), MXU_DTYPE),
            pltpu.VMEM((srows, LANES), F32),
            pltpu.VMEM((srows, LANES), F32),
            pltpu.VMEM((seq + SEL_STEP, 2 * LANES), MXU_DTYPE),
            pltpu.VMEM((seq + SEL_STEP, 2 * LANES), MXU_DTYPE),
        ],
        compiler_params=_params("parallel", "parallel", "arbitrary"),
        name="nsa_sel_attention",
    )(deltas, sel_query_features(), qkv, sel, qkv, qkv, kfeat, gate, ex, o_cmp, o_win)


def _nsa_win_kernel(nslope_ref, q_ref, k_in, v_in, o_ref, qs_ref, bias_scr, k_ref, v_ref):
    grp = pl.program_id(1)
    c = pl.program_id(2)
    span = NSA_WINDOW + Q_BLOCK
    lead = NSA_WINDOW // Q_BLOCK
    for j in range(Q_SUB):
        _build_q_stack(q_ref, qs_ref.at[j], j * Q_BLOCK)
    nslope = lambda rb: nslope_ref[_head_of_row_block(grp, rb)]

    def write(j, outs):
        for p, o in enumerate(outs):
            o_ref[j * Q_BLOCK:(j + 1) * Q_BLOCK, p * LANES:(p + 1) * LANES] = o

    @pl.when(c == 0)
    def _():
        _unpack_kv(k_in, v_in, grp, k_ref, v_ref)
        d = _pos_tiles(NSA_WINDOW, 0, span)
        for rb in range(GROUP):
            bias_scr[rb] = _band_bias(nslope(rb), d, NSA_WINDOW)

    def leading_block(j):
        d = _pos_tiles((c * Q_SUB + j) * Q_BLOCK, 0, span)
        write(j, _window_attend(qs_ref.at[j], k_ref[0:span, :], v_ref[0:span, :],
                                lambda rb: _band_bias(nslope(rb), d, NSA_WINDOW)))

    def later_block(j):
        start = pl.multiple_of((c * Q_SUB + j) * Q_BLOCK - NSA_WINDOW, Q_BLOCK)
        write(j, _window_attend(qs_ref.at[j], k_ref[pl.ds(start, span), :], v_ref[pl.ds(start, span), :],
                                lambda rb: bias_scr[rb]))

    lead_steps = max(lead // Q_SUB, 1)

    @pl.when(c < lead_steps)
    def _():
        for j in range(Q_SUB):
            if j < lead:
                leading_block(j)
            else:
                later_block(j)

    @pl.when(c >= lead_steps)
    def _():
        for j in range(Q_SUB):
            later_block(j)


def nsa_win_attention(qkv, nslopes, batch, seq, k_col, v_col):
    nstep = seq // (Q_SUB * Q_BLOCK)
    rows = Q_SUB * Q_BLOCK
    gw = PAIRS * LANES
    kd = KV_GROUPS * HEAD_DIM
    span = NSA_WINDOW + Q_BLOCK
    return pl.pallas_call(
        _nsa_win_kernel,
        grid=(batch, KV_GROUPS, nstep),
        in_specs=[
            pl.BlockSpec(memory_space=pltpu.SMEM),
            pl.BlockSpec((rows, gw), lambda b, g, c: (b * nstep + c, g)),
            _kv_block_spec(seq, k_col),
            _kv_block_spec(seq, v_col),
        ],
        out_specs=pl.BlockSpec((rows, gw), lambda b, g, c: (b * nstep + c, g)),
        out_shape=jax.ShapeDtypeStruct((batch * seq, N_HEADS * HEAD_DIM), F32),
        scratch_shapes=[
            pltpu.VMEM((Q_SUB, ROWS, LANES), MXU_DTYPE),
            pltpu.VMEM((GROUP, Q_BLOCK, span), F32),
            pltpu.VMEM((seq, LANES), MXU_DTYPE),
            pltpu.VMEM((seq, 2 * LANES), MXU_DTYPE),
        ],
        compiler_params=_params("parallel", "parallel", "arbitrary"),
        name="nsa_win_attention",
    )(nslopes, qkv, qkv, qkv)


def _overlap_matrix(seq):
    ncp = seq // CMP_STRIDE
    n_cmp = (seq - CMP_LEN) // CMP_STRIDE + 1
    cs = np.arange(n_cmp) * CMP_STRIDE
    ss = np.arange(seq // SEL_LEN) * SEL_LEN
    ov = (cs[:, None] < ss[None, :] + SEL_LEN) & (cs[:, None] + CMP_LEN > ss[None, :])
    out = np.zeros((ncp, LANES), np.float32)
    out[:n_cmp, :seq // SEL_LEN] = ov
    return out


def _swa_layer(h, x, w_in, b_in, sinks, w_o, b_o, g_post, g_next, nslopes, batch, seq):
    hd = N_HEADS * HEAD_DIM
    kd = KV_GROUPS * HEAD_DIM
    qkv = matmul_bias(h, w_in.astype(MXU_DTYPE), b_in, MXU_DTYPE, name="swa_in_proj")
    o = swa_attention(qkv, nslopes, sinks.astype(F32) * LOG2E, batch, seq)
    return matmul_norm_res(o, w_o.astype(MXU_DTYPE), b_o, g_post, g_next, x, tn=w_o.shape[1], name="swa_out_proj")


def _nsa_layer(h, x, w_in, cmp_pe, cmp_w1, cmp_b1, cmp_w2, cmp_b2, w_o, g_post, g_next, nslopes, batch, seq):
    hd = N_HEADS * HEAD_DIM
    kd = KV_GROUPS * HEAD_DIM
    t = batch * seq
    ncp = seq // CMP_STRIDE
    qkv = matmul_bias(h, w_in[:, :hd + 6 * kd].astype(MXU_DTYPE), jnp.zeros((hd + 6 * kd,), F32), MXU_DTYPE,
                      name="nsa_in_proj")
    n_gate = 3 * N_HEADS
    w_gate = jnp.pad(w_in[:, hd + 6 * kd:], ((0, 0), (0, LANES - n_gate))).astype(MXU_DTYPE)
    gate = matmul_bias(h, w_gate, jnp.zeros((LANES,), F32), F32, name="nsa_gate_proj")

    def kv(i):
        return qkv[:, hd + i * kd:hd + (i + 1) * kd]

    def slabs(a):
        return a.reshape(batch, seq, KV_GROUPS, HEAD_DIM).transpose(0, 2, 1, 3).reshape(
            batch, KV_GROUPS, ncp, CMP_STRIDE * HEAD_DIM)

    z = jnp.stack([slabs(kv(0)), slabs(kv(1))])
    half = CMP_STRIDE * HEAD_DIM
    cmp_out = compress(z, cmp_pe.reshape(2, 2, half).astype(F32), cmp_w1.astype(MXU_DTYPE),
                       cmp_b1.reshape(2, 1, -1), cmp_w2.astype(MXU_DTYPE), cmp_b2.reshape(2, 1, -1))
    kcm = cmp_out[0].astype(MXU_DTYPE)
    vcm = cmp_out[1].astype(MXU_DTYPE)
    kc2 = jnp.concatenate([kcm, kcm], axis=-1)
    ov = jnp.broadcast_to(jnp.asarray(_overlap_matrix(seq), MXU_DTYPE), (batch, KV_GROUPS, ncp, LANES))
    vc_aug = jnp.concatenate([vcm, vcm, jnp.ones((batch, KV_GROUPS, ncp, LANES), MXU_DTYPE), ov], axis=-1)

    o_cmp, sel = nsa_cmp_select(qkv, kc2, vc_aug, nslopes, batch, seq, 0)
    kv_col = lambda i: hd // kd + i
    o_win = nsa_win_attention(qkv, nslopes, batch, seq, kv_col(4), kv_col(5))
    o = nsa_sel_attention(qkv, sel, gate, o_cmp, o_win, batch, seq, kv_col(2), kv_col(3))
    return matmul_norm_res(o, w_o.astype(MXU_DTYPE), jnp.zeros((w_o.shape[1],), F32), g_post, g_next, x,
                           tn=w_o.shape[1], name="nsa_out_proj")


def kernel(x, norm_g, swa_w_in, swa_b_in, swa_sinks, swa_w_o, swa_b_o, nsa_w_in, nsa_cmp_pe, nsa_cmp_w1, nsa_cmp_b1, nsa_cmp_w2, nsa_cmp_b2, nsa_w_o, ffn_w_gate, ffn_w_up, ffn_conv_w, ffn_conv_b, ffn_w_down):
    batch, seq, d = x.shape
    depth = norm_g.shape[0]
    nslopes = jnp.asarray((_alibi_neg_slopes().astype(np.float64) * LOG2E).astype(np.float32))
    xf = x.reshape(batch * seq, d)
    h = rms_cast(xf, norm_g[0, 0])
    for i in range(depth):
        g = norm_g[i]
        j = i // 2
        if i % 2 == 0:
            xf, h = _swa_layer(h, xf, swa_w_in[j], swa_b_in[j], swa_sinks[j], swa_w_o[j], swa_b_o[j],
                               g[1], g[2], nslopes, batch, seq)
        else:
            xf, h = _nsa_layer(h, xf, nsa_w_in[j], nsa_cmp_pe[j], nsa_cmp_w1[j], nsa_cmp_b1[j], nsa_cmp_w2[j],
                               nsa_cmp_b2[j], nsa_w_o[j], g[1], g[2], nslopes, batch, seq)
        act = ffn_up(h, ffn_w_gate, ffn_w_up, i, ffn_conv_w[i], ffn_conv_b[i], seq)
        g_next = norm_g[i + 1, 0] if i + 1 < depth else jnp.ones((d,), F32)
        xf, h = matmul_norm_res(act, ffn_w_down[i].astype(MXU_DTYPE), jnp.zeros((d,), F32), g[3], g_next, xf,
                                tn=512, name="ffn_down")
    return xf.reshape(batch, seq, d)
```

```python
import functools

import numpy as np
import jax
import jax.numpy as jnp
from jax import lax
from jax.experimental import pallas as pl
from jax.experimental.pallas import tpu as pltpu

F32 = jnp.float32
MXU_DTYPE = jnp.bfloat16

N_HEADS = 32
HEAD_DIM = 64
KV_GROUPS = 4
GROUP = N_HEADS // KV_GROUPS
PAIRS = GROUP // 2
LANES = 128
Q_BLOCK = 128
ROWS = GROUP * Q_BLOCK
SWA_WINDOW = 128
CMP_LEN = 32
CMP_STRIDE = 16
SEL_LEN = 64
SEL_TOPK = 16
SEL_FORCED = 3
LOG2E = 1.4426950408889634
SEL_STEP = 512
SEL_Q = 2
NSA_WINDOW = 512
CONV_WIDTH = 3
RMS_EPS = 1e-6
MASK_VALUE = -1e30
VMEM_LIMIT = 60000 * 1024


def _params(*sem):
    return pltpu.CompilerParams(dimension_semantics=sem, vmem_limit_bytes=VMEM_LIMIT)


def _alibi_neg_slopes():
    return (-np.exp2(-8.0 * np.arange(1, N_HEADS + 1, dtype=np.float64) / N_HEADS)).astype(np.float32)


def _rms_cast_kernel(x_ref, g_ref, o_ref):
    x = x_ref[...]
    ms = jnp.mean(x * x, axis=-1, keepdims=True)
    o_ref[...] = (x * lax.rsqrt(ms + RMS_EPS) * g_ref[...]).astype(o_ref.dtype)


def rms_cast(x, g, tm=512):
    t, d = x.shape
    return pl.pallas_call(
        _rms_cast_kernel,
        grid=(t // tm,),
        in_specs=[pl.BlockSpec((tm, d), lambda i: (i, 0)), pl.BlockSpec((1, d), lambda i: (0, 0))],
        out_specs=pl.BlockSpec((tm, d), lambda i: (i, 0)),
        out_shape=jax.ShapeDtypeStruct((t, d), MXU_DTYPE),
        compiler_params=_params("parallel"),
        name="rms_cast",
    )(x, g.reshape(1, d))


def _matmul_kernel(a_ref, w_ref, b_ref, o_ref):
    acc = jnp.dot(a_ref[...], w_ref[...], preferred_element_type=F32)
    o_ref[...] = (acc + b_ref[...]).astype(o_ref.dtype)


def matmul_bias(a, w, b, out_dtype, tm=2048, tn=512, name="matmul_bias"):
    t, k = a.shape
    n = w.shape[1]
    tn = min(tn, n)
    return pl.pallas_call(
        _matmul_kernel,
        grid=(t // tm, n // tn),
        in_specs=[
            pl.BlockSpec((tm, k), lambda i, j: (i, 0)),
            pl.BlockSpec((k, tn), lambda i, j: (0, j)),
            pl.BlockSpec((1, tn), lambda i, j: (0, j)),
        ],
        out_specs=pl.BlockSpec((tm, tn), lambda i, j: (i, j)),
        out_shape=jax.ShapeDtypeStruct((t, n), out_dtype),
        compiler_params=_params("parallel", "arbitrary"),
        name=name,
    )(a, w, b.reshape(1, n))


def _mm_norm_res_kernel(a_ref, w_ref, b_ref, gpost_ref, gnext_ref, x_ref, xo_ref, ho_ref, y_scr, *, nj, tn, n):
    j = pl.program_id(1)
    y_scr[j] = jnp.dot(a_ref[...], w_ref[...], preferred_element_type=F32) + b_ref[...]

    @pl.when(j == nj - 1)
    def _():
        ss = jnp.sum(y_scr[0] * y_scr[0], axis=1, keepdims=True)
        for jj in range(1, nj):
            ss = ss + jnp.sum(y_scr[jj] * y_scr[jj], axis=1, keepdims=True)
        r = lax.rsqrt(ss / n + RMS_EPS)
        ss2 = jnp.zeros_like(ss)
        for jj in range(nj):
            cols = slice(jj * tn, (jj + 1) * tn)
            xn = x_ref[:, cols] + y_scr[jj] * r * gpost_ref[:, cols]
            xo_ref[:, cols] = xn
            ss2 = ss2 + jnp.sum(xn * xn, axis=1, keepdims=True)
        r2 = lax.rsqrt(ss2 / n + RMS_EPS)
        for jj in range(nj):
            cols = slice(jj * tn, (jj + 1) * tn)
            ho_ref[:, cols] = (xo_ref[:, cols] * r2 * gnext_ref[:, cols]).astype(ho_ref.dtype)


def matmul_norm_res(a, w, b, g_post, g_next, x, tm=512, tn=512, name="matmul_norm_res"):
    t, k = a.shape
    n = w.shape[1]
    nj = n // tn
    kern = functools.partial(_mm_norm_res_kernel, nj=nj, tn=tn, n=n)
    return pl.pallas_call(
        kern,
        grid=(t // tm, nj),
        in_specs=[
            pl.BlockSpec((tm, k), lambda i, j: (i, 0)),
            pl.BlockSpec((k, tn), lambda i, j: (0, j)),
            pl.BlockSpec((1, tn), lambda i, j: (0, j)),
            pl.BlockSpec((1, n), lambda i, j: (0, 0)),
            pl.BlockSpec((1, n), lambda i, j: (0, 0)),
            pl.BlockSpec((tm, n), lambda i, j: (i, 0)),
        ],
        out_specs=[
            pl.BlockSpec((tm, n), lambda i, j: (i, 0)),
            pl.BlockSpec((tm, n), lambda i, j: (i, 0)),
        ],
        out_shape=[jax.ShapeDtypeStruct((t, n), F32), jax.ShapeDtypeStruct((t, n), MXU_DTYPE)],
        scratch_shapes=[pltpu.VMEM((nj, tm, tn), F32)],
        compiler_params=_params("parallel", "arbitrary"),
        name=name,
    )(a, w, b.reshape(1, n), g_post.reshape(1, n), g_next.reshape(1, n), x)


def _mm_norm_res_ksplit_kernel(a_ref, w_ref, gpost_ref, gnext_ref, x_ref, xo_ref, ho_ref, *, nk, n, chunk):
    kk = pl.program_id(1)
    part = jnp.dot(a_ref[...], w_ref[...], preferred_element_type=F32)

    @pl.when(kk == 0)
    def _():
        xo_ref[...] = part

    @pl.when(kk > 0)
    def _():
        xo_ref[...] = xo_ref[...] + part

    @pl.when(kk == nk - 1)
    def _():
        nc = n // chunk
        ss = jnp.zeros((xo_ref.shape[0], 1), F32)
        for jj in range(nc):
            y = xo_ref[:, jj * chunk:(jj + 1) * chunk]
            ss = ss + jnp.sum(y * y, axis=1, keepdims=True)
        r = lax.rsqrt(ss / n + RMS_EPS)
        ss2 = jnp.zeros_like(ss)
        for jj in range(nc):
            cols = slice(jj * chunk, (jj + 1) * chunk)
            xn = x_ref[:, cols] + xo_ref[:, cols] * r * gpost_ref[:, cols]
            xo_ref[:, cols] = xn
            ss2 = ss2 + jnp.sum(xn * xn, axis=1, keepdims=True)
        r2 = lax.rsqrt(ss2 / n + RMS_EPS)
        for jj in range(nc):
            cols = slice(jj * chunk, (jj + 1) * chunk)
            ho_ref[:, cols] = (xo_ref[:, cols] * r2 * gnext_ref[:, cols]).astype(ho_ref.dtype)


def matmul_norm_res_ksplit(a, w, g_post, g_next, x, tm=1024, tk=512, name="matmul_norm_res_ksplit"):
    t, k = a.shape
    n = w.shape[1]
    nk = k // tk
    kern = functools.partial(_mm_norm_res_ksplit_kernel, nk=nk, n=n, chunk=512)
    return pl.pallas_call(
        kern,
        grid=(t // tm, nk),
        in_specs=[
            pl.BlockSpec((tm, tk), lambda i, kk: (i, kk)),
            pl.BlockSpec((tk, n), lambda i, kk: (kk, 0)),
            pl.BlockSpec((1, n), lambda i, kk: (0, 0)),
            pl.BlockSpec((1, n), lambda i, kk: (0, 0)),
            pl.BlockSpec((tm, n), lambda i, kk: (i, 0), pipeline_mode=pl.Buffered(1)),
        ],
        out_specs=[
            pl.BlockSpec((tm, n), lambda i, kk: (i, 0)),
            pl.BlockSpec((tm, n), lambda i, kk: (i, 0)),
        ],
        out_shape=[jax.ShapeDtypeStruct((t, n), F32), jax.ShapeDtypeStruct((t, n), MXU_DTYPE)],
        compiler_params=_params("parallel", "arbitrary"),
        name=name,
    )(a, w, g_post.reshape(1, n), g_next.reshape(1, n), x)


FIX_ROWS = 16
CARRY_ROWS = 8


def _ffn_up_kernel(h_ref, wg_ref, wu_ref, cw_ref, cb_ref, o_ref, wg_scr, wu_scr, carry_scr, *, tm, tiles_per_seq):
    i = pl.program_id(1)

    @pl.when(i == 0)
    def _():
        wg_scr[...] = wg_ref[...].astype(wg_scr.dtype)
        wu_scr[...] = wu_ref[...].astype(wu_scr.dtype)

    h = h_ref[...]
    gate = jnp.dot(h, wg_scr[...], preferred_element_type=F32)
    up = jnp.dot(h, wu_scr[...], preferred_element_type=F32)
    w0 = cw_ref[0:1, :]
    w1 = cw_ref[1:2, :]
    w2 = cw_ref[2:3, :]
    b = cb_ref[...]
    a = b + pltpu.roll(gate, 2, axis=0) * w0
    a = a + pltpu.roll(gate, 1, axis=0) * w1
    a = a + gate * w2
    o_ref[...] = (jax.nn.silu(a) * up).astype(o_ref.dtype)

    seq_start = (i % tiles_per_seq) == 0
    prev = jnp.where(seq_start, 0.0, carry_scr[...])
    head = gate[0:FIX_ROWS]
    ext = jnp.concatenate([prev, head], axis=0)
    af = b + ext[CARRY_ROWS - 2:CARRY_ROWS - 2 + FIX_ROWS] * w0
    af = af + ext[CARRY_ROWS - 1:CARRY_ROWS - 1 + FIX_ROWS] * w1
    af = af + head * w2
    o_ref[0:FIX_ROWS, :] = (jax.nn.silu(af) * up[0:FIX_ROWS]).astype(o_ref.dtype)
    carry_scr[...] = gate[tm - CARRY_ROWS:tm]


def ffn_up(h, wg, wu, layer, conv_w, conv_b, seq, tm=1024, tn=512):
    t, k = h.shape
    n = wg.shape[2]
    kern = functools.partial(_ffn_up_kernel, tm=tm, tiles_per_seq=seq // tm)
    return pl.pallas_call(
        kern,
        grid=(n // tn, t // tm),
        in_specs=[
            pl.BlockSpec((tm, k), lambda j, i: (i, 0)),
            pl.BlockSpec((None, k, tn), lambda j, i: (layer, 0, j)),
            pl.BlockSpec((None, k, tn), lambda j, i: (layer, 0, j)),
            pl.BlockSpec((CONV_WIDTH, tn), lambda j, i: (0, j)),
            pl.BlockSpec((1, tn), lambda j, i: (0, j)),
        ],
        out_specs=pl.BlockSpec((tm, tn), lambda j, i: (i, j)),
        out_shape=jax.ShapeDtypeStruct((t, n), MXU_DTYPE),
        scratch_shapes=[
            pltpu.VMEM((k, tn), MXU_DTYPE),
            pltpu.VMEM((k, tn), MXU_DTYPE),
            pltpu.VMEM((CARRY_ROWS, tn), F32),
        ],
        compiler_params=_params("arbitrary", "arbitrary"),
        name="ffn_up",
    )(h, wg, wu, conv_w, conv_b.reshape(1, n))


def _head_of_row_block(group, rb):
    return group * GROUP + 2 * (rb % PAIRS) + rb // PAIRS


Q_SUB = 8


def _build_q_stack(q_ref, qs_ref, row0=0):
    lane = lax.broadcasted_iota(jnp.int32, (Q_BLOCK, LANES), 1)
    even = lane < HEAD_DIM
    for p in range(PAIRS):
        qp = q_ref[row0:row0 + Q_BLOCK, p * LANES:(p + 1) * LANES].astype(F32) * (HEAD_DIM ** -0.5 * LOG2E)
        qs_ref[p * Q_BLOCK:(p + 1) * Q_BLOCK, :] = jnp.where(even, qp, 0.0).astype(qs_ref.dtype)
        qs_ref[(PAIRS + p) * Q_BLOCK:(PAIRS + p + 1) * Q_BLOCK, :] = jnp.where(even, 0.0, qp).astype(qs_ref.dtype)


def _rep(x, size):
    return x if size == LANES else jnp.concatenate([x] * (size // LANES), axis=1)


def _pairs(num, den):
    lane = lax.broadcasted_iota(jnp.int32, (Q_BLOCK, LANES), 1)
    even = lane < HEAD_DIM
    outs = []
    for p in range(PAIRS):
        oe = num(p) / jnp.maximum(den(p), 1e-30)
        oo = num(PAIRS + p) / jnp.maximum(den(PAIRS + p), 1e-30)
        outs.append(jnp.where(even, oe, oo))
    return outs


def _rows(rb):
    return slice(rb * Q_BLOCK, (rb + 1) * Q_BLOCK)


def _band_bias(nslope, d, window):
    return jnp.where((d >= 0) & (d < window), nslope * d.astype(F32), MASK_VALUE)


def _window_attend(qs_ref, k, v, bias, extra_logit=None):
    half = ROWS // 2
    dot_nt = (((1,), (1,)), ((), ()))
    s_halves = [lax.dot_general(qs_ref[hh * half:(hh + 1) * half, :], k, dot_nt, preferred_element_type=F32)
                for hh in range(2)]
    ps, extras = [], []
    for rb in range(GROUP):
        lo = (rb % PAIRS) * Q_BLOCK
        s = s_halves[rb // PAIRS][lo:lo + Q_BLOCK] + bias(rb)
        m = jnp.max(s, axis=1, keepdims=True)
        if extra_logit is not None:
            m = jnp.maximum(m, extra_logit(rb))
            extras.append(jnp.exp2(extra_logit(rb) - m))
        ps.append(jnp.exp2(s - m).astype(MXU_DTYPE))
    r_halves = [jnp.dot(jnp.concatenate(ps[hh * PAIRS:(hh + 1) * PAIRS], axis=0), v, preferred_element_type=F32)
                for hh in range(2)]

    def part(rb, cols):
        lo = (rb % PAIRS) * Q_BLOCK
        return r_halves[rb // PAIRS][lo:lo + Q_BLOCK, cols]

    num = lambda rb: part(rb, slice(0, LANES))
    if extra_logit is None:
        den = lambda rb: part(rb, slice(LANES, 2 * LANES))
    else:
        den = lambda rb: part(rb, slice(LANES, 2 * LANES)) + extras[rb]
    return _pairs(num, den)


KV_CHUNK = 1024


def _unpack_group(kv_ref, grp, write_chunk, fill_ref=None):
    n = kv_ref.shape[0]
    for odd in range(2):
        def fill(odd=odd):
            def body(i, carry):
                r0 = pl.multiple_of(i * KV_CHUNK, KV_CHUNK)
                x = kv_ref[pl.ds(r0, KV_CHUNK), :]
                xi = pltpu.bitcast(x, jnp.int32)
                xr = pltpu.roll(xi, HEAD_DIM, axis=1)
                low = lax.broadcasted_iota(jnp.int32, xi.shape, 1) < HEAD_DIM
                if fill_ref is None:
                    y = jnp.where(low, xr, xi) if odd else jnp.where(low, xi, xr)
                else:
                    f = pltpu.bitcast(fill_ref[pl.ds(r0, KV_CHUNK), 0:LANES], jnp.int32)
                    y = jnp.where(low, xr if odd else xi, f)
                write_chunk(r0, pltpu.bitcast(y, x.dtype))
                return carry

            lax.fori_loop(0, n // KV_CHUNK, body, 0)

        pl.when(grp % 2 == odd)(fill)


def _kv_block_spec(seq, col):
    pairs_per_tensor = KV_GROUPS * HEAD_DIM // LANES
    return pl.BlockSpec((seq, LANES), lambda b, g, c: (b, pairs_per_tensor * col + g // 2))


def _pos_tiles(t0, start, size):
    qi = lax.broadcasted_iota(jnp.int32, (Q_BLOCK, size), 0)
    ki = lax.broadcasted_iota(jnp.int32, (Q_BLOCK, size), 1)
    return (t0 - start) + (qi - ki)


def _unpack_kv(k_in, v_in, grp, k_ref, v_ref):
    def put_k(r0, y):
        k_ref[pl.ds(r0, KV_CHUNK), :] = y

    def put_v(r0, y):
        v_ref[pl.ds(r0, KV_CHUNK), 0:LANES] = y
        v_ref[pl.ds(r0, KV_CHUNK), LANES:2 * LANES] = jnp.ones((KV_CHUNK, LANES), v_ref.dtype)

    _unpack_group(k_in, grp, put_k)
    _unpack_group(v_in, grp, put_v)


def _swa_kernel(nslope_ref, sink_ref, q_ref, k_in, v_in, o_ref, qs_ref, bias_scr, k_ref, v_ref):
    grp = pl.program_id(1)
    c = pl.program_id(2)
    span = SWA_WINDOW + Q_BLOCK
    for j in range(Q_SUB):
        _build_q_stack(q_ref, qs_ref.at[j], j * Q_BLOCK)
    sink = lambda rb: sink_ref[_head_of_row_block(grp, rb)]

    def write(j, outs):
        for p, o in enumerate(outs):
            o_ref[j * Q_BLOCK:(j + 1) * Q_BLOCK, p * LANES:(p + 1) * LANES] = o.astype(o_ref.dtype)

    def later_block(j):
        start = pl.multiple_of((c * Q_SUB + j) * Q_BLOCK - SWA_WINDOW, Q_BLOCK)
        write(j, _window_attend(qs_ref.at[j], k_ref[pl.ds(start, span), :], v_ref[pl.ds(start, span), :],
                                lambda rb: bias_scr[rb], sink))

    @pl.when(c == 0)
    def _():
        _unpack_kv(k_in, v_in, grp, k_ref, v_ref)
        d = _pos_tiles(SWA_WINDOW, 0, span)
        for rb in range(GROUP):
            bias_scr[rb] = _band_bias(nslope_ref[_head_of_row_block(grp, rb)], d, SWA_WINDOW)

    @pl.when(c == 0)
    def _():
        write(0, _window_attend(qs_ref.at[0], k_ref[0:Q_BLOCK, :], v_ref[0:Q_BLOCK, :],
                                lambda rb: bias_scr[rb, :, SWA_WINDOW:span], sink))
        for j in range(1, Q_SUB):
            later_block(j)

    @pl.when(c > 0)
    def _():
        for j in range(Q_SUB):
            later_block(j)


def swa_attention(qkv, nslopes, sinks, batch, seq):
    nstep = seq // (Q_SUB * Q_BLOCK)
    rows = Q_SUB * Q_BLOCK
    gw = PAIRS * LANES
    kd = KV_GROUPS * HEAD_DIM
    span = SWA_WINDOW + Q_BLOCK
    k_col = N_HEADS * HEAD_DIM // kd
    return pl.pallas_call(
        _swa_kernel,
        grid=(batch, KV_GROUPS, nstep),
        in_specs=[
            pl.BlockSpec(memory_space=pltpu.SMEM),
            pl.BlockSpec(memory_space=pltpu.SMEM),
            pl.BlockSpec((rows, gw), lambda b, g, c: (b * nstep + c, g)),
            _kv_block_spec(seq, k_col),
            _kv_block_spec(seq, k_col + 1),
        ],
        out_specs=pl.BlockSpec((rows, gw), lambda b, g, c: (b * nstep + c, g)),
        out_shape=jax.ShapeDtypeStruct((batch * seq, N_HEADS * HEAD_DIM), MXU_DTYPE),
        scratch_shapes=[
            pltpu.VMEM((Q_SUB, ROWS, LANES), MXU_DTYPE),
            pltpu.VMEM((GROUP, Q_BLOCK, span), F32),
            pltpu.VMEM((seq, LANES), MXU_DTYPE),
            pltpu.VMEM((seq, 2 * LANES), MXU_DTYPE),
        ],
        compiler_params=_params("parallel", "parallel", "arbitrary"),
        name="swa_attention",
    )(nslopes, sinks, qkv, qkv, qkv)


def _compress_kernel(z_ref, pe_ref, w1_ref, b1_ref, w2_ref, b2_ref, o_ref, *, ncp):
    half = CMP_STRIDE * HEAD_DIM
    z = z_ref[0, 0, 0].astype(F32)
    top = (z + pe_ref[0, 0:1, :]).astype(MXU_DTYPE)
    bot = (z + pe_ref[0, 1:2, :]).astype(MXU_DTYPE)
    a = jnp.dot(top, w1_ref[0, 0:half, :], preferred_element_type=F32)
    bm = jnp.dot(bot, w1_ref[0, half:2 * half, :], preferred_element_type=F32)
    hid = a + pltpu.roll(bm, ncp - 1, axis=0) + b1_ref[0]
    act = jax.nn.gelu(hid).astype(MXU_DTYPE)
    o_ref[0, 0, 0] = jnp.dot(act, w2_ref[0], preferred_element_type=F32) + b2_ref[0]


def compress(z, pe, w1, b1, w2, b2):
    _, batch, groups, ncp, zw = z.shape
    hid = w1.shape[-1]
    kern = functools.partial(_compress_kernel, ncp=ncp)
    return pl.pallas_call(
        kern,
        grid=(2, batch, groups),
        in_specs=[
            pl.BlockSpec((1, 1, 1, ncp, zw), lambda s, b, g: (s, b, g, 0, 0)),
            pl.BlockSpec((1, 2, zw), lambda s, b, g: (s, 0, 0)),
            pl.BlockSpec((1, 2 * zw, hid), lambda s, b, g: (s, 0, 0)),
            pl.BlockSpec((1, 1, hid), lambda s, b, g: (s, 0, 0)),
            pl.BlockSpec((1, hid, HEAD_DIM), lambda s, b, g: (s, 0, 0)),
            pl.BlockSpec((1, 1, HEAD_DIM), lambda s, b, g: (s, 0, 0)),
        ],
        out_specs=pl.BlockSpec((1, 1, 1, ncp, HEAD_DIM), lambda s, b, g: (s, b, g, 0, 0)),
        out_shape=jax.ShapeDtypeStruct((2, batch, groups, ncp, HEAD_DIM), F32),
        compiler_params=_params("parallel", "parallel", "parallel"),
        name="nsa_compress",
    )(z, pe, w1, b1, w2, b2)


def _nsa_cmp_kernel(nslope_ref, q_ref, kc_ref, vc_ref, o_ref, sel_ref, qs_ref, e_scr, r_scr, *, ncp, n_cmp, n_sel):
    grp = pl.program_id(1)
    c = pl.program_id(2)
    starts = [(c * Q_SUB + j) * Q_BLOCK for j in range(Q_SUB)]
    for j in range(Q_SUB):
        _build_q_stack(q_ref, qs_ref.at[j], j * Q_BLOCK)

    def attend(width):
        for j, t0 in enumerate(starts):
            qi = lax.broadcasted_iota(jnp.int32, (Q_BLOCK, width), 0)
            ni = lax.broadcasted_iota(jnp.int32, (Q_BLOCK, width), 1)
            d = (t0 + qi) - (ni * CMP_STRIDE + (CMP_LEN - 1))
            negb = jnp.where((d >= 0) & (ni < n_cmp), 0.0, MASK_VALUE)
            dist = d.astype(F32)
            s_all = lax.dot_general(qs_ref[j], kc_ref[0, 0, 0:width, :], (((1,), (1,)), ((), ())),
                                    preferred_element_type=F32)
            for rb in range(GROUP):
                s = s_all[_rows(rb)] + nslope_ref[_head_of_row_block(grp, rb)] * dist + negb
                m = jnp.max(s, axis=1, keepdims=True)
                e_scr[j, _rows(rb), 0:width] = jnp.exp2(s - m).astype(e_scr.dtype)
            r_scr[j] = jnp.dot(e_scr[j, :, 0:width], vc_ref[0, 0, 0:width, :], preferred_element_type=F32)

    n_chunks = ncp // LANES
    need = jnp.minimum((starts[-1] + Q_BLOCK - CMP_LEN) // CMP_STRIDE // LANES + 1, n_chunks)
    for kq in range(1, n_chunks + 1):
        pl.when(need == kq)(functools.partial(attend, kq * LANES))

    lane = lax.broadcasted_iota(jnp.int32, (Q_BLOCK, LANES), 1)
    even = lane < HEAD_DIM
    ji = lax.broadcasted_iota(jnp.int32, (LANES, Q_BLOCK), 0)
    qt = lax.broadcasted_iota(jnp.int32, (LANES, Q_BLOCK), 1)
    neg_inf = -jnp.inf
    for j, t0 in enumerate(starts):
        orow = slice(j * Q_BLOCK, (j + 1) * Q_BLOCK)
        row_t = t0 + lax.broadcasted_iota(jnp.int32, (Q_BLOCK, LANES), 0)
        has_cmp = row_t >= (CMP_LEN - 1)
        imp = jnp.zeros((Q_BLOCK, LANES), F32)
        for p in range(PAIRS):
            re = _rows(p)
            ro = _rows(PAIRS + p)
            de = jnp.maximum(r_scr[j, re, LANES:2 * LANES], 1e-30)
            do = jnp.maximum(r_scr[j, ro, LANES:2 * LANES], 1e-30)
            o = jnp.where(even, r_scr[j, re, 0:LANES] / de, r_scr[j, ro, 0:LANES] / do)
            o_ref[orow, p * LANES:(p + 1) * LANES] = jnp.where(has_cmp, o, 0.0)
            imp = imp + r_scr[j, re, 2 * LANES:3 * LANES] / de + r_scr[j, ro, 2 * LANES:3 * LANES] / do
        imp = jnp.where(has_cmp, imp, 0.0)

        imp_t = imp.T
        cur = (t0 + qt) // SEL_LEN
        causal = ji <= cur
        forced = (ji == 0) | (ji == cur) | (ji == cur - 1)
        score = jnp.where(forced, neg_inf, jnp.where(causal, imp_t, MASK_VALUE))
        if n_sel < LANES:
            score = jnp.where(ji < n_sel, score, neg_inf)
        for _ in range(SEL_TOPK - SEL_FORCED):
            mx = jnp.max(score, axis=0, keepdims=True)
            first = jnp.min(jnp.where(score == mx, ji, LANES), axis=0, keepdims=True)
            score = jnp.where(ji == first, neg_inf, score)
        picked = jnp.where((score == neg_inf) & (ji < cur), 1.0, 0.0)
        sel_ref[0, 0, orow, :] = picked.T.astype(sel_ref.dtype)


def nsa_cmp_select(q, kc2, vc_aug, nslopes, batch, seq, q_col_block):
    nstep = seq // (Q_SUB * Q_BLOCK)
    rows = Q_SUB * Q_BLOCK
    ncp = seq // CMP_STRIDE
    n_cmp = (seq - CMP_LEN) // CMP_STRIDE + 1
    n_sel = seq // SEL_LEN
    gw = PAIRS * LANES
    kern = functools.partial(_nsa_cmp_kernel, ncp=ncp, n_cmp=n_cmp, n_sel=n_sel)
    return pl.pallas_call(
        kern,
        grid=(batch, KV_GROUPS, nstep),
        in_specs=[
            pl.BlockSpec(memory_space=pltpu.SMEM),
            pl.BlockSpec((rows, gw), lambda b, g, c: (b * nstep + c, q_col_block + g)),
            pl.BlockSpec((1, 1, ncp, LANES), lambda b, g, c: (b, g, 0, 0)),
            pl.BlockSpec((1, 1, ncp, 3 * LANES), lambda b, g, c: (b, g, 0, 0)),
        ],
        out_specs=[
            pl.BlockSpec((rows, gw), lambda b, g, c: (b * nstep + c, g)),
            pl.BlockSpec((1, 1, rows, LANES), lambda b, g, c: (b, g, c, 0)),
        ],
        out_shape=[
            jax.ShapeDtypeStruct((batch * seq, N_HEADS * HEAD_DIM), F32),
            jax.ShapeDtypeStruct((batch, KV_GROUPS, seq, LANES), MXU_DTYPE),
        ],
        scratch_shapes=[pltpu.VMEM((Q_SUB, ROWS, LANES), MXU_DTYPE), pltpu.VMEM((Q_SUB, ROWS, ncp), MXU_DTYPE),
                        pltpu.VMEM((Q_SUB, ROWS, 3 * LANES), F32)],
        compiler_params=_params("parallel", "parallel", "arbitrary"),
        name="nsa_cmp_select",
    )(nslopes, q, kc2, vc_aug)


N_FEAT = 6


def sel_query_features():
    s = jnp.asarray((-_alibi_neg_slopes().astype(np.float64) * LOG2E).astype(np.float32))
    s1 = s.astype(MXU_DTYPE).astype(F32)
    s2 = (s - s1).astype(MXU_DTYPE).astype(F32)
    s3 = (s - s1 - s2).astype(MXU_DTYPE).astype(F32)
    feat = jnp.zeros((N_HEADS, LANES), F32).at[:, HEAD_DIM:HEAD_DIM + N_FEAT].set(
        jnp.stack([s1, s2, s3, s1, s2, s3], axis=1))
    feat = feat.at[:, HEAD_DIM + N_FEAT].set(MASK_VALUE)
    order = np.array([[_head_of_row_block(g, rb) for rb in range(GROUP)] for g in range(KV_GROUPS)])
    return feat[order]


def sel_key_features(seq):
    pos = np.arange(seq)
    kk = pos % SEL_STEP
    f = np.zeros((seq + SEL_STEP, 2 * LANES), np.float32)
    f[:seq, HEAD_DIM:HEAD_DIM + 3] = (SEL_LEN * (kk // SEL_LEN))[:, None]
    f[:seq, HEAD_DIM + 3:HEAD_DIM + 6] = (kk % SEL_LEN)[:, None]
    f[seq:, HEAD_DIM + N_FEAT] = 1.0
    f[pos, LANES + pos // SEL_LEN] = 1.0
    return f


def _nsa_sel_kernel(delta_ref, qfeat_ref, q_ref, sel_ref, k_in, v_in, kfeat_ref, gate_ref, ex_ref, oc_ref, ow_ref,
                    o_ref, qa_ref, m_ref, acc_ref, s_a, s_b, p_a, p_b, al_a, al_b, k_ref, v_ref):
    grp = pl.program_id(1)
    c = pl.program_id(2)
    t0 = c * (SEL_Q * Q_BLOCK)
    seq = k_in.shape[0]
    srows = lambda j, rb: slice(j * ROWS + rb * Q_BLOCK, j * ROWS + (rb + 1) * Q_BLOCK)
    qrows = lambda j: slice(j * Q_BLOCK, (j + 1) * Q_BLOCK)

    @pl.when(c == 0)
    def _():
        def put_k(r0, y):
            k_ref[pl.ds(r0, KV_CHUNK), 0:LANES] = y
            k_ref[pl.ds(r0, KV_CHUNK), LANES:2 * LANES] = kfeat_ref[pl.ds(r0, KV_CHUNK), LANES:2 * LANES]

        def put_v(r0, y):
            v_ref[pl.ds(r0, KV_CHUNK), 0:LANES] = y
            v_ref[pl.ds(r0, KV_CHUNK), LANES:2 * LANES] = jnp.ones((KV_CHUNK, LANES), v_ref.dtype)

        _unpack_group(k_in, grp, put_k, fill_ref=kfeat_ref)
        _unpack_group(v_in, grp, put_v)
        k_ref[seq:seq + SEL_STEP, :] = kfeat_ref[seq:seq + SEL_STEP, :]
        v_ref[seq:seq + SEL_STEP, :] = jnp.zeros((SEL_STEP, 2 * LANES), v_ref.dtype)
    lane = lax.broadcasted_iota(jnp.int32, (Q_BLOCK, LANES), 1)
    low = lane < HEAD_DIM
    for j in range(SEL_Q):
        selneg = ((1.0 - sel_ref[0, 0, qrows(j), :].astype(F32)) * MASK_VALUE).astype(qa_ref.dtype)
        for p in range(PAIRS):
            qp = q_ref[qrows(j), p * LANES:(p + 1) * LANES].astype(F32) * (HEAD_DIM ** -0.5 * LOG2E)
            for rb, src in ((p, qp), (PAIRS + p, pltpu.roll(qp, HEAD_DIM, axis=1))):
                qa_ref[srows(j, rb), 0:LANES] = jnp.where(low, src, qfeat_ref[0, rb:rb + 1, :]).astype(qa_ref.dtype)
                qa_ref[srows(j, rb), LANES:2 * LANES] = selneg
    n_steps = t0 // SEL_STEP + 1
    n_pad_step = seq // SEL_STEP
    dot_nt = (((1,), (1,)), ((), ()))

    def key_start(step):
        return pl.multiple_of(jnp.clip(step, 0, n_pad_step) * SEL_STEP, SEL_STEP)

    def scores(step, s_out):
        k = k_ref[pl.ds(key_start(step), SEL_STEP), :]
        s_out[...] = lax.dot_general(qa_ref[...], k, dot_nt, preferred_element_type=F32)

    def softmax(s_in, p_out, al_out):
        for rb in range(SEL_Q * GROUP):
            m_prev = m_ref[_rows(rb), :] - delta_ref[_head_of_row_block(grp, rb % GROUP)]
            m_new = jnp.maximum(m_prev, jnp.max(s_in[_rows(rb), :], axis=1, keepdims=True))
            al_out[_rows(rb), :] = jnp.exp2(m_prev - m_new)
            m_ref[_rows(rb), :] = m_new
        for rb in range(SEL_Q * GROUP):
            p_out[_rows(rb), :] = jnp.exp2(s_in[_rows(rb), :] - _rep(m_ref[_rows(rb), :], SEL_STEP)).astype(p_out.dtype)

    def values(step, p_in, al_in):
        v = v_ref[pl.ds(key_start(step), SEL_STEP), :]
        pv = jnp.dot(p_in[...], v, preferred_element_type=F32)
        alpha = al_in[...]
        acc_ref[...] = acc_ref[...] * jnp.concatenate([alpha, alpha], axis=1) + pv

    def even_half(t):
        scores(t, s_a)
        softmax(s_b, p_b, al_b)
        values(t - 2, p_a, al_a)

    def odd_half(t):
        scores(t, s_b)
        softmax(s_a, p_a, al_a)
        values(t - 2, p_b, al_b)

    scores(0, s_a)
    qi = lax.broadcasted_iota(jnp.int32, (Q_BLOCK, Q_BLOCK), 0)
    ki = lax.broadcasted_iota(jnp.int32, (Q_BLOCK, Q_BLOCK), 1)
    own_bias = jnp.where((ki <= qi) & (ki // SEL_LEN == qi // SEL_LEN), 0.0, MASK_VALUE)
    owns = [pl.multiple_of(t0 + j * Q_BLOCK, Q_BLOCK) for j in range(SEL_Q)]
    s_owns = [lax.dot_general(qa_ref[j * ROWS:(j + 1) * ROWS, 0:LANES], k_ref[pl.ds(owns[j], Q_BLOCK), 0:LANES],
                              dot_nt, preferred_element_type=F32) for j in range(SEL_Q)]
    scores(1, s_b)
    back = jnp.full((Q_BLOCK, LANES), n_steps, jnp.int32).astype(F32)
    p_owns = []
    for j in range(SEL_Q):
        p_own = []
        for rb in range(GROUP):
            s = s_owns[j][_rows(rb)] + own_bias
            m = jnp.max(s, axis=1, keepdims=True)
            p_own.append(jnp.exp2(s - m).astype(MXU_DTYPE))
            m_ref[srows(j, rb), :] = m + back * delta_ref[_head_of_row_block(grp, rb)]
        p_owns.append(jnp.concatenate(p_own, axis=0))
    softmax(s_a, p_a, al_a)
    for j in range(SEL_Q):
        acc_ref[j * ROWS:(j + 1) * ROWS, :] = jnp.dot(p_owns[j], v_ref[pl.ds(owns[j], Q_BLOCK), :],
                                                      preferred_element_type=F32)

    def quad(j, carry):
        even_half(4 * j + 2)
        odd_half(4 * j + 3)
        even_half(4 * j + 4)
        odd_half(4 * j + 5)
        return carry

    lax.fori_loop(0, n_steps // 4, quad, 0)
    rest = 4 * (n_steps // 4) + 2

    @pl.when(n_steps % 4 >= 2)
    def _():
        even_half(rest)
        odd_half(rest + 1)

    @pl.when(n_steps % 2 == 1)
    def _():
        even_half(n_steps + 1)

    gw = PAIRS * LANES
    sig = jax.nn.sigmoid(gate_ref[...])
    hi = sig.astype(MXU_DTYPE)
    lo = (sig - hi.astype(F32)).astype(MXU_DTYPE)
    g = jnp.dot(jnp.concatenate([hi, lo], axis=1), ex_ref[0], preferred_element_type=F32)
    for j in range(SEL_Q):
        o_sel = _pairs(lambda rb: acc_ref[srows(j, rb), 0:LANES], lambda rb: acc_ref[srows(j, rb), LANES:2 * LANES])
        for p in range(PAIRS):
            cols = slice(p * LANES, (p + 1) * LANES)
            out = (g[qrows(j), cols] * oc_ref[qrows(j), cols]
                   + g[qrows(j), gw + p * LANES:gw + (p + 1) * LANES] * o_sel[p]
                   + g[qrows(j), 2 * gw + p * LANES:2 * gw + (p + 1) * LANES] * ow_ref[qrows(j), cols])
            o_ref[qrows(j), cols] = out.astype(o_ref.dtype)


def _gate_expansion():
    gw = PAIRS * LANES
    ex = np.zeros((KV_GROUPS, LANES, 3 * gw), np.float32)
    for g in range(KV_GROUPS):
        for hl in range(GROUP):
            for i in range(3):
                ex[g, 3 * (g * GROUP + hl) + i, i * gw + hl * HEAD_DIM:i * gw + (hl + 1) * HEAD_DIM] = 1.0
    return np.concatenate([ex, ex], axis=1)


def nsa_sel_attention(qkv, sel, gate, o_cmp, o_win, batch, seq, k_col, v_col):
    nstep = seq // (SEL_Q * Q_BLOCK)
    rows = SEL_Q * Q_BLOCK
    srows = SEL_Q * ROWS
    gw = PAIRS * LANES
    deltas = jnp.asarray((-_alibi_neg_slopes().astype(np.float64) * LOG2E * SEL_STEP).astype(np.float32))
    ex = jnp.asarray(_gate_expansion(), MXU_DTYPE)
    kfeat = jnp.asarray(sel_key_features(seq), MXU_DTYPE)
    blk = lambda b, g, c: (b * nstep + c, g)
    return pl.pallas_call(
        _nsa_sel_kernel,
        grid=(batch, KV_GROUPS, nstep),
        in_specs=[
            pl.BlockSpec(memory_space=pltpu.SMEM),
            pl.BlockSpec((1, GROUP, LANES), lambda b, g, c: (g, 0, 0)),
            pl.BlockSpec((rows, gw), blk),
            pl.BlockSpec((1, 1, rows, LANES), lambda b, g, c: (b, g, c, 0)),
            _kv_block_spec(seq, k_col),
            _kv_block_spec(seq, v_col),
            pl.BlockSpec((seq + SEL_STEP, 2 * LANES), lambda b, g, c: (0, 0), pipeline_mode=pl.Buffered(1)),
            pl.BlockSpec((rows, LANES), lambda b, g, c: (b * nstep + c, 0)),
            pl.BlockSpec((1, 2 * LANES, 3 * gw), lambda b, g, c: (g, 0, 0)),
            pl.BlockSpec((rows, gw), blk),
            pl.BlockSpec((rows, gw), blk),
        ],
        out_specs=pl.BlockSpec((rows, gw), blk),
        out_shape=jax.ShapeDtypeStruct((batch * seq, N_HEADS * HEAD_DIM), MXU_DTYPE),
        scratch_shapes=[
            pltpu.VMEM((srows, 2 * LANES), MXU_DTYPE),
            pltpu.VMEM((srows, LANES), F32),
            pltpu.VMEM((srows, 2 * LANES), F32),
            pltpu.VMEM((srows, SEL_STEP), F32),
            pltpu.VMEM((srows, SEL_STEP), F32),
            pltpu.VMEM((srows, SEL_STEP), MXU_DTYPE),
            pltpu.VMEM((srows, SEL_STEP), MXU_DTYPE),
            pltpu.VMEM((srows, LANES), F32),
            pltpu.VMEM((srows, LANES), F32),
            pltpu.VMEM((seq + SEL_STEP, 2 * LANES), MXU_DTYPE),
            pltpu.VMEM((seq + SEL_STEP, 2 * LANES), MXU_DTYPE),
        ],
        compiler_params=_params("parallel", "parallel", "arbitrary"),
        name="nsa_sel_attention",
    )(deltas, sel_query_features(), qkv, sel, qkv, qkv, kfeat, gate, ex, o_cmp, o_win)


def _nsa_win_kernel(nslope_ref, q_ref, k_in, v_in, o_ref, qs_ref, bias_scr, k_ref, v_ref):
    grp = pl.program_id(1)
    c = pl.program_id(2)
    span = NSA_WINDOW + Q_BLOCK
    lead = NSA_WINDOW // Q_BLOCK
    for j in range(Q_SUB):
        _build_q_stack(q_ref, qs_ref.at[j], j * Q_BLOCK)
    nslope = lambda rb: nslope_ref[_head_of_row_block(grp, rb)]

    def write(j, outs):
        for p, o in enumerate(outs):
            o_ref[j * Q_BLOCK:(j + 1) * Q_BLOCK, p * LANES:(p + 1) * LANES] = o

    @pl.when(c == 0)
    def _():
        _unpack_kv(k_in, v_in, grp, k_ref, v_ref)
        d = _pos_tiles(NSA_WINDOW, 0, span)
        for rb in range(GROUP):
            bias_scr[rb] = _band_bias(nslope(rb), d, NSA_WINDOW)

    def leading_block(j):
        d = _pos_tiles((c * Q_SUB + j) * Q_BLOCK, 0, span)
        write(j, _window_attend(qs_ref.at[j], k_ref[0:span, :], v_ref[0:span, :],
                                lambda rb: _band_bias(nslope(rb), d, NSA_WINDOW)))

    def later_block(j):
        start = pl.multiple_of((c * Q_SUB + j) * Q_BLOCK - NSA_WINDOW, Q_BLOCK)
        write(j, _window_attend(qs_ref.at[j], k_ref[pl.ds(start, span), :], v_ref[pl.ds(start, span), :],
                                lambda rb: bias_scr[rb]))

    lead_steps = max(lead // Q_SUB, 1)

    @pl.when(c < lead_steps)
    def _():
        for j in range(Q_SUB):
            if j < lead:
                leading_block(j)
            else:
                later_block(j)

    @pl.when(c >= lead_steps)
    def _():
        for j in range(Q_SUB):
            later_block(j)


def nsa_win_attention(qkv, nslopes, batch, seq, k_col, v_col):
    nstep = seq // (Q_SUB * Q_BLOCK)
    rows = Q_SUB * Q_BLOCK
    gw = PAIRS * LANES
    kd = KV_GROUPS * HEAD_DIM
    span = NSA_WINDOW + Q_BLOCK
    return pl.pallas_call(
        _nsa_win_kernel,
        grid=(batch, KV_GROUPS, nstep),
        in_specs=[
            pl.BlockSpec(memory_space=pltpu.SMEM),
            pl.BlockSpec((rows, gw), lambda b, g, c: (b * nstep + c, g)),
            _kv_block_spec(seq, k_col),
            _kv_block_spec(seq, v_col),
        ],
        out_specs=pl.BlockSpec((rows, gw), lambda b, g, c: (b * nstep + c, g)),
        out_shape=jax.ShapeDtypeStruct((batch * seq, N_HEADS * HEAD_DIM), F32),
        scratch_shapes=[
            pltpu.VMEM((Q_SUB, ROWS, LANES), MXU_DTYPE),
            pltpu.VMEM((GROUP, Q_BLOCK, span), F32),
            pltpu.VMEM((seq, LANES), MXU_DTYPE),
            pltpu.VMEM((seq, 2 * LANES), MXU_DTYPE),
        ],
        compiler_params=_params("parallel", "parallel", "arbitrary"),
        name="nsa_win_attention",
    )(nslopes, qkv, qkv, qkv)


def _overlap_matrix(seq):
    ncp = seq // CMP_STRIDE
    n_cmp = (seq - CMP_LEN) // CMP_STRIDE + 1
    cs = np.arange(n_cmp) * CMP_STRIDE
    ss = np.arange(seq // SEL_LEN) * SEL_LEN
    ov = (cs[:, None] < ss[None, :] + SEL_LEN) & (cs[:, None] + CMP_LEN > ss[None, :])
    out = np.zeros((ncp, LANES), np.float32)
    out[:n_cmp, :seq // SEL_LEN] = ov
    return out


def _swa_layer(h, x, w_in, b_in, sinks, w_o, b_o, g_post, g_next, nslopes, batch, seq):
    hd = N_HEADS * HEAD_DIM
    kd = KV_GROUPS * HEAD_DIM
    qkv = matmul_bias(h, w_in.astype(MXU_DTYPE), b_in, MXU_DTYPE, name="swa_in_proj")
    o = swa_attention(qkv, nslopes, sinks.astype(F32) * LOG2E, batch, seq)
    return matmul_norm_res(o, w_o.astype(MXU_DTYPE), b_o, g_post, g_next, x, tn=w_o.shape[1], name="swa_out_proj")


def _nsa_layer(h, x, w_in, cmp_pe, cmp_w1, cmp_b1, cmp_w2, cmp_b2, w_o, g_post, g_next, nslopes, batch, seq):
    hd = N_HEADS * HEAD_DIM
    kd = KV_GROUPS * HEAD_DIM
    t = batch * seq
    ncp = seq // CMP_STRIDE
    qkv = matmul_bias(h, w_in[:, :hd + 6 * kd].astype(MXU_DTYPE), jnp.zeros((hd + 6 * kd,), F32), MXU_DTYPE,
                      name="nsa_in_proj")
    n_gate = 3 * N_HEADS
    w_gate = jnp.pad(w_in[:, hd + 6 * kd:], ((0, 0), (0, LANES - n_gate))).astype(MXU_DTYPE)
    gate = matmul_bias(h, w_gate, jnp.zeros((LANES,), F32), F32, name="nsa_gate_proj")

    def kv(i):
        return qkv[:, hd + i * kd:hd + (i + 1) * kd]

    def slabs(a):
        return a.reshape(batch, seq, KV_GROUPS, HEAD_DIM).transpose(0, 2, 1, 3).reshape(
            batch, KV_GROUPS, ncp, CMP_STRIDE * HEAD_DIM)

    z = jnp.stack([slabs(kv(0)), slabs(kv(1))])
    half = CMP_STRIDE * HEAD_DIM
    cmp_out = compress(z, cmp_pe.reshape(2, 2, half).astype(F32), cmp_w1.astype(MXU_DTYPE),
                       cmp_b1.reshape(2, 1, -1), cmp_w2.astype(MXU_DTYPE), cmp_b2.reshape(2, 1, -1))
    kcm = cmp_out[0].astype(MXU_DTYPE)
    vcm = cmp_out[1].astype(MXU_DTYPE)
    kc2 = jnp.concatenate([kcm, kcm], axis=-1)
    ov = jnp.broadcast_to(jnp.asarray(_overlap_matrix(seq), MXU_DTYPE), (batch, KV_GROUPS, ncp, LANES))
    vc_aug = jnp.concatenate([vcm, vcm, jnp.ones((batch, KV_GROUPS, ncp, LANES), MXU_DTYPE), ov], axis=-1)

    o_cmp, sel = nsa_cmp_select(qkv, kc2, vc_aug, nslopes, batch, seq, 0)
    kv_col = lambda i: hd // kd + i
    o_win = nsa_win_attention(qkv, nslopes, batch, seq, kv_col(4), kv_col(5))
    o = nsa_sel_attention(qkv, sel, gate, o_cmp, o_win, batch, seq, kv_col(2), kv_col(3))
    return matmul_norm_res(o, w_o.astype(MXU_DTYPE), jnp.zeros((w_o.shape[1],), F32), g_post, g_next, x,
                           tn=w_o.shape[1], name="nsa_out_proj")


def kernel(x, norm_g, swa_w_in, swa_b_in, swa_sinks, swa_w_o, swa_b_o, nsa_w_in, nsa_cmp_pe, nsa_cmp_w1, nsa_cmp_b1, nsa_cmp_w2, nsa_cmp_b2, nsa_w_o, ffn_w_gate, ffn_w_up, ffn_conv_w, ffn_conv_b, ffn_w_down):
    batch, seq, d = x.shape
    depth = norm_g.shape[0]
    nslopes = jnp.asarray((_alibi_neg_slopes().astype(np.float64) * LOG2E).astype(np.float32))
    xf = x.reshape(batch * seq, d)
    h = rms_cast(xf, norm_g[0, 0])
    for i in range(depth):
        g = norm_g[i]
        j = i // 2
        if i % 2 == 0:
            xf, h = _swa_layer(h, xf, swa_w_in[j], swa_b_in[j], swa_sinks[j], swa_w_o[j], swa_b_o[j],
                               g[1], g[2], nslopes, batch, seq)
        else:
            xf, h = _nsa_layer(h, xf, nsa_w_in[j], nsa_cmp_pe[j], nsa_cmp_w1[j], nsa_cmp_b1[j], nsa_cmp_w2[j],
                               nsa_cmp_b2[j], nsa_w_o[j], g[1], g[2], nslopes, batch, seq)
        act = ffn_up(h, ffn_w_gate, ffn_w_up, i, ffn_conv_w[i], ffn_conv_b[i], seq)
        g_next = norm_g[i + 1, 0] if i + 1 < depth else jnp.ones((d,), F32)
        xf, h = matmul_norm_res_ksplit(act, ffn_w_down[i].astype(MXU_DTYPE), g[3], g_next, xf, name="ffn_down")
    return xf.reshape(batch, seq, d)
```
